```python
import math
import jax, jax.numpy as jnp
from jax import lax
import numpy as np

D_MODEL = 2048
BATCH = 32
SEQ = 256
DEPTH = 1
DEC_BATCH = 2
DEC_SEQ = 1024
PAST_LEN = 256

GRID_W = 64
D_MIX = D_MODEL
SSD_WIDTH = D_MIX // 2
SSD_HEAD_DIM = 64
SSD_HEADS = SSD_WIDTH // SSD_HEAD_DIM
SSD_GROUPS = 2
SSD_STATE = 128
SSD_CONV_W = 5
SSD_CHUNK = 128
SSD_CONV_CH = SSD_WIDTH + 2 * SSD_GROUPS * SSD_STATE
RWKV_WIDTH = D_MIX - SSD_WIDTH
RWKV_HEAD_DIM = 64
RWKV_HEADS = RWKV_WIDTH // RWKV_HEAD_DIM
DECAY_LORA = 64
ICLR_LORA = 64
GATE_LORA = 160
N_SSD_IN = 2 * SSD_WIDTH + 2 * SSD_GROUPS * SSD_STATE + 2 * SSD_HEADS
N_RWKV_IN = 3 * RWKV_WIDTH + 2 * DECAY_LORA + ICLR_LORA + GATE_LORA
N_IN = N_SSD_IN + N_RWKV_IN
RWKV_SPLITS = [RWKV_WIDTH, 2 * RWKV_WIDTH, 3 * RWKV_WIDTH, 3 * RWKV_WIDTH + DECAY_LORA,
               3 * RWKV_WIDTH + 2 * DECAY_LORA, 3 * RWKV_WIDTH + 2 * DECAY_LORA + ICLR_LORA]
N_EXPERTS = 16
EC_CAPACITY = 2
EXPERT_FF = 1024
N_MOD = 6
NORM_EPS = 1e-6
GN_EPS = 64e-5
DECAY_SCALE = 0.606531
POS_BASE = 10000.0

kernel_name = "hybrid_ssd_rwkv7_ec_diffusion_step"


def rms_norm(x, g):
    x32 = x.astype(jnp.float32)
    y = x32 * lax.rsqrt(jnp.mean(x32 * x32, axis=-1, keepdims=True) + NORM_EPS)
    return (y * g.astype(jnp.float32)).astype(x.dtype)


def adaln_modulation(cond, w, b):
    return (jax.nn.silu(cond) @ w + b).reshape(cond.shape[0], N_MOD, D_MODEL)


def grid_pos_embed(n_tokens, dim):
    rows = n_tokens // GRID_W
    row = jnp.repeat(jnp.arange(rows, dtype=jnp.float32), GRID_W)
    col = jnp.tile(jnp.arange(GRID_W, dtype=jnp.float32), rows)
    quarter = dim // 4
    freqs = jnp.exp(jnp.arange(quarter, dtype=jnp.float32) * (-math.log(POS_BASE) / quarter))

    def axis_embed(pos):
        ang = pos[:, None] * freqs[None, :]
        return jnp.concatenate([jnp.sin(ang), jnp.cos(ang)], axis=-1)

    return jnp.concatenate([axis_embed(row), axis_embed(col)], axis=-1)


def centred_depthwise_conv(u, w, b):
    pad = w.shape[0] // 2
    out = lax.conv_general_dilated(u, w[:, None, :].astype(u.dtype), window_strides=(1,),
                                   padding=[(pad, pad)], dimension_numbers=("NWC", "WIO", "NWC"),
                                   feature_group_count=u.shape[-1])
    return out + b


def centred_shift(u):
    up = jnp.pad(u, ((0, 0), (1, 1), (0, 0)))
    return 0.5 * (up[:, :-2] + up[:, 2:])


def ssd_chunked_scan(x, dt, A, Bm, Cm, h0):
    bsz, seq, nh, hp = x.shape
    nc = seq // SSD_CHUNK
    chunk = lambda t: t.reshape((bsz, nc, SSD_CHUNK) + t.shape[2:])
    x, dt, Bm, Cm = chunk(x), chunk(dt), chunk(Bm), chunk(Cm)
    a_cum = jnp.cumsum(dt * A, axis=2)
    pos = jnp.arange(SSD_CHUNK)
    lower = (pos[:, None] >= pos[None, :])[None, None, :, :, None]
    seg = a_cum[:, :, :, None, :] - a_cum[:, :, None, :, :]
    decay_in = jnp.exp(jnp.where(lower, seg, -jnp.inf))
    xdt = x * dt[..., None]
    scores = jnp.einsum("bcihn,bcjhn->bcijh", Cm, Bm) * decay_in
    y_diag = jnp.einsum("bcijh,bcjhp->bcihp", scores, xdt)
    decay_to_end = jnp.exp(a_cum[:, :, -1:, :] - a_cum)
    chunk_states = jnp.einsum("bcjhn,bcjhp->bchpn", Bm * decay_to_end[..., None], xdt)
    chunk_decay = jnp.exp(a_cum[:, :, -1, :])

    def step(h, inp):
        s_c, d_c = inp
        return h * d_c[:, :, None, None] + s_c, h

    h_final, h_in = lax.scan(step, h0, (jnp.moveaxis(chunk_states, 1, 0), jnp.moveaxis(chunk_decay, 1, 0)))
    h_in = jnp.moveaxis(h_in, 0, 1)
    y_off = jnp.einsum("bcihn,bchpn->bcihp", Cm * jnp.exp(a_cum)[..., None], h_in)
    return (y_diag + y_off).reshape(bsz, seq, nh, hp), h_final


def ssd_mixer(u, p, h0_f, h0_b):
    bsz, seq, _ = u.shape
    f32 = jnp.float32
    z, xbc, dt_raw = jnp.split(u, [SSD_WIDTH, SSD_WIDTH + SSD_CONV_CH], axis=-1)
    xbc = jax.nn.silu(centred_depthwise_conv(xbc, p["ssd_conv_w"], p["ssd_conv_b"])).astype(f32)
    xh, Bm, Cm = jnp.split(xbc, [SSD_WIDTH, SSD_WIDTH + SSD_GROUPS * SSD_STATE], axis=-1)
    xh = xh.reshape(bsz, seq, SSD_HEADS, SSD_HEAD_DIM)
    rep = SSD_HEADS // SSD_GROUPS
    Bm = jnp.repeat(Bm.reshape(bsz, seq, SSD_GROUPS, SSD_STATE), rep, axis=2)
    Cm = jnp.repeat(Cm.reshape(bsz, seq, SSD_GROUPS, SSD_STATE), rep, axis=2)
    dt = jax.nn.softplus(dt_raw.astype(f32).reshape(bsz, seq, 2, SSD_HEADS) + p["ssd_dt_bias"].astype(f32))
    A = -jnp.exp(p["ssd_A_log"].astype(f32))
    rev = lambda t: t[:, ::-1]
    y_f, h_f = ssd_chunked_scan(xh, dt[:, :, 0], A[0], Bm, Cm, h0_f.astype(f32))
    y_b, h_b = ssd_chunked_scan(rev(xh), rev(dt[:, :, 1]), A[1], rev(Bm), rev(Cm), h0_b.astype(f32))
    y = y_f + rev(y_b) + p["ssd_D"].astype(f32)[:, None] * xh
    y = y.reshape(bsz, seq, SSD_WIDTH) * jax.nn.silu(z.astype(f32))
    y = y.reshape(bsz, seq, SSD_GROUPS, SSD_WIDTH // SSD_GROUPS)
    y = y * lax.rsqrt(jnp.mean(y * y, axis=-1, keepdims=True) + NORM_EPS)
    y = y.reshape(bsz, seq, SSD_WIDTH) * p["ssd_norm_g"].astype(f32)
    return y, h_f, h_b


def rwkv7_scan(r, w, k, v, kk, a, s0):
    def step(S, inp):
        r_t, w_t, k_t, v_t, kk_t, a_t = inp
        sa = jnp.einsum("bhvk,bhk->bhv", S, -kk_t)
        S = S * w_t[:, :, None, :] + sa[..., None] * (kk_t * a_t)[:, :, None, :] + v_t[..., None] * k_t[:, :, None, :]
        return S, jnp.einsum("bhvk,bhk->bhv", S, r_t)

    xs = tuple(jnp.moveaxis(t, 1, 0) for t in (r, w, k, v, kk, a))
    s_final, ys = lax.scan(step, s0, xs)
    return jnp.moveaxis(ys, 0, 1), s_final


def rwkv7_mixer(u, p, s0_f, s0_b):
    bsz, seq, _ = u.shape
    f32 = jnp.float32
    u = u.astype(f32)
    u = u + p["rwkv_mu"] * (centred_shift(u) - u)
    r, k, v, wd_f, wd_b, ad, gd = jnp.split(u, RWKV_SPLITS, axis=-1)
    w0 = p["rwkv_w0"].astype(f32)
    w_up = p["rwkv_w_up"].astype(f32)
    decay_f = jnp.exp(-DECAY_SCALE * jax.nn.sigmoid(w0[0] + jnp.tanh(wd_f) @ w_up[0]))
    decay_b = jnp.exp(-DECAY_SCALE * jax.nn.sigmoid(w0[1] + jnp.tanh(wd_b) @ w_up[1]))
    a = jax.nn.sigmoid(p["rwkv_a0"] + ad @ p["rwkv_a_up"])
    g = jax.nn.sigmoid(gd) @ p["rwkv_g_up"]
    heads = lambda t: t.reshape(bsz, seq, RWKV_HEADS, RWKV_HEAD_DIM)
    kk = heads(k * p["rwkv_k_k"])
    kk = kk / jnp.maximum(jnp.sqrt(jnp.sum(kk * kk, axis=-1, keepdims=True)), 1e-12)
    k = heads(k * (1.0 + (a - 1.0) * p["rwkv_k_a"]))
    r, v, a = heads(r), heads(v), heads(a)
    rev = lambda t: t[:, ::-1]
    y_f, s_f = rwkv7_scan(r, heads(decay_f), k, v, kk, a, s0_f.astype(f32))
    y_b, s_b = rwkv7_scan(rev(r), rev(heads(decay_b)), rev(k), rev(v), rev(kk), rev(a), s0_b.astype(f32))
    y = y_f + rev(y_b)
    mean = jnp.mean(y, axis=-1, keepdims=True)
    var = jnp.mean(jnp.square(y - mean), axis=-1, keepdims=True)
    y = ((y - mean) * lax.rsqrt(var + GN_EPS)).reshape(bsz, seq, RWKV_WIDTH) * p["rwkv_ln_g"] + p["rwkv_ln_b"]
    bonus = jnp.sum(r * k * p["rwkv_r_k"].astype(f32), axis=-1, keepdims=True) * v
    y = (y + bonus.reshape(bsz, seq, RWKV_WIDTH)) * g
    return y, s_f, s_b


def expert_choice_ffn(h, p):
    bsz, seq, dm = h.shape
    t = h.reshape(bsz * seq, dm)
    n_tok = t.shape[0]
    cap = EC_CAPACITY * n_tok // N_EXPERTS
    aff = jax.nn.softmax((t @ p["router_w"]).astype(jnp.float32), axis=-1)
    gate, idx = lax.top_k(aff.T, cap)
    xs = t[idx]
    hid = jax.nn.silu(jnp.einsum("ecd,edf->ecf", xs, p["exp_w_gate"])) * jnp.einsum("ecd,edf->ecf", xs, p["exp_w_up"])
    out = jnp.einsum("ecf,efd->ecd", hid, p["exp_w_down"]) * gate[..., None].astype(t.dtype)
    y = jnp.zeros_like(t).at[idx.reshape(-1)].add(out.reshape(-1, dm).astype(t.dtype))
    return y.reshape(bsz, seq, dm)


def trunk_layer(x, mod, p, states):
    shift_m, scale_m, gate_m, shift_f, scale_f, gate_f = [mod[:, i, None, :] for i in range(N_MOD)]
    h = rms_norm(x, p["norm_mix_g"]) * (1 + scale_m) + shift_m
    proj = h @ p["w_in"]
    y_ssd, h_f, h_b = ssd_mixer(proj[..., :N_SSD_IN], p, states[0], states[1])
    y_rwkv, s_f, s_b = rwkv7_mixer(proj[..., N_SSD_IN:], p, states[2], states[3])
    mixed = jnp.concatenate([y_ssd, y_rwkv], axis=-1).astype(x.dtype) @ p["w_out"]
    x = x + gate_m * mixed
    h = rms_norm(x, p["norm_ffn_g"]) * (1 + scale_f) + shift_f
    x = x + gate_f * expert_choice_ffn(h, p)
    return x, (h_f, h_b, s_f, s_b)


def setup_inputs(seed: int = 0) -> dict:
    key = jax.random.key(seed)
    ks = iter(jax.random.split(key, 64))

    def nrm(shape, scale):
        return scale * jax.random.normal(next(ks), shape, jnp.float32)

    def near(shape, centre):
        return centre + nrm(shape, 0.05)

    L = DEPTH
    dt0 = jnp.exp(jax.random.uniform(next(ks), (L, 2, SSD_HEADS), jnp.float32, math.log(1e-3), math.log(1e-1)))
    return {
        "x_prompt": nrm((BATCH, SEQ, D_MODEL), 1.0),
        "x_sample": nrm((DEC_BATCH, DEC_SEQ, D_MODEL), 1.0),
        "state_ssd_fwd": nrm((DEC_BATCH, L, SSD_HEADS, SSD_HEAD_DIM, SSD_STATE), 0.5),
        "state_ssd_bwd": nrm((DEC_BATCH, L, SSD_HEADS, SSD_HEAD_DIM, SSD_STATE), 0.5),
        "state_rwkv_fwd": nrm((DEC_BATCH, L, RWKV_HEADS, RWKV_HEAD_DIM, RWKV_HEAD_DIM), 0.5),
        "state_rwkv_bwd": nrm((DEC_BATCH, L, RWKV_HEADS, RWKV_HEAD_DIM, RWKV_HEAD_DIM), 0.5),
        "c": nrm((DEC_BATCH, D_MODEL), 1.0),
        "c_ctx": nrm((D_MODEL,), 1.0),
        "w_ada": nrm((L, D_MODEL, N_MOD * D_MODEL), 0.5 * D_MODEL ** -0.5),
        "b_ada": nrm((L, N_MOD * D_MODEL), 0.02),
        "norm_mix_g": near((L, D_MODEL), 1.0),
        "norm_ffn_g": near((L, D_MODEL), 1.0),
        "w_in": nrm((L, D_MODEL, N_IN), D_MODEL ** -0.5),
        "w_out": nrm((L, D_MIX, D_MODEL), D_MIX ** -0.5),
        "ssd_conv_w": nrm((L, SSD_CONV_W, SSD_CONV_CH), SSD_CONV_W ** -0.5),
        "ssd_conv_b": nrm((L, SSD_CONV_CH), 0.02),
        "ssd_A_log": jnp.log(jax.random.uniform(next(ks), (L, 2, SSD_HEADS), jnp.float32, 1.0, 16.0)),
        "ssd_dt_bias": dt0 + jnp.log(-jnp.expm1(-dt0)),
        "ssd_D": near((L, SSD_HEADS), 1.0),
        "ssd_norm_g": near((L, SSD_WIDTH), 1.0),
        "rwkv_mu": jax.random.uniform(next(ks), (L, N_RWKV_IN), jnp.float32, 0.0, 1.0),
        "rwkv_w0": nrm((L, 2, RWKV_WIDTH), 1.0),
        "rwkv_w_up": nrm((L, 2, DECAY_LORA, RWKV_WIDTH), 0.5 * DECAY_LORA ** -0.5),
        "rwkv_a0": nrm((L, RWKV_WIDTH), 0.5),
        "rwkv_a_up": nrm((L, ICLR_LORA, RWKV_WIDTH), 0.5 * ICLR_LORA ** -0.5),
        "rwkv_g_up": nrm((L, GATE_LORA, RWKV_WIDTH), GATE_LORA ** -0.5),
        "rwkv_k_k": near((L, RWKV_WIDTH), 0.85),
        "rwkv_k_a": near((L, RWKV_WIDTH), 1.0),
        "rwkv_r_k": nrm((L, RWKV_HEADS, RWKV_HEAD_DIM), 0.1),
        "rwkv_ln_g": near((L, RWKV_WIDTH), 1.0),
        "rwkv_ln_b": nrm((L, RWKV_WIDTH), 0.02),
        "router_w": nrm((L, D_MODEL, N_EXPERTS), D_MODEL ** -0.5),
        "exp_w_gate": nrm((L, N_EXPERTS, D_MODEL, EXPERT_FF), D_MODEL ** -0.5),
        "exp_w_up": nrm((L, N_EXPERTS, D_MODEL, EXPERT_FF), D_MODEL ** -0.5),
        "exp_w_down": nrm((L, N_EXPERTS, EXPERT_FF, D_MODEL), EXPERT_FF ** -0.5),
        "final_norm_g": near((D_MODEL,), 1.0),
    }


def reference(x_prompt, x_sample, state_ssd_fwd, state_ssd_bwd, state_rwkv_fwd, state_rwkv_bwd, c, c_ctx,
              w_ada, b_ada, norm_mix_g, norm_ffn_g, w_in, w_out, ssd_conv_w, ssd_conv_b, ssd_A_log,
              ssd_dt_bias, ssd_D, ssd_norm_g, rwkv_mu, rwkv_w0, rwkv_w_up, rwkv_a0, rwkv_a_up, rwkv_g_up,
              rwkv_k_k, rwkv_k_a, rwkv_r_k, rwkv_ln_g, rwkv_ln_b, router_w, exp_w_gate, exp_w_up,
              exp_w_down, final_norm_g):
    layers = [dict(norm_mix_g=norm_mix_g[l], norm_ffn_g=norm_ffn_g[l], w_in=w_in[l], w_out=w_out[l],
                   ssd_conv_w=ssd_conv_w[l], ssd_conv_b=ssd_conv_b[l], ssd_A_log=ssd_A_log[l],
                   ssd_dt_bias=ssd_dt_bias[l], ssd_D=ssd_D[l], ssd_norm_g=ssd_norm_g[l],
                   rwkv_mu=rwkv_mu[l], rwkv_w0=rwkv_w0[l], rwkv_w_up=rwkv_w_up[l], rwkv_a0=rwkv_a0[l],
                   rwkv_a_up=rwkv_a_up[l], rwkv_g_up=rwkv_g_up[l], rwkv_k_k=rwkv_k_k[l], rwkv_k_a=rwkv_k_a[l],
                   rwkv_r_k=rwkv_r_k[l], rwkv_ln_g=rwkv_ln_g[l], rwkv_ln_b=rwkv_ln_b[l], router_w=router_w[l],
                   exp_w_gate=exp_w_gate[l], exp_w_up=exp_w_up[l], exp_w_down=exp_w_down[l])
              for l in range(DEPTH)]

    bp = x_prompt.shape[0]
    zero_states = (jnp.zeros((bp, SSD_HEADS, SSD_HEAD_DIM, SSD_STATE), jnp.float32),
                   jnp.zeros((bp, SSD_HEADS, SSD_HEAD_DIM, SSD_STATE), jnp.float32),
                   jnp.zeros((bp, RWKV_HEADS, RWKV_HEAD_DIM, RWKV_HEAD_DIM), jnp.float32),
                   jnp.zeros((bp, RWKV_HEADS, RWKV_HEAD_DIM, RWKV_HEAD_DIM), jnp.float32))
    xp = x_prompt
    ctx_states = []
    for l in range(DEPTH):
        mod_ctx = adaln_modulation(c_ctx[None, :], w_ada[l], b_ada[l])
        xp, st = trunk_layer(xp, mod_ctx, layers[l], zero_states)
        ctx_states.append(st)
    y_prompt = rms_norm(xp, final_norm_g)
    new_ssd_fwd = jnp.stack([s[0] for s in ctx_states], axis=1).astype(x_prompt.dtype)
    new_ssd_bwd = jnp.stack([s[1] for s in ctx_states], axis=1).astype(x_prompt.dtype)
    new_rwkv_fwd = jnp.stack([s[2] for s in ctx_states], axis=1).astype(x_prompt.dtype)
    new_rwkv_bwd = jnp.stack([s[3] for s in ctx_states], axis=1).astype(x_prompt.dtype)

    xs = x_sample + grid_pos_embed(x_sample.shape[1], D_MODEL).astype(x_sample.dtype)[None]
    for l in range(DEPTH):
        mod_lat = adaln_modulation(c, w_ada[l], b_ada[l])
        cached = (state_ssd_fwd[:, l], state_ssd_bwd[:, l], state_rwkv_fwd[:, l], state_rwkv_bwd[:, l])
        xs, _ = trunk_layer(xs, mod_lat, layers[l], cached)
    y_sample = rms_norm(xs, final_norm_g)

    return (y_prompt, y_sample, new_ssd_fwd, new_ssd_bwd, new_rwkv_fwd, new_rwkv_bwd)
```

```python
import functools
import math

import jax
import jax.numpy as jnp
from jax import lax
from jax.experimental import pallas as pl
from jax.experimental.pallas import tpu as pltpu

F32 = jnp.float32
BF16 = jnp.bfloat16
HIGHEST = lax.Precision.HIGHEST

D_MODEL = 2048
GRID_W = 64
SSD_WIDTH = 1024
SSD_HEAD_DIM = 64
SSD_HEADS = 16
SSD_GROUPS = 2
SSD_STATE = 128
SSD_CONV_W = 5
SSD_CHUNK = 128
RWKV_WIDTH = 1024
RWKV_HEAD_DIM = 64
RWKV_HEADS = 16
DECAY_LORA = 64
ICLR_LORA = 64
GATE_LORA = 160
N_EXPERTS = 16
EC_CAPACITY = 2
EXPERT_FF = 1024
N_MOD = 6
NORM_EPS = 1e-6
GN_EPS = 64e-5
DECAY_SCALE = 0.606531
POS_BASE = 10000.0
SEARCH_ITERS = 48

LANES = 128
SUBLANES = 8

COL_Z = 0
COL_X = 1024
COL_B = 2048
COL_C = 2304
COL_DT = 2560
COL_R = 2688
COL_K = 3712
COL_V = 4736
COL_LORA = 5760
LORA_W = 640
N_PROJ = COL_LORA + LORA_W

VMEM_LIMIT = 56 * 1024 * 1024


def _cparams(sem):
    return pltpu.CompilerParams(dimension_semantics=sem, vmem_limit_bytes=VMEM_LIMIT)


def _silu(x):
    return x * jax.nn.sigmoid(x)


def _ada_kernel(c_ref, w_ref, b_ref, o_ref):
    s = _silu(c_ref[...]).astype(BF16)
    o_ref[...] = jnp.dot(s, w_ref[...].astype(BF16), preferred_element_type=F32) + b_ref[...]


def ada_modulation(cond, w, b):
    m, d = cond.shape
    n = w.shape[1]
    tn = 1024
    return pl.pallas_call(
        _ada_kernel,
        grid=(n // tn,),
        in_specs=[pl.BlockSpec((m, d), lambda j: (0, 0)),
                  pl.BlockSpec((d, tn), lambda j: (0, j)),
                  pl.BlockSpec((1, tn), lambda j: (0, j))],
        out_specs=pl.BlockSpec((m, tn), lambda j: (0, j)),
        out_shape=jax.ShapeDtypeStruct((m, n), F32),
        compiler_params=_cparams(("parallel",)),
        name="ada_modulation",
    )(cond, w, b.reshape(1, n))


def _norm_mod_kernel(has_pos, *refs):
    if has_pos:
        x_ref, pos_ref, g_ref, sc_ref, sh_ref, h_ref, x0_ref = refs
    else:
        x_ref, g_ref, sc_ref, sh_ref, h_ref = refs
    x = x_ref[0]
    if has_pos:
        x = x + pos_ref[...]
        x0_ref[0] = x
    y = x * lax.rsqrt(jnp.mean(x * x, axis=-1, keepdims=True) + NORM_EPS) * g_ref[...]
    h_ref[0] = (y * (1.0 + sc_ref[0]) + sh_ref[0]).astype(BF16)


def norm_modulate(x, pos, g, scale, shift):
    bsz, seq, d = x.shape
    tl = 256
    per_b = scale.shape[0] > 1
    midx = (lambda b, i: (b, 0, 0)) if per_b else (lambda b, i: (0, 0, 0))
    xspec = pl.BlockSpec((1, tl, d), lambda b, i: (b, i, 0))
    in_specs = [xspec]
    args = [x]
    if pos is not None:
        in_specs.append(pl.BlockSpec((tl, d), lambda b, i: (i, 0)))
        args.append(pos)
    in_specs += [pl.BlockSpec((1, d), lambda b, i: (0, 0)),
                 pl.BlockSpec((1, 1, d), midx), pl.BlockSpec((1, 1, d), midx)]
    args += [g.reshape(1, d), scale, shift]
    out_shape = [jax.ShapeDtypeStruct((bsz, seq, d), BF16)]
    out_specs = [xspec]
    if pos is not None:
        out_shape.append(jax.ShapeDtypeStruct((bsz, seq, d), F32))
        out_specs.append(xspec)
    res = pl.pallas_call(
        functools.partial(_norm_mod_kernel, pos is not None),
        grid=(bsz, seq // tl),
        in_specs=in_specs, out_specs=out_specs, out_shape=out_shape,
        compiler_params=_cparams(("parallel", "parallel")),
        name="norm_modulate",
    )(*args)
    return (res[0], res[1]) if pos is not None else (res[0], x)


def _mm_kernel(a_ref, b_ref, o_ref):
    o_ref[...] = jnp.dot(a_ref[...], b_ref[...], preferred_element_type=F32)


def matmul_bf16(a, b, tm, tn):
    m, k = a.shape
    n = b.shape[1]
    return pl.pallas_call(
        _mm_kernel,
        grid=(m // tm, n // tn),
        in_specs=[pl.BlockSpec((tm, k), lambda i, j: (i, 0)),
                  pl.BlockSpec((k, tn), lambda i, j: (0, j))],
        out_specs=pl.BlockSpec((tm, tn), lambda i, j: (i, j)),
        out_shape=jax.ShapeDtypeStruct((m, n), F32),
        compiler_params=_cparams(("parallel", "arbitrary")),
        name="in_proj",
    )(a, b)


def _shifted(u, off):
    n = u.shape[0]
    row = lax.broadcasted_iota(jnp.int32, (n, 1), 0)
    rolled = pltpu.roll(u, (-off) % n, axis=0)
    valid = jnp.logical_and(row + off >= 0, row + off < n)
    return jnp.where(valid, rolled, 0.0)


def _conv_silu(u, w_ref, b_ref, cols):
    pad = SSD_CONV_W // 2
    acc = u * w_ref[pad:pad + 1, cols] + b_ref[:, cols]
    for j in range(SSD_CONV_W):
        if j != pad:
            acc = acc + _shifted(u, j - pad) * w_ref[j:j + 1, cols]
    return _silu(acc)


def _softplus(x):
    return jnp.maximum(x, 0.0) + jnp.log1p(jnp.exp(-jnp.abs(x)))


def _ssd_kernel(nc, zero_init, *refs):
    q = SSD_CHUNK
    (z_ref, x_ref, b_ref, c_ref, dt_ref, cwx, cbx, cwb, cbb, cwc, cbc,
     dtb_ref, alog_ref, d_ref, ng_ref) = refs[:15]
    refs = refs[15:]
    if not zero_init:
        h0f_ref, h0b_ref = refs[:2]
        refs = refs[2:]
    y_ref, hf_ref, hb_ref, xh_s, bm_s, cm_s, dt_s, a_s, at_s, y_s, h_s = refs

    slab = 256
    for cb in range(SSD_WIDTH // slab):
        cols = slice(cb * slab, (cb + 1) * slab)
        xh = _conv_silu(x_ref[0, :, cols], cwx, cbx, cols)
        for c in range(nc):
            rows = slice(c * q, (c + 1) * q)
            xh_s[c, :, cols] = xh[rows]
            y_s[c, :, cols] = xh[rows] * d_ref[:, cols]
    allc = slice(0, SSD_GROUPS * SSD_STATE)
    bm = _conv_silu(b_ref[0], cwb, cbb, allc)
    cm = _conv_silu(c_ref[0], cwc, cbc, allc)
    dt = _softplus(dt_ref[0] + dtb_ref[...])
    a = dt * (-jnp.exp(alog_ref[...]))
    ii = lax.broadcasted_iota(jnp.int32, (q, q), 0)
    jj = lax.broadcasted_iota(jnp.int32, (q, q), 1)
    eye = (ii == jj).astype(F32)
    lower = ii >= jj
    upper = ii <= jj
    for c in range(nc):
        rows = slice(c * q, (c + 1) * q)
        bm_s[c] = bm[rows]
        cm_s[c] = cm[rows]
        dt_s[c] = dt[rows]
        a_s[c] = a[rows]
        at_s[c] = lax.dot_general(eye, a[rows], (((1,), (1,)), ((), ())),
                                  precision=HIGHEST, preferred_element_type=F32)

    for d in range(2):
        mask = lower if d == 0 else upper
        tri = mask.astype(F32)
        tri_t = (upper if d == 0 else lower).astype(F32)
        if zero_init:
            h_s[...] = jnp.zeros_like(h_s)
        else:
            h_s[...] = (h0f_ref if d == 0 else h0b_ref)[0]

        def chunk_body(ci, carry, d=d, mask=mask, tri=tri, tri_t=tri_t):
            c = ci if d == 0 else nc - 1 - ci
            a_c = a_s[c]
            cum = jnp.dot(tri, a_c, precision=HIGHEST, preferred_element_type=F32)
            cum_t = jnp.dot(at_s[c], tri_t, precision=HIGHEST, preferred_element_type=F32)
            dt_c = dt_s[c]
            for g in range(SSD_GROUPS):
                gcols = slice(g * SSD_STATE, (g + 1) * SSD_STATE)
                bg = bm_s[c, :, gcols]
                cg = cm_s[c, :, gcols]
                gram = lax.dot_general(cg.astype(BF16), bg.astype(BF16), (((1,), (1,)), ((), ())),
                                       preferred_element_type=F32)
                for hh in range(SSD_HEADS // SSD_GROUPS):
                    h = g * (SSD_HEADS // SSD_GROUPS) + hh
                    col = d * SSD_HEADS + h
                    hcols = slice(h * SSD_HEAD_DIM, (h + 1) * SSD_HEAD_DIM)
                    cc = cum[:, col:col + 1]
                    cr = cum_t[col:col + 1, :]
                    dec = jnp.exp(jnp.where(mask, cc - cr, -jnp.inf))
                    m = (gram * dec).astype(BF16)
                    xdt = (xh_s[c, :, hcols] * dt_c[:, col:col + 1]).astype(BF16)
                    tot = cum[q - 1:q, col:col + 1] if d == 0 else cum[0:1, col:col + 1]
                    bd = (bg * jnp.exp(tot - cc)).astype(BF16)
                    cd = (cg * jnp.exp(cc)).astype(BF16)
                    hprev = h_s[h]
                    s_c = lax.dot_general(xdt, bd, (((0,), (0,)), ((), ())), preferred_element_type=F32)
                    y_off = lax.dot_general(cd, hprev.astype(BF16), (((1,), (1,)), ((), ())),
                                            preferred_element_type=F32)
                    y_diag = jnp.dot(m, xdt, preferred_element_type=F32)
                    h_s[h] = hprev * jnp.exp(tot) + s_c
                    y_s[c, :, hcols] += y_diag + y_off
            return carry

        lax.fori_loop(0, nc, chunk_body, 0)
        (hf_ref if d == 0 else hb_ref)[0] = h_s[...]

    gw = SSD_WIDTH // SSD_GROUPS
    for c in range(nc):
        rows = slice(c * q, (c + 1) * q)
        for g in range(SSD_GROUPS):
            cols = slice(g * gw, (g + 1) * gw)
            y = y_s[c, :, cols] * _silu(z_ref[0, rows, cols])
            y = y * lax.rsqrt(jnp.mean(y * y, axis=-1, keepdims=True) + NORM_EPS)
            y_ref[0, rows, cols] = (y * ng_ref[:, cols]).astype(BF16)


def ssd_mixer(proj, p, h0f, h0b):
    bsz, seq, _ = proj.shape
    nc = seq // SSD_CHUNK
    zero_init = h0f is None

    def col_spec(width, start):
        blk = start // width
        return pl.BlockSpec((1, seq, width), lambda b: (b, 0, blk))

    def full2(arr):
        return pl.BlockSpec(arr.shape, lambda b: (0, 0))

    st_spec = pl.BlockSpec((1, SSD_HEADS, SSD_HEAD_DIM, SSD_STATE), lambda b: (b, 0, 0, 0))
    bc = SSD_GROUPS * SSD_STATE
    small = [p["cw_x"], p["cb_x"], p["cw_b"], p["cb_b"], p["cw_c"], p["cb_c"],
             p["dt_bias"], p["a_log"], p["d_exp"], p["ssd_ng"]]
    in_specs = [col_spec(SSD_WIDTH, COL_Z), col_spec(SSD_WIDTH, COL_X), col_spec(bc, COL_B),
                col_spec(bc, COL_C), col_spec(LANES, COL_DT)] + [full2(s) for s in small]
    args = [proj] * 5 + small
    if not zero_init:
        in_specs += [st_spec, st_spec]
        args += [h0f, h0b]
    st_shape = jax.ShapeDtypeStruct((bsz, SSD_HEADS, SSD_HEAD_DIM, SSD_STATE), F32)
    q = SSD_CHUNK
    return pl.pallas_call(
        functools.partial(_ssd_kernel, nc, zero_init),
        grid=(bsz,),
        in_specs=in_specs,
        out_specs=[pl.BlockSpec((1, seq, SSD_WIDTH), lambda b: (b, 0, 0)), st_spec, st_spec],
        out_shape=[jax.ShapeDtypeStruct((bsz, seq, SSD_WIDTH), BF16), st_shape, st_shape],
        scratch_shapes=[pltpu.VMEM((nc, q, SSD_WIDTH), F32), pltpu.VMEM((nc, q, bc), F32),
                        pltpu.VMEM((nc, q, bc), F32), pltpu.VMEM((nc, q, LANES), F32),
                        pltpu.VMEM((nc, q, LANES), F32), pltpu.VMEM((nc, LANES, q), F32),
                        pltpu.VMEM((nc, q, SSD_WIDTH), F32),
                        pltpu.VMEM((SSD_HEADS, SSD_HEAD_DIM, SSD_STATE), F32)],
        compiler_params=_cparams(("parallel",)),
        name="ssd_mixer",
    )(*args)


def _seg_sum(x):
    lane = lax.broadcasted_iota(jnp.int32, x.shape, 1)
    first = lane < RWKV_HEAD_DIM
    s0 = jnp.sum(jnp.where(first, x, 0.0), axis=-1, keepdims=True)
    s1 = jnp.sum(jnp.where(first, 0.0, x), axis=-1, keepdims=True)
    return jnp.where(first, s0, s1)


def _shift_mix(u, mu):
    return u + mu * (0.5 * (_shifted(u, -1) + _shifted(u, 1)) - u)


def _rwkv_prep_kernel(r_ref, k_ref, v_ref, lo_ref, mur, muk, muv, mul, wupf, wupb, aup, gup,
                      w0f, w0b, a0, kk_ref, ka_ref,
                      ro, ko, vo, wo, alo, beo, go):
    r = _shift_mix(r_ref[0], mur[...])
    k = _shift_mix(k_ref[0], muk[...])
    v = _shift_mix(v_ref[0], muv[...])
    lo = _shift_mix(lo_ref[0], mul[...])
    wdf = jnp.tanh(lo[:, 0:128]).astype(BF16)
    wdb = jnp.tanh(lo[:, 128:256]).astype(BF16)
    ad = lo[:, 256:384].astype(BF16)
    gd = jax.nn.sigmoid(lo[:, 384:640]).astype(BF16)

    def mm(x, w_ref):
        return jnp.dot(x, w_ref[...].astype(BF16), preferred_element_type=F32)

    wo[0, 0] = jnp.exp(-DECAY_SCALE * jax.nn.sigmoid(w0f[...] + mm(wdf, wupf)))
    wo[1, 0] = jnp.exp(-DECAY_SCALE * jax.nn.sigmoid(w0b[...] + mm(wdb, wupb)))
    a = jax.nn.sigmoid(a0[...] + mm(ad, aup))
    go[0] = mm(gd, gup)
    kk = k * kk_ref[...]
    kk = kk / jnp.maximum(jnp.sqrt(_seg_sum(kk * kk)), 1e-12)
    ro[0] = r
    ko[0] = k * (1.0 + (a - 1.0) * ka_ref[...])
    vo[0] = v
    alo[0] = -kk
    beo[0] = kk * a


def rwkv_prep(proj, p):
    bsz, seq, _ = proj.shape
    nblk = RWKV_WIDTH // LANES

    def col_spec(start):
        blk = start // LANES
        return pl.BlockSpec((1, seq, LANES), lambda b, j: (b, 0, blk + j))

    vec = pl.BlockSpec((1, LANES), lambda b, j: (0, j))

    def mat(rows):
        return pl.BlockSpec((rows, LANES), lambda b, j: (0, j))

    out_spec = pl.BlockSpec((1, seq, LANES), lambda b, j: (b, 0, j))
    out_sh = jax.ShapeDtypeStruct((bsz, seq, RWKV_WIDTH), F32)
    return pl.pallas_call(
        _rwkv_prep_kernel,
        grid=(bsz, nblk),
        in_specs=[col_spec(COL_R), col_spec(COL_K), col_spec(COL_V),
                  pl.BlockSpec((1, seq, LORA_W), lambda b, j: (b, 0, COL_LORA // LORA_W)),
                  vec, vec, vec, pl.BlockSpec((1, LORA_W), lambda b, j: (0, 0)),
                  mat(128), mat(128), mat(128), mat(256),
                  vec, vec, vec, vec, vec],
        out_specs=[out_spec] * 3 + [pl.BlockSpec((2, 1, seq, LANES), lambda b, j: (0, b, 0, j))]
        + [out_spec] * 3,
        out_shape=[out_sh] * 3 + [jax.ShapeDtypeStruct((2, bsz, seq, RWKV_WIDTH), F32)] + [out_sh] * 3,
        compiler_params=_cparams(("parallel", "arbitrary")),
        name="rwkv_prep",
    )(proj, proj, proj, proj, p["mu_r"], p["mu_k"], p["mu_v"], p["mu_lora"],
      p["w_up_f"], p["w_up_b"], p["a_up"], p["g_up"],
      p["w0_f"], p["w0_b"], p["a0"], p["k_k"], p["k_a"])


def _rwkv_scan_kernel(tb, zero_init, *refs):
    if zero_init:
        r_ref, w_ref, k_ref, al_ref, be_ref, v_ref, y_ref, st_ref, s_s = refs
    else:
        r_ref, w_ref, k_ref, al_ref, be_ref, v_ref, s0_ref, y_ref, st_ref, s_s = refs
    kd = RWKV_HEAD_DIM
    d = pl.program_id(0)
    step_i = pl.program_id(2)

    def time_index(i):
        return i + d * (tb - 1 - 2 * i)

    @pl.when(step_i == 0)
    def _():
        if zero_init:
            s_s[...] = jnp.zeros_like(s_s)
        else:
            s_s[...] = s0_ref[...]

    t0 = time_index(0)
    sa0 = jnp.zeros(s_s.shape[1:], F32)
    sa1 = jnp.zeros(s_s.shape[1:], F32)
    for kq in range(0, kd, 2):
        sa0 = sa0 + s_s[kq] * al_ref[t0, kq:kq + 1, :]
        sa1 = sa1 + s_s[kq + 1] * al_ref[t0, kq + 1:kq + 2, :]

    def step(i, sa):
        t = time_index(i)
        tn = time_index(jnp.minimum(i + 1, tb - 1))
        vt = v_ref[t]
        y = jnp.zeros_like(sa)
        sa_next = jnp.zeros_like(sa)
        for kq in range(kd):
            row = slice(kq, kq + 1)
            sk = s_s[kq] * w_ref[t, row, :] + sa * be_ref[t, row, :] + vt * k_ref[t, row, :]
            s_s[kq] = sk
            y = y + sk * r_ref[t, row, :]
            sa_next = sa_next + sk * al_ref[tn, row, :]
        y_ref[t] = y
        return sa_next

    lax.fori_loop(0, tb, step, sa0 + sa1)

    @pl.when(step_i == pl.num_programs(2) - 1)
    def _():
        st_ref[...] = s_s[...]


def rwkv_scan(r, w, k, al, be, v, s0):
    seq, kd, nch = r.shape
    vv = v.shape[1]
    tb = 32
    nblk = seq // tb
    zero_init = s0 is None

    def tblk(d, i):
        return i + d * (nblk - 1 - 2 * i)

    kspec = pl.BlockSpec((tb, kd, LANES), lambda d, g, i: (tblk(d, i), 0, g))
    wspec = pl.BlockSpec((None, tb, kd, LANES), lambda d, g, i: (d, tblk(d, i), 0, g))
    vspec = pl.BlockSpec((tb, vv, LANES), lambda d, g, i: (tblk(d, i), 0, g))
    yspec = pl.BlockSpec((None, tb, vv, LANES), lambda d, g, i: (d, tblk(d, i), 0, g))
    sspec = pl.BlockSpec((None, kd, vv, LANES), lambda d, g, i: (d, 0, 0, g))
    in_specs = [kspec, wspec, kspec, kspec, kspec, vspec]
    args = [r, w, k, al, be, v]
    if not zero_init:
        in_specs.append(sspec)
        args.append(s0)
    return pl.pallas_call(
        functools.partial(_rwkv_scan_kernel, tb, zero_init),
        grid=(2, nch // LANES, nblk),
        in_specs=in_specs,
        out_specs=[yspec, sspec],
        out_shape=[jax.ShapeDtypeStruct((2, seq, vv, nch), F32),
                   jax.ShapeDtypeStruct((2, kd, vv, nch), F32)],
        scratch_shapes=[pltpu.VMEM((kd, vv, LANES), F32)],
        compiler_params=_cparams(("parallel", "parallel", "arbitrary")),
        name="rwkv_scan",
    )(*args)


def _rwkv_post_kernel(y_ref, r_ref, k_ref, v_ref, g_ref, lng, lnb, rk, o_ref):
    inv = 1.0 / RWKV_HEAD_DIM
    for cb in range(RWKV_WIDTH // LANES):
        cols = slice(cb * LANES, (cb + 1) * LANES)
        y = y_ref[0, 0, :, cols] + y_ref[1, 0, :, cols]
        mean = _seg_sum(y) * inv
        yc = y - mean
        var = _seg_sum(yc * yc) * inv
        yn = yc * lax.rsqrt(var + GN_EPS) * lng[:, cols] + lnb[:, cols]
        bonus = _seg_sum(r_ref[0, :, cols] * k_ref[0, :, cols] * rk[:, cols]) * v_ref[0, :, cols]
        o_ref[0, :, cols] = ((yn + bonus) * g_ref[0, :, cols]).astype(BF16)


def rwkv_post(y2, r, k, v, g, p):
    _, bsz, seq, wd = y2.shape
    tl = 256
    spec = pl.BlockSpec((1, tl, wd), lambda b, i: (b, i, 0))
    vec = pl.BlockSpec((1, wd), lambda b, i: (0, 0))
    return pl.pallas_call(
        _rwkv_post_kernel,
        grid=(bsz, seq // tl),
        in_specs=[pl.BlockSpec((2, 1, tl, wd), lambda b, i: (0, b, i, 0))] + [spec] * 4 + [vec] * 3,
        out_specs=spec,
        out_shape=jax.ShapeDtypeStruct((bsz, seq, wd), BF16),
        compiler_params=_cparams(("parallel", "parallel")),
        name="rwkv_post",
    )(y2, r, k, v, g, p["ln_g"], p["ln_b"], p["r_k"])


def _out_proj_kernel(ys_ref, yr_ref, wo_ref, x_ref, gm_ref, ng_ref, sc_ref, sh_ref, rw_ref,
                     x1_ref, h2_ref, lg_ref):
    acc = jnp.dot(ys_ref[0], wo_ref[0:SSD_WIDTH, :], preferred_element_type=F32)
    acc = acc + jnp.dot(yr_ref[0], wo_ref[SSD_WIDTH:, :], preferred_element_type=F32)
    x1 = x_ref[0] + gm_ref[0] * acc
    x1_ref[0] = x1
    hn = x1 * lax.rsqrt(jnp.mean(x1 * x1, axis=-1, keepdims=True) + NORM_EPS) * ng_ref[...]
    h2 = hn * (1.0 + sc_ref[0]) + sh_ref[0]
    h2_ref[0] = h2.astype(BF16)
    lg_ref[0] = jnp.dot(h2, rw_ref[...], precision=HIGHEST, preferred_element_type=F32)


def out_proj(y_ssd, y_rwkv, w_out, x, gate_m, norm_g, scale_f, shift_f, router_w):
    bsz, seq, d = x.shape
    tm = 256
    per_b = gate_m.shape[0] > 1
    midx = (lambda b, i: (b, 0, 0)) if per_b else (lambda b, i: (0, 0, 0))
    half = pl.BlockSpec((1, tm, d // 2), lambda b, i: (b, i, 0))
    full = pl.BlockSpec((1, tm, d), lambda b, i: (b, i, 0))
    mspec = pl.BlockSpec((1, 1, d), midx)
    return pl.pallas_call(
        _out_proj_kernel,
        grid=(bsz, seq // tm),
        in_specs=[half, half, pl.BlockSpec((d, d), lambda b, i: (0, 0)), full, mspec,
                  pl.BlockSpec((1, d), lambda b, i: (0, 0)), mspec, mspec,
                  pl.BlockSpec((d, LANES), lambda b, i: (0, 0))],
        out_specs=[full, full, pl.BlockSpec((1, tm, LANES), lambda b, i: (b, i, 0))],
        out_shape=[jax.ShapeDtypeStruct((bsz, seq, d), F32), jax.ShapeDtypeStruct((bsz, seq, d), BF16),
                   jax.ShapeDtypeStruct((bsz, seq, LANES), F32)],
        compiler_params=_cparams(("parallel", "parallel")),
        name="out_proj",
    )(y_ssd, y_rwkv, w_out, x, gate_m, norm_g.reshape(1, d), scale_f, shift_f, router_w)


def _route_kernel(nt, cap, lg_ref, selr_ref, posr_ref, selt_ref, post_ref, afft_ref, aff_s, pre_s):
    ne = N_EXPERTS
    lane = lax.broadcasted_iota(jnp.int32, (LANES, LANES), 1)
    sub = lax.broadcasted_iota(jnp.int32, (LANES, LANES), 0)
    upper_incl = (sub <= lane).astype(BF16)
    for i in range(nt):
        lg = jnp.where(lane < ne, lg_ref[i * LANES:(i + 1) * LANES, :], -jnp.inf)
        e = jnp.exp(lg - jnp.max(lg, axis=-1, keepdims=True))
        aff = e / jnp.sum(e, axis=-1, keepdims=True)
        afft_ref[i * LANES:(i + 1) * LANES, :] = aff
        aff_s[i] = aff.T[0:ne, :]
    aff3 = aff_s[...]

    def count(mask):
        s = jnp.sum(mask.astype(F32), axis=0, keepdims=True)
        return jnp.sum(s, axis=2, keepdims=True)

    def search(_, carry):
        lo, hi = carry
        mid = 0.5 * (lo + hi)
        ok = count(aff3 >= mid) >= cap
        return jnp.where(ok, mid, lo), jnp.where(ok, hi, mid)

    lo0 = jnp.zeros((1, ne, 1), F32)
    hi0 = jnp.full((1, ne, 1), 2.0, F32)
    lo, _ = lax.fori_loop(0, SEARCH_ITERS, search, (lo0, hi0))
    cand = jnp.where(aff3 >= lo, aff3, 4.0)
    thr = jnp.min(jnp.min(cand, axis=0, keepdims=True), axis=2, keepdims=True)
    gt = (aff3 > thr).astype(F32)
    eq = (aff3 == thr).astype(F32)
    need = cap - count(aff3 > thr)[0]

    def prefix_excl(m3):
        off = jnp.zeros((ne, 1), F32)
        for i in range(nt):
            inc = jnp.dot(m3[i].astype(BF16), upper_incl, preferred_element_type=F32)
            pre_s[i] = inc - m3[i] + off
            off = off + inc[:, LANES - 1:LANES]
        return pre_s[...]

    sel = jnp.maximum(gt, eq * (prefix_excl(eq) < need[None]).astype(F32))
    pos = prefix_excl(sel)
    selr_ref[...] = sel
    posr_ref[...] = pos
    zpad = jnp.zeros((LANES - ne, LANES), F32)
    for i in range(nt):
        rows = slice(i * LANES, (i + 1) * LANES)
        selt_ref[rows, :] = jnp.concatenate([sel[i], zpad], axis=0).T
        post_ref[rows, :] = jnp.concatenate([pos[i], zpad], axis=0).T


def route(logits, cap):
    n_tok = logits.shape[0]
    nt = n_tok // LANES
    row_sh = jax.ShapeDtypeStruct((nt, N_EXPERTS, LANES), F32)
    tm_sh = jax.ShapeDtypeStruct((n_tok, LANES), F32)
    return pl.pallas_call(
        functools.partial(_route_kernel, nt, cap),
        out_shape=[row_sh, row_sh, tm_sh, tm_sh, tm_sh],
        scratch_shapes=[pltpu.VMEM((nt, N_EXPERTS, LANES), F32), pltpu.VMEM((nt, N_EXPERTS, LANES), F32)],
        compiler_params=pltpu.CompilerParams(vmem_limit_bytes=VMEM_LIMIT),
        name="route",
    )(logits)


def _window_start(starts_ref, tile, e, cap, win):
    ps = starts_ref[tile * N_EXPERTS + e]
    return pl.multiple_of(jnp.minimum((ps // SUBLANES) * SUBLANES, cap - win), SUBLANES)


def _gather_kernel(cap, win, nsub, starts_ref, sel_ref, pos_ref, h_ref, o_ref, acc):
    e = pl.program_id(0)
    j = pl.program_id(1)

    @pl.when(j == 0)
    def _():
        acc[...] = jnp.zeros_like(acc)

    for s in range(nsub):
        s0 = _window_start(starts_ref, j * nsub + s, e, cap, win)
        slot = (lax.broadcasted_iota(jnp.int32, (win, LANES), 0) + s0).astype(F32)
        prow = pos_ref[s, pl.ds(e, 1), :]
        srow = sel_ref[s, pl.ds(e, 1), :]
        onehot = jnp.where(jnp.logical_and(slot == prow, srow > 0.0), 1.0, 0.0).astype(BF16)
        acc[pl.ds(s0, win), :] += jnp.dot(onehot, h_ref[s * LANES:(s + 1) * LANES, :],
                                         preferred_element_type=F32)

    @pl.when(j == pl.num_programs(1) - 1)
    def _():
        o_ref[0] = acc[...].astype(BF16)


def gather_tokens(starts, sel_r, pos_r, h, cap):
    n_tok, d = h.shape
    tt = min(1024, n_tok)
    nsub = tt // LANES
    win = min(LANES + SUBLANES, cap)
    rspec = pl.BlockSpec((nsub, N_EXPERTS, LANES), lambda e, j, st: (j, 0, 0))
    return pl.pallas_call(
        functools.partial(_gather_kernel, cap, win, nsub),
        grid_spec=pltpu.PrefetchScalarGridSpec(
            num_scalar_prefetch=1,
            grid=(N_EXPERTS, n_tok // tt),
            in_specs=[rspec, rspec, pl.BlockSpec((tt, d), lambda e, j, st: (j, 0))],
            out_specs=pl.BlockSpec((1, cap, d), lambda e, j, st: (e, 0, 0)),
            scratch_shapes=[pltpu.VMEM((cap, d), F32)]),
        out_shape=jax.ShapeDtypeStruct((N_EXPERTS, cap, d), BF16),
        compiler_params=_cparams(("parallel", "arbitrary")),
        name="gather_tokens",
    )(starts, sel_r, pos_r, h)


def _ffn_kernel(x_ref, wg_ref, wu_ref, wd_ref, o_ref, acc):
    f = pl.program_id(1)

    @pl.when(f == 0)
    def _():
        acc[...] = jnp.zeros_like(acc)

    x = x_ref[0]
    hg = jnp.dot(x, wg_ref[0].astype(BF16), preferred_element_type=F32)
    hu = jnp.dot(x, wu_ref[0].astype(BF16), preferred_element_type=F32)
    hid = (_silu(hg) * hu).astype(BF16)
    acc[...] += jnp.dot(hid, wd_ref[0].astype(BF16), preferred_element_type=F32)

    @pl.when(f == pl.num_programs(1) - 1)
    def _():
        o_ref[0] = acc[...].astype(BF16)


def expert_ffn(xs, wg, wu, wd):
    ne, cap, d = xs.shape
    ff = wg.shape[2]
    tf = 256
    return pl.pallas_call(
        _ffn_kernel,
        grid=(ne, ff // tf),
        in_specs=[pl.BlockSpec((1, cap, d), lambda e, f: (e, 0, 0)),
                  pl.BlockSpec((1, d, tf), lambda e, f: (e, 0, f)),
                  pl.BlockSpec((1, d, tf), lambda e, f: (e, 0, f)),
                  pl.BlockSpec((1, tf, d), lambda e, f: (e, f, 0))],
        out_specs=pl.BlockSpec((1, cap, d), lambda e, f: (e, 0, 0)),
        out_shape=jax.ShapeDtypeStruct((ne, cap, d), BF16),
        scratch_shapes=[pltpu.VMEM((cap, d), F32)],
        compiler_params=_cparams(("parallel", "arbitrary")),
        name="expert_ffn",
    )(xs, wg, wu, wd)


def _scatter_kernel(cap, win, nsub, starts_ref, sel_ref, pos_ref, aff_ref, ob_ref, y_ref):
    c = pl.program_id(0)
    e = pl.program_id(1)

    @pl.when(e == 0)
    def _():
        y_ref[...] = jnp.zeros_like(y_ref)

    mine = lax.broadcasted_iota(jnp.int32, (LANES, LANES), 1) == e
    for s in range(nsub):
        rows = slice(s * LANES, (s + 1) * LANES)

        def col(ref):
            return jnp.sum(jnp.where(mine, ref[rows, :], 0.0), axis=-1, keepdims=True)

        pcol, scol, gcol = col(pos_ref), col(sel_ref), col(aff_ref)
        s0 = _window_start(starts_ref, c * nsub + s, e, cap, win)
        slot = (lax.broadcasted_iota(jnp.int32, (LANES, win), 1) + s0).astype(F32)
        onehot = jnp.where(jnp.logical_and(slot == pcol, scol > 0.0), 1.0, 0.0).astype(BF16)
        y_ref[rows, :] += gcol * jnp.dot(onehot, ob_ref[0, pl.ds(s0, win), :],
                                         preferred_element_type=F32)


def scatter_combine(starts, sel_t, pos_t, aff_t, outbuf):
    n_tok = sel_t.shape[0]
    ne, cap, d = outbuf.shape
    chunk = min(2048, n_tok)
    nsub = chunk // LANES
    win = min(2 * LANES, cap)
    tspec = pl.BlockSpec((chunk, LANES), lambda c, e, st: (c, 0))
    return pl.pallas_call(
        functools.partial(_scatter_kernel, cap, win, nsub),
        grid_spec=pltpu.PrefetchScalarGridSpec(
            num_scalar_prefetch=1,
            grid=(n_tok // chunk, ne),
            in_specs=[tspec, tspec, tspec, pl.BlockSpec((1, cap, d), lambda c, e, st: (e, 0, 0))],
            out_specs=pl.BlockSpec((chunk, d), lambda c, e, st: (c, 0))),
        out_shape=jax.ShapeDtypeStruct((n_tok, d), F32),
        compiler_params=_cparams(("parallel", "arbitrary")),
        name="scatter_combine",
    )(starts, sel_t, pos_t, aff_t, outbuf)


def _final_kernel(x1_ref, y_ref, gf_ref, fg_ref, o_ref):
    x2 = x1_ref[0] + gf_ref[0] * y_ref[0]
    o_ref[0] = x2 * lax.rsqrt(jnp.mean(x2 * x2, axis=-1, keepdims=True) + NORM_EPS) * fg_ref[...]


def final_norm(x1, y, gate_f, final_g):
    bsz, seq, d = x1.shape
    tl = 256
    per_b = gate_f.shape[0] > 1
    midx = (lambda b, i: (b, 0, 0)) if per_b else (lambda b, i: (0, 0, 0))
    xspec = pl.BlockSpec((1, tl, d), lambda b, i: (b, i, 0))
    return pl.pallas_call(
        _final_kernel,
        grid=(bsz, seq // tl),
        in_specs=[xspec, xspec, pl.BlockSpec((1, 1, d), midx), pl.BlockSpec((1, d), lambda b, i: (0, 0))],
        out_specs=xspec,
        out_shape=jax.ShapeDtypeStruct((bsz, seq, d), F32),
        compiler_params=_cparams(("parallel", "parallel")),
        name="final_norm",
    )(x1, y, gate_f, final_g.reshape(1, d))


def _pad_cols(w, width):
    return jnp.pad(w, ((0, 0), (0, width - w.shape[1])))


def _pad_rows(w, rows):
    return jnp.pad(w, ((0, rows - w.shape[0]), (0, 0)))


def _relayout_columns(w):
    n_ssd = 2 * SSD_WIDTH + 2 * SSD_GROUPS * SSD_STATE + 2 * SSD_HEADS
    ssd, rw = w[:, :n_ssd], w[:, n_ssd:]
    o = 3 * RWKV_WIDTH
    parts = [ssd[:, :COL_DT], _pad_cols(ssd[:, COL_DT:], LANES), rw[:, :o],
             _pad_cols(rw[:, o:o + DECAY_LORA], LANES),
             _pad_cols(rw[:, o + DECAY_LORA:o + 2 * DECAY_LORA], LANES),
             _pad_cols(rw[:, o + 2 * DECAY_LORA:o + 2 * DECAY_LORA + ICLR_LORA], LANES),
             _pad_cols(rw[:, o + 2 * DECAY_LORA + ICLR_LORA:], 2 * LANES)]
    return jnp.concatenate(parts, axis=1)


def _layer_params(l, w_in, w_out, ssd_conv_w, ssd_conv_b, ssd_A_log, ssd_dt_bias, ssd_D, ssd_norm_g,
                  rwkv_mu, rwkv_w0, rwkv_w_up, rwkv_a0, rwkv_a_up, rwkv_g_up, rwkv_k_k, rwkv_k_a,
                  rwkv_r_k, rwkv_ln_g, rwkv_ln_b, router_w):
    row = lambda a: a.reshape(1, -1)
    n_ssd = 2 * SSD_WIDTH + 2 * SSD_GROUPS * SSD_STATE + 2 * SSD_HEADS
    mu_full = _relayout_columns(jnp.concatenate([jnp.zeros((1, n_ssd), F32), row(rwkv_mu[l])], axis=1))
    cw, cb = ssd_conv_w[l], row(ssd_conv_b[l])
    xe, be = SSD_WIDTH, SSD_WIDTH + SSD_GROUPS * SSD_STATE
    return dict(
        w_in=_relayout_columns(w_in[l]).astype(BF16),
        w_out=w_out[l].astype(BF16),
        cw_x=cw[:, :xe], cb_x=cb[:, :xe], cw_b=cw[:, xe:be], cb_b=cb[:, xe:be],
        cw_c=cw[:, be:], cb_c=cb[:, be:],
        dt_bias=_pad_cols(row(ssd_dt_bias[l]), LANES), a_log=_pad_cols(row(ssd_A_log[l]), LANES),
        d_exp=row(jnp.repeat(ssd_D[l], SSD_HEAD_DIM)), ssd_ng=row(ssd_norm_g[l]),
        mu_r=mu_full[:, COL_R:COL_K], mu_k=mu_full[:, COL_K:COL_V], mu_v=mu_full[:, COL_V:COL_LORA],
        mu_lora=mu_full[:, COL_LORA:],
        w_up_f=_pad_rows(rwkv_w_up[l, 0], LANES), w_up_b=_pad_rows(rwkv_w_up[l, 1], LANES),
        a_up=_pad_rows(rwkv_a_up[l], LANES), g_up=_pad_rows(rwkv_g_up[l], 2 * LANES),
        w0_f=row(rwkv_w0[l, 0]), w0_b=row(rwkv_w0[l, 1]), a0=row(rwkv_a0[l]),
        k_k=row(rwkv_k_k[l]), k_a=row(rwkv_k_a[l]), r_k=row(rwkv_r_k[l]),
        ln_g=row(rwkv_ln_g[l]), ln_b=row(rwkv_ln_b[l]),
        router_w=_pad_cols(router_w[l], LANES),
    )


def _to_chain(x):
    lead = x.shape[:-3]
    bsz, seq, _ = x.shape[-3:]
    n = len(lead)
    x = x.reshape(lead + (bsz, seq, RWKV_HEADS, RWKV_HEAD_DIM))
    perm = tuple(range(n)) + (n + 1, n + 3, n, n + 2)
    return jnp.transpose(x, perm).reshape(lead + (seq, RWKV_HEAD_DIM, bsz * RWKV_HEADS))


def _from_chain(y, bsz):
    lead = y.shape[:-3]
    seq = y.shape[-3]
    n = len(lead)
    y = y.reshape(lead + (seq, RWKV_HEAD_DIM, bsz, RWKV_HEADS))
    perm = tuple(range(n)) + (n + 2, n, n + 3, n + 1)
    return jnp.transpose(y, perm).reshape(lead + (bsz, seq, RWKV_WIDTH))


def _state_to_chain(s):
    return jnp.transpose(s, (0, 4, 3, 1, 2)).reshape(2, RWKV_HEAD_DIM, RWKV_HEAD_DIM, -1)


def _state_from_chain(s, bsz):
    return jnp.transpose(s.reshape(2, RWKV_HEAD_DIM, RWKV_HEAD_DIM, bsz, RWKV_HEADS), (0, 3, 4, 2, 1))


def rwkv_mixer_scan(r, k, v, w2, al, be, s0):
    bsz = r.shape[0]
    nch = bsz * RWKV_HEADS
    rep = max(1, LANES // nch)
    assert (nch * rep) % LANES == 0
    rc, kc, vc, alc, bec, wc = (_to_chain(t) for t in (r, k, v, al, be, w2))
    s0c = None if s0 is None else _state_to_chain(s0)
    if rep > 1:
        vq = RWKV_HEAD_DIM // rep
        tile = lambda t: jnp.concatenate([t] * rep, axis=-1)
        split = lambda t, ax: jnp.concatenate(
            [lax.slice_in_dim(t, q * vq, (q + 1) * vq, axis=ax) for q in range(rep)], axis=-1)
        rc, kc, alc, bec, wc = (tile(t) for t in (rc, kc, alc, bec, wc))
        vc = split(vc, 1)
        s0c = None if s0c is None else split(s0c, 2)
    y, st = rwkv_scan(rc, wc, kc, alc, bec, vc, s0c)
    if rep > 1:
        merge = lambda t: jnp.concatenate([t[..., q * nch:(q + 1) * nch] for q in range(rep)], axis=2)
        y, st = merge(y), merge(st)
    return _from_chain(y, bsz), _state_from_chain(st, bsz)


def _grid_pos_embed(n_tokens, dim):
    rows = n_tokens // GRID_W
    row = jnp.repeat(jnp.arange(rows, dtype=F32), GRID_W)
    col = jnp.tile(jnp.arange(GRID_W, dtype=F32), rows)
    quarter = dim // 4
    freqs = jnp.exp(jnp.arange(quarter, dtype=F32) * (-math.log(POS_BASE) / quarter))

    def axis_embed(pos):
        ang = pos[:, None] * freqs[None, :]
        return jnp.concatenate([jnp.sin(ang), jnp.cos(ang)], axis=-1)

    return jnp.concatenate([axis_embed(row), axis_embed(col)], axis=-1)


def trunk_layer(x, pos, mod, p, norm_mix_g, norm_ffn_g, exp_w, final_g, states):
    bsz, seq, d = x.shape
    shift_m, scale_m, gate_m, shift_f, scale_f, gate_f = mod
    h, x0 = norm_modulate(x, pos, norm_mix_g, scale_m, shift_m)
    n_tok = bsz * seq
    proj = matmul_bf16(h.reshape(n_tok, d), p["w_in"], min(1024, n_tok), 640).reshape(bsz, seq, N_PROJ)
    h0f, h0b, s0f, s0b = states
    y_ssd, hf, hb = ssd_mixer(proj, p, h0f, h0b)
    r, k, v, w2, al, be, g = rwkv_prep(proj, p)
    s0 = None if s0f is None else jnp.stack([s0f, s0b], axis=0)
    y2, st = rwkv_mixer_scan(r, k, v, w2, al, be, s0)
    y_rwkv = rwkv_post(y2, r, k, v, g, p)
    x1, h2, logits = out_proj(y_ssd, y_rwkv, p["w_out"], x0, gate_m, norm_ffn_g, scale_f, shift_f,
                              p["router_w"])
    cap = EC_CAPACITY * n_tok // N_EXPERTS
    sel_r, pos_r, sel_t, pos_t, aff_t = route(logits.reshape(n_tok, LANES), cap)
    starts = pos_r[:, :, 0].astype(jnp.int32).reshape(-1)
    xs = gather_tokens(starts, sel_r, pos_r, h2.reshape(n_tok, d), cap)
    outbuf = expert_ffn(xs, *exp_w)
    y_ffn = scatter_combine(starts, sel_t, pos_t, aff_t, outbuf).reshape(bsz, seq, d)
    y = final_norm(x1, y_ffn, gate_f, final_g)
    return y, (hf, hb, st[0], st[1])


def kernel(x_prompt, x_sample, state_ssd_fwd, state_ssd_bwd, state_rwkv_fwd, state_rwkv_bwd, c, c_ctx,
           w_ada, b_ada, norm_mix_g, norm_ffn_g, w_in, w_out, ssd_conv_w, ssd_conv_b, ssd_A_log,
           ssd_dt_bias, ssd_D, ssd_norm_g, rwkv_mu, rwkv_w0, rwkv_w_up, rwkv_a0, rwkv_a_up, rwkv_g_up,
           rwkv_k_k, rwkv_k_a, rwkv_r_k, rwkv_ln_g, rwkv_ln_b, router_w, exp_w_gate, exp_w_up,
           exp_w_down, final_norm_g):
    depth = w_in.shape[0]
    assert depth == 1, "final norm is fused into the last layer's scatter kernel"
    l = 0
    dec_b = x_sample.shape[0]
    p = _layer_params(l, w_in, w_out, ssd_conv_w, ssd_conv_b, ssd_A_log, ssd_dt_bias, ssd_D, ssd_norm_g,
                      rwkv_mu, rwkv_w0, rwkv_w_up, rwkv_a0, rwkv_a_up, rwkv_g_up, rwkv_k_k, rwkv_k_a,
                      rwkv_r_k, rwkv_ln_g, rwkv_ln_b, router_w)
    cond = jnp.concatenate([c_ctx[None, :], c, jnp.zeros((16 - 1 - dec_b, D_MODEL), F32)], axis=0)
    mod = ada_modulation(cond, w_ada[l], b_ada[l]).reshape(16, N_MOD, 1, D_MODEL)
    mod_ctx = [mod[0:1, i] for i in range(N_MOD)]
    mod_lat = [mod[1:1 + dec_b, i] for i in range(N_MOD)]
    exp_w = (exp_w_gate[l], exp_w_up[l], exp_w_down[l])

    y_prompt, st = trunk_layer(x_prompt, None, mod_ctx, p, norm_mix_g[l], norm_ffn_g[l], exp_w,
                               final_norm_g, (None, None, None, None))
    pos = _grid_pos_embed(x_sample.shape[1], D_MODEL)
    cached = (state_ssd_fwd[:, l], state_ssd_bwd[:, l], state_rwkv_fwd[:, l], state_rwkv_bwd[:, l])
    y_sample, _ = trunk_layer(x_sample, pos, mod_lat, p, norm_mix_g[l], norm_ffn_g[l], exp_w,
                              final_norm_g, cached)
    return (y_prompt, y_sample, st[0][:, None], st[1][:, None], st[2][:, None], st[3][:, None])
```

```python
import functools
import math

import jax
import jax.numpy as jnp
from jax import lax
from jax.experimental import pallas as pl
from jax.experimental.pallas import tpu as pltpu

F32 = jnp.float32
BF16 = jnp.bfloat16
HIGHEST = lax.Precision.HIGHEST

D_MODEL = 2048
GRID_W = 64
SSD_WIDTH = 1024
SSD_HEAD_DIM = 64
SSD_HEADS = 16
SSD_GROUPS = 2
SSD_STATE = 128
SSD_CONV_W = 5
SSD_CHUNK = 128
RWKV_WIDTH = 1024
RWKV_HEAD_DIM = 64
RWKV_HEADS = 16
DECAY_LORA = 64
ICLR_LORA = 64
GATE_LORA = 160
N_EXPERTS = 16
EC_CAPACITY = 2
EXPERT_FF = 1024
N_MOD = 6
NORM_EPS = 1e-6
GN_EPS = 64e-5
DECAY_SCALE = 0.606531
POS_BASE = 10000.0
SEARCH_ITERS = 48

LANES = 128
SUBLANES = 8

COL_Z = 0
COL_X = 1024
COL_B = 2048
COL_C = 2304
COL_DT = 2560
COL_R = 2688
COL_K = 3712
COL_V = 4736
COL_LORA = 5760
LORA_W = 640
N_PROJ = COL_LORA + LORA_W

VMEM_LIMIT = 56 * 1024 * 1024


def _cparams(sem):
    return pltpu.CompilerParams(dimension_semantics=sem, vmem_limit_bytes=VMEM_LIMIT)


def _silu(x):
    return x * jax.nn.sigmoid(x)


def _ada_kernel(c_ref, w_ref, b_ref, o_ref):
    s = _silu(c_ref[...]).astype(BF16)
    o_ref[...] = jnp.dot(s, w_ref[...].astype(BF16), preferred_element_type=F32) + b_ref[...]


def ada_modulation(cond, w, b):
    m, d = cond.shape
    n = w.shape[1]
    tn = 1024
    return pl.pallas_call(
        _ada_kernel,
        grid=(n // tn,),
        in_specs=[pl.BlockSpec((m, d), lambda j: (0, 0)),
                  pl.BlockSpec((d, tn), lambda j: (0, j)),
                  pl.BlockSpec((1, tn), lambda j: (0, j))],
        out_specs=pl.BlockSpec((m, tn), lambda j: (0, j)),
        out_shape=jax.ShapeDtypeStruct((m, n), F32),
        compiler_params=_cparams(("parallel",)),
        name="ada_modulation",
    )(cond, w, b.reshape(1, n))


def _norm_mod_kernel(has_pos, *refs):
    if has_pos:
        x_ref, pos_ref, g_ref, sc_ref, sh_ref, h_ref, x0_ref = refs
    else:
        x_ref, g_ref, sc_ref, sh_ref, h_ref = refs
    x = x_ref[0]
    if has_pos:
        x = x + pos_ref[...]
        x0_ref[0] = x
    y = x * lax.rsqrt(jnp.mean(x * x, axis=-1, keepdims=True) + NORM_EPS) * g_ref[...]
    h_ref[0] = (y * (1.0 + sc_ref[0]) + sh_ref[0]).astype(BF16)


def norm_modulate(x, pos, g, scale, shift):
    bsz, seq, d = x.shape
    tl = 256
    per_b = scale.shape[0] > 1
    midx = (lambda b, i: (b, 0, 0)) if per_b else (lambda b, i: (0, 0, 0))
    xspec = pl.BlockSpec((1, tl, d), lambda b, i: (b, i, 0))
    in_specs = [xspec]
    args = [x]
    if pos is not None:
        in_specs.append(pl.BlockSpec((tl, d), lambda b, i: (i, 0)))
        args.append(pos)
    in_specs += [pl.BlockSpec((1, d), lambda b, i: (0, 0)),
                 pl.BlockSpec((1, 1, d), midx), pl.BlockSpec((1, 1, d), midx)]
    args += [g.reshape(1, d), scale, shift]
    out_shape = [jax.ShapeDtypeStruct((bsz, seq, d), BF16)]
    out_specs = [xspec]
    if pos is not None:
        out_shape.append(jax.ShapeDtypeStruct((bsz, seq, d), F32))
        out_specs.append(xspec)
    res = pl.pallas_call(
        functools.partial(_norm_mod_kernel, pos is not None),
        grid=(bsz, seq // tl),
        in_specs=in_specs, out_specs=out_specs, out_shape=out_shape,
        compiler_params=_cparams(("parallel", "parallel")),
        name="norm_modulate",
    )(*args)
    return (res[0], res[1]) if pos is not None else (res[0], x)


def _mm_kernel(a_ref, b_ref, o_ref):
    o_ref[...] = jnp.dot(a_ref[...], b_ref[...], preferred_element_type=F32)


def matmul_bf16(a, b, tm, tn):
    m, k = a.shape
    n = b.shape[1]
    return pl.pallas_call(
        _mm_kernel,
        grid=(m // tm, n // tn),
        in_specs=[pl.BlockSpec((tm, k), lambda i, j: (i, 0)),
                  pl.BlockSpec((k, tn), lambda i, j: (0, j))],
        out_specs=pl.BlockSpec((tm, tn), lambda i, j: (i, j)),
        out_shape=jax.ShapeDtypeStruct((m, n), F32),
        compiler_params=_cparams(("parallel", "arbitrary")),
        name="in_proj",
    )(a, b)


def _shifted(u, off):
    n = u.shape[0]
    row = lax.broadcasted_iota(jnp.int32, (n, 1), 0)
    rolled = pltpu.roll(u, (-off) % n, axis=0)
    valid = jnp.logical_and(row + off >= 0, row + off < n)
    return jnp.where(valid, rolled, 0.0)


def _conv_silu(u, w_ref, b_ref, cols):
    pad = SSD_CONV_W // 2
    acc = u * w_ref[pad:pad + 1, cols] + b_ref[:, cols]
    for j in range(SSD_CONV_W):
        if j != pad:
            acc = acc + _shifted(u, j - pad) * w_ref[j:j + 1, cols]
    return _silu(acc)


def _softplus(x):
    return jnp.maximum(x, 0.0) + jnp.log1p(jnp.exp(-jnp.abs(x)))


def _ssd_kernel(nc, zero_init, *refs):
    q = SSD_CHUNK
    (z_ref, x_ref, b_ref, c_ref, dt_ref, cwx, cbx, cwb, cbb, cwc, cbc,
     dtb_ref, alog_ref, d_ref, ng_ref) = refs[:15]
    refs = refs[15:]
    if not zero_init:
        h0f_ref, h0b_ref = refs[:2]
        refs = refs[2:]
    y_ref, hf_ref, hb_ref, xh_s, bm_s, cm_s, dt_s, a_s, at_s, y_s, h_s = refs

    slab = 256
    for cb in range(SSD_WIDTH // slab):
        cols = slice(cb * slab, (cb + 1) * slab)
        xh = _conv_silu(x_ref[0, :, cols], cwx, cbx, cols)
        for c in range(nc):
            rows = slice(c * q, (c + 1) * q)
            xh_s[c, :, cols] = xh[rows]
            y_s[c, :, cols] = xh[rows] * d_ref[:, cols]
    allc = slice(0, SSD_GROUPS * SSD_STATE)
    bm = _conv_silu(b_ref[0], cwb, cbb, allc)
    cm = _conv_silu(c_ref[0], cwc, cbc, allc)
    dt = _softplus(dt_ref[0] + dtb_ref[...])
    a = dt * (-jnp.exp(alog_ref[...]))
    ii = lax.broadcasted_iota(jnp.int32, (q, q), 0)
    jj = lax.broadcasted_iota(jnp.int32, (q, q), 1)
    eye = (ii == jj).astype(F32)
    lower = ii >= jj
    upper = ii <= jj
    for c in range(nc):
        rows = slice(c * q, (c + 1) * q)
        bm_s[c] = bm[rows]
        cm_s[c] = cm[rows]
        dt_s[c] = dt[rows]
        a_s[c] = a[rows]
        at_s[c] = lax.dot_general(eye, a[rows], (((1,), (1,)), ((), ())),
                                  precision=HIGHEST, preferred_element_type=F32)

    for d in range(2):
        mask = lower if d == 0 else upper
        tri = mask.astype(F32)
        tri_t = (upper if d == 0 else lower).astype(F32)
        if zero_init:
            h_s[...] = jnp.zeros_like(h_s)
        else:
            h_s[...] = (h0f_ref if d == 0 else h0b_ref)[0]

        def chunk_body(ci, carry, d=d, mask=mask, tri=tri, tri_t=tri_t):
            c = ci if d == 0 else nc - 1 - ci
            a_c = a_s[c]
            cum = jnp.dot(tri, a_c, precision=HIGHEST, preferred_element_type=F32)
            cum_t = jnp.dot(at_s[c], tri_t, precision=HIGHEST, preferred_element_type=F32)
            dt_c = dt_s[c]
            for g in range(SSD_GROUPS):
                gcols = slice(g * SSD_STATE, (g + 1) * SSD_STATE)
                bg = bm_s[c, :, gcols]
                cg = cm_s[c, :, gcols]
                gram = lax.dot_general(cg.astype(BF16), bg.astype(BF16), (((1,), (1,)), ((), ())),
                                       preferred_element_type=F32)
                for hh in range(SSD_HEADS // SSD_GROUPS):
                    h = g * (SSD_HEADS // SSD_GROUPS) + hh
                    col = d * SSD_HEADS + h
                    hcols = slice(h * SSD_HEAD_DIM, (h + 1) * SSD_HEAD_DIM)
                    cc = cum[:, col:col + 1]
                    cr = cum_t[col:col + 1, :]
                    dec = jnp.exp(jnp.where(mask, cc - cr, -jnp.inf))
                    m = (gram * dec).astype(BF16)
                    xdt = (xh_s[c, :, hcols] * dt_c[:, col:col + 1]).astype(BF16)
                    tot = cum[q - 1:q, col:col + 1] if d == 0 else cum[0:1, col:col + 1]
                    bd = (bg * jnp.exp(tot - cc)).astype(BF16)
                    cd = (cg * jnp.exp(cc)).astype(BF16)
                    hprev = h_s[h]
                    s_c = lax.dot_general(xdt, bd, (((0,), (0,)), ((), ())), preferred_element_type=F32)
                    y_off = lax.dot_general(cd, hprev.astype(BF16), (((1,), (1,)), ((), ())),
                                            preferred_element_type=F32)
                    y_diag = jnp.dot(m, xdt, preferred_element_type=F32)
                    h_s[h] = hprev * jnp.exp(tot) + s_c
                    y_s[c, :, hcols] += y_diag + y_off
            return carry

        lax.fori_loop(0, nc, chunk_body, 0)
        (hf_ref if d == 0 else hb_ref)[0] = h_s[...]

    gw = SSD_WIDTH // SSD_GROUPS
    for c in range(nc):
        rows = slice(c * q, (c + 1) * q)
        for g in range(SSD_GROUPS):
            cols = slice(g * gw, (g + 1) * gw)
            y = y_s[c, :, cols] * _silu(z_ref[0, rows, cols])
            y = y * lax.rsqrt(jnp.mean(y * y, axis=-1, keepdims=True) + NORM_EPS)
            y_ref[0, rows, cols] = (y * ng_ref[:, cols]).astype(BF16)


def ssd_mixer(proj, p, h0f, h0b):
    bsz, seq, _ = proj.shape
    nc = seq // SSD_CHUNK
    zero_init = h0f is None

    def col_spec(width, start):
        blk = start // width
        return pl.BlockSpec((1, seq, width), lambda b: (b, 0, blk))

    def full2(arr):
        return pl.BlockSpec(arr.shape, lambda b: (0, 0))

    st_spec = pl.BlockSpec((1, SSD_HEADS, SSD_HEAD_DIM, SSD_STATE), lambda b: (b, 0, 0, 0))
    bc = SSD_GROUPS * SSD_STATE
    small = [p["cw_x"], p["cb_x"], p["cw_b"], p["cb_b"], p["cw_c"], p["cb_c"],
             p["dt_bias"], p["a_log"], p["d_exp"], p["ssd_ng"]]
    in_specs = [col_spec(SSD_WIDTH, COL_Z), col_spec(SSD_WIDTH, COL_X), col_spec(bc, COL_B),
                col_spec(bc, COL_C), col_spec(LANES, COL_DT)] + [full2(s) for s in small]
    args = [proj] * 5 + small
    if not zero_init:
        in_specs += [st_spec, st_spec]
        args += [h0f, h0b]
    st_shape = jax.ShapeDtypeStruct((bsz, SSD_HEADS, SSD_HEAD_DIM, SSD_STATE), F32)
    q = SSD_CHUNK
    return pl.pallas_call(
        functools.partial(_ssd_kernel, nc, zero_init),
        grid=(bsz,),
        in_specs=in_specs,
        out_specs=[pl.BlockSpec((1, seq, SSD_WIDTH), lambda b: (b, 0, 0)), st_spec, st_spec],
        out_shape=[jax.ShapeDtypeStruct((bsz, seq, SSD_WIDTH), BF16), st_shape, st_shape],
        scratch_shapes=[pltpu.VMEM((nc, q, SSD_WIDTH), F32), pltpu.VMEM((nc, q, bc), F32),
                        pltpu.VMEM((nc, q, bc), F32), pltpu.VMEM((nc, q, LANES), F32),
                        pltpu.VMEM((nc, q, LANES), F32), pltpu.VMEM((nc, LANES, q), F32),
                        pltpu.VMEM((nc, q, SSD_WIDTH), F32),
                        pltpu.VMEM((SSD_HEADS, SSD_HEAD_DIM, SSD_STATE), F32)],
        compiler_params=_cparams(("parallel",)),
        name="ssd_mixer",
    )(*args)


def _seg_sum(x):
    lane = lax.broadcasted_iota(jnp.int32, x.shape, 1)
    first = lane < RWKV_HEAD_DIM
    s0 = jnp.sum(jnp.where(first, x, 0.0), axis=-1, keepdims=True)
    s1 = jnp.sum(jnp.where(first, 0.0, x), axis=-1, keepdims=True)
    return jnp.where(first, s0, s1)


def _shift_mix(u, mu):
    return u + mu * (0.5 * (_shifted(u, -1) + _shifted(u, 1)) - u)


def _rwkv_prep_kernel(r_ref, k_ref, v_ref, lo_ref, mur, muk, muv, mul, wupf, wupb, aup, gup,
                      w0f, w0b, a0, kk_ref, ka_ref,
                      ro, ko, vo, wo, alo, beo, go):
    r = _shift_mix(r_ref[0], mur[...])
    k = _shift_mix(k_ref[0], muk[...])
    v = _shift_mix(v_ref[0], muv[...])
    lo = _shift_mix(lo_ref[0], mul[...])
    wdf = jnp.tanh(lo[:, 0:128]).astype(BF16)
    wdb = jnp.tanh(lo[:, 128:256]).astype(BF16)
    ad = lo[:, 256:384].astype(BF16)
    gd = jax.nn.sigmoid(lo[:, 384:640]).astype(BF16)

    def mm(x, w_ref):
        return jnp.dot(x, w_ref[...].astype(BF16), preferred_element_type=F32)

    wo[0, 0] = jnp.exp(-DECAY_SCALE * jax.nn.sigmoid(w0f[...] + mm(wdf, wupf)))
    wo[1, 0] = jnp.exp(-DECAY_SCALE * jax.nn.sigmoid(w0b[...] + mm(wdb, wupb)))
    a = jax.nn.sigmoid(a0[...] + mm(ad, aup))
    go[0] = mm(gd, gup)
    kk = k * kk_ref[...]
    kk = kk / jnp.maximum(jnp.sqrt(_seg_sum(kk * kk)), 1e-12)
    ro[0] = r
    ko[0] = k * (1.0 + (a - 1.0) * ka_ref[...])
    vo[0] = v
    alo[0] = -kk
    beo[0] = kk * a


def rwkv_prep(proj, p):
    bsz, seq, _ = proj.shape
    nblk = RWKV_WIDTH // LANES

    def col_spec(start):
        blk = start // LANES
        return pl.BlockSpec((1, seq, LANES), lambda b, j: (b, 0, blk + j))

    vec = pl.BlockSpec((1, LANES), lambda b, j: (0, j))

    def mat(rows):
        return pl.BlockSpec((rows, LANES), lambda b, j: (0, j))

    out_spec = pl.BlockSpec((1, seq, LANES), lambda b, j: (b, 0, j))
    out_sh = jax.ShapeDtypeStruct((bsz, seq, RWKV_WIDTH), F32)
    return pl.pallas_call(
        _rwkv_prep_kernel,
        grid=(bsz, nblk),
        in_specs=[col_spec(COL_R), col_spec(COL_K), col_spec(COL_V),
                  pl.BlockSpec((1, seq, LORA_W), lambda b, j: (b, 0, COL_LORA // LORA_W)),
                  vec, vec, vec, pl.BlockSpec((1, LORA_W), lambda b, j: (0, 0)),
                  mat(128), mat(128), mat(128), mat(256),
                  vec, vec, vec, vec, vec],
        out_specs=[out_spec] * 3 + [pl.BlockSpec((2, 1, seq, LANES), lambda b, j: (0, b, 0, j))]
        + [out_spec] * 3,
        out_shape=[out_sh] * 3 + [jax.ShapeDtypeStruct((2, bsz, seq, RWKV_WIDTH), F32)] + [out_sh] * 3,
        compiler_params=_cparams(("parallel", "arbitrary")),
        name="rwkv_prep",
    )(proj, proj, proj, proj, p["mu_r"], p["mu_k"], p["mu_v"], p["mu_lora"],
      p["w_up_f"], p["w_up_b"], p["a_up"], p["g_up"],
      p["w0_f"], p["w0_b"], p["a0"], p["k_k"], p["k_a"])


def _rwkv_scan_kernel(tb, zero_init, *refs):
    if zero_init:
        r_ref, w_ref, k_ref, al_ref, be_ref, v_ref, y_ref, st_ref, s_s = refs
    else:
        r_ref, w_ref, k_ref, al_ref, be_ref, v_ref, s0_ref, y_ref, st_ref, s_s = refs
    kd = RWKV_HEAD_DIM
    d = pl.program_id(0)
    step_i = pl.program_id(2)

    def time_index(i):
        return i + d * (tb - 1 - 2 * i)

    @pl.when(step_i == 0)
    def _():
        if zero_init:
            s_s[...] = jnp.zeros_like(s_s)
        else:
            s_s[...] = s0_ref[...]

    t0 = time_index(0)
    sa0 = jnp.zeros(s_s.shape[1:], F32)
    sa1 = jnp.zeros(s_s.shape[1:], F32)
    for kq in range(0, kd, 2):
        sa0 = sa0 + s_s[kq] * al_ref[t0, kq:kq + 1, :]
        sa1 = sa1 + s_s[kq + 1] * al_ref[t0, kq + 1:kq + 2, :]

    def step(i, sa):
        t = time_index(i)
        tn = time_index(jnp.minimum(i + 1, tb - 1))
        vt = v_ref[t]
        y = jnp.zeros_like(sa)
        sa_next = jnp.zeros_like(sa)
        for kq in range(kd):
            row = slice(kq, kq + 1)
            sk = s_s[kq] * w_ref[t, row, :] + sa * be_ref[t, row, :] + vt * k_ref[t, row, :]
            s_s[kq] = sk
            y = y + sk * r_ref[t, row, :]
            sa_next = sa_next + sk * al_ref[tn, row, :]
        y_ref[t] = y
        return sa_next

    lax.fori_loop(0, tb, step, sa0 + sa1)

    @pl.when(step_i == pl.num_programs(2) - 1)
    def _():
        st_ref[...] = s_s[...]


def rwkv_scan(r, w, k, al, be, v, s0):
    seq, kd, nch = r.shape
    vv = v.shape[1]
    tb = 32
    nblk = seq // tb
    zero_init = s0 is None

    def tblk(d, i):
        return i + d * (nblk - 1 - 2 * i)

    kspec = pl.BlockSpec((tb, kd, LANES), lambda d, g, i: (tblk(d, i), 0, g))
    wspec = pl.BlockSpec((None, tb, kd, LANES), lambda d, g, i: (d, tblk(d, i), 0, g))
    vspec = pl.BlockSpec((tb, vv, LANES), lambda d, g, i: (tblk(d, i), 0, g))
    yspec = pl.BlockSpec((None, tb, vv, LANES), lambda d, g, i: (d, tblk(d, i), 0, g))
    sspec = pl.BlockSpec((None, kd, vv, LANES), lambda d, g, i: (d, 0, 0, g))
    in_specs = [kspec, wspec, kspec, kspec, kspec, vspec]
    args = [r, w, k, al, be, v]
    if not zero_init:
        in_specs.append(sspec)
        args.append(s0)
    return pl.pallas_call(
        functools.partial(_rwkv_scan_kernel, tb, zero_init),
        grid=(2, nch // LANES, nblk),
        in_specs=in_specs,
        out_specs=[yspec, sspec],
        out_shape=[jax.ShapeDtypeStruct((2, seq, vv, nch), F32),
                   jax.ShapeDtypeStruct((2, kd, vv, nch), F32)],
        scratch_shapes=[pltpu.VMEM((kd, vv, LANES), F32)],
        compiler_params=_cparams(("parallel", "parallel", "arbitrary")),
        name="rwkv_scan",
    )(*args)


def _rwkv_post_kernel(y_ref, r_ref, k_ref, v_ref, g_ref, lng, lnb, rk, o_ref):
    inv = 1.0 / RWKV_HEAD_DIM
    for cb in range(RWKV_WIDTH // LANES):
        cols = slice(cb * LANES, (cb + 1) * LANES)
        y = y_ref[0, 0, :, cols] + y_ref[1, 0, :, cols]
        mean = _seg_sum(y) * inv
        yc = y - mean
        var = _seg_sum(yc * yc) * inv
        yn = yc * lax.rsqrt(var + GN_EPS) * lng[:, cols] + lnb[:, cols]
        bonus = _seg_sum(r_ref[0, :, cols] * k_ref[0, :, cols] * rk[:, cols]) * v_ref[0, :, cols]
        o_ref[0, :, cols] = ((yn + bonus) * g_ref[0, :, cols]).astype(BF16)


def rwkv_post(y2, r, k, v, g, p):
    _, bsz, seq, wd = y2.shape
    tl = 256
    spec = pl.BlockSpec((1, tl, wd), lambda b, i: (b, i, 0))
    vec = pl.BlockSpec((1, wd), lambda b, i: (0, 0))
    return pl.pallas_call(
        _rwkv_post_kernel,
        grid=(bsz, seq // tl),
        in_specs=[pl.BlockSpec((2, 1, tl, wd), lambda b, i: (0, b, i, 0))] + [spec] * 4 + [vec] * 3,
        out_specs=spec,
        out_shape=jax.ShapeDtypeStruct((bsz, seq, wd), BF16),
        compiler_params=_cparams(("parallel", "parallel")),
        name="rwkv_post",
    )(y2, r, k, v, g, p["ln_g"], p["ln_b"], p["r_k"])


def _split_bf16(x):
    hi = x.astype(BF16)
    return hi, (x - hi.astype(F32)).astype(BF16)


def _out_proj_kernel(ys_ref, yr_ref, wo_ref, x_ref, gm_ref, ng_ref, sc_ref, sh_ref, rwh_ref, rwl_ref,
                     x1_ref, h2_ref, lg_ref):
    acc = jnp.dot(ys_ref[0], wo_ref[0:SSD_WIDTH, :], preferred_element_type=F32)
    acc = acc + jnp.dot(yr_ref[0], wo_ref[SSD_WIDTH:, :], preferred_element_type=F32)
    x1 = x_ref[0] + gm_ref[0] * acc
    x1_ref[0] = x1
    hn = x1 * lax.rsqrt(jnp.mean(x1 * x1, axis=-1, keepdims=True) + NORM_EPS) * ng_ref[...]
    h2 = hn * (1.0 + sc_ref[0]) + sh_ref[0]
    hi, lo = _split_bf16(h2)
    h2_ref[0] = hi
    lg_ref[0] = (jnp.dot(hi, rwh_ref[...], preferred_element_type=F32)
                 + jnp.dot(lo, rwh_ref[...], preferred_element_type=F32)
                 + jnp.dot(hi, rwl_ref[...], preferred_element_type=F32))


def out_proj(y_ssd, y_rwkv, w_out, x, gate_m, norm_g, scale_f, shift_f, router_w):
    bsz, seq, d = x.shape
    tm = 256
    per_b = gate_m.shape[0] > 1
    midx = (lambda b, i: (b, 0, 0)) if per_b else (lambda b, i: (0, 0, 0))
    half = pl.BlockSpec((1, tm, d // 2), lambda b, i: (b, i, 0))
    full = pl.BlockSpec((1, tm, d), lambda b, i: (b, i, 0))
    mspec = pl.BlockSpec((1, 1, d), midx)
    return pl.pallas_call(
        _out_proj_kernel,
        grid=(bsz, seq // tm),
        in_specs=[half, half, pl.BlockSpec((d, d), lambda b, i: (0, 0)), full, mspec,
                  pl.BlockSpec((1, d), lambda b, i: (0, 0)), mspec, mspec,
                  pl.BlockSpec((d, LANES), lambda b, i: (0, 0)),
                  pl.BlockSpec((d, LANES), lambda b, i: (0, 0))],
        out_specs=[full, full, pl.BlockSpec((1, tm, LANES), lambda b, i: (b, i, 0))],
        out_shape=[jax.ShapeDtypeStruct((bsz, seq, d), F32), jax.ShapeDtypeStruct((bsz, seq, d), BF16),
                   jax.ShapeDtypeStruct((bsz, seq, LANES), F32)],
        compiler_params=_cparams(("parallel", "parallel")),
        name="out_proj",
    )(y_ssd, y_rwkv, w_out, x, gate_m, norm_g.reshape(1, d), scale_f, shift_f, *router_w)


def _route_kernel(nt, cap, lg_ref, selr_ref, posr_ref, selt_ref, post_ref, afft_ref, aff_s, pre_s):
    ne = N_EXPERTS
    lane = lax.broadcasted_iota(jnp.int32, (LANES, LANES), 1)
    sub = lax.broadcasted_iota(jnp.int32, (LANES, LANES), 0)
    upper_incl = (sub <= lane).astype(BF16)
    for i in range(nt):
        lg = jnp.where(lane < ne, lg_ref[i * LANES:(i + 1) * LANES, :], -jnp.inf)
        e = jnp.exp(lg - jnp.max(lg, axis=-1, keepdims=True))
        aff = e / jnp.sum(e, axis=-1, keepdims=True)
        afft_ref[i * LANES:(i + 1) * LANES, :] = aff
        aff_s[i] = aff.T[0:ne, :]
    aff3 = aff_s[...]

    def count(mask):
        s = jnp.sum(mask.astype(F32), axis=0, keepdims=True)
        return jnp.sum(s, axis=2, keepdims=True)

    def search(_, carry):
        lo, hi = carry
        mid = 0.5 * (lo + hi)
        ok = count(aff3 >= mid) >= cap
        return jnp.where(ok, mid, lo), jnp.where(ok, hi, mid)

    lo0 = jnp.zeros((1, ne, 1), F32)
    hi0 = jnp.full((1, ne, 1), 2.0, F32)
    lo, _ = lax.fori_loop(0, SEARCH_ITERS, search, (lo0, hi0))
    cand = jnp.where(aff3 >= lo, aff3, 4.0)
    thr = jnp.min(jnp.min(cand, axis=0, keepdims=True), axis=2, keepdims=True)
    gt = (aff3 > thr).astype(F32)
    eq = (aff3 == thr).astype(F32)
    need = cap - count(aff3 > thr)[0]

    def prefix_excl(m3):
        off = jnp.zeros((ne, 1), F32)
        for i in range(nt):
            inc = jnp.dot(m3[i].astype(BF16), upper_incl, preferred_element_type=F32)
            pre_s[i] = inc - m3[i] + off
            off = off + inc[:, LANES - 1:LANES]
        return pre_s[...]

    sel = jnp.maximum(gt, eq * (prefix_excl(eq) < need[None]).astype(F32))
    pos = prefix_excl(sel)
    selr_ref[...] = sel
    posr_ref[...] = pos
    zpad = jnp.zeros((LANES - ne, LANES), F32)
    for i in range(nt):
        rows = slice(i * LANES, (i + 1) * LANES)
        selt_ref[rows, :] = jnp.concatenate([sel[i], zpad], axis=0).T
        post_ref[rows, :] = jnp.concatenate([pos[i], zpad], axis=0).T


def route(logits, cap):
    n_tok = logits.shape[0]
    nt = n_tok // LANES
    row_sh = jax.ShapeDtypeStruct((nt, N_EXPERTS, LANES), F32)
    tm_sh = jax.ShapeDtypeStruct((n_tok, LANES), F32)
    return pl.pallas_call(
        functools.partial(_route_kernel, nt, cap),
        out_shape=[row_sh, row_sh, tm_sh, tm_sh, tm_sh],
        scratch_shapes=[pltpu.VMEM((nt, N_EXPERTS, LANES), F32), pltpu.VMEM((nt, N_EXPERTS, LANES), F32)],
        compiler_params=pltpu.CompilerParams(vmem_limit_bytes=VMEM_LIMIT),
        name="route",
    )(logits)


def _window_start(starts_ref, tile, e, cap, win):
    ps = starts_ref[tile * N_EXPERTS + e]
    return pl.multiple_of(jnp.minimum((ps // SUBLANES) * SUBLANES, cap - win), SUBLANES)


def _gather_kernel(cap, win, nsub, starts_ref, sel_ref, pos_ref, h_ref, o_ref, acc):
    e = pl.program_id(0)
    j = pl.program_id(1)

    @pl.when(j == 0)
    def _():
        acc[...] = jnp.zeros_like(acc)

    for s in range(0, nsub, 2):
        s0 = _window_start(starts_ref, j * nsub + s, e, cap, win)
        slot = (lax.broadcasted_iota(jnp.int32, (win, LANES), 0) + s0).astype(F32)
        halves = []
        for u in range(2):
            prow = pos_ref[s + u, pl.ds(e, 1), :]
            srow = sel_ref[s + u, pl.ds(e, 1), :]
            halves.append(jnp.where(jnp.logical_and(slot == prow, srow > 0.0), 1.0, 0.0).astype(BF16))
        onehot = jnp.concatenate(halves, axis=1)
        acc[pl.ds(s0, win), :] += jnp.dot(onehot, h_ref[s * LANES:(s + 2) * LANES, :],
                                         preferred_element_type=F32)

    @pl.when(j == pl.num_programs(1) - 1)
    def _():
        o_ref[0] = acc[...].astype(BF16)


def gather_tokens(starts, sel_r, pos_r, h, cap):
    n_tok, d = h.shape
    tt = min(1024, n_tok)
    nsub = tt // LANES
    win = min(2 * LANES + SUBLANES, cap)
    rspec = pl.BlockSpec((nsub, N_EXPERTS, LANES), lambda e, j, st: (j, 0, 0))
    return pl.pallas_call(
        functools.partial(_gather_kernel, cap, win, nsub),
        grid_spec=pltpu.PrefetchScalarGridSpec(
            num_scalar_prefetch=1,
            grid=(N_EXPERTS, n_tok // tt),
            in_specs=[rspec, rspec, pl.BlockSpec((tt, d), lambda e, j, st: (j, 0))],
            out_specs=pl.BlockSpec((1, cap, d), lambda e, j, st: (e, 0, 0)),
            scratch_shapes=[pltpu.VMEM((cap, d), F32)]),
        out_shape=jax.ShapeDtypeStruct((N_EXPERTS, cap, d), BF16),
        compiler_params=_cparams(("parallel", "arbitrary")),
        name="gather_tokens",
    )(starts, sel_r, pos_r, h)


def _ffn_kernel(n_sets, *refs):
    x_refs = refs[:n_sets]
    wg_ref, wu_ref, wd_ref = refs[n_sets:n_sets + 3]
    o_refs = refs[n_sets + 3:2 * n_sets + 3]
    accs = refs[2 * n_sets + 3:]
    f = pl.program_id(1)
    wg = wg_ref[0].astype(BF16)
    wu = wu_ref[0].astype(BF16)
    wd = wd_ref[0].astype(BF16)
    for x_ref, o_ref, acc in zip(x_refs, o_refs, accs):
        @pl.when(f == 0)
        def _():
            acc[...] = jnp.zeros_like(acc)

        x = x_ref[0]
        hg = jnp.dot(x, wg, preferred_element_type=F32)
        hu = jnp.dot(x, wu, preferred_element_type=F32)
        hid = (_silu(hg) * hu).astype(BF16)
        acc[...] += jnp.dot(hid, wd, preferred_element_type=F32)

        @pl.when(f == pl.num_programs(1) - 1)
        def _():
            o_ref[0] = acc[...].astype(BF16)


def expert_ffn(xs_sets, wg, wu, wd):
    ne, _, d = xs_sets[0].shape
    ff = wg.shape[2]
    tf = 256
    xspecs = [pl.BlockSpec((1, xs.shape[1], d), lambda e, f: (e, 0, 0)) for xs in xs_sets]
    return pl.pallas_call(
        functools.partial(_ffn_kernel, len(xs_sets)),
        grid=(ne, ff // tf),
        in_specs=xspecs + [pl.BlockSpec((1, d, tf), lambda e, f: (e, 0, f)),
                           pl.BlockSpec((1, d, tf), lambda e, f: (e, 0, f)),
                           pl.BlockSpec((1, tf, d), lambda e, f: (e, f, 0))],
        out_specs=xspecs,
        out_shape=[jax.ShapeDtypeStruct(xs.shape, BF16) for xs in xs_sets],
        scratch_shapes=[pltpu.VMEM(xs.shape[1:], F32) for xs in xs_sets],
        compiler_params=_cparams(("parallel", "arbitrary")),
        name="expert_ffn",
    )(*xs_sets, wg, wu, wd)


def _scatter_kernel(cap, win, nsub, starts_ref, sel_ref, pos_ref, aff_ref, ob_ref, y_ref):
    c = pl.program_id(0)
    e = pl.program_id(1)

    @pl.when(e == 0)
    def _():
        y_ref[...] = jnp.zeros_like(y_ref)

    mine = lax.broadcasted_iota(jnp.int32, (LANES, LANES), 1) == e
    for s in range(nsub):
        rows = slice(s * LANES, (s + 1) * LANES)

        def col(ref):
            return jnp.sum(jnp.where(mine, ref[rows, :], 0.0), axis=-1, keepdims=True)

        pcol, scol, gcol = col(pos_ref), col(sel_ref), col(aff_ref)
        s0 = _window_start(starts_ref, c * nsub + s, e, cap, win)
        slot = (lax.broadcasted_iota(jnp.int32, (LANES, win), 1) + s0).astype(F32)
        onehot = jnp.where(jnp.logical_and(slot == pcol, scol > 0.0), 1.0, 0.0).astype(BF16)
        y_ref[rows, :] += gcol * jnp.dot(onehot, ob_ref[0, pl.ds(s0, win), :],
                                         preferred_element_type=F32)


def scatter_combine(starts, sel_t, pos_t, aff_t, outbuf):
    n_tok = sel_t.shape[0]
    ne, cap, d = outbuf.shape
    chunk = min(2048, n_tok)
    nsub = chunk // LANES
    win = min(2 * LANES, cap)
    tspec = pl.BlockSpec((chunk, LANES), lambda c, e, st: (c, 0))
    return pl.pallas_call(
        functools.partial(_scatter_kernel, cap, win, nsub),
        grid_spec=pltpu.PrefetchScalarGridSpec(
            num_scalar_prefetch=1,
            grid=(n_tok // chunk, ne),
            in_specs=[tspec, tspec, tspec, pl.BlockSpec((1, cap, d), lambda c, e, st: (e, 0, 0))],
            out_specs=pl.BlockSpec((chunk, d), lambda c, e, st: (c, 0))),
        out_shape=jax.ShapeDtypeStruct((n_tok, d), F32),
        compiler_params=_cparams(("parallel", "arbitrary")),
        name="scatter_combine",
    )(starts, sel_t, pos_t, aff_t, outbuf)


def _final_kernel(x1_ref, y_ref, gf_ref, fg_ref, o_ref):
    x2 = x1_ref[0] + gf_ref[0] * y_ref[0]
    o_ref[0] = x2 * lax.rsqrt(jnp.mean(x2 * x2, axis=-1, keepdims=True) + NORM_EPS) * fg_ref[...]


def final_norm(x1, y, gate_f, final_g):
    bsz, seq, d = x1.shape
    tl = 256
    per_b = gate_f.shape[0] > 1
    midx = (lambda b, i: (b, 0, 0)) if per_b else (lambda b, i: (0, 0, 0))
    xspec = pl.BlockSpec((1, tl, d), lambda b, i: (b, i, 0))
    return pl.pallas_call(
        _final_kernel,
        grid=(bsz, seq // tl),
        in_specs=[xspec, xspec, pl.BlockSpec((1, 1, d), midx), pl.BlockSpec((1, d), lambda b, i: (0, 0))],
        out_specs=xspec,
        out_shape=jax.ShapeDtypeStruct((bsz, seq, d), F32),
        compiler_params=_cparams(("parallel", "parallel")),
        name="final_norm",
    )(x1, y, gate_f, final_g.reshape(1, d))


def _pad_cols(w, width):
    return jnp.pad(w, ((0, 0), (0, width - w.shape[1])))


def _pad_rows(w, rows):
    return jnp.pad(w, ((0, rows - w.shape[0]), (0, 0)))


def _relayout_columns(w):
    n_ssd = 2 * SSD_WIDTH + 2 * SSD_GROUPS * SSD_STATE + 2 * SSD_HEADS
    ssd, rw = w[:, :n_ssd], w[:, n_ssd:]
    o = 3 * RWKV_WIDTH
    parts = [ssd[:, :COL_DT], _pad_cols(ssd[:, COL_DT:], LANES), rw[:, :o],
             _pad_cols(rw[:, o:o + DECAY_LORA], LANES),
             _pad_cols(rw[:, o + DECAY_LORA:o + 2 * DECAY_LORA], LANES),
             _pad_cols(rw[:, o + 2 * DECAY_LORA:o + 2 * DECAY_LORA + ICLR_LORA], LANES),
             _pad_cols(rw[:, o + 2 * DECAY_LORA + ICLR_LORA:], 2 * LANES)]
    return jnp.concatenate(parts, axis=1)


def _layer_params(l, w_in, w_out, ssd_conv_w, ssd_conv_b, ssd_A_log, ssd_dt_bias, ssd_D, ssd_norm_g,
                  rwkv_mu, rwkv_w0, rwkv_w_up, rwkv_a0, rwkv_a_up, rwkv_g_up, rwkv_k_k, rwkv_k_a,
                  rwkv_r_k, rwkv_ln_g, rwkv_ln_b, router_w):
    row = lambda a: a.reshape(1, -1)
    n_ssd = 2 * SSD_WIDTH + 2 * SSD_GROUPS * SSD_STATE + 2 * SSD_HEADS
    mu_full = _relayout_columns(jnp.concatenate([jnp.zeros((1, n_ssd), F32), row(rwkv_mu[l])], axis=1))
    cw, cb = ssd_conv_w[l], row(ssd_conv_b[l])
    xe, be = SSD_WIDTH, SSD_WIDTH + SSD_GROUPS * SSD_STATE
    return dict(
        w_in=_relayout_columns(w_in[l]).astype(BF16),
        w_out=w_out[l].astype(BF16),
        cw_x=cw[:, :xe], cb_x=cb[:, :xe], cw_b=cw[:, xe:be], cb_b=cb[:, xe:be],
        cw_c=cw[:, be:], cb_c=cb[:, be:],
        dt_bias=_pad_cols(row(ssd_dt_bias[l]), LANES), a_log=_pad_cols(row(ssd_A_log[l]), LANES),
        d_exp=row(jnp.repeat(ssd_D[l], SSD_HEAD_DIM)), ssd_ng=row(ssd_norm_g[l]),
        mu_r=mu_full[:, COL_R:COL_K], mu_k=mu_full[:, COL_K:COL_V], mu_v=mu_full[:, COL_V:COL_LORA],
        mu_lora=mu_full[:, COL_LORA:],
        w_up_f=_pad_rows(rwkv_w_up[l, 0], LANES), w_up_b=_pad_rows(rwkv_w_up[l, 1], LANES),
        a_up=_pad_rows(rwkv_a_up[l], LANES), g_up=_pad_rows(rwkv_g_up[l], 2 * LANES),
        w0_f=row(rwkv_w0[l, 0]), w0_b=row(rwkv_w0[l, 1]), a0=row(rwkv_a0[l]),
        k_k=row(rwkv_k_k[l]), k_a=row(rwkv_k_a[l]), r_k=row(rwkv_r_k[l]),
        ln_g=row(rwkv_ln_g[l]), ln_b=row(rwkv_ln_b[l]),
        router_w=tuple(_pad_cols(t, LANES) for t in _split_bf16(router_w[l])),
    )


def _to_chain(x):
    lead = x.shape[:-3]
    bsz, seq, _ = x.shape[-3:]
    n = len(lead)
    x = x.reshape(lead + (bsz, seq, RWKV_HEADS, RWKV_HEAD_DIM))
    perm = tuple(range(n)) + (n + 1, n + 3, n, n + 2)
    return jnp.transpose(x, perm).reshape(lead + (seq, RWKV_HEAD_DIM, bsz * RWKV_HEADS))


def _from_chain(y, bsz):
    lead = y.shape[:-3]
    seq = y.shape[-3]
    n = len(lead)
    y = y.reshape(lead + (seq, RWKV_HEAD_DIM, bsz, RWKV_HEADS))
    perm = tuple(range(n)) + (n + 2, n, n + 3, n + 1)
    return jnp.transpose(y, perm).reshape(lead + (bsz, seq, RWKV_WIDTH))


def _state_to_chain(s):
    return jnp.transpose(s, (0, 4, 3, 1, 2)).reshape(2, RWKV_HEAD_DIM, RWKV_HEAD_DIM, -1)


def _state_from_chain(s, bsz):
    return jnp.transpose(s.reshape(2, RWKV_HEAD_DIM, RWKV_HEAD_DIM, bsz, RWKV_HEADS), (0, 3, 4, 2, 1))


def rwkv_mixer_scan(r, k, v, w2, al, be, s0):
    bsz, seq, _ = r.shape
    nh, hd = RWKV_HEADS, RWKV_HEAD_DIM
    nch = bsz * nh
    rep = max(1, LANES // nch)
    assert (nch * rep) % LANES == 0
    if rep == 1:
        rc, kc, vc, alc, bec, wc = (_to_chain(t) for t in (r, k, v, al, be, w2))
        s0c = None if s0 is None else _state_to_chain(s0)
        y, st = rwkv_scan(rc, wc, kc, alc, bec, vc, s0c)
        return _from_chain(y, bsz), _state_from_chain(st, bsz)

    vq = hd // rep

    def keys_to_chain(x):
        lead = x.shape[:-3]
        n = len(lead)
        x = jnp.broadcast_to(x.reshape(lead + (1, bsz, seq, nh, hd)), lead + (rep, bsz, seq, nh, hd))
        perm = tuple(range(n)) + (n + 2, n + 4, n, n + 1, n + 3)
        return jnp.transpose(x, perm).reshape(lead + (seq, hd, LANES))

    rc, kc, alc, bec, wc = (keys_to_chain(t) for t in (r, k, al, be, w2))
    vc = jnp.transpose(v.reshape(bsz, seq, nh, rep, vq), (1, 4, 3, 0, 2)).reshape(seq, vq, LANES)
    s0c = None
    if s0 is not None:
        s0c = jnp.transpose(s0.reshape(2, bsz, nh, rep, vq, hd), (0, 5, 4, 3, 1, 2)).reshape(
            2, hd, vq, LANES)
    y, st = rwkv_scan(rc, wc, kc, alc, bec, vc, s0c)
    y = jnp.transpose(y.reshape(2, seq, vq, rep, bsz, nh), (0, 4, 1, 5, 3, 2)).reshape(
        2, bsz, seq, RWKV_WIDTH)
    st = jnp.transpose(st.reshape(2, hd, vq, rep, bsz, nh), (0, 4, 5, 3, 2, 1)).reshape(
        2, bsz, nh, hd, hd)
    return y, st


def _grid_pos_embed(n_tokens, dim):
    rows = n_tokens // GRID_W
    row = jnp.repeat(jnp.arange(rows, dtype=F32), GRID_W)
    col = jnp.tile(jnp.arange(GRID_W, dtype=F32), rows)
    quarter = dim // 4
    freqs = jnp.exp(jnp.arange(quarter, dtype=F32) * (-math.log(POS_BASE) / quarter))

    def axis_embed(pos):
        ang = pos[:, None] * freqs[None, :]
        return jnp.concatenate([jnp.sin(ang), jnp.cos(ang)], axis=-1)

    return jnp.concatenate([axis_embed(row), axis_embed(col)], axis=-1)


def layer_to_dispatch(x, pos, mod, p, norm_mix_g, norm_ffn_g, states):
    bsz, seq, d = x.shape
    shift_m, scale_m, gate_m, shift_f, scale_f, gate_f = mod
    h, x0 = norm_modulate(x, pos, norm_mix_g, scale_m, shift_m)
    n_tok = bsz * seq
    proj = matmul_bf16(h.reshape(n_tok, d), p["w_in"], min(1024, n_tok), 640).reshape(bsz, seq, N_PROJ)
    h0f, h0b, s0f, s0b = states
    y_ssd, hf, hb = ssd_mixer(proj, p, h0f, h0b)
    r, k, v, w2, al, be, g = rwkv_prep(proj, p)
    s0 = None if s0f is None else jnp.stack([s0f, s0b], axis=0)
    y2, st = rwkv_mixer_scan(r, k, v, w2, al, be, s0)
    y_rwkv = rwkv_post(y2, r, k, v, g, p)
    x1, h2, logits = out_proj(y_ssd, y_rwkv, p["w_out"], x0, gate_m, norm_ffn_g, scale_f, shift_f,
                              p["router_w"])
    cap = EC_CAPACITY * n_tok // N_EXPERTS
    sel_r, pos_r, sel_t, pos_t, aff_t = route(logits.reshape(n_tok, LANES), cap)
    starts = pos_r[:, :, 0].astype(jnp.int32).reshape(-1)
    xs = gather_tokens(starts, sel_r, pos_r, h2.reshape(n_tok, d), cap)
    routed = dict(xs=xs, starts=starts, sel_t=sel_t, pos_t=pos_t, aff_t=aff_t, x1=x1, gate_f=gate_f)
    return routed, (hf, hb, st[0], st[1])


def combine_and_finish(routed, outbuf, final_g):
    x1 = routed["x1"]
    y_ffn = scatter_combine(routed["starts"], routed["sel_t"], routed["pos_t"], routed["aff_t"], outbuf)
    return final_norm(x1, y_ffn.reshape(x1.shape), routed["gate_f"], final_g)


def kernel(x_prompt, x_sample, state_ssd_fwd, state_ssd_bwd, state_rwkv_fwd, state_rwkv_bwd, c, c_ctx,
           w_ada, b_ada, norm_mix_g, norm_ffn_g, w_in, w_out, ssd_conv_w, ssd_conv_b, ssd_A_log,
           ssd_dt_bias, ssd_D, ssd_norm_g, rwkv_mu, rwkv_w0, rwkv_w_up, rwkv_a0, rwkv_a_up, rwkv_g_up,
           rwkv_k_k, rwkv_k_a, rwkv_r_k, rwkv_ln_g, rwkv_ln_b, router_w, exp_w_gate, exp_w_up,
           exp_w_down, final_norm_g):
    depth = w_in.shape[0]
    assert depth == 1, "the final norm runs right after the single layer's FFN residual"
    l = 0
    dec_b = x_sample.shape[0]
    p = _layer_params(l, w_in, w_out, ssd_conv_w, ssd_conv_b, ssd_A_log, ssd_dt_bias, ssd_D, ssd_norm_g,
                      rwkv_mu, rwkv_w0, rwkv_w_up, rwkv_a0, rwkv_a_up, rwkv_g_up, rwkv_k_k, rwkv_k_a,
                      rwkv_r_k, rwkv_ln_g, rwkv_ln_b, router_w)
    cond = jnp.concatenate([c_ctx[None, :], c, jnp.zeros((16 - 1 - dec_b, D_MODEL), F32)], axis=0)
    mod = ada_modulation(cond, w_ada[l], b_ada[l]).reshape(16, N_MOD, 1, D_MODEL)
    mod_ctx = [mod[0:1, i] for i in range(N_MOD)]
    mod_lat = [mod[1:1 + dec_b, i] for i in range(N_MOD)]

    routed_ctx, st = layer_to_dispatch(x_prompt, None, mod_ctx, p, norm_mix_g[l], norm_ffn_g[l],
                                       (None, None, None, None))
    pos = _grid_pos_embed(x_sample.shape[1], D_MODEL)
    cached = (state_ssd_fwd[:, l], state_ssd_bwd[:, l], state_rwkv_fwd[:, l], state_rwkv_bwd[:, l])
    routed_lat, _ = layer_to_dispatch(x_sample, pos, mod_lat, p, norm_mix_g[l], norm_ffn_g[l], cached)
    out_ctx, out_lat = expert_ffn([routed_ctx["xs"], routed_lat["xs"]],
                                  exp_w_gate[l], exp_w_up[l], exp_w_down[l])
    y_prompt = combine_and_finish(routed_ctx, out_ctx, final_norm_g)
    y_sample = combine_and_finish(routed_lat, out_lat, final_norm_g)
    return (y_prompt, y_sample, st[0][:, None], st[1][:, None], st[2][:, None], st[3][:, None])
```

```python
import functools
import math

import jax
import jax.numpy as jnp
from jax import lax
from jax.experimental import pallas as pl
from jax.experimental.pallas import tpu as pltpu

F32 = jnp.float32
BF16 = jnp.bfloat16
HIGHEST = lax.Precision.HIGHEST

D_MODEL = 2048
GRID_W = 64
SSD_WIDTH = 1024
SSD_HEAD_DIM = 64
SSD_HEADS = 16
SSD_GROUPS = 2
SSD_STATE = 128
SSD_CONV_W = 5
SSD_CHUNK = 128
RWKV_WIDTH = 1024
RWKV_HEAD_DIM = 64
RWKV_HEADS = 16
DECAY_LORA = 64
ICLR_LORA = 64
GATE_LORA = 160
N_EXPERTS = 16
EC_CAPACITY = 2
EXPERT_FF = 1024
N_MOD = 6
NORM_EPS = 1e-6
GN_EPS = 64e-5
DECAY_SCALE = 0.606531
POS_BASE = 10000.0
SEARCH_ITERS = 48

LANES = 128
SUBLANES = 8

COL_Z = 0
COL_X = 1024
COL_B = 2048
COL_C = 2304
COL_DT = 2560
COL_R = 2688
COL_K = 3712
COL_V = 4736
COL_LORA = 5760
LORA_W = 640
N_PROJ = COL_LORA + LORA_W

VMEM_LIMIT = 56 * 1024 * 1024


def _cparams(sem):
    return pltpu.CompilerParams(dimension_semantics=sem, vmem_limit_bytes=VMEM_LIMIT)


def _silu(x):
    return x * jax.nn.sigmoid(x)


def _ada_kernel(c_ref, w_ref, b_ref, o_ref):
    s = _silu(c_ref[...]).astype(BF16)
    o_ref[...] = jnp.dot(s, w_ref[...].astype(BF16), preferred_element_type=F32) + b_ref[...]


def ada_modulation(cond, w, b):
    m, d = cond.shape
    n = w.shape[1]
    tn = 1024
    return pl.pallas_call(
        _ada_kernel,
        grid=(n // tn,),
        in_specs=[pl.BlockSpec((m, d), lambda j: (0, 0)),
                  pl.BlockSpec((d, tn), lambda j: (0, j)),
                  pl.BlockSpec((1, tn), lambda j: (0, j))],
        out_specs=pl.BlockSpec((m, tn), lambda j: (0, j)),
        out_shape=jax.ShapeDtypeStruct((m, n), F32),
        compiler_params=_cparams(("parallel",)),
        name="ada_modulation",
    )(cond, w, b.reshape(1, n))


def _norm_mod_kernel(has_pos, *refs):
    if has_pos:
        x_ref, pos_ref, g_ref, sc_ref, sh_ref, h_ref, x0_ref = refs
    else:
        x_ref, g_ref, sc_ref, sh_ref, h_ref = refs
    x = x_ref[0]
    if has_pos:
        x = x + pos_ref[...]
        x0_ref[0] = x
    y = x * lax.rsqrt(jnp.mean(x * x, axis=-1, keepdims=True) + NORM_EPS) * g_ref[...]
    h_ref[0] = (y * (1.0 + sc_ref[0]) + sh_ref[0]).astype(BF16)


def norm_modulate(x, pos, g, scale, shift):
    bsz, seq, d = x.shape
    tl = 256
    per_b = scale.shape[0] > 1
    midx = (lambda b, i: (b, 0, 0)) if per_b else (lambda b, i: (0, 0, 0))
    xspec = pl.BlockSpec((1, tl, d), lambda b, i: (b, i, 0))
    in_specs = [xspec]
    args = [x]
    if pos is not None:
        in_specs.append(pl.BlockSpec((tl, d), lambda b, i: (i, 0)))
        args.append(pos)
    in_specs += [pl.BlockSpec((1, d), lambda b, i: (0, 0)),
                 pl.BlockSpec((1, 1, d), midx), pl.BlockSpec((1, 1, d), midx)]
    args += [g.reshape(1, d), scale, shift]
    out_shape = [jax.ShapeDtypeStruct((bsz, seq, d), BF16)]
    out_specs = [xspec]
    if pos is not None:
        out_shape.append(jax.ShapeDtypeStruct((bsz, seq, d), F32))
        out_specs.append(xspec)
    res = pl.pallas_call(
        functools.partial(_norm_mod_kernel, pos is not None),
        grid=(bsz, seq // tl),
        in_specs=in_specs, out_specs=out_specs, out_shape=out_shape,
        compiler_params=_cparams(("parallel", "parallel")),
        name="norm_modulate",
    )(*args)
    return (res[0], res[1]) if pos is not None else (res[0], x)


def _mm_kernel(a_ref, b_ref, o_ref):
    o_ref[...] = jnp.dot(a_ref[...], b_ref[...], preferred_element_type=F32)


def matmul_bf16(a, b, tm, tn):
    m, k = a.shape
    n = b.shape[1]
    return pl.pallas_call(
        _mm_kernel,
        grid=(m // tm, n // tn),
        in_specs=[pl.BlockSpec((tm, k), lambda i, j: (i, 0)),
                  pl.BlockSpec((k, tn), lambda i, j: (0, j))],
        out_specs=pl.BlockSpec((tm, tn), lambda i, j: (i, j)),
        out_shape=jax.ShapeDtypeStruct((m, n), F32),
        compiler_params=_cparams(("parallel", "arbitrary")),
        name="in_proj",
    )(a, b)


def _shifted(u, off):
    n = u.shape[0]
    row = lax.broadcasted_iota(jnp.int32, (n, 1), 0)
    rolled = pltpu.roll(u, (-off) % n, axis=0)
    valid = jnp.logical_and(row + off >= 0, row + off < n)
    return jnp.where(valid, rolled, 0.0)


def _conv_silu(u, w_ref, b_ref, cols):
    pad = SSD_CONV_W // 2
    acc = u * w_ref[pad:pad + 1, cols] + b_ref[:, cols]
    for j in range(SSD_CONV_W):
        if j != pad:
            acc = acc + _shifted(u, j - pad) * w_ref[j:j + 1, cols]
    return _silu(acc)


def _softplus(x):
    return jnp.maximum(x, 0.0) + jnp.log1p(jnp.exp(-jnp.abs(x)))


def _ssd_kernel(nc, zero_init, *refs):
    q = SSD_CHUNK
    (z_ref, x_ref, b_ref, c_ref, dt_ref, cwx, cbx, cwb, cbb, cwc, cbc,
     dtb_ref, alog_ref, d_ref, ng_ref) = refs[:15]
    refs = refs[15:]
    if not zero_init:
        h0f_ref, h0b_ref = refs[:2]
        refs = refs[2:]
    y_ref, hf_ref, hb_ref, xh_s, bm_s, cm_s, dt_s, a_s, at_s, y_s, h_s = refs

    slab = 256
    for cb in range(SSD_WIDTH // slab):
        cols = slice(cb * slab, (cb + 1) * slab)
        xh = _conv_silu(x_ref[0, :, cols], cwx, cbx, cols)
        for c in range(nc):
            rows = slice(c * q, (c + 1) * q)
            xh_s[c, :, cols] = xh[rows]
            y_s[c, :, cols] = xh[rows] * d_ref[:, cols]
    allc = slice(0, SSD_GROUPS * SSD_STATE)
    bm = _conv_silu(b_ref[0], cwb, cbb, allc)
    cm = _conv_silu(c_ref[0], cwc, cbc, allc)
    dt = _softplus(dt_ref[0] + dtb_ref[...])
    a = dt * (-jnp.exp(alog_ref[...]))
    ii = lax.broadcasted_iota(jnp.int32, (q, q), 0)
    jj = lax.broadcasted_iota(jnp.int32, (q, q), 1)
    eye = (ii == jj).astype(F32)
    lower = ii >= jj
    upper = ii <= jj
    for c in range(nc):
        rows = slice(c * q, (c + 1) * q)
        bm_s[c] = bm[rows]
        cm_s[c] = cm[rows]
        dt_s[c] = dt[rows]
        a_s[c] = a[rows]
        at_s[c] = lax.dot_general(eye, a[rows], (((1,), (1,)), ((), ())),
                                  precision=HIGHEST, preferred_element_type=F32)

    for d in range(2):
        mask = lower if d == 0 else upper
        tri = mask.astype(F32)
        tri_t = (upper if d == 0 else lower).astype(F32)
        if zero_init:
            h_s[...] = jnp.zeros_like(h_s)
        else:
            h_s[...] = (h0f_ref if d == 0 else h0b_ref)[0]

        def chunk_body(ci, carry, d=d, mask=mask, tri=tri, tri_t=tri_t):
            c = ci if d == 0 else nc - 1 - ci
            a_c = a_s[c]
            cum = jnp.dot(tri, a_c, precision=HIGHEST, preferred_element_type=F32)
            cum_t = jnp.dot(at_s[c], tri_t, precision=HIGHEST, preferred_element_type=F32)
            dt_c = dt_s[c]
            for g in range(SSD_GROUPS):
                gcols = slice(g * SSD_STATE, (g + 1) * SSD_STATE)
                bg = bm_s[c, :, gcols]
                cg = cm_s[c, :, gcols]
                gram = lax.dot_general(cg.astype(BF16), bg.astype(BF16), (((1,), (1,)), ((), ())),
                                       preferred_element_type=F32)
                for hh in range(SSD_HEADS // SSD_GROUPS):
                    h = g * (SSD_HEADS // SSD_GROUPS) + hh
                    col = d * SSD_HEADS + h
                    hcols = slice(h * SSD_HEAD_DIM, (h + 1) * SSD_HEAD_DIM)
                    cc = cum[:, col:col + 1]
                    cr = cum_t[col:col + 1, :]
                    dec = jnp.exp(jnp.where(mask, cc - cr, -jnp.inf))
                    m = (gram * dec).astype(BF16)
                    xdt = (xh_s[c, :, hcols] * dt_c[:, col:col + 1]).astype(BF16)
                    tot = cum[q - 1:q, col:col + 1] if d == 0 else cum[0:1, col:col + 1]
                    bd = (bg * jnp.exp(tot - cc)).astype(BF16)
                    cd = (cg * jnp.exp(cc)).astype(BF16)
                    hprev = h_s[h]
                    s_c = lax.dot_general(xdt, bd, (((0,), (0,)), ((), ())), preferred_element_type=F32)
                    y_off = lax.dot_general(cd, hprev.astype(BF16), (((1,), (1,)), ((), ())),
                                            preferred_element_type=F32)
                    y_diag = jnp.dot(m, xdt, preferred_element_type=F32)
                    h_s[h] = hprev * jnp.exp(tot) + s_c
                    y_s[c, :, hcols] += y_diag + y_off
            return carry

        lax.fori_loop(0, nc, chunk_body, 0)
        (hf_ref if d == 0 else hb_ref)[0] = h_s[...]

    gw = SSD_WIDTH // SSD_GROUPS
    for c in range(nc):
        rows = slice(c * q, (c + 1) * q)
        for g in range(SSD_GROUPS):
            cols = slice(g * gw, (g + 1) * gw)
            y = y_s[c, :, cols] * _silu(z_ref[0, rows, cols])
            y = y * lax.rsqrt(jnp.mean(y * y, axis=-1, keepdims=True) + NORM_EPS)
            y_ref[0, rows, cols] = (y * ng_ref[:, cols]).astype(BF16)


def ssd_mixer(proj, p, h0f, h0b):
    bsz, seq, _ = proj.shape
    nc = seq // SSD_CHUNK
    zero_init = h0f is None

    def col_spec(width, start):
        blk = start // width
        return pl.BlockSpec((1, seq, width), lambda b: (b, 0, blk))

    def full2(arr):
        return pl.BlockSpec(arr.shape, lambda b: (0, 0))

    st_spec = pl.BlockSpec((1, SSD_HEADS, SSD_HEAD_DIM, SSD_STATE), lambda b: (b, 0, 0, 0))
    bc = SSD_GROUPS * SSD_STATE
    small = [p["cw_x"], p["cb_x"], p["cw_b"], p["cb_b"], p["cw_c"], p["cb_c"],
             p["dt_bias"], p["a_log"], p["d_exp"], p["ssd_ng"]]
    in_specs = [col_spec(SSD_WIDTH, COL_Z), col_spec(SSD_WIDTH, COL_X), col_spec(bc, COL_B),
                col_spec(bc, COL_C), col_spec(LANES, COL_DT)] + [full2(s) for s in small]
    args = [proj] * 5 + small
    if not zero_init:
        in_specs += [st_spec, st_spec]
        args += [h0f, h0b]
    st_shape = jax.ShapeDtypeStruct((bsz, SSD_HEADS, SSD_HEAD_DIM, SSD_STATE), F32)
    q = SSD_CHUNK
    return pl.pallas_call(
        functools.partial(_ssd_kernel, nc, zero_init),
        grid=(bsz,),
        in_specs=in_specs,
        out_specs=[pl.BlockSpec((1, seq, SSD_WIDTH), lambda b: (b, 0, 0)), st_spec, st_spec],
        out_shape=[jax.ShapeDtypeStruct((bsz, seq, SSD_WIDTH), BF16), st_shape, st_shape],
        scratch_shapes=[pltpu.VMEM((nc, q, SSD_WIDTH), F32), pltpu.VMEM((nc, q, bc), F32),
                        pltpu.VMEM((nc, q, bc), F32), pltpu.VMEM((nc, q, LANES), F32),
                        pltpu.VMEM((nc, q, LANES), F32), pltpu.VMEM((nc, LANES, q), F32),
                        pltpu.VMEM((nc, q, SSD_WIDTH), F32),
                        pltpu.VMEM((SSD_HEADS, SSD_HEAD_DIM, SSD_STATE), F32)],
        compiler_params=_cparams(("parallel",)),
        name="ssd_mixer",
    )(*args)


def _seg_sum(x):
    lane = lax.broadcasted_iota(jnp.int32, x.shape, 1)
    first = lane < RWKV_HEAD_DIM
    s0 = jnp.sum(jnp.where(first, x, 0.0), axis=-1, keepdims=True)
    s1 = jnp.sum(jnp.where(first, 0.0, x), axis=-1, keepdims=True)
    return jnp.where(first, s0, s1)


def _shift_mix(u, mu):
    return u + mu * (0.5 * (_shifted(u, -1) + _shifted(u, 1)) - u)


def _rwkv_prep_kernel(r_ref, k_ref, v_ref, lo_ref, mur, muk, muv, mul, wupf, wupb, aup, gup,
                      w0f, w0b, a0, kk_ref, ka_ref,
                      ro, ko, vo, wo, alo, beo, go):
    r = _shift_mix(r_ref[0], mur[...])
    k = _shift_mix(k_ref[0], muk[...])
    v = _shift_mix(v_ref[0], muv[...])
    lo = _shift_mix(lo_ref[0], mul[...])
    wdf = jnp.tanh(lo[:, 0:128]).astype(BF16)
    wdb = jnp.tanh(lo[:, 128:256]).astype(BF16)
    ad = lo[:, 256:384].astype(BF16)
    gd = jax.nn.sigmoid(lo[:, 384:640]).astype(BF16)

    def mm(x, w_ref):
        return jnp.dot(x, w_ref[...].astype(BF16), preferred_element_type=F32)

    wo[0, 0] = jnp.exp(-DECAY_SCALE * jax.nn.sigmoid(w0f[...] + mm(wdf, wupf)))
    wo[1, 0] = jnp.exp(-DECAY_SCALE * jax.nn.sigmoid(w0b[...] + mm(wdb, wupb)))
    a = jax.nn.sigmoid(a0[...] + mm(ad, aup))
    go[0] = mm(gd, gup)
    kk = k * kk_ref[...]
    kk = kk / jnp.maximum(jnp.sqrt(_seg_sum(kk * kk)), 1e-12)
    ro[0] = r
    ko[0] = k * (1.0 + (a - 1.0) * ka_ref[...])
    vo[0] = v
    alo[0] = -kk
    beo[0] = kk * a


def rwkv_prep(proj, p):
    bsz, seq, _ = proj.shape
    nblk = RWKV_WIDTH // LANES

    def col_spec(start):
        blk = start // LANES
        return pl.BlockSpec((1, seq, LANES), lambda b, j: (b, 0, blk + j))

    vec = pl.BlockSpec((1, LANES), lambda b, j: (0, j))

    def mat(rows):
        return pl.BlockSpec((rows, LANES), lambda b, j: (0, j))

    out_spec = pl.BlockSpec((1, seq, LANES), lambda b, j: (b, 0, j))
    out_sh = jax.ShapeDtypeStruct((bsz, seq, RWKV_WIDTH), F32)
    return pl.pallas_call(
        _rwkv_prep_kernel,
        grid=(bsz, nblk),
        in_specs=[col_spec(COL_R), col_spec(COL_K), col_spec(COL_V),
                  pl.BlockSpec((1, seq, LORA_W), lambda b, j: (b, 0, COL_LORA // LORA_W)),
                  vec, vec, vec, pl.BlockSpec((1, LORA_W), lambda b, j: (0, 0)),
                  mat(128), mat(128), mat(128), mat(256),
                  vec, vec, vec, vec, vec],
        out_specs=[out_spec] * 3 + [pl.BlockSpec((2, 1, seq, LANES), lambda b, j: (0, b, 0, j))]
        + [out_spec] * 3,
        out_shape=[out_sh] * 3 + [jax.ShapeDtypeStruct((2, bsz, seq, RWKV_WIDTH), F32)] + [out_sh] * 3,
        compiler_params=_cparams(("parallel", "arbitrary")),
        name="rwkv_prep",
    )(proj, proj, proj, proj, p["mu_r"], p["mu_k"], p["mu_v"], p["mu_lora"],
      p["w_up_f"], p["w_up_b"], p["a_up"], p["g_up"],
      p["w0_f"], p["w0_b"], p["a0"], p["k_k"], p["k_a"])


_ROW_R, _ROW_K, _ROW_AL, _ROW_BE, _ROW_W = range(5)
RELAYOUT_UNROLL = 4


def _rwkv_scan_kernel(tb, nb, rep, zero_init, *refs):
    if zero_init:
        (r_ref, w_ref, k_ref, al_ref, be_ref, v_ref, y_ref, st_ref,
         s_s, stg_s, rows_s, vrow_s, ystg_s) = refs
    else:
        (r_ref, w_ref, k_ref, al_ref, be_ref, v_ref, s0_ref, y_ref, st_ref,
         s_s, stg_s, rows_s, vrow_s, ystg_s) = refs
    kd = RWKV_HEAD_DIM
    nh = RWKV_HEADS
    vv = kd // rep
    nrow = nb * nh
    d = pl.program_id(0)
    step_i = pl.program_id(2)
    lane_q = lax.broadcasted_iota(jnp.int32, (vv, LANES), 1) // nrow

    def time_index(i):
        return i + d * (tb - 1 - 2 * i)

    @pl.when(step_i == 0)
    def _():
        ystg_s[...] = jnp.zeros_like(ystg_s)
        if zero_init:
            s_s[...] = jnp.zeros_like(s_s)
        else:
            s_s[...] = s0_ref[...]

    def relayout_in(t, u):
        pairs = ((r_ref, k_ref), (al_ref, be_ref), (w_ref, v_ref))
        for pi, (a_ref, b_ref) in enumerate(pairs):
            for bl in range(nb):
                ta = a_ref[bl, t]
                tb_ = b_ref[bl, t]
                for q in range(rep):
                    rows = slice(q * nrow + bl * nh, q * nrow + (bl + 1) * nh)
                    stg_s[u, pi, rows, 0:kd] = ta
                    stg_s[u, pi, rows, kd:2 * kd] = tb_
            tr = stg_s[u, pi].T
            rows_s[t, 2 * pi] = tr[0:kd]
            if pi < 2:
                rows_s[t, 2 * pi + 1] = tr[kd:2 * kd]
            else:
                full = tr[kd:2 * kd]
                vt = full[0:vv]
                for q in range(1, rep):
                    vt = jnp.where(lane_q == q, full[q * vv:(q + 1) * vv], vt)
                vrow_s[t] = vt

    def relayout_out(t, u):
        y = vrow_s[t]
        for q in range(rep):
            ystg_s[u, q * vv:(q + 1) * vv, :] = y if rep == 1 else jnp.where(lane_q == q, y, 0.0)
        tr = ystg_s[u].T
        acc = tr[0:nrow]
        for q in range(1, rep):
            acc = acc + tr[q * nrow:(q + 1) * nrow]
        for bl in range(nb):
            y_ref[bl, t] = acc[bl * nh:(bl + 1) * nh, 0:kd]

    def unrolled(fn):
        def body(i, carry):
            for u in range(RELAYOUT_UNROLL):
                fn(i * RELAYOUT_UNROLL + u, u)
            return carry
        lax.fori_loop(0, tb // RELAYOUT_UNROLL, body, 0)

    unrolled(relayout_in)

    t0 = time_index(0)
    sa0 = jnp.zeros(s_s.shape[1:], F32)
    sa1 = jnp.zeros(s_s.shape[1:], F32)
    for kq in range(0, kd, 2):
        sa0 = sa0 + s_s[kq] * rows_s[t0, _ROW_AL, kq:kq + 1, :]
        sa1 = sa1 + s_s[kq + 1] * rows_s[t0, _ROW_AL, kq + 1:kq + 2, :]

    def step(i, sa):
        t = time_index(i)
        tn = time_index(jnp.minimum(i + 1, tb - 1))
        vt = vrow_s[t]
        y = jnp.zeros_like(sa)
        sa_next = jnp.zeros_like(sa)
        for kq in range(kd):
            row = slice(kq, kq + 1)
            sk = (s_s[kq] * rows_s[t, _ROW_W, row, :] + sa * rows_s[t, _ROW_BE, row, :]
                  + vt * rows_s[t, _ROW_K, row, :])
            s_s[kq] = sk
            y = y + sk * rows_s[t, _ROW_R, row, :]
            sa_next = sa_next + sk * rows_s[tn, _ROW_AL, row, :]
        vrow_s[t] = y
        return sa_next

    lax.fori_loop(0, tb, step, sa0 + sa1)
    unrolled(relayout_out)

    @pl.when(step_i == pl.num_programs(2) - 1)
    def _():
        st_ref[...] = s_s[...]


def rwkv_scan(r, w, k, al, be, v, s0):
    bsz, seq, nh, kd = r.shape
    nb = min(bsz, LANES // nh)
    rep = LANES // (nb * nh)
    assert bsz % nb == 0 and nb * nh * rep == LANES and kd % rep == 0
    vv = kd // rep
    ngrp = bsz // nb
    tb = 32
    nblk = seq // tb
    zero_init = s0 is None

    def tblk(d, i):
        return i + d * (nblk - 1 - 2 * i)

    tspec = pl.BlockSpec((nb, tb, nh, kd), lambda d, g, i: (g, tblk(d, i), 0, 0))
    dspec = pl.BlockSpec((None, nb, tb, nh, kd), lambda d, g, i: (d, g, tblk(d, i), 0, 0))
    sspec = pl.BlockSpec((None, kd, vv, LANES), lambda d, g, i: (d, 0, 0, g))
    in_specs = [tspec, dspec, tspec, tspec, tspec, tspec]
    args = [r, w, k, al, be, v]
    if not zero_init:
        in_specs.append(sspec)
        args.append(s0)
    return pl.pallas_call(
        functools.partial(_rwkv_scan_kernel, tb, nb, rep, zero_init),
        grid=(2, ngrp, nblk),
        in_specs=in_specs,
        out_specs=[dspec, sspec],
        out_shape=[jax.ShapeDtypeStruct((2, bsz, seq, nh, kd), F32),
                   jax.ShapeDtypeStruct((2, kd, vv, ngrp * LANES), F32)],
        scratch_shapes=[pltpu.VMEM((kd, vv, LANES), F32),
                        pltpu.VMEM((RELAYOUT_UNROLL, 3, LANES, LANES), F32),
                        pltpu.VMEM((tb, 5, kd, LANES), F32), pltpu.VMEM((tb, vv, LANES), F32),
                        pltpu.VMEM((RELAYOUT_UNROLL, LANES, LANES), F32)],
        compiler_params=_cparams(("parallel", "parallel", "arbitrary")),
        name="rwkv_scan",
    )(*args)


def _rwkv_post_kernel(y_ref, r_ref, k_ref, v_ref, g_ref, lng, lnb, rk, o_ref):
    inv = 1.0 / RWKV_HEAD_DIM
    for cb in range(RWKV_WIDTH // LANES):
        cols = slice(cb * LANES, (cb + 1) * LANES)
        y = y_ref[0, 0, :, cols] + y_ref[1, 0, :, cols]
        mean = _seg_sum(y) * inv
        yc = y - mean
        var = _seg_sum(yc * yc) * inv
        yn = yc * lax.rsqrt(var + GN_EPS) * lng[:, cols] + lnb[:, cols]
        bonus = _seg_sum(r_ref[0, :, cols] * k_ref[0, :, cols] * rk[:, cols]) * v_ref[0, :, cols]
        o_ref[0, :, cols] = ((yn + bonus) * g_ref[0, :, cols]).astype(BF16)


def rwkv_post(y2, r, k, v, g, p):
    _, bsz, seq, wd = y2.shape
    tl = 256
    spec = pl.BlockSpec((1, tl, wd), lambda b, i: (b, i, 0))
    vec = pl.BlockSpec((1, wd), lambda b, i: (0, 0))
    return pl.pallas_call(
        _rwkv_post_kernel,
        grid=(bsz, seq // tl),
        in_specs=[pl.BlockSpec((2, 1, tl, wd), lambda b, i: (0, b, i, 0))] + [spec] * 4 + [vec] * 3,
        out_specs=spec,
        out_shape=jax.ShapeDtypeStruct((bsz, seq, wd), BF16),
        compiler_params=_cparams(("parallel", "parallel")),
        name="rwkv_post",
    )(y2, r, k, v, g, p["ln_g"], p["ln_b"], p["r_k"])


def _split_bf16(x):
    hi = x.astype(BF16)
    return hi, (x - hi.astype(F32)).astype(BF16)


def _out_proj_kernel(ys_ref, yr_ref, wo_ref, x_ref, gm_ref, ng_ref, sc_ref, sh_ref, rwh_ref, rwl_ref,
                     x1_ref, h2_ref, lg_ref):
    acc = jnp.dot(ys_ref[0], wo_ref[0:SSD_WIDTH, :], preferred_element_type=F32)
    acc = acc + jnp.dot(yr_ref[0], wo_ref[SSD_WIDTH:, :], preferred_element_type=F32)
    x1 = x_ref[0] + gm_ref[0] * acc
    x1_ref[0] = x1
    hn = x1 * lax.rsqrt(jnp.mean(x1 * x1, axis=-1, keepdims=True) + NORM_EPS) * ng_ref[...]
    h2 = hn * (1.0 + sc_ref[0]) + sh_ref[0]
    hi, lo = _split_bf16(h2)
    h2_ref[0] = hi
    lg_ref[0] = (jnp.dot(hi, rwh_ref[...], preferred_element_type=F32)
                 + jnp.dot(lo, rwh_ref[...], preferred_element_type=F32)
                 + jnp.dot(hi, rwl_ref[...], preferred_element_type=F32))


def out_proj(y_ssd, y_rwkv, w_out, x, gate_m, norm_g, scale_f, shift_f, router_w):
    bsz, seq, d = x.shape
    tm = 256
    per_b = gate_m.shape[0] > 1
    midx = (lambda b, i: (b, 0, 0)) if per_b else (lambda b, i: (0, 0, 0))
    half = pl.BlockSpec((1, tm, d // 2), lambda b, i: (b, i, 0))
    full = pl.BlockSpec((1, tm, d), lambda b, i: (b, i, 0))
    mspec = pl.BlockSpec((1, 1, d), midx)
    return pl.pallas_call(
        _out_proj_kernel,
        grid=(bsz, seq // tm),
        in_specs=[half, half, pl.BlockSpec((d, d), lambda b, i: (0, 0)), full, mspec,
                  pl.BlockSpec((1, d), lambda b, i: (0, 0)), mspec, mspec,
                  pl.BlockSpec((d, LANES), lambda b, i: (0, 0)),
                  pl.BlockSpec((d, LANES), lambda b, i: (0, 0))],
        out_specs=[full, full, pl.BlockSpec((1, tm, LANES), lambda b, i: (b, i, 0))],
        out_shape=[jax.ShapeDtypeStruct((bsz, seq, d), F32), jax.ShapeDtypeStruct((bsz, seq, d), BF16),
                   jax.ShapeDtypeStruct((bsz, seq, LANES), F32)],
        compiler_params=_cparams(("parallel", "parallel")),
        name="out_proj",
    )(y_ssd, y_rwkv, w_out, x, gate_m, norm_g.reshape(1, d), scale_f, shift_f, *router_w)


def _route_kernel(nt, cap, lg_ref, selr_ref, posr_ref, selt_ref, post_ref, afft_ref, aff_s, pre_s):
    ne = N_EXPERTS
    lane = lax.broadcasted_iota(jnp.int32, (LANES, LANES), 1)
    sub = lax.broadcasted_iota(jnp.int32, (LANES, LANES), 0)
    upper_incl = (sub <= lane).astype(BF16)
    for i in range(nt):
        lg = jnp.where(lane < ne, lg_ref[i * LANES:(i + 1) * LANES, :], -jnp.inf)
        e = jnp.exp(lg - jnp.max(lg, axis=-1, keepdims=True))
        aff = e / jnp.sum(e, axis=-1, keepdims=True)
        afft_ref[i * LANES:(i + 1) * LANES, :] = aff
        aff_s[i] = aff.T[0:ne, :]
    aff3 = aff_s[...]

    def count(mask):
        s = jnp.sum(mask.astype(F32), axis=0, keepdims=True)
        return jnp.sum(s, axis=2, keepdims=True)

    def search(_, carry):
        lo, hi = carry
        mid = 0.5 * (lo + hi)
        ok = count(aff3 >= mid) >= cap
        return jnp.where(ok, mid, lo), jnp.where(ok, hi, mid)

    lo0 = jnp.zeros((1, ne, 1), F32)
    hi0 = jnp.full((1, ne, 1), 2.0, F32)
    lo, _ = lax.fori_loop(0, SEARCH_ITERS, search, (lo0, hi0))
    cand = jnp.where(aff3 >= lo, aff3, 4.0)
    thr = jnp.min(jnp.min(cand, axis=0, keepdims=True), axis=2, keepdims=True)
    gt = (aff3 > thr).astype(F32)
    eq = (aff3 == thr).astype(F32)
    need = cap - count(aff3 > thr)[0]

    def prefix_excl(m3):
        off = jnp.zeros((ne, 1), F32)
        for i in range(nt):
            inc = jnp.dot(m3[i].astype(BF16), upper_incl, preferred_element_type=F32)
            pre_s[i] = inc - m3[i] + off
            off = off + inc[:, LANES - 1:LANES]
        return pre_s[...]

    sel = jnp.maximum(gt, eq * (prefix_excl(eq) < need[None]).astype(F32))
    pos = prefix_excl(sel)
    selr_ref[...] = sel
    posr_ref[...] = pos
    zpad = jnp.zeros((LANES - ne, LANES), F32)
    for i in range(nt):
        rows = slice(i * LANES, (i + 1) * LANES)
        selt_ref[rows, :] = jnp.concatenate([sel[i], zpad], axis=0).T
        post_ref[rows, :] = jnp.concatenate([pos[i], zpad], axis=0).T


def route(logits, cap):
    n_tok = logits.shape[0]
    nt = n_tok // LANES
    row_sh = jax.ShapeDtypeStruct((nt, N_EXPERTS, LANES), F32)
    tm_sh = jax.ShapeDtypeStruct((n_tok, LANES), F32)
    return pl.pallas_call(
        functools.partial(_route_kernel, nt, cap),
        out_shape=[row_sh, row_sh, tm_sh, tm_sh, tm_sh],
        scratch_shapes=[pltpu.VMEM((nt, N_EXPERTS, LANES), F32), pltpu.VMEM((nt, N_EXPERTS, LANES), F32)],
        compiler_params=pltpu.CompilerParams(vmem_limit_bytes=VMEM_LIMIT),
        name="route",
    )(logits)


def _window_start(starts_ref, tile, e, cap, win):
    ps = starts_ref[tile * N_EXPERTS + e]
    return pl.multiple_of(jnp.minimum((ps // SUBLANES) * SUBLANES, cap - win), SUBLANES)


def _gather_kernel(cap, win, nsub, starts_ref, sel_ref, pos_ref, h_ref, o_ref, acc):
    e = pl.program_id(0)
    j = pl.program_id(1)

    @pl.when(j == 0)
    def _():
        acc[...] = jnp.zeros_like(acc)

    for s in range(0, nsub, 2):
        s0 = _window_start(starts_ref, j * nsub + s, e, cap, win)
        slot = (lax.broadcasted_iota(jnp.int32, (win, LANES), 0) + s0).astype(F32)
        halves = []
        for u in range(2):
            prow = pos_ref[s + u, pl.ds(e, 1), :]
            srow = sel_ref[s + u, pl.ds(e, 1), :]
            halves.append(jnp.where(jnp.logical_and(slot == prow, srow > 0.0), 1.0, 0.0).astype(BF16))
        onehot = jnp.concatenate(halves, axis=1)
        acc[pl.ds(s0, win), :] += jnp.dot(onehot, h_ref[s * LANES:(s + 2) * LANES, :],
                                         preferred_element_type=F32)

    @pl.when(j == pl.num_programs(1) - 1)
    def _():
        o_ref[0] = acc[...].astype(BF16)


def gather_tokens(starts, sel_r, pos_r, h, cap):
    n_tok, d = h.shape
    tt = min(1024, n_tok)
    nsub = tt // LANES
    win = min(2 * LANES + SUBLANES, cap)
    rspec = pl.BlockSpec((nsub, N_EXPERTS, LANES), lambda e, j, st: (j, 0, 0))
    return pl.pallas_call(
        functools.partial(_gather_kernel, cap, win, nsub),
        grid_spec=pltpu.PrefetchScalarGridSpec(
            num_scalar_prefetch=1,
            grid=(N_EXPERTS, n_tok // tt),
            in_specs=[rspec, rspec, pl.BlockSpec((tt, d), lambda e, j, st: (j, 0))],
            out_specs=pl.BlockSpec((1, cap, d), lambda e, j, st: (e, 0, 0)),
            scratch_shapes=[pltpu.VMEM((cap, d), F32)]),
        out_shape=jax.ShapeDtypeStruct((N_EXPERTS, cap, d), BF16),
        compiler_params=_cparams(("parallel", "arbitrary")),
        name="gather_tokens",
    )(starts, sel_r, pos_r, h)


def _ffn_kernel(n_sets, *refs):
    x_refs = refs[:n_sets]
    wg_ref, wu_ref, wd_ref = refs[n_sets:n_sets + 3]
    o_refs = refs[n_sets + 3:2 * n_sets + 3]
    accs = refs[2 * n_sets + 3:]
    f = pl.program_id(1)
    wg = wg_ref[0].astype(BF16)
    wu = wu_ref[0].astype(BF16)
    wd = wd_ref[0].astype(BF16)
    for x_ref, o_ref, acc in zip(x_refs, o_refs, accs):
        @pl.when(f == 0)
        def _():
            acc[...] = jnp.zeros_like(acc)

        x = x_ref[0]
        hg = jnp.dot(x, wg, preferred_element_type=F32)
        hu = jnp.dot(x, wu, preferred_element_type=F32)
        hid = (_silu(hg) * hu).astype(BF16)
        acc[...] += jnp.dot(hid, wd, preferred_element_type=F32)

        @pl.when(f == pl.num_programs(1) - 1)
        def _():
            o_ref[0] = acc[...].astype(BF16)


def expert_ffn(xs_sets, wg, wu, wd):
    ne, _, d = xs_sets[0].shape
    ff = wg.shape[2]
    tf = 256
    xspecs = [pl.BlockSpec((1, xs.shape[1], d), lambda e, f: (e, 0, 0)) for xs in xs_sets]
    return pl.pallas_call(
        functools.partial(_ffn_kernel, len(xs_sets)),
        grid=(ne, ff // tf),
        in_specs=xspecs + [pl.BlockSpec((1, d, tf), lambda e, f: (e, 0, f)),
                           pl.BlockSpec((1, d, tf), lambda e, f: (e, 0, f)),
                           pl.BlockSpec((1, tf, d), lambda e, f: (e, f, 0))],
        out_specs=xspecs,
        out_shape=[jax.ShapeDtypeStruct(xs.shape, BF16) for xs in xs_sets],
        scratch_shapes=[pltpu.VMEM(xs.shape[1:], F32) for xs in xs_sets],
        compiler_params=_cparams(("parallel", "arbitrary")),
        name="expert_ffn",
    )(*xs_sets, wg, wu, wd)


def _scatter_kernel(cap, win, nsub, starts_ref, sel_ref, pos_ref, aff_ref, ob_ref, y_ref):
    c = pl.program_id(0)
    e = pl.program_id(1)

    @pl.when(e == 0)
    def _():
        y_ref[...] = jnp.zeros_like(y_ref)

    mine = lax.broadcasted_iota(jnp.int32, (LANES, LANES), 1) == e
    for s in range(nsub):
        rows = slice(s * LANES, (s + 1) * LANES)

        def col(ref):
            return jnp.sum(jnp.where(mine, ref[rows, :], 0.0), axis=-1, keepdims=True)

        pcol, scol, gcol = col(pos_ref), col(sel_ref), col(aff_ref)
        s0 = _window_start(starts_ref, c * nsub + s, e, cap, win)
        slot = (lax.broadcasted_iota(jnp.int32, (LANES, win), 1) + s0).astype(F32)
        onehot = jnp.where(jnp.logical_and(slot == pcol, scol > 0.0), 1.0, 0.0).astype(BF16)
        y_ref[rows, :] += gcol * jnp.dot(onehot, ob_ref[0, pl.ds(s0, win), :],
                                         preferred_element_type=F32)


def scatter_combine(starts, sel_t, pos_t, aff_t, outbuf):
    n_tok = sel_t.shape[0]
    ne, cap, d = outbuf.shape
    chunk = min(2048, n_tok)
    nsub = chunk // LANES
    win = min(2 * LANES, cap)
    tspec = pl.BlockSpec((chunk, LANES), lambda c, e, st: (c, 0))
    return pl.pallas_call(
        functools.partial(_scatter_kernel, cap, win, nsub),
        grid_spec=pltpu.PrefetchScalarGridSpec(
            num_scalar_prefetch=1,
            grid=(n_tok // chunk, ne),
            in_specs=[tspec, tspec, tspec, pl.BlockSpec((1, cap, d), lambda c, e, st: (e, 0, 0))],
            out_specs=pl.BlockSpec((chunk, d), lambda c, e, st: (c, 0))),
        out_shape=jax.ShapeDtypeStruct((n_tok, d), F32),
        compiler_params=_cparams(("parallel", "arbitrary")),
        name="scatter_combine",
    )(starts, sel_t, pos_t, aff_t, outbuf)


def _final_kernel(x1_ref, y_ref, gf_ref, fg_ref, o_ref):
    x2 = x1_ref[0] + gf_ref[0] * y_ref[0]
    o_ref[0] = x2 * lax.rsqrt(jnp.mean(x2 * x2, axis=-1, keepdims=True) + NORM_EPS) * fg_ref[...]


def final_norm(x1, y, gate_f, final_g):
    bsz, seq, d = x1.shape
    tl = 256
    per_b = gate_f.shape[0] > 1
    midx = (lambda b, i: (b, 0, 0)) if per_b else (lambda b, i: (0, 0, 0))
    xspec = pl.BlockSpec((1, tl, d), lambda b, i: (b, i, 0))
    return pl.pallas_call(
        _final_kernel,
        grid=(bsz, seq // tl),
        in_specs=[xspec, xspec, pl.BlockSpec((1, 1, d), midx), pl.BlockSpec((1, d), lambda b, i: (0, 0))],
        out_specs=xspec,
        out_shape=jax.ShapeDtypeStruct((bsz, seq, d), F32),
        compiler_params=_cparams(("parallel", "parallel")),
        name="final_norm",
    )(x1, y, gate_f, final_g.reshape(1, d))


def _pad_cols(w, width):
    return jnp.pad(w, ((0, 0), (0, width - w.shape[1])))


def _pad_rows(w, rows):
    return jnp.pad(w, ((0, rows - w.shape[0]), (0, 0)))


def _relayout_columns(w):
    n_ssd = 2 * SSD_WIDTH + 2 * SSD_GROUPS * SSD_STATE + 2 * SSD_HEADS
    ssd, rw = w[:, :n_ssd], w[:, n_ssd:]
    o = 3 * RWKV_WIDTH
    parts = [ssd[:, :COL_DT], _pad_cols(ssd[:, COL_DT:], LANES), rw[:, :o],
             _pad_cols(rw[:, o:o + DECAY_LORA], LANES),
             _pad_cols(rw[:, o + DECAY_LORA:o + 2 * DECAY_LORA], LANES),
             _pad_cols(rw[:, o + 2 * DECAY_LORA:o + 2 * DECAY_LORA + ICLR_LORA], LANES),
             _pad_cols(rw[:, o + 2 * DECAY_LORA + ICLR_LORA:], 2 * LANES)]
    return jnp.concatenate(parts, axis=1)


def _layer_params(l, w_in, w_out, ssd_conv_w, ssd_conv_b, ssd_A_log, ssd_dt_bias, ssd_D, ssd_norm_g,
                  rwkv_mu, rwkv_w0, rwkv_w_up, rwkv_a0, rwkv_a_up, rwkv_g_up, rwkv_k_k, rwkv_k_a,
                  rwkv_r_k, rwkv_ln_g, rwkv_ln_b, router_w):
    row = lambda a: a.reshape(1, -1)
    n_ssd = 2 * SSD_WIDTH + 2 * SSD_GROUPS * SSD_STATE + 2 * SSD_HEADS
    mu_full = _relayout_columns(jnp.concatenate([jnp.zeros((1, n_ssd), F32), row(rwkv_mu[l])], axis=1))
    cw, cb = ssd_conv_w[l], row(ssd_conv_b[l])
    xe, be = SSD_WIDTH, SSD_WIDTH + SSD_GROUPS * SSD_STATE
    return dict(
        w_in=_relayout_columns(w_in[l]).astype(BF16),
        w_out=w_out[l].astype(BF16),
        cw_x=cw[:, :xe], cb_x=cb[:, :xe], cw_b=cw[:, xe:be], cb_b=cb[:, xe:be],
        cw_c=cw[:, be:], cb_c=cb[:, be:],
        dt_bias=_pad_cols(row(ssd_dt_bias[l]), LANES), a_log=_pad_cols(row(ssd_A_log[l]), LANES),
        d_exp=row(jnp.repeat(ssd_D[l], SSD_HEAD_DIM)), ssd_ng=row(ssd_norm_g[l]),
        mu_r=mu_full[:, COL_R:COL_K], mu_k=mu_full[:, COL_K:COL_V], mu_v=mu_full[:, COL_V:COL_LORA],
        mu_lora=mu_full[:, COL_LORA:],
        w_up_f=_pad_rows(rwkv_w_up[l, 0], LANES), w_up_b=_pad_rows(rwkv_w_up[l, 1], LANES),
        a_up=_pad_rows(rwkv_a_up[l], LANES), g_up=_pad_rows(rwkv_g_up[l], 2 * LANES),
        w0_f=row(rwkv_w0[l, 0]), w0_b=row(rwkv_w0[l, 1]), a0=row(rwkv_a0[l]),
        k_k=row(rwkv_k_k[l]), k_a=row(rwkv_k_a[l]), r_k=row(rwkv_r_k[l]),
        ln_g=row(rwkv_ln_g[l]), ln_b=row(rwkv_ln_b[l]),
        router_w=tuple(_pad_cols(t, LANES) for t in _split_bf16(router_w[l])),
    )


def rwkv_mixer_scan(r, k, v, w2, al, be, s0):
    bsz, seq, _ = r.shape
    nh, hd = RWKV_HEADS, RWKV_HEAD_DIM
    nb = min(bsz, LANES // nh)
    rep = LANES // (nb * nh)
    vq = hd // rep
    ngrp = bsz // nb
    heads = lambda t: t.reshape(t.shape[:-1] + (nh, hd))
    s0c = None
    if s0 is not None:
        s0c = jnp.transpose(s0.reshape(2, ngrp, nb, nh, rep, vq, hd), (0, 6, 5, 1, 4, 2, 3)).reshape(
            2, hd, vq, ngrp * LANES)
    y, st = rwkv_scan(heads(r), heads(w2), heads(k), heads(al), heads(be), heads(v), s0c)
    st = jnp.transpose(st.reshape(2, hd, vq, ngrp, rep, nb, nh), (0, 3, 5, 6, 4, 2, 1)).reshape(
        2, bsz, nh, hd, hd)
    return y.reshape(2, bsz, seq, RWKV_WIDTH), st


def _grid_pos_embed(n_tokens, dim):
    rows = n_tokens // GRID_W
    row = jnp.repeat(jnp.arange(rows, dtype=F32), GRID_W)
    col = jnp.tile(jnp.arange(GRID_W, dtype=F32), rows)
    quarter = dim // 4
    freqs = jnp.exp(jnp.arange(quarter, dtype=F32) * (-math.log(POS_BASE) / quarter))

    def axis_embed(pos):
        ang = pos[:, None] * freqs[None, :]
        return jnp.concatenate([jnp.sin(ang), jnp.cos(ang)], axis=-1)

    return jnp.concatenate([axis_embed(row), axis_embed(col)], axis=-1)


def layer_to_dispatch(x, pos, mod, p, norm_mix_g, norm_ffn_g, states):
    bsz, seq, d = x.shape
    shift_m, scale_m, gate_m, shift_f, scale_f, gate_f = mod
    h, x0 = norm_modulate(x, pos, norm_mix_g, scale_m, shift_m)
    n_tok = bsz * seq
    proj = matmul_bf16(h.reshape(n_tok, d), p["w_in"], min(1024, n_tok), 640).reshape(bsz, seq, N_PROJ)
    h0f, h0b, s0f, s0b = states
    y_ssd, hf, hb = ssd_mixer(proj, p, h0f, h0b)
    r, k, v, w2, al, be, g = rwkv_prep(proj, p)
    s0 = None if s0f is None else jnp.stack([s0f, s0b], axis=0)
    y2, st = rwkv_mixer_scan(r, k, v, w2, al, be, s0)
    y_rwkv = rwkv_post(y2, r, k, v, g, p)
    x1, h2, logits = out_proj(y_ssd, y_rwkv, p["w_out"], x0, gate_m, norm_ffn_g, scale_f, shift_f,
                              p["router_w"])
    cap = EC_CAPACITY * n_tok // N_EXPERTS
    sel_r, pos_r, sel_t, pos_t, aff_t = route(logits.reshape(n_tok, LANES), cap)
    starts = pos_r[:, :, 0].astype(jnp.int32).reshape(-1)
    xs = gather_tokens(starts, sel_r, pos_r, h2.reshape(n_tok, d), cap)
    routed = dict(xs=xs, starts=starts, sel_t=sel_t, pos_t=pos_t, aff_t=aff_t, x1=x1, gate_f=gate_f)
    return routed, (hf, hb, st[0], st[1])


def combine_and_finish(routed, outbuf, final_g):
    x1 = routed["x1"]
    y_ffn = scatter_combine(routed["starts"], routed["sel_t"], routed["pos_t"], routed["aff_t"], outbuf)
    return final_norm(x1, y_ffn.reshape(x1.shape), routed["gate_f"], final_g)


def kernel(x_prompt, x_sample, state_ssd_fwd, state_ssd_bwd, state_rwkv_fwd, state_rwkv_bwd, c, c_ctx,
           w_ada, b_ada, norm_mix_g, norm_ffn_g, w_in, w_out, ssd_conv_w, ssd_conv_b, ssd_A_log,
           ssd_dt_bias, ssd_D, ssd_norm_g, rwkv_mu, rwkv_w0, rwkv_w_up, rwkv_a0, rwkv_a_up, rwkv_g_up,
           rwkv_k_k, rwkv_k_a, rwkv_r_k, rwkv_ln_g, rwkv_ln_b, router_w, exp_w_gate, exp_w_up,
           exp_w_down, final_norm_g):
    depth = w_in.shape[0]
    assert depth == 1, "the final norm runs right after the single layer's FFN residual"
    l = 0
    dec_b = x_sample.shape[0]
    p = _layer_params(l, w_in, w_out, ssd_conv_w, ssd_conv_b, ssd_A_log, ssd_dt_bias, ssd_D, ssd_norm_g,
                      rwkv_mu, rwkv_w0, rwkv_w_up, rwkv_a0, rwkv_a_up, rwkv_g_up, rwkv_k_k, rwkv_k_a,
                      rwkv_r_k, rwkv_ln_g, rwkv_ln_b, router_w)
    cond = jnp.concatenate([c_ctx[None, :], c, jnp.zeros((16 - 1 - dec_b, D_MODEL), F32)], axis=0)
    mod = ada_modulation(cond, w_ada[l], b_ada[l]).reshape(16, N_MOD, 1, D_MODEL)
    mod_ctx = [mod[0:1, i] for i in range(N_MOD)]
    mod_lat = [mod[1:1 + dec_b, i] for i in range(N_MOD)]

    routed_ctx, st = layer_to_dispatch(x_prompt, None, mod_ctx, p, norm_mix_g[l], norm_ffn_g[l],
                                       (None, None, None, None))
    pos = _grid_pos_embed(x_sample.shape[1], D_MODEL)
    cached = (state_ssd_fwd[:, l], state_ssd_bwd[:, l], state_rwkv_fwd[:, l], state_rwkv_bwd[:, l])
    routed_lat, _ = layer_to_dispatch(x_sample, pos, mod_lat, p, norm_mix_g[l], norm_ffn_g[l], cached)
    out_ctx, out_lat = expert_ffn([routed_ctx["xs"], routed_lat["xs"]],
                                  exp_w_gate[l], exp_w_up[l], exp_w_down[l])
    y_prompt = combine_and_finish(routed_ctx, out_ctx, final_norm_g)
    y_sample = combine_and_finish(routed_lat, out_lat, final_norm_g)
    return (y_prompt, y_sample, st[0][:, None], st[1][:, None], st[2][:, None], st[3][:, None])
```

```python
import functools
import math

import jax
import jax.numpy as jnp
from jax import lax
from jax.experimental import pallas as pl
from jax.experimental.pallas import tpu as pltpu

F32 = jnp.float32
BF16 = jnp.bfloat16
HIGHEST = lax.Precision.HIGHEST

D_MODEL = 2048
GRID_W = 64
SSD_WIDTH = 1024
SSD_HEAD_DIM = 64
SSD_HEADS = 16
SSD_GROUPS = 2
SSD_STATE = 128
SSD_CONV_W = 5
SSD_CHUNK = 128
RWKV_WIDTH = 1024
RWKV_HEAD_DIM = 64
RWKV_HEADS = 16
DECAY_LORA = 64
ICLR_LORA = 64
GATE_LORA = 160
N_EXPERTS = 16
EC_CAPACITY = 2
EXPERT_FF = 1024
N_MOD = 6
NORM_EPS = 1e-6
GN_EPS = 64e-5
DECAY_SCALE = 0.606531
POS_BASE = 10000.0
SEARCH_ITERS = 48

LANES = 128
SUBLANES = 8

COL_Z = 0
COL_X = 1024
COL_B = 2048
COL_C = 2304
COL_DT = 2560
COL_R = 2688
COL_K = 3712
COL_V = 4736
COL_LORA = 5760
LORA_W = 640
N_PROJ = COL_LORA + LORA_W

VMEM_LIMIT = 56 * 1024 * 1024


def _cparams(sem):
    return pltpu.CompilerParams(dimension_semantics=sem, vmem_limit_bytes=VMEM_LIMIT)


def _silu(x):
    return x * jax.nn.sigmoid(x)


def _ada_kernel(c_ref, w_ref, b_ref, o_ref):
    s = _silu(c_ref[...]).astype(BF16)
    o_ref[...] = jnp.dot(s, w_ref[...].astype(BF16), preferred_element_type=F32) + b_ref[...]


def ada_modulation(cond, w, b):
    m, d = cond.shape
    n = w.shape[1]
    tn = 1024
    return pl.pallas_call(
        _ada_kernel,
        grid=(n // tn,),
        in_specs=[pl.BlockSpec((m, d), lambda j: (0, 0)),
                  pl.BlockSpec((d, tn), lambda j: (0, j)),
                  pl.BlockSpec((1, tn), lambda j: (0, j))],
        out_specs=pl.BlockSpec((m, tn), lambda j: (0, j)),
        out_shape=jax.ShapeDtypeStruct((m, n), F32),
        compiler_params=_cparams(("parallel",)),
        name="ada_modulation",
    )(cond, w, b.reshape(1, n))


def _norm_mod_kernel(has_pos, *refs):
    if has_pos:
        x_ref, pos_ref, g_ref, sc_ref, sh_ref, h_ref, x0_ref = refs
    else:
        x_ref, g_ref, sc_ref, sh_ref, h_ref = refs
    x = x_ref[0]
    if has_pos:
        x = x + pos_ref[...]
        x0_ref[0] = x
    y = x * lax.rsqrt(jnp.mean(x * x, axis=-1, keepdims=True) + NORM_EPS) * g_ref[...]
    h_ref[0] = (y * (1.0 + sc_ref[0]) + sh_ref[0]).astype(BF16)


def norm_modulate(x, pos, g, scale, shift):
    bsz, seq, d = x.shape
    tl = 256
    per_b = scale.shape[0] > 1
    midx = (lambda b, i: (b, 0, 0)) if per_b else (lambda b, i: (0, 0, 0))
    xspec = pl.BlockSpec((1, tl, d), lambda b, i: (b, i, 0))
    in_specs = [xspec]
    args = [x]
    if pos is not None:
        in_specs.append(pl.BlockSpec((tl, d), lambda b, i: (i, 0)))
        args.append(pos)
    in_specs += [pl.BlockSpec((1, d), lambda b, i: (0, 0)),
                 pl.BlockSpec((1, 1, d), midx), pl.BlockSpec((1, 1, d), midx)]
    args += [g.reshape(1, d), scale, shift]
    out_shape = [jax.ShapeDtypeStruct((bsz, seq, d), BF16)]
    out_specs = [xspec]
    if pos is not None:
        out_shape.append(jax.ShapeDtypeStruct((bsz, seq, d), F32))
        out_specs.append(xspec)
    res = pl.pallas_call(
        functools.partial(_norm_mod_kernel, pos is not None),
        grid=(bsz, seq // tl),
        in_specs=in_specs, out_specs=out_specs, out_shape=out_shape,
        compiler_params=_cparams(("parallel", "parallel")),
        name="norm_modulate",
    )(*args)
    return (res[0], res[1]) if pos is not None else (res[0], x)


def _mm_kernel(a_ref, b_ref, o_ref):
    o_ref[...] = jnp.dot(a_ref[...], b_ref[...], preferred_element_type=F32)


def matmul_bf16(a, b, tm, tn):
    m, k = a.shape
    n = b.shape[1]
    return pl.pallas_call(
        _mm_kernel,
        grid=(m // tm, n // tn),
        in_specs=[pl.BlockSpec((tm, k), lambda i, j: (i, 0)),
                  pl.BlockSpec((k, tn), lambda i, j: (0, j))],
        out_specs=pl.BlockSpec((tm, tn), lambda i, j: (i, j)),
        out_shape=jax.ShapeDtypeStruct((m, n), F32),
        compiler_params=_cparams(("parallel", "arbitrary")),
        name="in_proj",
    )(a, b)


def _shifted(u, off):
    n = u.shape[0]
    row = lax.broadcasted_iota(jnp.int32, (n, 1), 0)
    rolled = pltpu.roll(u, (-off) % n, axis=0)
    valid = jnp.logical_and(row + off >= 0, row + off < n)
    return jnp.where(valid, rolled, 0.0)


def _conv_silu(u, w_ref, b_ref, cols):
    pad = SSD_CONV_W // 2
    acc = u * w_ref[pad:pad + 1, cols] + b_ref[:, cols]
    for j in range(SSD_CONV_W):
        if j != pad:
            acc = acc + _shifted(u, j - pad) * w_ref[j:j + 1, cols]
    return _silu(acc)


def _softplus(x):
    return jnp.maximum(x, 0.0) + jnp.log1p(jnp.exp(-jnp.abs(x)))


def _ssd_kernel(nc, zero_init, *refs):
    q = SSD_CHUNK
    (z_ref, x_ref, b_ref, c_ref, dt_ref, cwx, cbx, cwb, cbb, cwc, cbc,
     dtb_ref, alog_ref, d_ref, ng_ref) = refs[:15]
    refs = refs[15:]
    if not zero_init:
        h0f_ref, h0b_ref = refs[:2]
        refs = refs[2:]
    y_ref, hf_ref, hb_ref, xh_s, bm_s, cm_s, dt_s, a_s, at_s, y_s, h_s = refs

    slab = 256
    for cb in range(SSD_WIDTH // slab):
        cols = slice(cb * slab, (cb + 1) * slab)
        xh = _conv_silu(x_ref[0, :, cols], cwx, cbx, cols)
        for c in range(nc):
            rows = slice(c * q, (c + 1) * q)
            xh_s[c, :, cols] = xh[rows]
            y_s[c, :, cols] = xh[rows] * d_ref[:, cols]
    allc = slice(0, SSD_GROUPS * SSD_STATE)
    bm = _conv_silu(b_ref[0], cwb, cbb, allc)
    cm = _conv_silu(c_ref[0], cwc, cbc, allc)
    dt = _softplus(dt_ref[0] + dtb_ref[...])
    a = dt * (-jnp.exp(alog_ref[...]))
    ii = lax.broadcasted_iota(jnp.int32, (q, q), 0)
    jj = lax.broadcasted_iota(jnp.int32, (q, q), 1)
    eye = (ii == jj).astype(F32)
    lower = ii >= jj
    upper = ii <= jj
    first_half = jj < SSD_HEAD_DIM
    for c in range(nc):
        rows = slice(c * q, (c + 1) * q)
        for g in range(SSD_GROUPS):
            gcols = slice(g * SSD_STATE, (g + 1) * SSD_STATE)
            bm_s[c, gcols, :] = bm[rows, gcols].T
        cm_s[c] = cm[rows]
        dt_s[c] = dt[rows]
        a_s[c] = a[rows]
        at_s[c] = lax.dot_general(eye, a[rows], (((1,), (1,)), ((), ())),
                                  precision=HIGHEST, preferred_element_type=F32)

    for d in range(2):
        mask = lower if d == 0 else upper
        tri = mask.astype(F32)
        tri_t = (upper if d == 0 else lower).astype(F32)
        if zero_init:
            h_s[...] = jnp.zeros_like(h_s)
        else:
            h0_ref = h0f_ref if d == 0 else h0b_ref
            for p in range(SSD_HEADS // 2):
                h_s[p] = jnp.concatenate([h0_ref[0, 2 * p], h0_ref[0, 2 * p + 1]], axis=0).T

        def chunk_body(ci, carry, d=d, mask=mask, tri=tri, tri_t=tri_t):
            c = ci if d == 0 else nc - 1 - ci
            a_c = a_s[c]
            cum = jnp.dot(tri, a_c, precision=HIGHEST, preferred_element_type=F32)
            cum_t = jnp.dot(at_s[c], tri_t, precision=HIGHEST, preferred_element_type=F32)
            dt_c = dt_s[c]
            for g in range(SSD_GROUPS):
                gcols = slice(g * SSD_STATE, (g + 1) * SSD_STATE)
                bt = bm_s[c, gcols, :]
                cg = cm_s[c, :, gcols]
                gram = jnp.dot(cg.astype(BF16), bt.astype(BF16), preferred_element_type=F32)
                for pp in range(SSD_HEADS // SSD_GROUPS // 2):
                    p = g * (SSD_HEADS // SSD_GROUPS // 2) + pp
                    pcols = slice(p * LANES, (p + 1) * LANES)
                    xh_pair = xh_s[c, :, pcols]
                    hprev = h_s[p]
                    hprev_b = hprev.astype(BF16)
                    parts = []
                    for e in range(2):
                        col = d * SSD_HEADS + 2 * p + e
                        mine = first_half if e == 0 else jnp.logical_not(first_half)
                        cc = cum[:, col:col + 1]
                        cr = cum_t[col:col + 1, :]
                        dec = jnp.exp(jnp.where(mask, cc - cr, -jnp.inf))
                        m = (gram * dec).astype(BF16)
                        xdt = jnp.where(mine, xh_pair * dt_c[:, col:col + 1], 0.0).astype(BF16)
                        tot = cum[q - 1:q, col:col + 1] if d == 0 else cum[0:1, col:col + 1]
                        bd_t = (bt * jnp.exp(tot - cr)).astype(BF16)
                        cd = (cg * jnp.exp(cc)).astype(BF16)
                        s_c = jnp.dot(bd_t, xdt, preferred_element_type=F32)
                        y_off = jnp.dot(cd, hprev_b, preferred_element_type=F32)
                        y_diag = jnp.dot(m, xdt, preferred_element_type=F32)
                        parts.append((s_c, y_off, y_diag, jnp.exp(tot)))
                    (s0, yo0, yd0, g0), (s1, yo1, yd1, g1) = parts
                    h_s[p] = hprev * jnp.where(first_half, g0, g1) + (s0 + s1)
                    y_s[c, :, pcols] += (yd0 + yd1) + jnp.where(first_half, yo0, yo1)
            return carry

        lax.fori_loop(0, nc, chunk_body, 0)
        out_ref = hf_ref if d == 0 else hb_ref
        for p in range(SSD_HEADS // 2):
            tr = h_s[p].T
            out_ref[0, 2 * p] = tr[0:SSD_HEAD_DIM]
            out_ref[0, 2 * p + 1] = tr[SSD_HEAD_DIM:]

    gw = SSD_WIDTH // SSD_GROUPS
    for c in range(nc):
        rows = slice(c * q, (c + 1) * q)
        for g in range(SSD_GROUPS):
            cols = slice(g * gw, (g + 1) * gw)
            y = y_s[c, :, cols] * _silu(z_ref[0, rows, cols])
            y = y * lax.rsqrt(jnp.mean(y * y, axis=-1, keepdims=True) + NORM_EPS)
            y_ref[0, rows, cols] = (y * ng_ref[:, cols]).astype(BF16)


def ssd_mixer(proj, p, h0f, h0b):
    bsz, seq, _ = proj.shape
    nc = seq // SSD_CHUNK
    zero_init = h0f is None

    def col_spec(width, start):
        blk = start // width
        return pl.BlockSpec((1, seq, width), lambda b: (b, 0, blk))

    def full2(arr):
        return pl.BlockSpec(arr.shape, lambda b: (0, 0))

    st_spec = pl.BlockSpec((1, SSD_HEADS, SSD_HEAD_DIM, SSD_STATE), lambda b: (b, 0, 0, 0))
    bc = SSD_GROUPS * SSD_STATE
    small = [p["cw_x"], p["cb_x"], p["cw_b"], p["cb_b"], p["cw_c"], p["cb_c"],
             p["dt_bias"], p["a_log"], p["d_exp"], p["ssd_ng"]]
    in_specs = [col_spec(SSD_WIDTH, COL_Z), col_spec(SSD_WIDTH, COL_X), col_spec(bc, COL_B),
                col_spec(bc, COL_C), col_spec(LANES, COL_DT)] + [full2(s) for s in small]
    args = [proj] * 5 + small
    if not zero_init:
        in_specs += [st_spec, st_spec]
        args += [h0f, h0b]
    st_shape = jax.ShapeDtypeStruct((bsz, SSD_HEADS, SSD_HEAD_DIM, SSD_STATE), F32)
    q = SSD_CHUNK
    return pl.pallas_call(
        functools.partial(_ssd_kernel, nc, zero_init),
        grid=(bsz,),
        in_specs=in_specs,
        out_specs=[pl.BlockSpec((1, seq, SSD_WIDTH), lambda b: (b, 0, 0)), st_spec, st_spec],
        out_shape=[jax.ShapeDtypeStruct((bsz, seq, SSD_WIDTH), BF16), st_shape, st_shape],
        scratch_shapes=[pltpu.VMEM((nc, q, SSD_WIDTH), F32), pltpu.VMEM((nc, bc, q), F32),
                        pltpu.VMEM((nc, q, bc), F32), pltpu.VMEM((nc, q, LANES), F32),
                        pltpu.VMEM((nc, q, LANES), F32), pltpu.VMEM((nc, LANES, q), F32),
                        pltpu.VMEM((nc, q, SSD_WIDTH), F32),
                        pltpu.VMEM((SSD_HEADS // 2, SSD_STATE, 2 * SSD_HEAD_DIM), F32)],
        compiler_params=_cparams(("parallel",)),
        name="ssd_mixer",
    )(*args)


def _seg_sum(x):
    lane = lax.broadcasted_iota(jnp.int32, x.shape, 1)
    first = lane < RWKV_HEAD_DIM
    s0 = jnp.sum(jnp.where(first, x, 0.0), axis=-1, keepdims=True)
    s1 = jnp.sum(jnp.where(first, 0.0, x), axis=-1, keepdims=True)
    return jnp.where(first, s0, s1)


def _shift_mix(u, mu):
    return u + mu * (0.5 * (_shifted(u, -1) + _shifted(u, 1)) - u)


def _rwkv_prep_kernel(r_ref, k_ref, v_ref, lo_ref, mur, muk, muv, mul, wupf, wupb, aup, gup,
                      w0f, w0b, a0, kk_ref, ka_ref,
                      ro, ko, vo, wo, alo, beo, go):
    r = _shift_mix(r_ref[0], mur[...])
    k = _shift_mix(k_ref[0], muk[...])
    v = _shift_mix(v_ref[0], muv[...])
    lo = _shift_mix(lo_ref[0], mul[...])
    wdf = jnp.tanh(lo[:, 0:128]).astype(BF16)
    wdb = jnp.tanh(lo[:, 128:256]).astype(BF16)
    ad = lo[:, 256:384].astype(BF16)
    gd = jax.nn.sigmoid(lo[:, 384:640]).astype(BF16)

    def mm(x, w_ref):
        return jnp.dot(x, w_ref[...].astype(BF16), preferred_element_type=F32)

    wo[0, 0] = jnp.exp(-DECAY_SCALE * jax.nn.sigmoid(w0f[...] + mm(wdf, wupf)))
    wo[1, 0] = jnp.exp(-DECAY_SCALE * jax.nn.sigmoid(w0b[...] + mm(wdb, wupb)))
    a = jax.nn.sigmoid(a0[...] + mm(ad, aup))
    go[0] = mm(gd, gup)
    kk = k * kk_ref[...]
    kk = kk / jnp.maximum(jnp.sqrt(_seg_sum(kk * kk)), 1e-12)
    ro[0] = r
    ko[0] = k * (1.0 + (a - 1.0) * ka_ref[...])
    vo[0] = v
    alo[0] = -kk
    beo[0] = kk * a


def rwkv_prep(proj, p):
    bsz, seq, _ = proj.shape
    nblk = RWKV_WIDTH // LANES

    def col_spec(start):
        blk = start // LANES
        return pl.BlockSpec((1, seq, LANES), lambda b, j: (b, 0, blk + j))

    vec = pl.BlockSpec((1, LANES), lambda b, j: (0, j))

    def mat(rows):
        return pl.BlockSpec((rows, LANES), lambda b, j: (0, j))

    out_spec = pl.BlockSpec((1, seq, LANES), lambda b, j: (b, 0, j))
    out_sh = jax.ShapeDtypeStruct((bsz, seq, RWKV_WIDTH), F32)
    return pl.pallas_call(
        _rwkv_prep_kernel,
        grid=(bsz, nblk),
        in_specs=[col_spec(COL_R), col_spec(COL_K), col_spec(COL_V),
                  pl.BlockSpec((1, seq, LORA_W), lambda b, j: (b, 0, COL_LORA // LORA_W)),
                  vec, vec, vec, pl.BlockSpec((1, LORA_W), lambda b, j: (0, 0)),
                  mat(128), mat(128), mat(128), mat(256),
                  vec, vec, vec, vec, vec],
        out_specs=[out_spec] * 3 + [pl.BlockSpec((2, 1, seq, LANES), lambda b, j: (0, b, 0, j))]
        + [out_spec] * 3,
        out_shape=[out_sh] * 3 + [jax.ShapeDtypeStruct((2, bsz, seq, RWKV_WIDTH), F32)] + [out_sh] * 3,
        compiler_params=_cparams(("parallel", "arbitrary")),
        name="rwkv_prep",
    )(proj, proj, proj, proj, p["mu_r"], p["mu_k"], p["mu_v"], p["mu_lora"],
      p["w_up_f"], p["w_up_b"], p["a_up"], p["g_up"],
      p["w0_f"], p["w0_b"], p["a0"], p["k_k"], p["k_a"])


def _rwkv_scan_kernel(tb, zero_init, *refs):
    if zero_init:
        r_ref, w_ref, k_ref, al_ref, be_ref, v_ref, y_ref, st_ref, s_s = refs
    else:
        r_ref, w_ref, k_ref, al_ref, be_ref, v_ref, s0_ref, y_ref, st_ref, s_s = refs
    kd = RWKV_HEAD_DIM
    d = pl.program_id(0)
    step_i = pl.program_id(2)

    def time_index(i):
        return i + d * (tb - 1 - 2 * i)

    @pl.when(step_i == 0)
    def _():
        if zero_init:
            s_s[...] = jnp.zeros_like(s_s)
        else:
            s_s[...] = s0_ref[...]

    t0 = time_index(0)
    sa0 = jnp.zeros(s_s.shape[1:], F32)
    sa1 = jnp.zeros(s_s.shape[1:], F32)
    for kq in range(0, kd, 2):
        sa0 = sa0 + s_s[kq] * al_ref[t0, kq:kq + 1, :]
        sa1 = sa1 + s_s[kq + 1] * al_ref[t0, kq + 1:kq + 2, :]

    def step(i, sa):
        t = time_index(i)
        tn = time_index(jnp.minimum(i + 1, tb - 1))
        vt = v_ref[t]
        y = jnp.zeros_like(sa)
        sa_next = jnp.zeros_like(sa)
        for kq in range(kd):
            row = slice(kq, kq + 1)
            sk = s_s[kq] * w_ref[t, row, :] + sa * be_ref[t, row, :] + vt * k_ref[t, row, :]
            s_s[kq] = sk
            y = y + sk * r_ref[t, row, :]
            sa_next = sa_next + sk * al_ref[tn, row, :]
        y_ref[t] = y
        return sa_next

    lax.fori_loop(0, tb, step, sa0 + sa1)

    @pl.when(step_i == pl.num_programs(2) - 1)
    def _():
        st_ref[...] = s_s[...]


def rwkv_scan(r, w, k, al, be, v, s0):
    seq, kd, nch = r.shape
    vv = v.shape[1]
    tb = 32
    nblk = seq // tb
    zero_init = s0 is None

    def tblk(d, i):
        return i + d * (nblk - 1 - 2 * i)

    kspec = pl.BlockSpec((tb, kd, LANES), lambda d, g, i: (tblk(d, i), 0, g))
    wspec = pl.BlockSpec((None, tb, kd, LANES), lambda d, g, i: (d, tblk(d, i), 0, g))
    vspec = pl.BlockSpec((tb, vv, LANES), lambda d, g, i: (tblk(d, i), 0, g))
    yspec = pl.BlockSpec((None, tb, vv, LANES), lambda d, g, i: (d, tblk(d, i), 0, g))
    sspec = pl.BlockSpec((None, kd, vv, LANES), lambda d, g, i: (d, 0, 0, g))
    in_specs = [kspec, wspec, kspec, kspec, kspec, vspec]
    args = [r, w, k, al, be, v]
    if not zero_init:
        in_specs.append(sspec)
        args.append(s0)
    return pl.pallas_call(
        functools.partial(_rwkv_scan_kernel, tb, zero_init),
        grid=(2, nch // LANES, nblk),
        in_specs=in_specs,
        out_specs=[yspec, sspec],
        out_shape=[jax.ShapeDtypeStruct((2, seq, vv, nch), F32),
                   jax.ShapeDtypeStruct((2, kd, vv, nch), F32)],
        scratch_shapes=[pltpu.VMEM((kd, vv, LANES), F32)],
        compiler_params=_cparams(("parallel", "parallel", "arbitrary")),
        name="rwkv_scan",
    )(*args)


def _rwkv_post_kernel(y_ref, r_ref, k_ref, v_ref, g_ref, lng, lnb, rk, o_ref):
    inv = 1.0 / RWKV_HEAD_DIM
    for cb in range(RWKV_WIDTH // LANES):
        cols = slice(cb * LANES, (cb + 1) * LANES)
        y = y_ref[0, 0, :, cols] + y_ref[1, 0, :, cols]
        mean = _seg_sum(y) * inv
        yc = y - mean
        var = _seg_sum(yc * yc) * inv
        yn = yc * lax.rsqrt(var + GN_EPS) * lng[:, cols] + lnb[:, cols]
        bonus = _seg_sum(r_ref[0, :, cols] * k_ref[0, :, cols] * rk[:, cols]) * v_ref[0, :, cols]
        o_ref[0, :, cols] = ((yn + bonus) * g_ref[0, :, cols]).astype(BF16)


def rwkv_post(y2, r, k, v, g, p):
    _, bsz, seq, wd = y2.shape
    tl = 256
    spec = pl.BlockSpec((1, tl, wd), lambda b, i: (b, i, 0))
    vec = pl.BlockSpec((1, wd), lambda b, i: (0, 0))
    return pl.pallas_call(
        _rwkv_post_kernel,
        grid=(bsz, seq // tl),
        in_specs=[pl.BlockSpec((2, 1, tl, wd), lambda b, i: (0, b, i, 0))] + [spec] * 4 + [vec] * 3,
        out_specs=spec,
        out_shape=jax.ShapeDtypeStruct((bsz, seq, wd), BF16),
        compiler_params=_cparams(("parallel", "parallel")),
        name="rwkv_post",
    )(y2, r, k, v, g, p["ln_g"], p["ln_b"], p["r_k"])


def _split_bf16(x):
    hi = x.astype(BF16)
    return hi, (x - hi.astype(F32)).astype(BF16)


def _out_proj_kernel(ys_ref, yr_ref, wo_ref, x_ref, gm_ref, ng_ref, sc_ref, sh_ref, rwh_ref, rwl_ref,
                     x1_ref, h2_ref, lg_ref):
    acc = jnp.dot(ys_ref[0], wo_ref[0:SSD_WIDTH, :], preferred_element_type=F32)
    acc = acc + jnp.dot(yr_ref[0], wo_ref[SSD_WIDTH:, :], preferred_element_type=F32)
    x1 = x_ref[0] + gm_ref[0] * acc
    x1_ref[0] = x1
    hn = x1 * lax.rsqrt(jnp.mean(x1 * x1, axis=-1, keepdims=True) + NORM_EPS) * ng_ref[...]
    h2 = hn * (1.0 + sc_ref[0]) + sh_ref[0]
    hi, lo = _split_bf16(h2)
    h2_ref[0] = hi
    lg_ref[0] = (jnp.dot(hi, rwh_ref[...], preferred_element_type=F32)
                 + jnp.dot(lo, rwh_ref[...], preferred_element_type=F32)
                 + jnp.dot(hi, rwl_ref[...], preferred_element_type=F32))


def out_proj(y_ssd, y_rwkv, w_out, x, gate_m, norm_g, scale_f, shift_f, router_w):
    bsz, seq, d = x.shape
    tm = 256
    per_b = gate_m.shape[0] > 1
    midx = (lambda b, i: (b, 0, 0)) if per_b else (lambda b, i: (0, 0, 0))
    half = pl.BlockSpec((1, tm, d // 2), lambda b, i: (b, i, 0))
    full = pl.BlockSpec((1, tm, d), lambda b, i: (b, i, 0))
    mspec = pl.BlockSpec((1, 1, d), midx)
    return pl.pallas_call(
        _out_proj_kernel,
        grid=(bsz, seq // tm),
        in_specs=[half, half, pl.BlockSpec((d, d), lambda b, i: (0, 0)), full, mspec,
                  pl.BlockSpec((1, d), lambda b, i: (0, 0)), mspec, mspec,
                  pl.BlockSpec((d, LANES), lambda b, i: (0, 0)),
                  pl.BlockSpec((d, LANES), lambda b, i: (0, 0))],
        out_specs=[full, full, pl.BlockSpec((1, tm, LANES), lambda b, i: (b, i, 0))],
        out_shape=[jax.ShapeDtypeStruct((bsz, seq, d), F32), jax.ShapeDtypeStruct((bsz, seq, d), BF16),
                   jax.ShapeDtypeStruct((bsz, seq, LANES), F32)],
        compiler_params=_cparams(("parallel", "parallel")),
        name="out_proj",
    )(y_ssd, y_rwkv, w_out, x, gate_m, norm_g.reshape(1, d), scale_f, shift_f, *router_w)


def _route_kernel(nt, cap, lg_ref, selr_ref, posr_ref, selt_ref, post_ref, afft_ref, aff_s, pre_s):
    ne = N_EXPERTS
    lane = lax.broadcasted_iota(jnp.int32, (LANES, LANES), 1)
    sub = lax.broadcasted_iota(jnp.int32, (LANES, LANES), 0)
    upper_incl = (sub <= lane).astype(BF16)
    for i in range(nt):
        lg = jnp.where(lane < ne, lg_ref[i * LANES:(i + 1) * LANES, :], -jnp.inf)
        e = jnp.exp(lg - jnp.max(lg, axis=-1, keepdims=True))
        aff = e / jnp.sum(e, axis=-1, keepdims=True)
        afft_ref[i * LANES:(i + 1) * LANES, :] = aff
        aff_s[i] = aff.T[0:ne, :]
    aff3 = aff_s[...]

    def count(mask):
        s = jnp.sum(mask.astype(F32), axis=0, keepdims=True)
        return jnp.sum(s, axis=2, keepdims=True)

    def search(_, carry):
        lo, hi = carry
        mid = 0.5 * (lo + hi)
        ok = count(aff3 >= mid) >= cap
        return jnp.where(ok, mid, lo), jnp.where(ok, hi, mid)

    lo0 = jnp.zeros((1, ne, 1), F32)
    hi0 = jnp.full((1, ne, 1), 2.0, F32)
    lo, _ = lax.fori_loop(0, SEARCH_ITERS, search, (lo0, hi0))
    cand = jnp.where(aff3 >= lo, aff3, 4.0)
    thr = jnp.min(jnp.min(cand, axis=0, keepdims=True), axis=2, keepdims=True)
    gt = (aff3 > thr).astype(F32)
    eq = (aff3 == thr).astype(F32)
    need = cap - count(aff3 > thr)[0]

    def prefix_excl(m3):
        off = jnp.zeros((ne, 1), F32)
        for i in range(nt):
            inc = jnp.dot(m3[i].astype(BF16), upper_incl, preferred_element_type=F32)
            pre_s[i] = inc - m3[i] + off
            off = off + inc[:, LANES - 1:LANES]
        return pre_s[...]

    sel = jnp.maximum(gt, eq * (prefix_excl(eq) < need[None]).astype(F32))
    pos = prefix_excl(sel)
    selr_ref[...] = sel
    posr_ref[...] = pos
    zpad = jnp.zeros((LANES - ne, LANES), F32)
    for i in range(nt):
        rows = slice(i * LANES, (i + 1) * LANES)
        selt_ref[rows, :] = jnp.concatenate([sel[i], zpad], axis=0).T
        post_ref[rows, :] = jnp.concatenate([pos[i], zpad], axis=0).T


def route(logits, cap):
    n_tok = logits.shape[0]
    nt = n_tok // LANES
    row_sh = jax.ShapeDtypeStruct((nt, N_EXPERTS, LANES), F32)
    tm_sh = jax.ShapeDtypeStruct((n_tok, LANES), F32)
    return pl.pallas_call(
        functools.partial(_route_kernel, nt, cap),
        out_shape=[row_sh, row_sh, tm_sh, tm_sh, tm_sh],
        scratch_shapes=[pltpu.VMEM((nt, N_EXPERTS, LANES), F32), pltpu.VMEM((nt, N_EXPERTS, LANES), F32)],
        compiler_params=pltpu.CompilerParams(vmem_limit_bytes=VMEM_LIMIT),
        name="route",
    )(logits)


def _window_start(starts_ref, tile, e, cap, win):
    ps = starts_ref[tile * N_EXPERTS + e]
    return pl.multiple_of(jnp.minimum((ps // SUBLANES) * SUBLANES, cap - win), SUBLANES)


def _gather_kernel(cap, win, nsub, starts_ref, sel_ref, pos_ref, h_ref, o_ref, acc):
    e = pl.program_id(0)
    j = pl.program_id(1)

    @pl.when(j == 0)
    def _():
        acc[...] = jnp.zeros_like(acc)

    for s in range(0, nsub, 2):
        s0 = _window_start(starts_ref, j * nsub + s, e, cap, win)
        slot = (lax.broadcasted_iota(jnp.int32, (win, LANES), 0) + s0).astype(F32)
        halves = []
        for u in range(2):
            prow = pos_ref[s + u, pl.ds(e, 1), :]
            srow = sel_ref[s + u, pl.ds(e, 1), :]
            halves.append(jnp.where(jnp.logical_and(slot == prow, srow > 0.0), 1.0, 0.0).astype(BF16))
        onehot = jnp.concatenate(halves, axis=1)
        acc[pl.ds(s0, win), :] += jnp.dot(onehot, h_ref[s * LANES:(s + 2) * LANES, :],
                                         preferred_element_type=F32)

    @pl.when(j == pl.num_programs(1) - 1)
    def _():
        o_ref[0] = acc[...].astype(BF16)


def gather_tokens(starts, sel_r, pos_r, h, cap):
    n_tok, d = h.shape
    tt = min(1024, n_tok)
    nsub = tt // LANES
    win = min(2 * LANES + SUBLANES, cap)
    rspec = pl.BlockSpec((nsub, N_EXPERTS, LANES), lambda e, j, st: (j, 0, 0))
    return pl.pallas_call(
        functools.partial(_gather_kernel, cap, win, nsub),
        grid_spec=pltpu.PrefetchScalarGridSpec(
            num_scalar_prefetch=1,
            grid=(N_EXPERTS, n_tok // tt),
            in_specs=[rspec, rspec, pl.BlockSpec((tt, d), lambda e, j, st: (j, 0))],
            out_specs=pl.BlockSpec((1, cap, d), lambda e, j, st: (e, 0, 0)),
            scratch_shapes=[pltpu.VMEM((cap, d), F32)]),
        out_shape=jax.ShapeDtypeStruct((N_EXPERTS, cap, d), BF16),
        compiler_params=_cparams(("parallel", "arbitrary")),
        name="gather_tokens",
    )(starts, sel_r, pos_r, h)


def _ffn_kernel(n_sets, *refs):
    x_refs = refs[:n_sets]
    wg_ref, wu_ref, wd_ref = refs[n_sets:n_sets + 3]
    o_refs = refs[n_sets + 3:2 * n_sets + 3]
    accs = refs[2 * n_sets + 3:]
    f = pl.program_id(1)
    wg = wg_ref[0].astype(BF16)
    wu = wu_ref[0].astype(BF16)
    wd = wd_ref[0].astype(BF16)
    for x_ref, o_ref, acc in zip(x_refs, o_refs, accs):
        @pl.when(f == 0)
        def _():
            acc[...] = jnp.zeros_like(acc)

        x = x_ref[0]
        hg = jnp.dot(x, wg, preferred_element_type=F32)
        hu = jnp.dot(x, wu, preferred_element_type=F32)
        hid = (_silu(hg) * hu).astype(BF16)
        acc[...] += jnp.dot(hid, wd, preferred_element_type=F32)

        @pl.when(f == pl.num_programs(1) - 1)
        def _():
            o_ref[0] = acc[...].astype(BF16)


def expert_ffn(xs_sets, wg, wu, wd):
    ne, _, d = xs_sets[0].shape
    ff = wg.shape[2]
    tf = 256
    xspecs = [pl.BlockSpec((1, xs.shape[1], d), lambda e, f: (e, 0, 0)) for xs in xs_sets]
    return pl.pallas_call(
        functools.partial(_ffn_kernel, len(xs_sets)),
        grid=(ne, ff // tf),
        in_specs=xspecs + [pl.BlockSpec((1, d, tf), lambda e, f: (e, 0, f)),
                           pl.BlockSpec((1, d, tf), lambda e, f: (e, 0, f)),
                           pl.BlockSpec((1, tf, d), lambda e, f: (e, f, 0))],
        out_specs=xspecs,
        out_shape=[jax.ShapeDtypeStruct(xs.shape, BF16) for xs in xs_sets],
        scratch_shapes=[pltpu.VMEM(xs.shape[1:], F32) for xs in xs_sets],
        compiler_params=_cparams(("parallel", "arbitrary")),
        name="expert_ffn",
    )(*xs_sets, wg, wu, wd)


def _scatter_kernel(cap, win, nsub, starts_ref, sel_ref, pos_ref, aff_ref, ob_ref, y_ref):
    c = pl.program_id(0)
    e = pl.program_id(1)

    @pl.when(e == 0)
    def _():
        y_ref[...] = jnp.zeros_like(y_ref)

    mine = lax.broadcasted_iota(jnp.int32, (LANES, LANES), 1) == e
    for s in range(nsub):
        rows = slice(s * LANES, (s + 1) * LANES)

        def col(ref):
            return jnp.sum(jnp.where(mine, ref[rows, :], 0.0), axis=-1, keepdims=True)

        pcol, scol, gcol = col(pos_ref), col(sel_ref), col(aff_ref)
        s0 = _window_start(starts_ref, c * nsub + s, e, cap, win)
        slot = (lax.broadcasted_iota(jnp.int32, (LANES, win), 1) + s0).astype(F32)
        onehot = jnp.where(jnp.logical_and(slot == pcol, scol > 0.0), 1.0, 0.0).astype(BF16)
        y_ref[rows, :] += gcol * jnp.dot(onehot, ob_ref[0, pl.ds(s0, win), :],
                                         preferred_element_type=F32)


def scatter_combine(starts, sel_t, pos_t, aff_t, outbuf):
    n_tok = sel_t.shape[0]
    ne, cap, d = outbuf.shape
    chunk = min(2048, n_tok)
    nsub = chunk // LANES
    win = min(2 * LANES, cap)
    tspec = pl.BlockSpec((chunk, LANES), lambda c, e, st: (c, 0))
    return pl.pallas_call(
        functools.partial(_scatter_kernel, cap, win, nsub),
        grid_spec=pltpu.PrefetchScalarGridSpec(
            num_scalar_prefetch=1,
            grid=(n_tok // chunk, ne),
            in_specs=[tspec, tspec, tspec, pl.BlockSpec((1, cap, d), lambda c, e, st: (e, 0, 0))],
            out_specs=pl.BlockSpec((chunk, d), lambda c, e, st: (c, 0))),
        out_shape=jax.ShapeDtypeStruct((n_tok, d), F32),
        compiler_params=_cparams(("parallel", "arbitrary")),
        name="scatter_combine",
    )(starts, sel_t, pos_t, aff_t, outbuf)


def _final_kernel(x1_ref, y_ref, gf_ref, fg_ref, o_ref):
    x2 = x1_ref[0] + gf_ref[0] * y_ref[0]
    o_ref[0] = x2 * lax.rsqrt(jnp.mean(x2 * x2, axis=-1, keepdims=True) + NORM_EPS) * fg_ref[...]


def final_norm(x1, y, gate_f, final_g):
    bsz, seq, d = x1.shape
    tl = 256
    per_b = gate_f.shape[0] > 1
    midx = (lambda b, i: (b, 0, 0)) if per_b else (lambda b, i: (0, 0, 0))
    xspec = pl.BlockSpec((1, tl, d), lambda b, i: (b, i, 0))
    return pl.pallas_call(
        _final_kernel,
        grid=(bsz, seq // tl),
        in_specs=[xspec, xspec, pl.BlockSpec((1, 1, d), midx), pl.BlockSpec((1, d), lambda b, i: (0, 0))],
        out_specs=xspec,
        out_shape=jax.ShapeDtypeStruct((bsz, seq, d), F32),
        compiler_params=_cparams(("parallel", "parallel")),
        name="final_norm",
    )(x1, y, gate_f, final_g.reshape(1, d))


def _pad_cols(w, width):
    return jnp.pad(w, ((0, 0), (0, width - w.shape[1])))


def _pad_rows(w, rows):
    return jnp.pad(w, ((0, rows - w.shape[0]), (0, 0)))


def _relayout_columns(w):
    n_ssd = 2 * SSD_WIDTH + 2 * SSD_GROUPS * SSD_STATE + 2 * SSD_HEADS
    ssd, rw = w[:, :n_ssd], w[:, n_ssd:]
    o = 3 * RWKV_WIDTH
    parts = [ssd[:, :COL_DT], _pad_cols(ssd[:, COL_DT:], LANES), rw[:, :o],
             _pad_cols(rw[:, o:o + DECAY_LORA], LANES),
             _pad_cols(rw[:, o + DECAY_LORA:o + 2 * DECAY_LORA], LANES),
             _pad_cols(rw[:, o + 2 * DECAY_LORA:o + 2 * DECAY_LORA + ICLR_LORA], LANES),
             _pad_cols(rw[:, o + 2 * DECAY_LORA + ICLR_LORA:], 2 * LANES)]
    return jnp.concatenate(parts, axis=1)


def _layer_params(l, w_in, w_out, ssd_conv_w, ssd_conv_b, ssd_A_log, ssd_dt_bias, ssd_D, ssd_norm_g,
                  rwkv_mu, rwkv_w0, rwkv_w_up, rwkv_a0, rwkv_a_up, rwkv_g_up, rwkv_k_k, rwkv_k_a,
                  rwkv_r_k, rwkv_ln_g, rwkv_ln_b, router_w):
    row = lambda a: a.reshape(1, -1)
    n_ssd = 2 * SSD_WIDTH + 2 * SSD_GROUPS * SSD_STATE + 2 * SSD_HEADS
    mu_full = _relayout_columns(jnp.concatenate([jnp.zeros((1, n_ssd), F32), row(rwkv_mu[l])], axis=1))
    cw, cb = ssd_conv_w[l], row(ssd_conv_b[l])
    xe, be = SSD_WIDTH, SSD_WIDTH + SSD_GROUPS * SSD_STATE
    return dict(
        w_in=_relayout_columns(w_in[l]).astype(BF16),
        w_out=w_out[l].astype(BF16),
        cw_x=cw[:, :xe], cb_x=cb[:, :xe], cw_b=cw[:, xe:be], cb_b=cb[:, xe:be],
        cw_c=cw[:, be:], cb_c=cb[:, be:],
        dt_bias=_pad_cols(row(ssd_dt_bias[l]), LANES), a_log=_pad_cols(row(ssd_A_log[l]), LANES),
        d_exp=row(jnp.repeat(ssd_D[l], SSD_HEAD_DIM)), ssd_ng=row(ssd_norm_g[l]),
        mu_r=mu_full[:, COL_R:COL_K], mu_k=mu_full[:, COL_K:COL_V], mu_v=mu_full[:, COL_V:COL_LORA],
        mu_lora=mu_full[:, COL_LORA:],
        w_up_f=_pad_rows(rwkv_w_up[l, 0], LANES), w_up_b=_pad_rows(rwkv_w_up[l, 1], LANES),
        a_up=_pad_rows(rwkv_a_up[l], LANES), g_up=_pad_rows(rwkv_g_up[l], 2 * LANES),
        w0_f=row(rwkv_w0[l, 0]), w0_b=row(rwkv_w0[l, 1]), a0=row(rwkv_a0[l]),
        k_k=row(rwkv_k_k[l]), k_a=row(rwkv_k_a[l]), r_k=row(rwkv_r_k[l]),
        ln_g=row(rwkv_ln_g[l]), ln_b=row(rwkv_ln_b[l]),
        router_w=tuple(_pad_cols(t, LANES) for t in _split_bf16(router_w[l])),
    )


def _to_chain(x):
    lead = x.shape[:-3]
    bsz, seq, _ = x.shape[-3:]
    n = len(lead)
    x = x.reshape(lead + (bsz, seq, RWKV_HEADS, RWKV_HEAD_DIM))
    perm = tuple(range(n)) + (n + 1, n + 3, n, n + 2)
    return jnp.transpose(x, perm).reshape(lead + (seq, RWKV_HEAD_DIM, bsz * RWKV_HEADS))


def _from_chain(y, bsz):
    lead = y.shape[:-3]
    seq = y.shape[-3]
    n = len(lead)
    y = y.reshape(lead + (seq, RWKV_HEAD_DIM, bsz, RWKV_HEADS))
    perm = tuple(range(n)) + (n + 2, n, n + 3, n + 1)
    return jnp.transpose(y, perm).reshape(lead + (bsz, seq, RWKV_WIDTH))


def _state_to_chain(s):
    return jnp.transpose(s, (0, 4, 3, 1, 2)).reshape(2, RWKV_HEAD_DIM, RWKV_HEAD_DIM, -1)


def _state_from_chain(s, bsz):
    return jnp.transpose(s.reshape(2, RWKV_HEAD_DIM, RWKV_HEAD_DIM, bsz, RWKV_HEADS), (0, 3, 4, 2, 1))


def rwkv_mixer_scan(r, k, v, w2, al, be, s0):
    bsz, seq, _ = r.shape
    nh, hd = RWKV_HEADS, RWKV_HEAD_DIM
    nch = bsz * nh
    rep = max(1, LANES // nch)
    assert (nch * rep) % LANES == 0
    if rep == 1:
        rc, kc, vc, alc, bec, wc = (_to_chain(t) for t in (r, k, v, al, be, w2))
        s0c = None if s0 is None else _state_to_chain(s0)
        y, st = rwkv_scan(rc, wc, kc, alc, bec, vc, s0c)
        return _from_chain(y, bsz), _state_from_chain(st, bsz)

    vq = hd // rep

    def keys_to_chain(x):
        lead = x.shape[:-3]
        n = len(lead)
        x = jnp.broadcast_to(x.reshape(lead + (1, bsz, seq, nh, hd)), lead + (rep, bsz, seq, nh, hd))
        perm = tuple(range(n)) + (n + 2, n + 4, n, n + 1, n + 3)
        return jnp.transpose(x, perm).reshape(lead + (seq, hd, LANES))

    rc, kc, alc, bec, wc = (keys_to_chain(t) for t in (r, k, al, be, w2))
    vc = jnp.transpose(v.reshape(bsz, seq, nh, rep, vq), (1, 4, 3, 0, 2)).reshape(seq, vq, LANES)
    s0c = None
    if s0 is not None:
        s0c = jnp.transpose(s0.reshape(2, bsz, nh, rep, vq, hd), (0, 5, 4, 3, 1, 2)).reshape(
            2, hd, vq, LANES)
    y, st = rwkv_scan(rc, wc, kc, alc, bec, vc, s0c)
    y = jnp.transpose(y.reshape(2, seq, vq, rep, bsz, nh), (0, 4, 1, 5, 3, 2)).reshape(
        2, bsz, seq, RWKV_WIDTH)
    st = jnp.transpose(st.reshape(2, hd, vq, rep, bsz, nh), (0, 4, 5, 3, 2, 1)).reshape(
        2, bsz, nh, hd, hd)
    return y, st


def _grid_pos_embed(n_tokens, dim):
    rows = n_tokens // GRID_W
    row = jnp.repeat(jnp.arange(rows, dtype=F32), GRID_W)
    col = jnp.tile(jnp.arange(GRID_W, dtype=F32), rows)
    quarter = dim // 4
    freqs = jnp.exp(jnp.arange(quarter, dtype=F32) * (-math.log(POS_BASE) / quarter))

    def axis_embed(pos):
        ang = pos[:, None] * freqs[None, :]
        return jnp.concatenate([jnp.sin(ang), jnp.cos(ang)], axis=-1)

    return jnp.concatenate([axis_embed(row), axis_embed(col)], axis=-1)


def layer_to_dispatch(x, pos, mod, p, norm_mix_g, norm_ffn_g, states):
    bsz, seq, d = x.shape
    shift_m, scale_m, gate_m, shift_f, scale_f, gate_f = mod
    h, x0 = norm_modulate(x, pos, norm_mix_g, scale_m, shift_m)
    n_tok = bsz * seq
    proj = matmul_bf16(h.reshape(n_tok, d), p["w_in"], min(1024, n_tok), 640).reshape(bsz, seq, N_PROJ)
    h0f, h0b, s0f, s0b = states
    y_ssd, hf, hb = ssd_mixer(proj, p, h0f, h0b)
    r, k, v, w2, al, be, g = rwkv_prep(proj, p)
    s0 = None if s0f is None else jnp.stack([s0f, s0b], axis=0)
    y2, st = rwkv_mixer_scan(r, k, v, w2, al, be, s0)
    y_rwkv = rwkv_post(y2, r, k, v, g, p)
    x1, h2, logits = out_proj(y_ssd, y_rwkv, p["w_out"], x0, gate_m, norm_ffn_g, scale_f, shift_f,
                              p["router_w"])
    cap = EC_CAPACITY * n_tok // N_EXPERTS
    sel_r, pos_r, sel_t, pos_t, aff_t = route(logits.reshape(n_tok, LANES), cap)
    starts = pos_r[:, :, 0].astype(jnp.int32).reshape(-1)
    xs = gather_tokens(starts, sel_r, pos_r, h2.reshape(n_tok, d), cap)
    routed = dict(xs=xs, starts=starts, sel_t=sel_t, pos_t=pos_t, aff_t=aff_t, x1=x1, gate_f=gate_f)
    return routed, (hf, hb, st[0], st[1])


def combine_and_finish(routed, outbuf, final_g):
    x1 = routed["x1"]
    y_ffn = scatter_combine(routed["starts"], routed["sel_t"], routed["pos_t"], routed["aff_t"], outbuf)
    return final_norm(x1, y_ffn.reshape(x1.shape), routed["gate_f"], final_g)


def kernel(x_prompt, x_sample, state_ssd_fwd, state_ssd_bwd, state_rwkv_fwd, state_rwkv_bwd, c, c_ctx,
           w_ada, b_ada, norm_mix_g, norm_ffn_g, w_in, w_out, ssd_conv_w, ssd_conv_b, ssd_A_log,
           ssd_dt_bias, ssd_D, ssd_norm_g, rwkv_mu, rwkv_w0, rwkv_w_up, rwkv_a0, rwkv_a_up, rwkv_g_up,
           rwkv_k_k, rwkv_k_a, rwkv_r_k, rwkv_ln_g, rwkv_ln_b, router_w, exp_w_gate, exp_w_up,
           exp_w_down, final_norm_g):
    depth = w_in.shape[0]
    assert depth == 1, "the final norm runs right after the single layer's FFN residual"
    l = 0
    dec_b = x_sample.shape[0]
    p = _layer_params(l, w_in, w_out, ssd_conv_w, ssd_conv_b, ssd_A_log, ssd_dt_bias, ssd_D, ssd_norm_g,
                      rwkv_mu, rwkv_w0, rwkv_w_up, rwkv_a0, rwkv_a_up, rwkv_g_up, rwkv_k_k, rwkv_k_a,
                      rwkv_r_k, rwkv_ln_g, rwkv_ln_b, router_w)
    cond = jnp.concatenate([c_ctx[None, :], c, jnp.zeros((16 - 1 - dec_b, D_MODEL), F32)], axis=0)
    mod = ada_modulation(cond, w_ada[l], b_ada[l]).reshape(16, N_MOD, 1, D_MODEL)
    mod_ctx = [mod[0:1, i] for i in range(N_MOD)]
    mod_lat = [mod[1:1 + dec_b, i] for i in range(N_MOD)]

    routed_ctx, st = layer_to_dispatch(x_prompt, None, mod_ctx, p, norm_mix_g[l], norm_ffn_g[l],
                                       (None, None, None, None))
    pos = _grid_pos_embed(x_sample.shape[1], D_MODEL)
    cached = (state_ssd_fwd[:, l], state_ssd_bwd[:, l], state_rwkv_fwd[:, l], state_rwkv_bwd[:, l])
    routed_lat, _ = layer_to_dispatch(x_sample, pos, mod_lat, p, norm_mix_g[l], norm_ffn_g[l], cached)
    out_ctx, out_lat = expert_ffn([routed_ctx["xs"], routed_lat["xs"]],
                                  exp_w_gate[l], exp_w_up[l], exp_w_down[l])
    y_prompt = combine_and_finish(routed_ctx, out_ctx, final_norm_g)
    y_sample = combine_and_finish(routed_lat, out_lat, final_norm_g)
    return (y_prompt, y_sample, st[0][:, None], st[1][:, None], st[2][:, None], st[3][:, None])
```

```python
import functools
import math

import jax
import jax.numpy as jnp
from jax import lax
from jax.experimental import pallas as pl
from jax.experimental.pallas import tpu as pltpu

F32 = jnp.float32
BF16 = jnp.bfloat16
HIGHEST = lax.Precision.HIGHEST

D_MODEL = 2048
GRID_W = 64
SSD_WIDTH = 1024
SSD_HEAD_DIM = 64
SSD_HEADS = 16
SSD_GROUPS = 2
SSD_STATE = 128
SSD_CONV_W = 5
SSD_CHUNK = 128
RWKV_WIDTH = 1024
RWKV_HEAD_DIM = 64
RWKV_HEADS = 16
DECAY_LORA = 64
ICLR_LORA = 64
GATE_LORA = 160
N_EXPERTS = 16
EC_CAPACITY = 2
EXPERT_FF = 1024
N_MOD = 6
NORM_EPS = 1e-6
GN_EPS = 64e-5
DECAY_SCALE = 0.606531
POS_BASE = 10000.0
SEARCH_ITERS = 48

LANES = 128
SUBLANES = 8
GATHER_ROWS = 64

COL_Z = 0
COL_X = 1024
COL_R = 2048
COL_K = 3072
COL_V = 4096
COL_B = 5120
COL_C = 5376
COL_DT = 5632
COL_LORA = 5760
PREP_BLOCK_ELEMS = 256 * 1024
LORA_W = 640
N_PROJ = COL_LORA + LORA_W

VMEM_LIMIT = 56 * 1024 * 1024


def _cparams(sem):
    return pltpu.CompilerParams(dimension_semantics=sem, vmem_limit_bytes=VMEM_LIMIT)


def _silu(x):
    return x * jax.nn.sigmoid(x)


def _ada_kernel(c_ref, w_ref, b_ref, o_ref):
    s = _silu(c_ref[...]).astype(BF16)
    o_ref[...] = jnp.dot(s, w_ref[...].astype(BF16), preferred_element_type=F32) + b_ref[...]


def ada_modulation(cond, w, b):
    m, d = cond.shape
    n = w.shape[1]
    tn = 1024
    return pl.pallas_call(
        _ada_kernel,
        grid=(n // tn,),
        in_specs=[pl.BlockSpec((m, d), lambda j: (0, 0)),
                  pl.BlockSpec((d, tn), lambda j: (0, j)),
                  pl.BlockSpec((1, tn), lambda j: (0, j))],
        out_specs=pl.BlockSpec((m, tn), lambda j: (0, j)),
        out_shape=jax.ShapeDtypeStruct((m, n), F32),
        compiler_params=_cparams(("parallel",)),
        name="ada_modulation",
    )(cond, w, b.reshape(1, n))


def _norm_mod_kernel(has_pos, *refs):
    if has_pos:
        x_ref, pos_ref, g_ref, sc_ref, sh_ref, h_ref, x0_ref = refs
    else:
        x_ref, g_ref, sc_ref, sh_ref, h_ref = refs
    x = x_ref[0]
    if has_pos:
        x = x + pos_ref[...]
        x0_ref[0] = x
    y = x * lax.rsqrt(jnp.mean(x * x, axis=-1, keepdims=True) + NORM_EPS) * g_ref[...]
    h_ref[0] = (y * (1.0 + sc_ref[0]) + sh_ref[0]).astype(BF16)


def norm_modulate(x, pos, g, scale, shift):
    bsz, seq, d = x.shape
    tl = 256
    per_b = scale.shape[0] > 1
    midx = (lambda b, i: (b, 0, 0)) if per_b else (lambda b, i: (0, 0, 0))
    xspec = pl.BlockSpec((1, tl, d), lambda b, i: (b, i, 0))
    in_specs = [xspec]
    args = [x]
    if pos is not None:
        in_specs.append(pl.BlockSpec((tl, d), lambda b, i: (i, 0)))
        args.append(pos)
    in_specs += [pl.BlockSpec((1, d), lambda b, i: (0, 0)),
                 pl.BlockSpec((1, 1, d), midx), pl.BlockSpec((1, 1, d), midx)]
    args += [g.reshape(1, d), scale, shift]
    out_shape = [jax.ShapeDtypeStruct((bsz, seq, d), BF16)]
    out_specs = [xspec]
    if pos is not None:
        out_shape.append(jax.ShapeDtypeStruct((bsz, seq, d), F32))
        out_specs.append(xspec)
    res = pl.pallas_call(
        functools.partial(_norm_mod_kernel, pos is not None),
        grid=(bsz, seq // tl),
        in_specs=in_specs, out_specs=out_specs, out_shape=out_shape,
        compiler_params=_cparams(("parallel", "parallel")),
        name="norm_modulate",
    )(*args)
    return (res[0], res[1]) if pos is not None else (res[0], x)


def _mm_kernel(a_ref, b_ref, o_ref):
    o_ref[...] = jnp.dot(a_ref[...], b_ref[...], preferred_element_type=F32)


def matmul_bf16(a, b, tm, tn):
    m, k = a.shape
    n = b.shape[1]
    return pl.pallas_call(
        _mm_kernel,
        grid=(m // tm, n // tn),
        in_specs=[pl.BlockSpec((tm, k), lambda i, j: (i, 0)),
                  pl.BlockSpec((k, tn), lambda i, j: (0, j))],
        out_specs=pl.BlockSpec((tm, tn), lambda i, j: (i, j)),
        out_shape=jax.ShapeDtypeStruct((m, n), F32),
        compiler_params=_cparams(("parallel", "arbitrary")),
        name="in_proj",
    )(a, b)


def _shifted(u, off):
    n = u.shape[0]
    row = lax.broadcasted_iota(jnp.int32, (n, 1), 0)
    rolled = pltpu.roll(u, (-off) % n, axis=0)
    valid = jnp.logical_and(row + off >= 0, row + off < n)
    return jnp.where(valid, rolled, 0.0)


def _conv_silu(u, w_ref, b_ref, cols):
    pad = SSD_CONV_W // 2
    acc = u * w_ref[pad:pad + 1, cols] + b_ref[:, cols]
    for j in range(SSD_CONV_W):
        if j != pad:
            acc = acc + _shifted(u, j - pad) * w_ref[j:j + 1, cols]
    return _silu(acc)


def _softplus(x):
    return jnp.maximum(x, 0.0) + jnp.log1p(jnp.exp(-jnp.abs(x)))


def _ssd_kernel(nc, zero_init, *refs):
    q = SSD_CHUNK
    (z_ref, x_ref, b_ref, c_ref, dt_ref, cwx, cbx, cwb, cbb, cwc, cbc,
     dtb_ref, alog_ref, d_ref, ng_ref) = refs[:15]
    refs = refs[15:]
    if not zero_init:
        h0f_ref, h0b_ref = refs[:2]
        refs = refs[2:]
    y_ref, hf_ref, hb_ref, xh_s, bm_s, cm_s, dt_s, a_s, at_s, y_s, h_s = refs

    slab = 256
    for cb in range(SSD_WIDTH // slab):
        cols = slice(cb * slab, (cb + 1) * slab)
        xh = _conv_silu(x_ref[0, :, cols], cwx, cbx, cols)
        for c in range(nc):
            rows = slice(c * q, (c + 1) * q)
            xh_s[c, :, cols] = xh[rows]
            y_s[c, :, cols] = xh[rows] * d_ref[:, cols]
    allc = slice(0, SSD_GROUPS * SSD_STATE)
    bm = _conv_silu(b_ref[0], cwb, cbb, allc)
    cm = _conv_silu(c_ref[0], cwc, cbc, allc)
    dt = _softplus(dt_ref[0] + dtb_ref[...])
    a = dt * (-jnp.exp(alog_ref[...]))
    ii = lax.broadcasted_iota(jnp.int32, (q, q), 0)
    jj = lax.broadcasted_iota(jnp.int32, (q, q), 1)
    eye = (ii == jj).astype(F32)
    lower = ii >= jj
    upper = ii <= jj
    first_half = jj < SSD_HEAD_DIM
    for c in range(nc):
        rows = slice(c * q, (c + 1) * q)
        for g in range(SSD_GROUPS):
            gcols = slice(g * SSD_STATE, (g + 1) * SSD_STATE)
            bm_s[c, gcols, :] = bm[rows, gcols].T
        cm_s[c] = cm[rows]
        dt_s[c] = dt[rows]
        a_s[c] = a[rows]
        at_s[c] = lax.dot_general(eye, a[rows], (((1,), (1,)), ((), ())),
                                  precision=HIGHEST, preferred_element_type=F32)

    for d in range(2):
        mask = lower if d == 0 else upper
        tri = mask.astype(F32)
        tri_t = (upper if d == 0 else lower).astype(F32)
        if zero_init:
            h_s[...] = jnp.zeros_like(h_s)
        else:
            h0_ref = h0f_ref if d == 0 else h0b_ref
            for p in range(SSD_HEADS // 2):
                h_s[p] = jnp.concatenate([h0_ref[0, 2 * p], h0_ref[0, 2 * p + 1]], axis=0).T

        def chunk_body(ci, carry, d=d, mask=mask, tri=tri, tri_t=tri_t):
            c = ci if d == 0 else nc - 1 - ci
            a_c = a_s[c]
            cum = jnp.dot(tri, a_c, precision=HIGHEST, preferred_element_type=F32)
            cum_t = jnp.dot(at_s[c], tri_t, precision=HIGHEST, preferred_element_type=F32)
            dt_c = dt_s[c]
            for g in range(SSD_GROUPS):
                gcols = slice(g * SSD_STATE, (g + 1) * SSD_STATE)
                bt = bm_s[c, gcols, :]
                cg = cm_s[c, :, gcols]
                gram = jnp.dot(cg.astype(BF16), bt.astype(BF16), preferred_element_type=F32)
                for pp in range(SSD_HEADS // SSD_GROUPS // 2):
                    p = g * (SSD_HEADS // SSD_GROUPS // 2) + pp
                    pcols = slice(p * LANES, (p + 1) * LANES)
                    xh_pair = xh_s[c, :, pcols]
                    hprev = h_s[p]
                    hprev_b = hprev.astype(BF16)
                    parts = []
                    for e in range(2):
                        col = d * SSD_HEADS + 2 * p + e
                        mine = first_half if e == 0 else jnp.logical_not(first_half)
                        cc = cum[:, col:col + 1]
                        cr = cum_t[col:col + 1, :]
                        dec = jnp.exp(jnp.where(mask, cc - cr, -jnp.inf))
                        m = (gram * dec).astype(BF16)
                        xdt = jnp.where(mine, xh_pair * dt_c[:, col:col + 1], 0.0).astype(BF16)
                        tot = cum[q - 1:q, col:col + 1] if d == 0 else cum[0:1, col:col + 1]
                        bd_t = (bt * jnp.exp(tot - cr)).astype(BF16)
                        cd = (cg * jnp.exp(cc)).astype(BF16)
                        s_c = jnp.dot(bd_t, xdt, preferred_element_type=F32)
                        y_off = jnp.dot(cd, hprev_b, preferred_element_type=F32)
                        y_diag = jnp.dot(m, xdt, preferred_element_type=F32)
                        parts.append((s_c, y_off, y_diag, jnp.exp(tot)))
                    (s0, yo0, yd0, g0), (s1, yo1, yd1, g1) = parts
                    h_s[p] = hprev * jnp.where(first_half, g0, g1) + (s0 + s1)
                    y_s[c, :, pcols] += (yd0 + yd1) + jnp.where(first_half, yo0, yo1)
            return carry

        lax.fori_loop(0, nc, chunk_body, 0)
        out_ref = hf_ref if d == 0 else hb_ref
        for p in range(SSD_HEADS // 2):
            tr = h_s[p].T
            out_ref[0, 2 * p] = tr[0:SSD_HEAD_DIM]
            out_ref[0, 2 * p + 1] = tr[SSD_HEAD_DIM:]

    gw = SSD_WIDTH // SSD_GROUPS
    for c in range(nc):
        rows = slice(c * q, (c + 1) * q)
        for g in range(SSD_GROUPS):
            cols = slice(g * gw, (g + 1) * gw)
            y = y_s[c, :, cols] * _silu(z_ref[0, rows, cols])
            y = y * lax.rsqrt(jnp.mean(y * y, axis=-1, keepdims=True) + NORM_EPS)
            y_ref[0, rows, cols] = (y * ng_ref[:, cols]).astype(BF16)


def ssd_mixer(proj, p, h0f, h0b):
    bsz, seq, _ = proj.shape
    nc = seq // SSD_CHUNK
    zero_init = h0f is None

    def col_spec(width, start):
        blk = start // width
        return pl.BlockSpec((1, seq, width), lambda b: (b, 0, blk))

    def full2(arr):
        return pl.BlockSpec(arr.shape, lambda b: (0, 0))

    st_spec = pl.BlockSpec((1, SSD_HEADS, SSD_HEAD_DIM, SSD_STATE), lambda b: (b, 0, 0, 0))
    bc = SSD_GROUPS * SSD_STATE
    small = [p["cw_x"], p["cb_x"], p["cw_b"], p["cb_b"], p["cw_c"], p["cb_c"],
             p["dt_bias"], p["a_log"], p["d_exp"], p["ssd_ng"]]
    in_specs = [col_spec(SSD_WIDTH, COL_Z), col_spec(SSD_WIDTH, COL_X), col_spec(bc, COL_B),
                col_spec(bc, COL_C), col_spec(LANES, COL_DT)] + [full2(s) for s in small]
    args = [proj] * 5 + small
    if not zero_init:
        in_specs += [st_spec, st_spec]
        args += [h0f, h0b]
    st_shape = jax.ShapeDtypeStruct((bsz, SSD_HEADS, SSD_HEAD_DIM, SSD_STATE), F32)
    q = SSD_CHUNK
    return pl.pallas_call(
        functools.partial(_ssd_kernel, nc, zero_init),
        grid=(bsz,),
        in_specs=in_specs,
        out_specs=[pl.BlockSpec((1, seq, SSD_WIDTH), lambda b: (b, 0, 0)), st_spec, st_spec],
        out_shape=[jax.ShapeDtypeStruct((bsz, seq, SSD_WIDTH), BF16), st_shape, st_shape],
        scratch_shapes=[pltpu.VMEM((nc, q, SSD_WIDTH), F32), pltpu.VMEM((nc, bc, q), F32),
                        pltpu.VMEM((nc, q, bc), F32), pltpu.VMEM((nc, q, LANES), F32),
                        pltpu.VMEM((nc, q, LANES), F32), pltpu.VMEM((nc, LANES, q), F32),
                        pltpu.VMEM((nc, q, SSD_WIDTH), F32),
                        pltpu.VMEM((SSD_HEADS // 2, SSD_STATE, 2 * SSD_HEAD_DIM), F32)],
        compiler_params=_cparams(("parallel",)),
        name="ssd_mixer",
    )(*args)


def _seg_sum(x):
    lane = lax.broadcasted_iota(jnp.int32, x.shape, 1)
    first = lane < RWKV_HEAD_DIM
    s0 = jnp.sum(jnp.where(first, x, 0.0), axis=-1, keepdims=True)
    s1 = jnp.sum(jnp.where(first, 0.0, x), axis=-1, keepdims=True)
    return jnp.where(first, s0, s1)


def _shift_mix(u, mu):
    return u + mu * (0.5 * (_shifted(u, -1) + _shifted(u, 1)) - u)


def _rwkv_prep_kernel(ncb, r_ref, k_ref, v_ref, lo_ref, mur, muk, muv, mul, wupf, wupb, aup, gup,
                      w0f, w0b, a0, kk_ref, ka_ref,
                      ro, ko, vo, wo, alo, beo, go):
    lo = _shift_mix(lo_ref[0], mul[...])
    wdf = jnp.tanh(lo[:, 0:128]).astype(BF16)
    wdb = jnp.tanh(lo[:, 128:256]).astype(BF16)
    ad = lo[:, 256:384].astype(BF16)
    gd = jax.nn.sigmoid(lo[:, 384:640]).astype(BF16)
    for cb in range(ncb):
        cols = slice(cb * LANES, (cb + 1) * LANES)

        def mm(x, w_ref, cols=cols):
            return jnp.dot(x, w_ref[:, cols].astype(BF16), preferred_element_type=F32)

        r = _shift_mix(r_ref[0, :, cols], mur[:, cols])
        k = _shift_mix(k_ref[0, :, cols], muk[:, cols])
        v = _shift_mix(v_ref[0, :, cols], muv[:, cols])
        wo[0, 0, :, cols] = jnp.exp(-DECAY_SCALE * jax.nn.sigmoid(w0f[:, cols] + mm(wdf, wupf)))
        wo[1, 0, :, cols] = jnp.exp(-DECAY_SCALE * jax.nn.sigmoid(w0b[:, cols] + mm(wdb, wupb)))
        a = jax.nn.sigmoid(a0[:, cols] + mm(ad, aup))
        go[0, :, cols] = mm(gd, gup)
        kk = k * kk_ref[:, cols]
        kk = kk / jnp.maximum(jnp.sqrt(_seg_sum(kk * kk)), 1e-12)
        ro[0, :, cols] = r
        ko[0, :, cols] = k * (1.0 + (a - 1.0) * ka_ref[:, cols])
        vo[0, :, cols] = v
        alo[0, :, cols] = -kk
        beo[0, :, cols] = kk * a


def rwkv_prep(proj, p):
    bsz, seq, _ = proj.shape
    cw = min(RWKV_WIDTH, PREP_BLOCK_ELEMS // seq)
    nblk = RWKV_WIDTH // cw

    def col_spec(start):
        blk = start // cw
        return pl.BlockSpec((1, seq, cw), lambda b, j: (b, 0, blk + j))

    vec = pl.BlockSpec((1, cw), lambda b, j: (0, j))

    def mat(rows):
        return pl.BlockSpec((rows, cw), lambda b, j: (0, j))

    out_spec = pl.BlockSpec((1, seq, cw), lambda b, j: (b, 0, j))
    out_sh = jax.ShapeDtypeStruct((bsz, seq, RWKV_WIDTH), F32)
    return pl.pallas_call(
        functools.partial(_rwkv_prep_kernel, cw // LANES),
        grid=(bsz, nblk),
        in_specs=[col_spec(COL_R), col_spec(COL_K), col_spec(COL_V),
                  pl.BlockSpec((1, seq, LORA_W), lambda b, j: (b, 0, COL_LORA // LORA_W)),
                  vec, vec, vec, pl.BlockSpec((1, LORA_W), lambda b, j: (0, 0)),
                  mat(128), mat(128), mat(128), mat(256),
                  vec, vec, vec, vec, vec],
        out_specs=[out_spec] * 3 + [pl.BlockSpec((2, 1, seq, cw), lambda b, j: (0, b, 0, j))]
        + [out_spec] * 3,
        out_shape=[out_sh] * 3 + [jax.ShapeDtypeStruct((2, bsz, seq, RWKV_WIDTH), F32)] + [out_sh] * 3,
        compiler_params=_cparams(("parallel", "arbitrary")),
        name="rwkv_prep",
    )(proj, proj, proj, proj, p["mu_r"], p["mu_k"], p["mu_v"], p["mu_lora"],
      p["w_up_f"], p["w_up_b"], p["a_up"], p["g_up"],
      p["w0_f"], p["w0_b"], p["a0"], p["k_k"], p["k_a"])


def _rwkv_scan_kernel(tb, zero_init, *refs):
    if zero_init:
        r_ref, w_ref, k_ref, al_ref, be_ref, v_ref, y_ref, st_ref, s_s = refs
    else:
        r_ref, w_ref, k_ref, al_ref, be_ref, v_ref, s0_ref, y_ref, st_ref, s_s = refs
    kd = RWKV_HEAD_DIM
    d = pl.program_id(0)
    step_i = pl.program_id(2)

    def time_index(i):
        return i + d * (tb - 1 - 2 * i)

    @pl.when(step_i == 0)
    def _():
        if zero_init:
            s_s[...] = jnp.zeros_like(s_s)
        else:
            s_s[...] = s0_ref[...]

    t0 = time_index(0)
    sa0 = jnp.zeros(s_s.shape[1:], F32)
    sa1 = jnp.zeros(s_s.shape[1:], F32)
    for kq in range(0, kd, 2):
        sa0 = sa0 + s_s[kq] * al_ref[t0, kq:kq + 1, :]
        sa1 = sa1 + s_s[kq + 1] * al_ref[t0, kq + 1:kq + 2, :]

    def step(i, sa):
        t = time_index(i)
        tn = time_index(jnp.minimum(i + 1, tb - 1))
        vt = v_ref[t]
        y = jnp.zeros_like(sa)
        sa_next = jnp.zeros_like(sa)
        for kq in range(kd):
            row = slice(kq, kq + 1)
            sk = s_s[kq] * w_ref[t, row, :] + sa * be_ref[t, row, :] + vt * k_ref[t, row, :]
            s_s[kq] = sk
            y = y + sk * r_ref[t, row, :]
            sa_next = sa_next + sk * al_ref[tn, row, :]
        y_ref[t] = y
        return sa_next

    lax.fori_loop(0, tb, step, sa0 + sa1)

    @pl.when(step_i == pl.num_programs(2) - 1)
    def _():
        st_ref[...] = s_s[...]


def rwkv_scan(r, w, k, al, be, v, s0):
    seq, kd, nch = r.shape
    vv = v.shape[1]
    tb = 32
    nblk = seq // tb
    zero_init = s0 is None

    def tblk(d, i):
        return i + d * (nblk - 1 - 2 * i)

    kspec = pl.BlockSpec((tb, kd, LANES), lambda d, g, i: (tblk(d, i), 0, g))
    wspec = pl.BlockSpec((None, tb, kd, LANES), lambda d, g, i: (d, tblk(d, i), 0, g))
    vspec = pl.BlockSpec((tb, vv, LANES), lambda d, g, i: (tblk(d, i), 0, g))
    yspec = pl.BlockSpec((None, tb, vv, LANES), lambda d, g, i: (d, tblk(d, i), 0, g))
    sspec = pl.BlockSpec((None, kd, vv, LANES), lambda d, g, i: (d, 0, 0, g))
    in_specs = [kspec, wspec, kspec, kspec, kspec, vspec]
    args = [r, w, k, al, be, v]
    if not zero_init:
        in_specs.append(sspec)
        args.append(s0)
    return pl.pallas_call(
        functools.partial(_rwkv_scan_kernel, tb, zero_init),
        grid=(2, nch // LANES, nblk),
        in_specs=in_specs,
        out_specs=[yspec, sspec],
        out_shape=[jax.ShapeDtypeStruct((2, seq, vv, nch), F32),
                   jax.ShapeDtypeStruct((2, kd, vv, nch), F32)],
        scratch_shapes=[pltpu.VMEM((kd, vv, LANES), F32)],
        compiler_params=_cparams(("parallel", "parallel", "arbitrary")),
        name="rwkv_scan",
    )(*args)


def _rwkv_post_kernel(y_ref, r_ref, k_ref, v_ref, g_ref, lng, lnb, rk, o_ref):
    inv = 1.0 / RWKV_HEAD_DIM
    for cb in range(RWKV_WIDTH // LANES):
        cols = slice(cb * LANES, (cb + 1) * LANES)
        y = y_ref[0, 0, :, cols] + y_ref[1, 0, :, cols]
        mean = _seg_sum(y) * inv
        yc = y - mean
        var = _seg_sum(yc * yc) * inv
        yn = yc * lax.rsqrt(var + GN_EPS) * lng[:, cols] + lnb[:, cols]
        bonus = _seg_sum(r_ref[0, :, cols] * k_ref[0, :, cols] * rk[:, cols]) * v_ref[0, :, cols]
        o_ref[0, :, cols] = ((yn + bonus) * g_ref[0, :, cols]).astype(BF16)


def rwkv_post(y2, r, k, v, g, p):
    _, bsz, seq, wd = y2.shape
    tl = 256
    spec = pl.BlockSpec((1, tl, wd), lambda b, i: (b, i, 0))
    vec = pl.BlockSpec((1, wd), lambda b, i: (0, 0))
    return pl.pallas_call(
        _rwkv_post_kernel,
        grid=(bsz, seq // tl),
        in_specs=[pl.BlockSpec((2, 1, tl, wd), lambda b, i: (0, b, i, 0))] + [spec] * 4 + [vec] * 3,
        out_specs=spec,
        out_shape=jax.ShapeDtypeStruct((bsz, seq, wd), BF16),
        compiler_params=_cparams(("parallel", "parallel")),
        name="rwkv_post",
    )(y2, r, k, v, g, p["ln_g"], p["ln_b"], p["r_k"])


def _split_bf16(x):
    hi = x.astype(BF16)
    return hi, (x - hi.astype(F32)).astype(BF16)


def _out_proj_kernel(ys_ref, yr_ref, wo_ref, x_ref, gm_ref, ng_ref, sc_ref, sh_ref, rwh_ref, rwl_ref,
                     x1_ref, h2_ref, lg_ref):
    acc = jnp.dot(ys_ref[0], wo_ref[0:SSD_WIDTH, :], preferred_element_type=F32)
    acc = acc + jnp.dot(yr_ref[0], wo_ref[SSD_WIDTH:, :], preferred_element_type=F32)
    x1 = x_ref[0] + gm_ref[0] * acc
    x1_ref[0] = x1
    hn = x1 * lax.rsqrt(jnp.mean(x1 * x1, axis=-1, keepdims=True) + NORM_EPS) * ng_ref[...]
    h2 = hn * (1.0 + sc_ref[0]) + sh_ref[0]
    hi, lo = _split_bf16(h2)
    h2_ref[0] = hi
    lg_ref[0] = (jnp.dot(hi, rwh_ref[...], preferred_element_type=F32)
                 + jnp.dot(lo, rwh_ref[...], preferred_element_type=F32)
                 + jnp.dot(hi, rwl_ref[...], preferred_element_type=F32))


def out_proj(y_ssd, y_rwkv, w_out, x, gate_m, norm_g, scale_f, shift_f, router_w):
    bsz, seq, d = x.shape
    tm = 256
    per_b = gate_m.shape[0] > 1
    midx = (lambda b, i: (b, 0, 0)) if per_b else (lambda b, i: (0, 0, 0))
    half = pl.BlockSpec((1, tm, d // 2), lambda b, i: (b, i, 0))
    full = pl.BlockSpec((1, tm, d), lambda b, i: (b, i, 0))
    mspec = pl.BlockSpec((1, 1, d), midx)
    return pl.pallas_call(
        _out_proj_kernel,
        grid=(bsz, seq // tm),
        in_specs=[half, half, pl.BlockSpec((d, d), lambda b, i: (0, 0)), full, mspec,
                  pl.BlockSpec((1, d), lambda b, i: (0, 0)), mspec, mspec,
                  pl.BlockSpec((d, LANES), lambda b, i: (0, 0)),
                  pl.BlockSpec((d, LANES), lambda b, i: (0, 0))],
        out_specs=[full, full, pl.BlockSpec((1, tm, LANES), lambda b, i: (b, i, 0))],
        out_shape=[jax.ShapeDtypeStruct((bsz, seq, d), F32), jax.ShapeDtypeStruct((bsz, seq, d), BF16),
                   jax.ShapeDtypeStruct((bsz, seq, LANES), F32)],
        compiler_params=_cparams(("parallel", "parallel")),
        name="out_proj",
    )(y_ssd, y_rwkv, w_out, x, gate_m, norm_g.reshape(1, d), scale_f, shift_f, *router_w)


def _route_kernel(nt, cap, lg_ref, selr_ref, posr_ref, selt_ref, post_ref, afft_ref, aff_s, pre_s):
    ne = N_EXPERTS
    lane = lax.broadcasted_iota(jnp.int32, (LANES, LANES), 1)
    sub = lax.broadcasted_iota(jnp.int32, (LANES, LANES), 0)
    upper_incl = (sub <= lane).astype(BF16)
    for i in range(nt):
        lg = jnp.where(lane < ne, lg_ref[i * LANES:(i + 1) * LANES, :], -jnp.inf)
        e = jnp.exp(lg - jnp.max(lg, axis=-1, keepdims=True))
        aff = e / jnp.sum(e, axis=-1, keepdims=True)
        afft_ref[i * LANES:(i + 1) * LANES, :] = aff
        aff_s[i] = aff.T[0:ne, :]
    aff3 = aff_s[...]

    def count(mask):
        s = jnp.sum(mask.astype(F32), axis=0, keepdims=True)
        return jnp.sum(s, axis=2, keepdims=True)

    def search(_, carry):
        lo, hi = carry
        mid = 0.5 * (lo + hi)
        ok = count(aff3 >= mid) >= cap
        return jnp.where(ok, mid, lo), jnp.where(ok, hi, mid)

    lo0 = jnp.zeros((1, ne, 1), F32)
    hi0 = jnp.full((1, ne, 1), 2.0, F32)
    lo, _ = lax.fori_loop(0, SEARCH_ITERS, search, (lo0, hi0))
    cand = jnp.where(aff3 >= lo, aff3, 4.0)
    thr = jnp.min(jnp.min(cand, axis=0, keepdims=True), axis=2, keepdims=True)
    gt = (aff3 > thr).astype(F32)
    eq = (aff3 == thr).astype(F32)
    need = cap - count(aff3 > thr)[0]

    def prefix_excl(m3):
        off = jnp.zeros((ne, 1), F32)
        for i in range(nt):
            inc = jnp.dot(m3[i].astype(BF16), upper_incl, preferred_element_type=F32)
            pre_s[i] = inc - m3[i] + off
            off = off + inc[:, LANES - 1:LANES]
        return pre_s[...]

    sel = jnp.maximum(gt, eq * (prefix_excl(eq) < need[None]).astype(F32))
    pos = prefix_excl(sel)
    selr_ref[...] = sel
    posr_ref[...] = pos
    zpad = jnp.zeros((LANES - ne, LANES), F32)
    for i in range(nt):
        rows = slice(i * LANES, (i + 1) * LANES)
        selt_ref[rows, :] = jnp.concatenate([sel[i], zpad], axis=0).T
        post_ref[rows, :] = jnp.concatenate([pos[i], zpad], axis=0).T


def route(logits, cap):
    n_tok = logits.shape[0]
    nt = n_tok // LANES
    row_sh = jax.ShapeDtypeStruct((nt, N_EXPERTS, LANES), F32)
    tm_sh = jax.ShapeDtypeStruct((n_tok, LANES), F32)
    return pl.pallas_call(
        functools.partial(_route_kernel, nt, cap),
        out_shape=[row_sh, row_sh, tm_sh, tm_sh, tm_sh],
        scratch_shapes=[pltpu.VMEM((nt, N_EXPERTS, LANES), F32), pltpu.VMEM((nt, N_EXPERTS, LANES), F32)],
        compiler_params=pltpu.CompilerParams(vmem_limit_bytes=VMEM_LIMIT),
        name="route",
    )(logits)


def _window_start(starts_ref, tile, e, cap, win):
    ps = starts_ref[tile * N_EXPERTS + e]
    return pl.multiple_of(jnp.minimum((ps // SUBLANES) * SUBLANES, cap - win), SUBLANES)


def _gather_kernel(cap, win, nsub, starts_ref, sel_ref, pos_ref, h_ref, o_ref, acc):
    e = pl.program_id(0)
    j = pl.program_id(1)

    @pl.when(j == 0)
    def _():
        acc[...] = jnp.zeros_like(acc)

    for s in range(0, nsub, 2):
        tile = j * nsub + s
        s0 = _window_start(starts_ref, tile, e, cap, win)
        lo = starts_ref[tile * N_EXPERTS + e]
        hi = starts_ref[(tile + 2) * N_EXPERTS + e]
        prows = [pos_ref[s + u, pl.ds(e, 1), :] for u in range(2)]
        srows = [sel_ref[s + u, pl.ds(e, 1), :] for u in range(2)]
        for piece in range(win // GATHER_ROWS):
            p0 = pl.multiple_of(s0 + piece * GATHER_ROWS, SUBLANES)

            @pl.when(jnp.logical_and(p0 < hi, p0 + GATHER_ROWS > lo))
            def _(p0=p0, s=s, prows=prows, srows=srows):
                slot = (lax.broadcasted_iota(jnp.int32, (GATHER_ROWS, LANES), 0) + p0).astype(F32)
                onehot = jnp.concatenate(
                    [jnp.where(jnp.logical_and(slot == prows[u], srows[u] > 0.0), 1.0, 0.0).astype(BF16)
                     for u in range(2)], axis=1)
                acc[pl.ds(p0, GATHER_ROWS), :] += jnp.dot(onehot, h_ref[s * LANES:(s + 2) * LANES, :],
                                                          preferred_element_type=F32)

    @pl.when(j == pl.num_programs(1) - 1)
    def _():
        o_ref[0] = acc[...].astype(BF16)


def gather_tokens(starts, sel_r, pos_r, h, cap):
    n_tok, d = h.shape
    tt = min(1024, n_tok)
    nsub = tt // LANES
    win = min(-(-(2 * LANES + SUBLANES) // GATHER_ROWS) * GATHER_ROWS, cap)
    assert win % GATHER_ROWS == 0
    rspec = pl.BlockSpec((nsub, N_EXPERTS, LANES), lambda e, j, st: (j, 0, 0))
    return pl.pallas_call(
        functools.partial(_gather_kernel, cap, win, nsub),
        grid_spec=pltpu.PrefetchScalarGridSpec(
            num_scalar_prefetch=1,
            grid=(N_EXPERTS, n_tok // tt),
            in_specs=[rspec, rspec, pl.BlockSpec((tt, d), lambda e, j, st: (j, 0))],
            out_specs=pl.BlockSpec((1, cap, d), lambda e, j, st: (e, 0, 0)),
            scratch_shapes=[pltpu.VMEM((cap, d), F32)]),
        out_shape=jax.ShapeDtypeStruct((N_EXPERTS, cap, d), BF16),
        compiler_params=_cparams(("parallel", "arbitrary")),
        name="gather_tokens",
    )(starts, sel_r, pos_r, h)


def _ffn_kernel(n_sets, *refs):
    x_refs = refs[:n_sets]
    wg_ref, wu_ref, wd_ref = refs[n_sets:n_sets + 3]
    o_refs = refs[n_sets + 3:2 * n_sets + 3]
    accs = refs[2 * n_sets + 3:]
    f = pl.program_id(1)
    wg = wg_ref[0].astype(BF16)
    wu = wu_ref[0].astype(BF16)
    wd = wd_ref[0].astype(BF16)
    for x_ref, o_ref, acc in zip(x_refs, o_refs, accs):
        @pl.when(f == 0)
        def _():
            acc[...] = jnp.zeros_like(acc)

        x = x_ref[0]
        hg = jnp.dot(x, wg, preferred_element_type=F32)
        hu = jnp.dot(x, wu, preferred_element_type=F32)
        hid = (_silu(hg) * hu).astype(BF16)
        acc[...] += jnp.dot(hid, wd, preferred_element_type=F32)

        @pl.when(f == pl.num_programs(1) - 1)
        def _():
            o_ref[0] = acc[...].astype(BF16)


def expert_ffn(xs_sets, wg, wu, wd):
    ne, _, d = xs_sets[0].shape
    ff = wg.shape[2]
    tf = 256
    xspecs = [pl.BlockSpec((1, xs.shape[1], d), lambda e, f: (e, 0, 0)) for xs in xs_sets]
    return pl.pallas_call(
        functools.partial(_ffn_kernel, len(xs_sets)),
        grid=(ne, ff // tf),
        in_specs=xspecs + [pl.BlockSpec((1, d, tf), lambda e, f: (e, 0, f)),
                           pl.BlockSpec((1, d, tf), lambda e, f: (e, 0, f)),
                           pl.BlockSpec((1, tf, d), lambda e, f: (e, f, 0))],
        out_specs=xspecs,
        out_shape=[jax.ShapeDtypeStruct(xs.shape, BF16) for xs in xs_sets],
        scratch_shapes=[pltpu.VMEM(xs.shape[1:], F32) for xs in xs_sets],
        compiler_params=_cparams(("parallel", "arbitrary")),
        name="expert_ffn",
    )(*xs_sets, wg, wu, wd)


def _scatter_kernel(cap, win, nsub, starts_ref, sel_ref, pos_ref, aff_ref, ob_ref, y_ref):
    c = pl.program_id(0)
    e = pl.program_id(1)

    @pl.when(e == 0)
    def _():
        y_ref[...] = jnp.zeros_like(y_ref)

    mine = lax.broadcasted_iota(jnp.int32, (LANES, LANES), 1) == e
    for s in range(nsub):
        rows = slice(s * LANES, (s + 1) * LANES)

        def col(ref):
            return jnp.sum(jnp.where(mine, ref[rows, :], 0.0), axis=-1, keepdims=True)

        pcol, scol, gcol = col(pos_ref), col(sel_ref), col(aff_ref)
        s0 = _window_start(starts_ref, c * nsub + s, e, cap, win)
        slot = (lax.broadcasted_iota(jnp.int32, (LANES, win), 1) + s0).astype(F32)
        onehot = jnp.where(jnp.logical_and(slot == pcol, scol > 0.0), 1.0, 0.0).astype(BF16)
        y_ref[rows, :] += gcol * jnp.dot(onehot, ob_ref[0, pl.ds(s0, win), :],
                                         preferred_element_type=F32)


def scatter_combine(starts, sel_t, pos_t, aff_t, outbuf):
    n_tok = sel_t.shape[0]
    ne, cap, d = outbuf.shape
    chunk = min(2048, n_tok)
    nsub = chunk // LANES
    win = min(2 * LANES, cap)
    tspec = pl.BlockSpec((chunk, LANES), lambda c, e, st: (c, 0))
    return pl.pallas_call(
        functools.partial(_scatter_kernel, cap, win, nsub),
        grid_spec=pltpu.PrefetchScalarGridSpec(
            num_scalar_prefetch=1,
            grid=(n_tok // chunk, ne),
            in_specs=[tspec, tspec, tspec, pl.BlockSpec((1, cap, d), lambda c, e, st: (e, 0, 0))],
            out_specs=pl.BlockSpec((chunk, d), lambda c, e, st: (c, 0))),
        out_shape=jax.ShapeDtypeStruct((n_tok, d), F32),
        compiler_params=_cparams(("parallel", "arbitrary")),
        name="scatter_combine",
    )(starts, sel_t, pos_t, aff_t, outbuf)


def _final_kernel(x1_ref, y_ref, gf_ref, fg_ref, o_ref):
    x2 = x1_ref[0] + gf_ref[0] * y_ref[0]
    o_ref[0] = x2 * lax.rsqrt(jnp.mean(x2 * x2, axis=-1, keepdims=True) + NORM_EPS) * fg_ref[...]


def final_norm(x1, y, gate_f, final_g):
    bsz, seq, d = x1.shape
    tl = 256
    per_b = gate_f.shape[0] > 1
    midx = (lambda b, i: (b, 0, 0)) if per_b else (lambda b, i: (0, 0, 0))
    xspec = pl.BlockSpec((1, tl, d), lambda b, i: (b, i, 0))
    return pl.pallas_call(
        _final_kernel,
        grid=(bsz, seq // tl),
        in_specs=[xspec, xspec, pl.BlockSpec((1, 1, d), midx), pl.BlockSpec((1, d), lambda b, i: (0, 0))],
        out_specs=xspec,
        out_shape=jax.ShapeDtypeStruct((bsz, seq, d), F32),
        compiler_params=_cparams(("parallel", "parallel")),
        name="final_norm",
    )(x1, y, gate_f, final_g.reshape(1, d))


def _pad_cols(w, width):
    return jnp.pad(w, ((0, 0), (0, width - w.shape[1])))


def _pad_rows(w, rows):
    return jnp.pad(w, ((0, rows - w.shape[0]), (0, 0)))


def _relayout_columns(w):
    n_ssd = 2 * SSD_WIDTH + 2 * SSD_GROUPS * SSD_STATE + 2 * SSD_HEADS
    ssd, rw = w[:, :n_ssd], w[:, n_ssd:]
    o = 3 * RWKV_WIDTH
    zx = 2 * SSD_WIDTH
    bc = zx + 2 * SSD_GROUPS * SSD_STATE
    parts = [ssd[:, :zx], rw[:, :o], ssd[:, zx:bc], _pad_cols(ssd[:, bc:], LANES),
             _pad_cols(rw[:, o:o + DECAY_LORA], LANES),
             _pad_cols(rw[:, o + DECAY_LORA:o + 2 * DECAY_LORA], LANES),
             _pad_cols(rw[:, o + 2 * DECAY_LORA:o + 2 * DECAY_LORA + ICLR_LORA], LANES),
             _pad_cols(rw[:, o + 2 * DECAY_LORA + ICLR_LORA:], 2 * LANES)]
    return jnp.concatenate(parts, axis=1)


def _layer_params(l, w_in, w_out, ssd_conv_w, ssd_conv_b, ssd_A_log, ssd_dt_bias, ssd_D, ssd_norm_g,
                  rwkv_mu, rwkv_w0, rwkv_w_up, rwkv_a0, rwkv_a_up, rwkv_g_up, rwkv_k_k, rwkv_k_a,
                  rwkv_r_k, rwkv_ln_g, rwkv_ln_b, router_w):
    row = lambda a: a.reshape(1, -1)
    n_ssd = 2 * SSD_WIDTH + 2 * SSD_GROUPS * SSD_STATE + 2 * SSD_HEADS
    mu_full = _relayout_columns(jnp.concatenate([jnp.zeros((1, n_ssd), F32), row(rwkv_mu[l])], axis=1))
    cw, cb = ssd_conv_w[l], row(ssd_conv_b[l])
    xe, be = SSD_WIDTH, SSD_WIDTH + SSD_GROUPS * SSD_STATE
    return dict(
        w_in=_relayout_columns(w_in[l]).astype(BF16),
        w_out=w_out[l].astype(BF16),
        cw_x=cw[:, :xe], cb_x=cb[:, :xe], cw_b=cw[:, xe:be], cb_b=cb[:, xe:be],
        cw_c=cw[:, be:], cb_c=cb[:, be:],
        dt_bias=_pad_cols(row(ssd_dt_bias[l]), LANES), a_log=_pad_cols(row(ssd_A_log[l]), LANES),
        d_exp=row(jnp.repeat(ssd_D[l], SSD_HEAD_DIM)), ssd_ng=row(ssd_norm_g[l]),
        mu_r=mu_full[:, COL_R:COL_K], mu_k=mu_full[:, COL_K:COL_V], mu_v=mu_full[:, COL_V:COL_B],
        mu_lora=mu_full[:, COL_LORA:],
        w_up_f=_pad_rows(rwkv_w_up[l, 0], LANES), w_up_b=_pad_rows(rwkv_w_up[l, 1], LANES),
        a_up=_pad_rows(rwkv_a_up[l], LANES), g_up=_pad_rows(rwkv_g_up[l], 2 * LANES),
        w0_f=row(rwkv_w0[l, 0]), w0_b=row(rwkv_w0[l, 1]), a0=row(rwkv_a0[l]),
        k_k=row(rwkv_k_k[l]), k_a=row(rwkv_k_a[l]), r_k=row(rwkv_r_k[l]),
        ln_g=row(rwkv_ln_g[l]), ln_b=row(rwkv_ln_b[l]),
        router_w=tuple(_pad_cols(t, LANES) for t in _split_bf16(router_w[l])),
    )


def _to_chain(x):
    lead = x.shape[:-3]
    bsz, seq, _ = x.shape[-3:]
    n = len(lead)
    x = x.reshape(lead + (bsz, seq, RWKV_HEADS, RWKV_HEAD_DIM))
    perm = tuple(range(n)) + (n + 1, n + 3, n, n + 2)
    return jnp.transpose(x, perm).reshape(lead + (seq, RWKV_HEAD_DIM, bsz * RWKV_HEADS))


def _from_chain(y, bsz):
    lead = y.shape[:-3]
    seq = y.shape[-3]
    n = len(lead)
    y = y.reshape(lead + (seq, RWKV_HEAD_DIM, bsz, RWKV_HEADS))
    perm = tuple(range(n)) + (n + 2, n, n + 3, n + 1)
    return jnp.transpose(y, perm).reshape(lead + (bsz, seq, RWKV_WIDTH))


def _state_to_chain(s):
    return jnp.transpose(s, (0, 4, 3, 1, 2)).reshape(2, RWKV_HEAD_DIM, RWKV_HEAD_DIM, -1)


def _state_from_chain(s, bsz):
    return jnp.transpose(s.reshape(2, RWKV_HEAD_DIM, RWKV_HEAD_DIM, bsz, RWKV_HEADS), (0, 3, 4, 2, 1))


def rwkv_mixer_scan(r, k, v, w2, al, be, s0):
    bsz, seq, _ = r.shape
    nh, hd = RWKV_HEADS, RWKV_HEAD_DIM
    nch = bsz * nh
    rep = max(1, LANES // nch)
    assert (nch * rep) % LANES == 0
    if rep == 1:
        rc, kc, vc, alc, bec, wc = (_to_chain(t) for t in (r, k, v, al, be, w2))
        s0c = None if s0 is None else _state_to_chain(s0)
        y, st = rwkv_scan(rc, wc, kc, alc, bec, vc, s0c)
        return _from_chain(y, bsz), _state_from_chain(st, bsz)

    vq = hd // rep

    def keys_to_chain(x):
        lead = x.shape[:-3]
        n = len(lead)
        x = jnp.broadcast_to(x.reshape(lead + (1, bsz, seq, nh, hd)), lead + (rep, bsz, seq, nh, hd))
        perm = tuple(range(n)) + (n + 2, n + 4, n, n + 1, n + 3)
        return jnp.transpose(x, perm).reshape(lead + (seq, hd, LANES))

    rc, kc, alc, bec, wc = (keys_to_chain(t) for t in (r, k, al, be, w2))
    vc = jnp.transpose(v.reshape(bsz, seq, nh, rep, vq), (1, 4, 3, 0, 2)).reshape(seq, vq, LANES)
    s0c = None
    if s0 is not None:
        s0c = jnp.transpose(s0.reshape(2, bsz, nh, rep, vq, hd), (0, 5, 4, 3, 1, 2)).reshape(
            2, hd, vq, LANES)
    y, st = rwkv_scan(rc, wc, kc, alc, bec, vc, s0c)
    y = jnp.transpose(y.reshape(2, seq, vq, rep, bsz, nh), (0, 4, 1, 5, 3, 2)).reshape(
        2, bsz, seq, RWKV_WIDTH)
    st = jnp.transpose(st.reshape(2, hd, vq, rep, bsz, nh), (0, 4, 5, 3, 2, 1)).reshape(
        2, bsz, nh, hd, hd)
    return y, st


def _grid_pos_embed(n_tokens, dim):
    rows = n_tokens // GRID_W
    row = jnp.repeat(jnp.arange(rows, dtype=F32), GRID_W)
    col = jnp.tile(jnp.arange(GRID_W, dtype=F32), rows)
    quarter = dim // 4
    freqs = jnp.exp(jnp.arange(quarter, dtype=F32) * (-math.log(POS_BASE) / quarter))

    def axis_embed(pos):
        ang = pos[:, None] * freqs[None, :]
        return jnp.concatenate([jnp.sin(ang), jnp.cos(ang)], axis=-1)

    return jnp.concatenate([axis_embed(row), axis_embed(col)], axis=-1)


def layer_to_dispatch(x, pos, mod, p, norm_mix_g, norm_ffn_g, states):
    bsz, seq, d = x.shape
    shift_m, scale_m, gate_m, shift_f, scale_f, gate_f = mod
    h, x0 = norm_modulate(x, pos, norm_mix_g, scale_m, shift_m)
    n_tok = bsz * seq
    proj = matmul_bf16(h.reshape(n_tok, d), p["w_in"], min(1024, n_tok), 640).reshape(bsz, seq, N_PROJ)
    h0f, h0b, s0f, s0b = states
    y_ssd, hf, hb = ssd_mixer(proj, p, h0f, h0b)
    r, k, v, w2, al, be, g = rwkv_prep(proj, p)
    s0 = None if s0f is None else jnp.stack([s0f, s0b], axis=0)
    y2, st = rwkv_mixer_scan(r, k, v, w2, al, be, s0)
    y_rwkv = rwkv_post(y2, r, k, v, g, p)
    x1, h2, logits = out_proj(y_ssd, y_rwkv, p["w_out"], x0, gate_m, norm_ffn_g, scale_f, shift_f,
                              p["router_w"])
    cap = EC_CAPACITY * n_tok // N_EXPERTS
    sel_r, pos_r, sel_t, pos_t, aff_t = route(logits.reshape(n_tok, LANES), cap)
    starts = jnp.concatenate([pos_r[:, :, 0], jnp.full((1, N_EXPERTS), cap, F32)], axis=0)
    starts = starts.astype(jnp.int32).reshape(-1)
    xs = gather_tokens(starts, sel_r, pos_r, h2.reshape(n_tok, d), cap)
    routed = dict(xs=xs, starts=starts, sel_t=sel_t, pos_t=pos_t, aff_t=aff_t, x1=x1, gate_f=gate_f)
    return routed, (hf, hb, st[0], st[1])


def combine_and_finish(routed, outbuf, final_g):
    x1 = routed["x1"]
    y_ffn = scatter_combine(routed["starts"], routed["sel_t"], routed["pos_t"], routed["aff_t"], outbuf)
    return final_norm(x1, y_ffn.reshape(x1.shape), routed["gate_f"], final_g)


def kernel(x_prompt, x_sample, state_ssd_fwd, state_ssd_bwd, state_rwkv_fwd, state_rwkv_bwd, c, c_ctx,
           w_ada, b_ada, norm_mix_g, norm_ffn_g, w_in, w_out, ssd_conv_w, ssd_conv_b, ssd_A_log,
           ssd_dt_bias, ssd_D, ssd_norm_g, rwkv_mu, rwkv_w0, rwkv_w_up, rwkv_a0, rwkv_a_up, rwkv_g_up,
           rwkv_k_k, rwkv_k_a, rwkv_r_k, rwkv_ln_g, rwkv_ln_b, router_w, exp_w_gate, exp_w_up,
           exp_w_down, final_norm_g):
    depth = w_in.shape[0]
    assert depth == 1, "the final norm runs right after the single layer's FFN residual"
    l = 0
    dec_b = x_sample.shape[0]
    p = _layer_params(l, w_in, w_out, ssd_conv_w, ssd_conv_b, ssd_A_log, ssd_dt_bias, ssd_D, ssd_norm_g,
                      rwkv_mu, rwkv_w0, rwkv_w_up, rwkv_a0, rwkv_a_up, rwkv_g_up, rwkv_k_k, rwkv_k_a,
                      rwkv_r_k, rwkv_ln_g, rwkv_ln_b, router_w)
    cond = jnp.concatenate([c_ctx[None, :], c, jnp.zeros((16 - 1 - dec_b, D_MODEL), F32)], axis=0)
    mod = ada_modulation(cond, w_ada[l], b_ada[l]).reshape(16, N_MOD, 1, D_MODEL)
    mod_ctx = [mod[0:1, i] for i in range(N_MOD)]
    mod_lat = [mod[1:1 + dec_b, i] for i in range(N_MOD)]

    routed_ctx, st = layer_to_dispatch(x_prompt, None, mod_ctx, p, norm_mix_g[l], norm_ffn_g[l],
                                       (None, None, None, None))
    pos = _grid_pos_embed(x_sample.shape[1], D_MODEL)
    cached = (state_ssd_fwd[:, l], state_ssd_bwd[:, l], state_rwkv_fwd[:, l], state_rwkv_bwd[:, l])
    routed_lat, _ = layer_to_dispatch(x_sample, pos, mod_lat, p, norm_mix_g[l], norm_ffn_g[l], cached)
    out_ctx, out_lat = expert_ffn([routed_ctx["xs"], routed_lat["xs"]],
                                  exp_w_gate[l], exp_w_up[l], exp_w_down[l])
    y_prompt = combine_and_finish(routed_ctx, out_ctx, final_norm_g)
    y_sample = combine_and_finish(routed_lat, out_lat, final_norm_g)
    return (y_prompt, y_sample, st[0][:, None], st[1][:, None], st[2][:, None], st[3][:, None])
```

```python
import functools
import math

import jax
import jax.numpy as jnp
from jax import lax
from jax.experimental import pallas as pl
from jax.experimental.pallas import tpu as pltpu

F32 = jnp.float32
BF16 = jnp.bfloat16
HIGHEST = lax.Precision.HIGHEST

D_MODEL = 2048
GRID_W = 64
SSD_WIDTH = 1024
SSD_HEAD_DIM = 64
SSD_HEADS = 16
SSD_GROUPS = 2
SSD_STATE = 128
SSD_CONV_W = 5
SSD_CHUNK = 128
RWKV_WIDTH = 1024
RWKV_HEAD_DIM = 64
RWKV_HEADS = 16
DECAY_LORA = 64
ICLR_LORA = 64
GATE_LORA = 160
N_EXPERTS = 16
EC_CAPACITY = 2
EXPERT_FF = 1024
N_MOD = 6
NORM_EPS = 1e-6
GN_EPS = 64e-5
DECAY_SCALE = 0.606531
POS_BASE = 10000.0
SEARCH_ITERS = 48

LANES = 128
SUBLANES = 8
GATHER_ROWS = 64
GATHER_EXPERTS = 4

COL_Z = 0
COL_X = 1024
COL_R = 2048
COL_K = 3072
COL_V = 4096
COL_B = 5120
COL_C = 5376
COL_DT = 5632
COL_LORA = 5760
PREP_BLOCK_ELEMS = 256 * 1024
LORA_W = 640
N_PROJ = COL_LORA + LORA_W

VMEM_LIMIT = 56 * 1024 * 1024


def _cparams(sem):
    return pltpu.CompilerParams(dimension_semantics=sem, vmem_limit_bytes=VMEM_LIMIT)


def _silu(x):
    return x * jax.nn.sigmoid(x)


def _ada_kernel(c_ref, w_ref, b_ref, o_ref):
    s = _silu(c_ref[...]).astype(BF16)
    o_ref[...] = jnp.dot(s, w_ref[...].astype(BF16), preferred_element_type=F32) + b_ref[...]


def ada_modulation(cond, w, b):
    m, d = cond.shape
    n = w.shape[1]
    tn = 1024
    return pl.pallas_call(
        _ada_kernel,
        grid=(n // tn,),
        in_specs=[pl.BlockSpec((m, d), lambda j: (0, 0)),
                  pl.BlockSpec((d, tn), lambda j: (0, j)),
                  pl.BlockSpec((1, tn), lambda j: (0, j))],
        out_specs=pl.BlockSpec((m, tn), lambda j: (0, j)),
        out_shape=jax.ShapeDtypeStruct((m, n), F32),
        compiler_params=_cparams(("parallel",)),
        name="ada_modulation",
    )(cond, w, b.reshape(1, n))


def _norm_mod_kernel(has_pos, *refs):
    if has_pos:
        x_ref, pos_ref, g_ref, sc_ref, sh_ref, h_ref, x0_ref = refs
    else:
        x_ref, g_ref, sc_ref, sh_ref, h_ref = refs
    x = x_ref[0]
    if has_pos:
        x = x + pos_ref[...]
        x0_ref[0] = x
    y = x * lax.rsqrt(jnp.mean(x * x, axis=-1, keepdims=True) + NORM_EPS) * g_ref[...]
    h_ref[0] = (y * (1.0 + sc_ref[0]) + sh_ref[0]).astype(BF16)


def norm_modulate(x, pos, g, scale, shift):
    bsz, seq, d = x.shape
    tl = 256
    per_b = scale.shape[0] > 1
    midx = (lambda b, i: (b, 0, 0)) if per_b else (lambda b, i: (0, 0, 0))
    xspec = pl.BlockSpec((1, tl, d), lambda b, i: (b, i, 0))
    in_specs = [xspec]
    args = [x]
    if pos is not None:
        in_specs.append(pl.BlockSpec((tl, d), lambda b, i: (i, 0)))
        args.append(pos)
    in_specs += [pl.BlockSpec((1, d), lambda b, i: (0, 0)),
                 pl.BlockSpec((1, 1, d), midx), pl.BlockSpec((1, 1, d), midx)]
    args += [g.reshape(1, d), scale, shift]
    out_shape = [jax.ShapeDtypeStruct((bsz, seq, d), BF16)]
    out_specs = [xspec]
    if pos is not None:
        out_shape.append(jax.ShapeDtypeStruct((bsz, seq, d), F32))
        out_specs.append(xspec)
    res = pl.pallas_call(
        functools.partial(_norm_mod_kernel, pos is not None),
        grid=(bsz, seq // tl),
        in_specs=in_specs, out_specs=out_specs, out_shape=out_shape,
        compiler_params=_cparams(("parallel", "parallel")),
        name="norm_modulate",
    )(*args)
    return (res[0], res[1]) if pos is not None else (res[0], x)


def _mm_kernel(a_ref, b_ref, o_ref):
    o_ref[...] = jnp.dot(a_ref[...], b_ref[...], preferred_element_type=F32)


def matmul_bf16(a, b, tm, tn):
    m, k = a.shape
    n = b.shape[1]
    return pl.pallas_call(
        _mm_kernel,
        grid=(m // tm, n // tn),
        in_specs=[pl.BlockSpec((tm, k), lambda i, j: (i, 0)),
                  pl.BlockSpec((k, tn), lambda i, j: (0, j))],
        out_specs=pl.BlockSpec((tm, tn), lambda i, j: (i, j)),
        out_shape=jax.ShapeDtypeStruct((m, n), F32),
        compiler_params=_cparams(("parallel", "arbitrary")),
        name="in_proj",
    )(a, b)


def _shifted(u, off):
    n = u.shape[0]
    row = lax.broadcasted_iota(jnp.int32, (n, 1), 0)
    rolled = pltpu.roll(u, (-off) % n, axis=0)
    valid = jnp.logical_and(row + off >= 0, row + off < n)
    return jnp.where(valid, rolled, 0.0)


def _conv_silu(u, w_ref, b_ref, cols):
    pad = SSD_CONV_W // 2
    acc = u * w_ref[pad:pad + 1, cols] + b_ref[:, cols]
    for j in range(SSD_CONV_W):
        if j != pad:
            acc = acc + _shifted(u, j - pad) * w_ref[j:j + 1, cols]
    return _silu(acc)


def _softplus(x):
    return jnp.maximum(x, 0.0) + jnp.log1p(jnp.exp(-jnp.abs(x)))


def _ssd_kernel(nc, zero_init, *refs):
    q = SSD_CHUNK
    (z_ref, x_ref, b_ref, c_ref, dt_ref, cwx, cbx, cwb, cbb, cwc, cbc,
     dtb_ref, alog_ref, d_ref, ng_ref) = refs[:15]
    refs = refs[15:]
    if not zero_init:
        h0f_ref, h0b_ref = refs[:2]
        refs = refs[2:]
    y_ref, hf_ref, hb_ref, xh_s, bm_s, cm_s, dt_s, a_s, at_s, y_s, h_s = refs

    slab = 256
    for cb in range(SSD_WIDTH // slab):
        cols = slice(cb * slab, (cb + 1) * slab)
        xh = _conv_silu(x_ref[0, :, cols], cwx, cbx, cols)
        for c in range(nc):
            rows = slice(c * q, (c + 1) * q)
            xh_s[c, :, cols] = xh[rows]
            y_s[c, :, cols] = xh[rows] * d_ref[:, cols]
    allc = slice(0, SSD_GROUPS * SSD_STATE)
    bm = _conv_silu(b_ref[0], cwb, cbb, allc)
    cm = _conv_silu(c_ref[0], cwc, cbc, allc)
    dt = _softplus(dt_ref[0] + dtb_ref[...])
    a = dt * (-jnp.exp(alog_ref[...]))
    ii = lax.broadcasted_iota(jnp.int32, (q, q), 0)
    jj = lax.broadcasted_iota(jnp.int32, (q, q), 1)
    eye = (ii == jj).astype(F32)
    lower = ii >= jj
    upper = ii <= jj
    first_half = jj < SSD_HEAD_DIM
    for c in range(nc):
        rows = slice(c * q, (c + 1) * q)
        for g in range(SSD_GROUPS):
            gcols = slice(g * SSD_STATE, (g + 1) * SSD_STATE)
            bm_s[c, gcols, :] = bm[rows, gcols].T
        cm_s[c] = cm[rows]
        dt_s[c] = dt[rows]
        a_s[c] = a[rows]
        at_s[c] = lax.dot_general(eye, a[rows], (((1,), (1,)), ((), ())),
                                  precision=HIGHEST, preferred_element_type=F32)

    for d in range(2):
        mask = lower if d == 0 else upper
        tri = mask.astype(F32)
        tri_t = (upper if d == 0 else lower).astype(F32)
        if zero_init:
            h_s[...] = jnp.zeros_like(h_s)
        else:
            h0_ref = h0f_ref if d == 0 else h0b_ref
            for p in range(SSD_HEADS // 2):
                h_s[p] = jnp.concatenate([h0_ref[0, 2 * p], h0_ref[0, 2 * p + 1]], axis=0).T

        def chunk_body(ci, carry, d=d, mask=mask, tri=tri, tri_t=tri_t):
            c = ci if d == 0 else nc - 1 - ci
            a_c = a_s[c]
            cum = jnp.dot(tri, a_c, precision=HIGHEST, preferred_element_type=F32)
            cum_t = jnp.dot(at_s[c], tri_t, precision=HIGHEST, preferred_element_type=F32)
            dt_c = dt_s[c]
            for g in range(SSD_GROUPS):
                gcols = slice(g * SSD_STATE, (g + 1) * SSD_STATE)
                bt = bm_s[c, gcols, :]
                cg = cm_s[c, :, gcols]
                gram = jnp.dot(cg.astype(BF16), bt.astype(BF16), preferred_element_type=F32)
                for pp in range(SSD_HEADS // SSD_GROUPS // 2):
                    p = g * (SSD_HEADS // SSD_GROUPS // 2) + pp
                    pcols = slice(p * LANES, (p + 1) * LANES)
                    xh_pair = xh_s[c, :, pcols]
                    hprev = h_s[p]
                    hprev_b = hprev.astype(BF16)
                    parts = []
                    for e in range(2):
                        col = d * SSD_HEADS + 2 * p + e
                        mine = first_half if e == 0 else jnp.logical_not(first_half)
                        cc = cum[:, col:col + 1]
                        cr = cum_t[col:col + 1, :]
                        dec = jnp.exp(jnp.where(mask, cc - cr, -jnp.inf))
                        m = (gram * dec).astype(BF16)
                        xdt = jnp.where(mine, xh_pair * dt_c[:, col:col + 1], 0.0).astype(BF16)
                        tot = cum[q - 1:q, col:col + 1] if d == 0 else cum[0:1, col:col + 1]
                        bd_t = (bt * jnp.exp(tot - cr)).astype(BF16)
                        cd = (cg * jnp.exp(cc)).astype(BF16)
                        s_c = jnp.dot(bd_t, xdt, preferred_element_type=F32)
                        y_off = jnp.dot(cd, hprev_b, preferred_element_type=F32)
                        y_diag = jnp.dot(m, xdt, preferred_element_type=F32)
                        parts.append((s_c, y_off, y_diag, jnp.exp(tot)))
                    (s0, yo0, yd0, g0), (s1, yo1, yd1, g1) = parts
                    h_s[p] = hprev * jnp.where(first_half, g0, g1) + (s0 + s1)
                    y_s[c, :, pcols] += (yd0 + yd1) + jnp.where(first_half, yo0, yo1)
            return carry

        lax.fori_loop(0, nc, chunk_body, 0)
        out_ref = hf_ref if d == 0 else hb_ref
        for p in range(SSD_HEADS // 2):
            tr = h_s[p].T
            out_ref[0, 2 * p] = tr[0:SSD_HEAD_DIM]
            out_ref[0, 2 * p + 1] = tr[SSD_HEAD_DIM:]

    gw = SSD_WIDTH // SSD_GROUPS
    for c in range(nc):
        rows = slice(c * q, (c + 1) * q)
        for g in range(SSD_GROUPS):
            cols = slice(g * gw, (g + 1) * gw)
            y = y_s[c, :, cols] * _silu(z_ref[0, rows, cols])
            y = y * lax.rsqrt(jnp.mean(y * y, axis=-1, keepdims=True) + NORM_EPS)
            y_ref[0, rows, cols] = (y * ng_ref[:, cols]).astype(BF16)


def ssd_mixer(proj, p, h0f, h0b):
    bsz, seq, _ = proj.shape
    nc = seq // SSD_CHUNK
    zero_init = h0f is None

    def col_spec(width, start):
        blk = start // width
        return pl.BlockSpec((1, seq, width), lambda b: (b, 0, blk))

    def full2(arr):
        return pl.BlockSpec(arr.shape, lambda b: (0, 0))

    st_spec = pl.BlockSpec((1, SSD_HEADS, SSD_HEAD_DIM, SSD_STATE), lambda b: (b, 0, 0, 0))
    bc = SSD_GROUPS * SSD_STATE
    small = [p["cw_x"], p["cb_x"], p["cw_b"], p["cb_b"], p["cw_c"], p["cb_c"],
             p["dt_bias"], p["a_log"], p["d_exp"], p["ssd_ng"]]
    in_specs = [col_spec(SSD_WIDTH, COL_Z), col_spec(SSD_WIDTH, COL_X), col_spec(bc, COL_B),
                col_spec(bc, COL_C), col_spec(LANES, COL_DT)] + [full2(s) for s in small]
    args = [proj] * 5 + small
    if not zero_init:
        in_specs += [st_spec, st_spec]
        args += [h0f, h0b]
    st_shape = jax.ShapeDtypeStruct((bsz, SSD_HEADS, SSD_HEAD_DIM, SSD_STATE), F32)
    q = SSD_CHUNK
    return pl.pallas_call(
        functools.partial(_ssd_kernel, nc, zero_init),
        grid=(bsz,),
        in_specs=in_specs,
        out_specs=[pl.BlockSpec((1, seq, SSD_WIDTH), lambda b: (b, 0, 0)), st_spec, st_spec],
        out_shape=[jax.ShapeDtypeStruct((bsz, seq, SSD_WIDTH), BF16), st_shape, st_shape],
        scratch_shapes=[pltpu.VMEM((nc, q, SSD_WIDTH), F32), pltpu.VMEM((nc, bc, q), F32),
                        pltpu.VMEM((nc, q, bc), F32), pltpu.VMEM((nc, q, LANES), F32),
                        pltpu.VMEM((nc, q, LANES), F32), pltpu.VMEM((nc, LANES, q), F32),
                        pltpu.VMEM((nc, q, SSD_WIDTH), F32),
                        pltpu.VMEM((SSD_HEADS // 2, SSD_STATE, 2 * SSD_HEAD_DIM), F32)],
        compiler_params=_cparams(("parallel",)),
        name="ssd_mixer",
    )(*args)


def _seg_sum(x):
    lane = lax.broadcasted_iota(jnp.int32, x.shape, 1)
    first = lane < RWKV_HEAD_DIM
    s0 = jnp.sum(jnp.where(first, x, 0.0), axis=-1, keepdims=True)
    s1 = jnp.sum(jnp.where(first, 0.0, x), axis=-1, keepdims=True)
    return jnp.where(first, s0, s1)


def _shift_mix(u, mu):
    return u + mu * (0.5 * (_shifted(u, -1) + _shifted(u, 1)) - u)


def _rwkv_prep_kernel(ncb, r_ref, k_ref, v_ref, lo_ref, mur, muk, muv, mul, wupf, wupb, aup, gup,
                      w0f, w0b, a0, kk_ref, ka_ref,
                      ro, ko, vo, wo, alo, beo, go):
    lo = _shift_mix(lo_ref[0], mul[...])
    wdf = jnp.tanh(lo[:, 0:128]).astype(BF16)
    wdb = jnp.tanh(lo[:, 128:256]).astype(BF16)
    ad = lo[:, 256:384].astype(BF16)
    gd = jax.nn.sigmoid(lo[:, 384:640]).astype(BF16)
    for cb in range(ncb):
        cols = slice(cb * LANES, (cb + 1) * LANES)

        def mm(x, w_ref, cols=cols):
            return jnp.dot(x, w_ref[:, cols].astype(BF16), preferred_element_type=F32)

        r = _shift_mix(r_ref[0, :, cols], mur[:, cols])
        k = _shift_mix(k_ref[0, :, cols], muk[:, cols])
        v = _shift_mix(v_ref[0, :, cols], muv[:, cols])
        wo[0, 0, :, cols] = jnp.exp(-DECAY_SCALE * jax.nn.sigmoid(w0f[:, cols] + mm(wdf, wupf)))
        wo[1, 0, :, cols] = jnp.exp(-DECAY_SCALE * jax.nn.sigmoid(w0b[:, cols] + mm(wdb, wupb)))
        a = jax.nn.sigmoid(a0[:, cols] + mm(ad, aup))
        go[0, :, cols] = mm(gd, gup)
        kk = k * kk_ref[:, cols]
        kk = kk / jnp.maximum(jnp.sqrt(_seg_sum(kk * kk)), 1e-12)
        ro[0, :, cols] = r
        ko[0, :, cols] = k * (1.0 + (a - 1.0) * ka_ref[:, cols])
        vo[0, :, cols] = v
        alo[0, :, cols] = -kk
        beo[0, :, cols] = kk * a


def rwkv_prep(proj, p):
    bsz, seq, _ = proj.shape
    cw = min(RWKV_WIDTH, PREP_BLOCK_ELEMS // seq)
    nblk = RWKV_WIDTH // cw

    def col_spec(start):
        blk = start // cw
        return pl.BlockSpec((1, seq, cw), lambda b, j: (b, 0, blk + j))

    vec = pl.BlockSpec((1, cw), lambda b, j: (0, j))

    def mat(rows):
        return pl.BlockSpec((rows, cw), lambda b, j: (0, j))

    out_spec = pl.BlockSpec((1, seq, cw), lambda b, j: (b, 0, j))
    out_sh = jax.ShapeDtypeStruct((bsz, seq, RWKV_WIDTH), F32)
    return pl.pallas_call(
        functools.partial(_rwkv_prep_kernel, cw // LANES),
        grid=(bsz, nblk),
        in_specs=[col_spec(COL_R), col_spec(COL_K), col_spec(COL_V),
                  pl.BlockSpec((1, seq, LORA_W), lambda b, j: (b, 0, COL_LORA // LORA_W)),
                  vec, vec, vec, pl.BlockSpec((1, LORA_W), lambda b, j: (0, 0)),
                  mat(128), mat(128), mat(128), mat(256),
                  vec, vec, vec, vec, vec],
        out_specs=[out_spec] * 3 + [pl.BlockSpec((2, 1, seq, cw), lambda b, j: (0, b, 0, j))]
        + [out_spec] * 3,
        out_shape=[out_sh] * 3 + [jax.ShapeDtypeStruct((2, bsz, seq, RWKV_WIDTH), F32)] + [out_sh] * 3,
        compiler_params=_cparams(("parallel", "arbitrary")),
        name="rwkv_prep",
    )(proj, proj, proj, proj, p["mu_r"], p["mu_k"], p["mu_v"], p["mu_lora"],
      p["w_up_f"], p["w_up_b"], p["a_up"], p["g_up"],
      p["w0_f"], p["w0_b"], p["a0"], p["k_k"], p["k_a"])


def _rwkv_scan_kernel(tb, zero_init, *refs):
    if zero_init:
        r_ref, w_ref, k_ref, al_ref, be_ref, v_ref, y_ref, st_ref, s_s = refs
    else:
        r_ref, w_ref, k_ref, al_ref, be_ref, v_ref, s0_ref, y_ref, st_ref, s_s = refs
    kd = RWKV_HEAD_DIM
    d = pl.program_id(0)
    step_i = pl.program_id(2)

    def time_index(i):
        return i + d * (tb - 1 - 2 * i)

    @pl.when(step_i == 0)
    def _():
        if zero_init:
            s_s[...] = jnp.zeros_like(s_s)
        else:
            s_s[...] = s0_ref[...]

    t0 = time_index(0)
    sa0 = jnp.zeros(s_s.shape[1:], F32)
    sa1 = jnp.zeros(s_s.shape[1:], F32)
    for kq in range(0, kd, 2):
        sa0 = sa0 + s_s[kq] * al_ref[t0, kq:kq + 1, :]
        sa1 = sa1 + s_s[kq + 1] * al_ref[t0, kq + 1:kq + 2, :]

    def step(i, sa):
        t = time_index(i)
        tn = time_index(jnp.minimum(i + 1, tb - 1))
        vt = v_ref[t]
        y = jnp.zeros_like(sa)
        sa_next = jnp.zeros_like(sa)
        for kq in range(kd):
            row = slice(kq, kq + 1)
            sk = s_s[kq] * w_ref[t, row, :] + sa * be_ref[t, row, :] + vt * k_ref[t, row, :]
            s_s[kq] = sk
            y = y + sk * r_ref[t, row, :]
            sa_next = sa_next + sk * al_ref[tn, row, :]
        y_ref[t] = y
        return sa_next

    lax.fori_loop(0, tb, step, sa0 + sa1)

    @pl.when(step_i == pl.num_programs(2) - 1)
    def _():
        st_ref[...] = s_s[...]


def rwkv_scan(r, w, k, al, be, v, s0):
    seq, kd, nch = r.shape
    vv = v.shape[1]
    tb = 32
    nblk = seq // tb
    zero_init = s0 is None

    def tblk(d, i):
        return i + d * (nblk - 1 - 2 * i)

    kspec = pl.BlockSpec((tb, kd, LANES), lambda d, g, i: (tblk(d, i), 0, g))
    wspec = pl.BlockSpec((None, tb, kd, LANES), lambda d, g, i: (d, tblk(d, i), 0, g))
    vspec = pl.BlockSpec((tb, vv, LANES), lambda d, g, i: (tblk(d, i), 0, g))
    yspec = pl.BlockSpec((None, tb, vv, LANES), lambda d, g, i: (d, tblk(d, i), 0, g))
    sspec = pl.BlockSpec((None, kd, vv, LANES), lambda d, g, i: (d, 0, 0, g))
    in_specs = [kspec, wspec, kspec, kspec, kspec, vspec]
    args = [r, w, k, al, be, v]
    if not zero_init:
        in_specs.append(sspec)
        args.append(s0)
    return pl.pallas_call(
        functools.partial(_rwkv_scan_kernel, tb, zero_init),
        grid=(2, nch // LANES, nblk),
        in_specs=in_specs,
        out_specs=[yspec, sspec],
        out_shape=[jax.ShapeDtypeStruct((2, seq, vv, nch), F32),
                   jax.ShapeDtypeStruct((2, kd, vv, nch), F32)],
        scratch_shapes=[pltpu.VMEM((kd, vv, LANES), F32)],
        compiler_params=_cparams(("parallel", "parallel", "arbitrary")),
        name="rwkv_scan",
    )(*args)


def _rwkv_post_kernel(y_ref, r_ref, k_ref, v_ref, g_ref, lng, lnb, rk, o_ref):
    inv = 1.0 / RWKV_HEAD_DIM
    for cb in range(RWKV_WIDTH // LANES):
        cols = slice(cb * LANES, (cb + 1) * LANES)
        y = y_ref[0, 0, :, cols] + y_ref[1, 0, :, cols]
        mean = _seg_sum(y) * inv
        yc = y - mean
        var = _seg_sum(yc * yc) * inv
        yn = yc * lax.rsqrt(var + GN_EPS) * lng[:, cols] + lnb[:, cols]
        bonus = _seg_sum(r_ref[0, :, cols] * k_ref[0, :, cols] * rk[:, cols]) * v_ref[0, :, cols]
        o_ref[0, :, cols] = ((yn + bonus) * g_ref[0, :, cols]).astype(BF16)


def rwkv_post(y2, r, k, v, g, p):
    _, bsz, seq, wd = y2.shape
    tl = 256
    spec = pl.BlockSpec((1, tl, wd), lambda b, i: (b, i, 0))
    vec = pl.BlockSpec((1, wd), lambda b, i: (0, 0))
    return pl.pallas_call(
        _rwkv_post_kernel,
        grid=(bsz, seq // tl),
        in_specs=[pl.BlockSpec((2, 1, tl, wd), lambda b, i: (0, b, i, 0))] + [spec] * 4 + [vec] * 3,
        out_specs=spec,
        out_shape=jax.ShapeDtypeStruct((bsz, seq, wd), BF16),
        compiler_params=_cparams(("parallel", "parallel")),
        name="rwkv_post",
    )(y2, r, k, v, g, p["ln_g"], p["ln_b"], p["r_k"])


def _split_bf16(x):
    hi = x.astype(BF16)
    return hi, (x - hi.astype(F32)).astype(BF16)


def _out_proj_kernel(ys_ref, yr_ref, wo_ref, x_ref, gm_ref, ng_ref, sc_ref, sh_ref, rwh_ref, rwl_ref,
                     x1_ref, h2_ref, lg_ref):
    acc = jnp.dot(ys_ref[0], wo_ref[0:SSD_WIDTH, :], preferred_element_type=F32)
    acc = acc + jnp.dot(yr_ref[0], wo_ref[SSD_WIDTH:, :], preferred_element_type=F32)
    x1 = x_ref[0] + gm_ref[0] * acc
    x1_ref[0] = x1
    hn = x1 * lax.rsqrt(jnp.mean(x1 * x1, axis=-1, keepdims=True) + NORM_EPS) * ng_ref[...]
    h2 = hn * (1.0 + sc_ref[0]) + sh_ref[0]
    hi, lo = _split_bf16(h2)
    h2_ref[0] = hi
    lg_ref[0] = (jnp.dot(hi, rwh_ref[...], preferred_element_type=F32)
                 + jnp.dot(lo, rwh_ref[...], preferred_element_type=F32)
                 + jnp.dot(hi, rwl_ref[...], preferred_element_type=F32))


def out_proj(y_ssd, y_rwkv, w_out, x, gate_m, norm_g, scale_f, shift_f, router_w):
    bsz, seq, d = x.shape
    tm = 256
    per_b = gate_m.shape[0] > 1
    midx = (lambda b, i: (b, 0, 0)) if per_b else (lambda b, i: (0, 0, 0))
    half = pl.BlockSpec((1, tm, d // 2), lambda b, i: (b, i, 0))
    full = pl.BlockSpec((1, tm, d), lambda b, i: (b, i, 0))
    mspec = pl.BlockSpec((1, 1, d), midx)
    return pl.pallas_call(
        _out_proj_kernel,
        grid=(bsz, seq // tm),
        in_specs=[half, half, pl.BlockSpec((d, d), lambda b, i: (0, 0)), full, mspec,
                  pl.BlockSpec((1, d), lambda b, i: (0, 0)), mspec, mspec,
                  pl.BlockSpec((d, LANES), lambda b, i: (0, 0)),
                  pl.BlockSpec((d, LANES), lambda b, i: (0, 0))],
        out_specs=[full, full, pl.BlockSpec((1, tm, LANES), lambda b, i: (b, i, 0))],
        out_shape=[jax.ShapeDtypeStruct((bsz, seq, d), F32), jax.ShapeDtypeStruct((bsz, seq, d), BF16),
                   jax.ShapeDtypeStruct((bsz, seq, LANES), F32)],
        compiler_params=_cparams(("parallel", "parallel")),
        name="out_proj",
    )(y_ssd, y_rwkv, w_out, x, gate_m, norm_g.reshape(1, d), scale_f, shift_f, *router_w)


def _route_kernel(nt, cap, lg_ref, selr_ref, posr_ref, selt_ref, post_ref, afft_ref, aff_s, pre_s):
    ne = N_EXPERTS
    lane = lax.broadcasted_iota(jnp.int32, (LANES, LANES), 1)
    sub = lax.broadcasted_iota(jnp.int32, (LANES, LANES), 0)
    upper_incl = (sub <= lane).astype(BF16)
    for i in range(nt):
        lg = jnp.where(lane < ne, lg_ref[i * LANES:(i + 1) * LANES, :], -jnp.inf)
        e = jnp.exp(lg - jnp.max(lg, axis=-1, keepdims=True))
        aff = e / jnp.sum(e, axis=-1, keepdims=True)
        afft_ref[i * LANES:(i + 1) * LANES, :] = aff
        aff_s[i] = aff.T[0:ne, :]
    aff3 = aff_s[...]

    def count(mask):
        s = jnp.sum(mask.astype(F32), axis=0, keepdims=True)
        return jnp.sum(s, axis=2, keepdims=True)

    def search(_, carry):
        lo, hi = carry
        mid = 0.5 * (lo + hi)
        ok = count(aff3 >= mid) >= cap
        return jnp.where(ok, mid, lo), jnp.where(ok, hi, mid)

    lo0 = jnp.zeros((1, ne, 1), F32)
    hi0 = jnp.full((1, ne, 1), 2.0, F32)
    lo, _ = lax.fori_loop(0, SEARCH_ITERS, search, (lo0, hi0))
    cand = jnp.where(aff3 >= lo, aff3, 4.0)
    thr = jnp.min(jnp.min(cand, axis=0, keepdims=True), axis=2, keepdims=True)
    gt = (aff3 > thr).astype(F32)
    eq = (aff3 == thr).astype(F32)
    need = cap - count(aff3 > thr)[0]

    def prefix_excl(m3):
        off = jnp.zeros((ne, 1), F32)
        for i in range(nt):
            inc = jnp.dot(m3[i].astype(BF16), upper_incl, preferred_element_type=F32)
            pre_s[i] = inc - m3[i] + off
            off = off + inc[:, LANES - 1:LANES]
        return pre_s[...]

    sel = jnp.maximum(gt, eq * (prefix_excl(eq) < need[None]).astype(F32))
    pos = prefix_excl(sel)
    selr_ref[...] = sel
    posr_ref[...] = pos
    zpad = jnp.zeros((LANES - ne, LANES), F32)
    for i in range(nt):
        rows = slice(i * LANES, (i + 1) * LANES)
        selt_ref[rows, :] = jnp.concatenate([sel[i], zpad], axis=0).T
        post_ref[rows, :] = jnp.concatenate([pos[i], zpad], axis=0).T


def route(logits, cap):
    n_tok = logits.shape[0]
    nt = n_tok // LANES
    row_sh = jax.ShapeDtypeStruct((nt, N_EXPERTS, LANES), F32)
    tm_sh = jax.ShapeDtypeStruct((n_tok, LANES), F32)
    return pl.pallas_call(
        functools.partial(_route_kernel, nt, cap),
        out_shape=[row_sh, row_sh, tm_sh, tm_sh, tm_sh],
        scratch_shapes=[pltpu.VMEM((nt, N_EXPERTS, LANES), F32), pltpu.VMEM((nt, N_EXPERTS, LANES), F32)],
        compiler_params=pltpu.CompilerParams(vmem_limit_bytes=VMEM_LIMIT),
        name="route",
    )(logits)


def _window_start(starts_ref, tile, e, cap, win, align=SUBLANES):
    ps = starts_ref[tile * N_EXPERTS + e]
    return pl.multiple_of(jnp.minimum((ps // align) * align, cap - win), align)


def _gather_kernel(cap, win, nsub, starts_ref, sel_ref, pos_ref, h_ref, o_ref):
    eg = pl.program_id(0)
    j = pl.program_id(1)

    @pl.when(j == 0)
    def _():
        o_ref[...] = jnp.zeros_like(o_ref)

    for el in range(GATHER_EXPERTS):
        e = eg * GATHER_EXPERTS + el
        for s in range(0, nsub, 2):
            tile = j * nsub + s
            s0 = _window_start(starts_ref, tile, e, cap, win, align=2 * SUBLANES)
            lo = starts_ref[tile * N_EXPERTS + e]
            hi = starts_ref[(tile + 2) * N_EXPERTS + e]
            prows = [pos_ref[s + u, pl.ds(e, 1), :] for u in range(2)]
            srows = [sel_ref[s + u, pl.ds(e, 1), :] for u in range(2)]
            for piece in range(win // GATHER_ROWS):
                p0 = pl.multiple_of(s0 + piece * GATHER_ROWS, 2 * SUBLANES)

                @pl.when(jnp.logical_and(p0 < hi, p0 + GATHER_ROWS > lo))
                def _(p0=p0, s=s, el=el, prows=prows, srows=srows):
                    slot = (lax.broadcasted_iota(jnp.int32, (GATHER_ROWS, LANES), 0) + p0).astype(F32)
                    onehot = jnp.concatenate(
                        [jnp.where(jnp.logical_and(slot == prows[u], srows[u] > 0.0), 1.0, 0.0).astype(BF16)
                         for u in range(2)], axis=1)
                    rows = pl.ds(p0, GATHER_ROWS)
                    got = jnp.dot(onehot, h_ref[s * LANES:(s + 2) * LANES, :], preferred_element_type=F32)
                    o_ref[el, rows, :] = (o_ref[el, rows, :].astype(F32) + got).astype(BF16)


def gather_tokens(starts, sel_r, pos_r, h, cap):
    n_tok, d = h.shape
    tt = min(1024, n_tok)
    nsub = tt // LANES
    win = min(-(-(2 * LANES + 2 * SUBLANES) // GATHER_ROWS) * GATHER_ROWS, cap)
    assert win % GATHER_ROWS == 0 and (cap - win) % (2 * SUBLANES) == 0
    rspec = pl.BlockSpec((nsub, N_EXPERTS, LANES), lambda e, j, st: (j, 0, 0))
    return pl.pallas_call(
        functools.partial(_gather_kernel, cap, win, nsub),
        grid_spec=pltpu.PrefetchScalarGridSpec(
            num_scalar_prefetch=1,
            grid=(N_EXPERTS // GATHER_EXPERTS, n_tok // tt),
            in_specs=[rspec, rspec, pl.BlockSpec((tt, d), lambda e, j, st: (j, 0))],
            out_specs=pl.BlockSpec((GATHER_EXPERTS, cap, d), lambda e, j, st: (e, 0, 0))),
        out_shape=jax.ShapeDtypeStruct((N_EXPERTS, cap, d), BF16),
        compiler_params=_cparams(("parallel", "arbitrary")),
        name="gather_tokens",
    )(starts, sel_r, pos_r, h)


def _ffn_kernel(n_sets, *refs):
    x_refs = refs[:n_sets]
    wg_ref, wu_ref, wd_ref = refs[n_sets:n_sets + 3]
    o_refs = refs[n_sets + 3:2 * n_sets + 3]
    accs = refs[2 * n_sets + 3:]
    f = pl.program_id(1)
    wg = wg_ref[0].astype(BF16)
    wu = wu_ref[0].astype(BF16)
    wd = wd_ref[0].astype(BF16)
    for x_ref, o_ref, acc in zip(x_refs, o_refs, accs):
        @pl.when(f == 0)
        def _():
            acc[...] = jnp.zeros_like(acc)

        x = x_ref[0]
        hg = jnp.dot(x, wg, preferred_element_type=F32)
        hu = jnp.dot(x, wu, preferred_element_type=F32)
        hid = (_silu(hg) * hu).astype(BF16)
        acc[...] += jnp.dot(hid, wd, preferred_element_type=F32)

        @pl.when(f == pl.num_programs(1) - 1)
        def _():
            o_ref[0] = acc[...].astype(BF16)


def expert_ffn(xs_sets, wg, wu, wd):
    ne, _, d = xs_sets[0].shape
    ff = wg.shape[2]
    tf = 256
    xspecs = [pl.BlockSpec((1, xs.shape[1], d), lambda e, f: (e, 0, 0)) for xs in xs_sets]
    return pl.pallas_call(
        functools.partial(_ffn_kernel, len(xs_sets)),
        grid=(ne, ff // tf),
        in_specs=xspecs + [pl.BlockSpec((1, d, tf), lambda e, f: (e, 0, f)),
                           pl.BlockSpec((1, d, tf), lambda e, f: (e, 0, f)),
                           pl.BlockSpec((1, tf, d), lambda e, f: (e, f, 0))],
        out_specs=xspecs,
        out_shape=[jax.ShapeDtypeStruct(xs.shape, BF16) for xs in xs_sets],
        scratch_shapes=[pltpu.VMEM(xs.shape[1:], F32) for xs in xs_sets],
        compiler_params=_cparams(("parallel", "arbitrary")),
        name="expert_ffn",
    )(*xs_sets, wg, wu, wd)


def _scatter_kernel(cap, win, nsub, starts_ref, sel_ref, pos_ref, aff_ref, ob_ref, y_ref):
    c = pl.program_id(0)
    e = pl.program_id(1)

    @pl.when(e == 0)
    def _():
        y_ref[...] = jnp.zeros_like(y_ref)

    mine = lax.broadcasted_iota(jnp.int32, (LANES, LANES), 1) == e
    for s in range(nsub):
        rows = slice(s * LANES, (s + 1) * LANES)

        def col(ref):
            return jnp.sum(jnp.where(mine, ref[rows, :], 0.0), axis=-1, keepdims=True)

        pcol, scol, gcol = col(pos_ref), col(sel_ref), col(aff_ref)
        s0 = _window_start(starts_ref, c * nsub + s, e, cap, win)
        slot = (lax.broadcasted_iota(jnp.int32, (LANES, win), 1) + s0).astype(F32)
        onehot = jnp.where(jnp.logical_and(slot == pcol, scol > 0.0), 1.0, 0.0).astype(BF16)
        y_ref[rows, :] += gcol * jnp.dot(onehot, ob_ref[0, pl.ds(s0, win), :],
                                         preferred_element_type=F32)


def scatter_combine(starts, sel_t, pos_t, aff_t, outbuf):
    n_tok = sel_t.shape[0]
    ne, cap, d = outbuf.shape
    chunk = min(2048, n_tok)
    nsub = chunk // LANES
    win = min(2 * LANES, cap)
    tspec = pl.BlockSpec((chunk, LANES), lambda c, e, st: (c, 0))
    return pl.pallas_call(
        functools.partial(_scatter_kernel, cap, win, nsub),
        grid_spec=pltpu.PrefetchScalarGridSpec(
            num_scalar_prefetch=1,
            grid=(n_tok // chunk, ne),
            in_specs=[tspec, tspec, tspec, pl.BlockSpec((1, cap, d), lambda c, e, st: (e, 0, 0))],
            out_specs=pl.BlockSpec((chunk, d), lambda c, e, st: (c, 0))),
        out_shape=jax.ShapeDtypeStruct((n_tok, d), F32),
        compiler_params=_cparams(("parallel", "arbitrary")),
        name="scatter_combine",
    )(starts, sel_t, pos_t, aff_t, outbuf)


def _final_kernel(x1_ref, y_ref, gf_ref, fg_ref, o_ref):
    x2 = x1_ref[0] + gf_ref[0] * y_ref[0]
    o_ref[0] = x2 * lax.rsqrt(jnp.mean(x2 * x2, axis=-1, keepdims=True) + NORM_EPS) * fg_ref[...]


def final_norm(x1, y, gate_f, final_g):
    bsz, seq, d = x1.shape
    tl = 256
    per_b = gate_f.shape[0] > 1
    midx = (lambda b, i: (b, 0, 0)) if per_b else (lambda b, i: (0, 0, 0))
    xspec = pl.BlockSpec((1, tl, d), lambda b, i: (b, i, 0))
    return pl.pallas_call(
        _final_kernel,
        grid=(bsz, seq // tl),
        in_specs=[xspec, xspec, pl.BlockSpec((1, 1, d), midx), pl.BlockSpec((1, d), lambda b, i: (0, 0))],
        out_specs=xspec,
        out_shape=jax.ShapeDtypeStruct((bsz, seq, d), F32),
        compiler_params=_cparams(("parallel", "parallel")),
        name="final_norm",
    )(x1, y, gate_f, final_g.reshape(1, d))


def _pad_cols(w, width):
    return jnp.pad(w, ((0, 0), (0, width - w.shape[1])))


def _pad_rows(w, rows):
    return jnp.pad(w, ((0, rows - w.shape[0]), (0, 0)))


def _relayout_columns(w):
    n_ssd = 2 * SSD_WIDTH + 2 * SSD_GROUPS * SSD_STATE + 2 * SSD_HEADS
    ssd, rw = w[:, :n_ssd], w[:, n_ssd:]
    o = 3 * RWKV_WIDTH
    zx = 2 * SSD_WIDTH
    bc = zx + 2 * SSD_GROUPS * SSD_STATE
    parts = [ssd[:, :zx], rw[:, :o], ssd[:, zx:bc], _pad_cols(ssd[:, bc:], LANES),
             _pad_cols(rw[:, o:o + DECAY_LORA], LANES),
             _pad_cols(rw[:, o + DECAY_LORA:o + 2 * DECAY_LORA], LANES),
             _pad_cols(rw[:, o + 2 * DECAY_LORA:o + 2 * DECAY_LORA + ICLR_LORA], LANES),
             _pad_cols(rw[:, o + 2 * DECAY_LORA + ICLR_LORA:], 2 * LANES)]
    return jnp.concatenate(parts, axis=1)


def _layer_params(l, w_in, w_out, ssd_conv_w, ssd_conv_b, ssd_A_log, ssd_dt_bias, ssd_D, ssd_norm_g,
                  rwkv_mu, rwkv_w0, rwkv_w_up, rwkv_a0, rwkv_a_up, rwkv_g_up, rwkv_k_k, rwkv_k_a,
                  rwkv_r_k, rwkv_ln_g, rwkv_ln_b, router_w):
    row = lambda a: a.reshape(1, -1)
    n_ssd = 2 * SSD_WIDTH + 2 * SSD_GROUPS * SSD_STATE + 2 * SSD_HEADS
    mu_full = _relayout_columns(jnp.concatenate([jnp.zeros((1, n_ssd), F32), row(rwkv_mu[l])], axis=1))
    cw, cb = ssd_conv_w[l], row(ssd_conv_b[l])
    xe, be = SSD_WIDTH, SSD_WIDTH + SSD_GROUPS * SSD_STATE
    return dict(
        w_in=_relayout_columns(w_in[l]).astype(BF16),
        w_out=w_out[l].astype(BF16),
        cw_x=cw[:, :xe], cb_x=cb[:, :xe], cw_b=cw[:, xe:be], cb_b=cb[:, xe:be],
        cw_c=cw[:, be:], cb_c=cb[:, be:],
        dt_bias=_pad_cols(row(ssd_dt_bias[l]), LANES), a_log=_pad_cols(row(ssd_A_log[l]), LANES),
        d_exp=row(jnp.repeat(ssd_D[l], SSD_HEAD_DIM)), ssd_ng=row(ssd_norm_g[l]),
        mu_r=mu_full[:, COL_R:COL_K], mu_k=mu_full[:, COL_K:COL_V], mu_v=mu_full[:, COL_V:COL_B],
        mu_lora=mu_full[:, COL_LORA:],
        w_up_f=_pad_rows(rwkv_w_up[l, 0], LANES), w_up_b=_pad_rows(rwkv_w_up[l, 1], LANES),
        a_up=_pad_rows(rwkv_a_up[l], LANES), g_up=_pad_rows(rwkv_g_up[l], 2 * LANES),
        w0_f=row(rwkv_w0[l, 0]), w0_b=row(rwkv_w0[l, 1]), a0=row(rwkv_a0[l]),
        k_k=row(rwkv_k_k[l]), k_a=row(rwkv_k_a[l]), r_k=row(rwkv_r_k[l]),
        ln_g=row(rwkv_ln_g[l]), ln_b=row(rwkv_ln_b[l]),
        router_w=tuple(_pad_cols(t, LANES) for t in _split_bf16(router_w[l])),
    )


def _to_chain(x):
    lead = x.shape[:-3]
    bsz, seq, _ = x.shape[-3:]
    n = len(lead)
    x = x.reshape(lead + (bsz, seq, RWKV_HEADS, RWKV_HEAD_DIM))
    perm = tuple(range(n)) + (n + 1, n + 3, n, n + 2)
    return jnp.transpose(x, perm).reshape(lead + (seq, RWKV_HEAD_DIM, bsz * RWKV_HEADS))


def _from_chain(y, bsz):
    lead = y.shape[:-3]
    seq = y.shape[-3]
    n = len(lead)
    y = y.reshape(lead + (seq, RWKV_HEAD_DIM, bsz, RWKV_HEADS))
    perm = tuple(range(n)) + (n + 2, n, n + 3, n + 1)
    return jnp.transpose(y, perm).reshape(lead + (bsz, seq, RWKV_WIDTH))


def _state_to_chain(s):
    return jnp.transpose(s, (0, 4, 3, 1, 2)).reshape(2, RWKV_HEAD_DIM, RWKV_HEAD_DIM, -1)


def _state_from_chain(s, bsz):
    return jnp.transpose(s.reshape(2, RWKV_HEAD_DIM, RWKV_HEAD_DIM, bsz, RWKV_HEADS), (0, 3, 4, 2, 1))


def rwkv_mixer_scan(r, k, v, w2, al, be, s0):
    bsz, seq, _ = r.shape
    nh, hd = RWKV_HEADS, RWKV_HEAD_DIM
    nch = bsz * nh
    rep = max(1, LANES // nch)
    assert (nch * rep) % LANES == 0
    if rep == 1:
        rc, kc, vc, alc, bec, wc = (_to_chain(t) for t in (r, k, v, al, be, w2))
        s0c = None if s0 is None else _state_to_chain(s0)
        y, st = rwkv_scan(rc, wc, kc, alc, bec, vc, s0c)
        return _from_chain(y, bsz), _state_from_chain(st, bsz)

    vq = hd // rep

    def keys_to_chain(x):
        lead = x.shape[:-3]
        n = len(lead)
        x = jnp.broadcast_to(x.reshape(lead + (1, bsz, seq, nh, hd)), lead + (rep, bsz, seq, nh, hd))
        perm = tuple(range(n)) + (n + 2, n + 4, n, n + 1, n + 3)
        return jnp.transpose(x, perm).reshape(lead + (seq, hd, LANES))

    rc, kc, alc, bec, wc = (keys_to_chain(t) for t in (r, k, al, be, w2))
    vc = jnp.transpose(v.reshape(bsz, seq, nh, rep, vq), (1, 4, 3, 0, 2)).reshape(seq, vq, LANES)
    s0c = None
    if s0 is not None:
        s0c = jnp.transpose(s0.reshape(2, bsz, nh, rep, vq, hd), (0, 5, 4, 3, 1, 2)).reshape(
            2, hd, vq, LANES)
    y, st = rwkv_scan(rc, wc, kc, alc, bec, vc, s0c)
    y = jnp.transpose(y.reshape(2, seq, vq, rep, bsz, nh), (0, 4, 1, 5, 3, 2)).reshape(
        2, bsz, seq, RWKV_WIDTH)
    st = jnp.transpose(st.reshape(2, hd, vq, rep, bsz, nh), (0, 4, 5, 3, 2, 1)).reshape(
        2, bsz, nh, hd, hd)
    return y, st


def _grid_pos_embed(n_tokens, dim):
    rows = n_tokens // GRID_W
    row = jnp.repeat(jnp.arange(rows, dtype=F32), GRID_W)
    col = jnp.tile(jnp.arange(GRID_W, dtype=F32), rows)
    quarter = dim // 4
    freqs = jnp.exp(jnp.arange(quarter, dtype=F32) * (-math.log(POS_BASE) / quarter))

    def axis_embed(pos):
        ang = pos[:, None] * freqs[None, :]
        return jnp.concatenate([jnp.sin(ang), jnp.cos(ang)], axis=-1)

    return jnp.concatenate([axis_embed(row), axis_embed(col)], axis=-1)


def layer_to_dispatch(x, pos, mod, p, norm_mix_g, norm_ffn_g, states):
    bsz, seq, d = x.shape
    shift_m, scale_m, gate_m, shift_f, scale_f, gate_f = mod
    h, x0 = norm_modulate(x, pos, norm_mix_g, scale_m, shift_m)
    n_tok = bsz * seq
    proj = matmul_bf16(h.reshape(n_tok, d), p["w_in"], min(2048, n_tok), 640).reshape(bsz, seq, N_PROJ)
    h0f, h0b, s0f, s0b = states
    y_ssd, hf, hb = ssd_mixer(proj, p, h0f, h0b)
    r, k, v, w2, al, be, g = rwkv_prep(proj, p)
    s0 = None if s0f is None else jnp.stack([s0f, s0b], axis=0)
    y2, st = rwkv_mixer_scan(r, k, v, w2, al, be, s0)
    y_rwkv = rwkv_post(y2, r, k, v, g, p)
    x1, h2, logits = out_proj(y_ssd, y_rwkv, p["w_out"], x0, gate_m, norm_ffn_g, scale_f, shift_f,
                              p["router_w"])
    cap = EC_CAPACITY * n_tok // N_EXPERTS
    sel_r, pos_r, sel_t, pos_t, aff_t = route(logits.reshape(n_tok, LANES), cap)
    starts = jnp.concatenate([pos_r[:, :, 0], jnp.full((1, N_EXPERTS), cap, F32)], axis=0)
    starts = starts.astype(jnp.int32).reshape(-1)
    xs = gather_tokens(starts, sel_r, pos_r, h2.reshape(n_tok, d), cap)
    routed = dict(xs=xs, starts=starts, sel_t=sel_t, pos_t=pos_t, aff_t=aff_t, x1=x1, gate_f=gate_f)
    return routed, (hf, hb, st[0], st[1])


def combine_and_finish(routed, outbuf, final_g):
    x1 = routed["x1"]
    y_ffn = scatter_combine(routed["starts"], routed["sel_t"], routed["pos_t"], routed["aff_t"], outbuf)
    return final_norm(x1, y_ffn.reshape(x1.shape), routed["gate_f"], final_g)


def kernel(x_prompt, x_sample, state_ssd_fwd, state_ssd_bwd, state_rwkv_fwd, state_rwkv_bwd, c, c_ctx,
           w_ada, b_ada, norm_mix_g, norm_ffn_g, w_in, w_out, ssd_conv_w, ssd_conv_b, ssd_A_log,
           ssd_dt_bias, ssd_D, ssd_norm_g, rwkv_mu, rwkv_w0, rwkv_w_up, rwkv_a0, rwkv_a_up, rwkv_g_up,
           rwkv_k_k, rwkv_k_a, rwkv_r_k, rwkv_ln_g, rwkv_ln_b, router_w, exp_w_gate, exp_w_up,
           exp_w_down, final_norm_g):
    depth = w_in.shape[0]
    assert depth == 1, "the final norm runs right after the single layer's FFN residual"
    l = 0
    dec_b = x_sample.shape[0]
    p = _layer_params(l, w_in, w_out, ssd_conv_w, ssd_conv_b, ssd_A_log, ssd_dt_bias, ssd_D, ssd_norm_g,
                      rwkv_mu, rwkv_w0, rwkv_w_up, rwkv_a0, rwkv_a_up, rwkv_g_up, rwkv_k_k, rwkv_k_a,
                      rwkv_r_k, rwkv_ln_g, rwkv_ln_b, router_w)
    cond = jnp.concatenate([c_ctx[None, :], c, jnp.zeros((16 - 1 - dec_b, D_MODEL), F32)], axis=0)
    mod = ada_modulation(cond, w_ada[l], b_ada[l]).reshape(16, N_MOD, 1, D_MODEL)
    mod_ctx = [mod[0:1, i] for i in range(N_MOD)]
    mod_lat = [mod[1:1 + dec_b, i] for i in range(N_MOD)]

    routed_ctx, st = layer_to_dispatch(x_prompt, None, mod_ctx, p, norm_mix_g[l], norm_ffn_g[l],
                                       (None, None, None, None))
    pos = _grid_pos_embed(x_sample.shape[1], D_MODEL)
    cached = (state_ssd_fwd[:, l], state_ssd_bwd[:, l], state_rwkv_fwd[:, l], state_rwkv_bwd[:, l])
    routed_lat, _ = layer_to_dispatch(x_sample, pos, mod_lat, p, norm_mix_g[l], norm_ffn_g[l], cached)
    out_ctx, out_lat = expert_ffn([routed_ctx["xs"], routed_lat["xs"]],
                                  exp_w_gate[l], exp_w_up[l], exp_w_down[l])
    y_prompt = combine_and_finish(routed_ctx, out_ctx, final_norm_g)
    y_sample = combine_and_finish(routed_lat, out_lat, final_norm_g)
    return (y_prompt, y_sample, st[0][:, None], st[1][:, None], st[2][:, None], st[3][:, None])
```

```python
import functools
import math

import jax
import jax.numpy as jnp
from jax import lax
from jax.experimental import pallas as pl
from jax.experimental.pallas import tpu as pltpu

F32 = jnp.float32
BF16 = jnp.bfloat16
HIGHEST = lax.Precision.HIGHEST

D_MODEL = 2048
GRID_W = 64
SSD_WIDTH = 1024
SSD_HEAD_DIM = 64
SSD_HEADS = 16
SSD_GROUPS = 2
SSD_STATE = 128
SSD_CONV_W = 5
SSD_CHUNK = 128
RWKV_WIDTH = 1024
RWKV_HEAD_DIM = 64
RWKV_HEADS = 16
DECAY_LORA = 64
ICLR_LORA = 64
GATE_LORA = 160
N_EXPERTS = 16
EC_CAPACITY = 2
EXPERT_FF = 1024
N_MOD = 6
NORM_EPS = 1e-6
GN_EPS = 64e-5
DECAY_SCALE = 0.606531
POS_BASE = 10000.0
SEARCH_ITERS = 48

LANES = 128
SUBLANES = 8
GATHER_ROWS = 64
GATHER_EXPERTS = 4

COL_Z = 0
COL_X = 1024
COL_R = 2048
COL_K = 3072
COL_V = 4096
COL_B = 5120
COL_C = 5376
COL_DT = 5632
COL_LORA = 5760
PREP_BLOCK_ELEMS = 256 * 1024
LORA_W = 640
N_PROJ = COL_LORA + LORA_W

VMEM_LIMIT = 56 * 1024 * 1024


def _cparams(sem):
    return pltpu.CompilerParams(dimension_semantics=sem, vmem_limit_bytes=VMEM_LIMIT)


def _silu(x):
    return x * jax.nn.sigmoid(x)


def _ada_kernel(c_ref, w_ref, b_ref, o_ref):
    s = _silu(c_ref[...]).astype(BF16)
    o_ref[...] = jnp.dot(s, w_ref[...].astype(BF16), preferred_element_type=F32) + b_ref[...]


def ada_modulation(cond, w, b):
    m, d = cond.shape
    n = w.shape[1]
    tn = 1024
    return pl.pallas_call(
        _ada_kernel,
        grid=(n // tn,),
        in_specs=[pl.BlockSpec((m, d), lambda j: (0, 0)),
                  pl.BlockSpec((d, tn), lambda j: (0, j)),
                  pl.BlockSpec((1, tn), lambda j: (0, j))],
        out_specs=pl.BlockSpec((m, tn), lambda j: (0, j)),
        out_shape=jax.ShapeDtypeStruct((m, n), F32),
        compiler_params=_cparams(("parallel",)),
        name="ada_modulation",
    )(cond, w, b.reshape(1, n))


def _norm_mod_kernel(has_pos, *refs):
    if has_pos:
        x_ref, pos_ref, g_ref, sc_ref, sh_ref, h_ref, x0_ref = refs
    else:
        x_ref, g_ref, sc_ref, sh_ref, h_ref = refs
    x = x_ref[0]
    if has_pos:
        x = x + pos_ref[...]
        x0_ref[0] = x
    y = x * lax.rsqrt(jnp.mean(x * x, axis=-1, keepdims=True) + NORM_EPS) * g_ref[...]
    h_ref[0] = (y * (1.0 + sc_ref[0]) + sh_ref[0]).astype(BF16)


def norm_modulate(x, pos, g, scale, shift):
    bsz, seq, d = x.shape
    tl = 256
    per_b = scale.shape[0] > 1
    midx = (lambda b, i: (b, 0, 0)) if per_b else (lambda b, i: (0, 0, 0))
    xspec = pl.BlockSpec((1, tl, d), lambda b, i: (b, i, 0))
    in_specs = [xspec]
    args = [x]
    if pos is not None:
        in_specs.append(pl.BlockSpec((tl, d), lambda b, i: (i, 0)))
        args.append(pos)
    in_specs += [pl.BlockSpec((1, d), lambda b, i: (0, 0)),
                 pl.BlockSpec((1, 1, d), midx), pl.BlockSpec((1, 1, d), midx)]
    args += [g.reshape(1, d), scale, shift]
    out_shape = [jax.ShapeDtypeStruct((bsz, seq, d), BF16)]
    out_specs = [xspec]
    if pos is not None:
        out_shape.append(jax.ShapeDtypeStruct((bsz, seq, d), F32))
        out_specs.append(xspec)
    res = pl.pallas_call(
        functools.partial(_norm_mod_kernel, pos is not None),
        grid=(bsz, seq // tl),
        in_specs=in_specs, out_specs=out_specs, out_shape=out_shape,
        compiler_params=_cparams(("parallel", "parallel")),
        name="norm_modulate",
    )(*args)
    return (res[0], res[1]) if pos is not None else (res[0], x)


def _mm_kernel(a_ref, b_ref, o_ref):
    o_ref[...] = jnp.dot(a_ref[...], b_ref[...], preferred_element_type=F32)


def matmul_bf16(a, b, tm, tn):
    m, k = a.shape
    n = b.shape[1]
    return pl.pallas_call(
        _mm_kernel,
        grid=(m // tm, n // tn),
        in_specs=[pl.BlockSpec((tm, k), lambda i, j: (i, 0)),
                  pl.BlockSpec((k, tn), lambda i, j: (0, j))],
        out_specs=pl.BlockSpec((tm, tn), lambda i, j: (i, j)),
        out_shape=jax.ShapeDtypeStruct((m, n), F32),
        compiler_params=_cparams(("parallel", "arbitrary")),
        name="in_proj",
    )(a, b)


def _shifted(u, off):
    n = u.shape[0]
    row = lax.broadcasted_iota(jnp.int32, (n, 1), 0)
    rolled = pltpu.roll(u, (-off) % n, axis=0)
    valid = jnp.logical_and(row + off >= 0, row + off < n)
    return jnp.where(valid, rolled, 0.0)


def _conv_silu(u, w_ref, b_ref, cols):
    pad = SSD_CONV_W // 2
    acc = u * w_ref[pad:pad + 1, cols] + b_ref[:, cols]
    for j in range(SSD_CONV_W):
        if j != pad:
            acc = acc + _shifted(u, j - pad) * w_ref[j:j + 1, cols]
    return _silu(acc)


def _softplus(x):
    return jnp.maximum(x, 0.0) + jnp.log1p(jnp.exp(-jnp.abs(x)))


def _ssd_kernel(nc, zero_init, *refs):
    q = SSD_CHUNK
    (z_ref, x_ref, b_ref, c_ref, dt_ref, cwx, cbx, cwb, cbb, cwc, cbc,
     dtb_ref, alog_ref, d_ref, ng_ref) = refs[:15]
    refs = refs[15:]
    if not zero_init:
        h0f_ref, h0b_ref = refs[:2]
        refs = refs[2:]
    y_ref, hf_ref, hb_ref, xh_s, bm_s, cm_s, dt_s, a_s, at_s, y_s, h_s = refs

    slab = 256
    for cb in range(SSD_WIDTH // slab):
        cols = slice(cb * slab, (cb + 1) * slab)
        xh = _conv_silu(x_ref[0, :, cols], cwx, cbx, cols)
        for c in range(nc):
            rows = slice(c * q, (c + 1) * q)
            xh_s[c, :, cols] = xh[rows]
            y_s[c, :, cols] = xh[rows] * d_ref[:, cols]
    allc = slice(0, SSD_GROUPS * SSD_STATE)
    bm = _conv_silu(b_ref[0], cwb, cbb, allc)
    cm = _conv_silu(c_ref[0], cwc, cbc, allc)
    dt = _softplus(dt_ref[0] + dtb_ref[...])
    a = dt * (-jnp.exp(alog_ref[...]))
    ii = lax.broadcasted_iota(jnp.int32, (q, q), 0)
    jj = lax.broadcasted_iota(jnp.int32, (q, q), 1)
    eye = (ii == jj).astype(F32)
    lower = ii >= jj
    upper = ii <= jj
    first_half = jj < SSD_HEAD_DIM
    for c in range(nc):
        rows = slice(c * q, (c + 1) * q)
        for g in range(SSD_GROUPS):
            gcols = slice(g * SSD_STATE, (g + 1) * SSD_STATE)
            bm_s[c, gcols, :] = bm[rows, gcols].T
        cm_s[c] = cm[rows]
        dt_s[c] = dt[rows]
        a_s[c] = a[rows]
        at_s[c] = lax.dot_general(eye, a[rows], (((1,), (1,)), ((), ())),
                                  precision=HIGHEST, preferred_element_type=F32)

    for d in range(2):
        mask = lower if d == 0 else upper
        tri = mask.astype(F32)
        tri_t = (upper if d == 0 else lower).astype(F32)
        if zero_init:
            h_s[...] = jnp.zeros_like(h_s)
        else:
            h0_ref = h0f_ref if d == 0 else h0b_ref
            for p in range(SSD_HEADS // 2):
                h_s[p] = jnp.concatenate([h0_ref[0, 2 * p], h0_ref[0, 2 * p + 1]], axis=0).T

        def chunk_body(ci, carry, d=d, mask=mask, tri=tri, tri_t=tri_t):
            c = ci if d == 0 else nc - 1 - ci
            a_c = a_s[c]
            cum = jnp.dot(tri, a_c, precision=HIGHEST, preferred_element_type=F32)
            cum_t = jnp.dot(at_s[c], tri_t, precision=HIGHEST, preferred_element_type=F32)
            dt_c = dt_s[c]
            for g in range(SSD_GROUPS):
                gcols = slice(g * SSD_STATE, (g + 1) * SSD_STATE)
                bt = bm_s[c, gcols, :]
                cg = cm_s[c, :, gcols]
                gram = jnp.dot(cg.astype(BF16), bt.astype(BF16), preferred_element_type=F32)
                for pp in range(SSD_HEADS // SSD_GROUPS // 2):
                    p = g * (SSD_HEADS // SSD_GROUPS // 2) + pp
                    pcols = slice(p * LANES, (p + 1) * LANES)
                    xh_pair = xh_s[c, :, pcols]
                    hprev = h_s[p]
                    hprev_b = hprev.astype(BF16)
                    parts = []
                    for e in range(2):
                        col = d * SSD_HEADS + 2 * p + e
                        mine = first_half if e == 0 else jnp.logical_not(first_half)
                        cc = jnp.broadcast_to(cum[:, col:col + 1], (q, q))
                        cr = cum_t[col:col + 1, :]
                        dec = jnp.exp(jnp.where(mask, cc - cr, -jnp.inf))
                        m = (gram * dec).astype(BF16)
                        xdt = jnp.where(mine, xh_pair * dt_c[:, col:col + 1], 0.0).astype(BF16)
                        tot = cum[q - 1:q, col:col + 1] if d == 0 else cum[0:1, col:col + 1]
                        bd_t = (bt * jnp.exp(tot - cr)).astype(BF16)
                        cd = (cg * jnp.exp(cc)).astype(BF16)
                        s_c = jnp.dot(bd_t, xdt, preferred_element_type=F32)
                        y_off = jnp.dot(cd, hprev_b, preferred_element_type=F32)
                        y_diag = jnp.dot(m, xdt, preferred_element_type=F32)
                        parts.append((s_c, y_off, y_diag, jnp.exp(tot)))
                    (s0, yo0, yd0, g0), (s1, yo1, yd1, g1) = parts
                    h_s[p] = hprev * jnp.where(first_half, g0, g1) + (s0 + s1)
                    y_s[c, :, pcols] += (yd0 + yd1) + jnp.where(first_half, yo0, yo1)
            return carry

        lax.fori_loop(0, nc, chunk_body, 0)
        out_ref = hf_ref if d == 0 else hb_ref
        for p in range(SSD_HEADS // 2):
            tr = h_s[p].T
            out_ref[0, 2 * p] = tr[0:SSD_HEAD_DIM]
            out_ref[0, 2 * p + 1] = tr[SSD_HEAD_DIM:]

    gw = SSD_WIDTH // SSD_GROUPS
    for c in range(nc):
        rows = slice(c * q, (c + 1) * q)
        for g in range(SSD_GROUPS):
            cols = slice(g * gw, (g + 1) * gw)
            y = y_s[c, :, cols] * _silu(z_ref[0, rows, cols])
            y = y * lax.rsqrt(jnp.mean(y * y, axis=-1, keepdims=True) + NORM_EPS)
            y_ref[0, rows, cols] = (y * ng_ref[:, cols]).astype(BF16)


def ssd_mixer(proj, p, h0f, h0b):
    bsz, seq, _ = proj.shape
    nc = seq // SSD_CHUNK
    zero_init = h0f is None

    def col_spec(width, start):
        blk = start // width
        return pl.BlockSpec((1, seq, width), lambda b: (b, 0, blk))

    def full2(arr):
        return pl.BlockSpec(arr.shape, lambda b: (0, 0))

    st_spec = pl.BlockSpec((1, SSD_HEADS, SSD_HEAD_DIM, SSD_STATE), lambda b: (b, 0, 0, 0))
    bc = SSD_GROUPS * SSD_STATE
    small = [p["cw_x"], p["cb_x"], p["cw_b"], p["cb_b"], p["cw_c"], p["cb_c"],
             p["dt_bias"], p["a_log"], p["d_exp"], p["ssd_ng"]]
    in_specs = [col_spec(SSD_WIDTH, COL_Z), col_spec(SSD_WIDTH, COL_X), col_spec(bc, COL_B),
                col_spec(bc, COL_C), col_spec(LANES, COL_DT)] + [full2(s) for s in small]
    args = [proj] * 5 + small
    if not zero_init:
        in_specs += [st_spec, st_spec]
        args += [h0f, h0b]
    st_shape = jax.ShapeDtypeStruct((bsz, SSD_HEADS, SSD_HEAD_DIM, SSD_STATE), F32)
    q = SSD_CHUNK
    return pl.pallas_call(
        functools.partial(_ssd_kernel, nc, zero_init),
        grid=(bsz,),
        in_specs=in_specs,
        out_specs=[pl.BlockSpec((1, seq, SSD_WIDTH), lambda b: (b, 0, 0)), st_spec, st_spec],
        out_shape=[jax.ShapeDtypeStruct((bsz, seq, SSD_WIDTH), BF16), st_shape, st_shape],
        scratch_shapes=[pltpu.VMEM((nc, q, SSD_WIDTH), F32), pltpu.VMEM((nc, bc, q), F32),
                        pltpu.VMEM((nc, q, bc), F32), pltpu.VMEM((nc, q, LANES), F32),
                        pltpu.VMEM((nc, q, LANES), F32), pltpu.VMEM((nc, LANES, q), F32),
                        pltpu.VMEM((nc, q, SSD_WIDTH), F32),
                        pltpu.VMEM((SSD_HEADS // 2, SSD_STATE, 2 * SSD_HEAD_DIM), F32)],
        compiler_params=_cparams(("parallel",)),
        name="ssd_mixer",
    )(*args)


def _seg_sum(x):
    lane = lax.broadcasted_iota(jnp.int32, x.shape, 1)
    first = lane < RWKV_HEAD_DIM
    s0 = jnp.sum(jnp.where(first, x, 0.0), axis=-1, keepdims=True)
    s1 = jnp.sum(jnp.where(first, 0.0, x), axis=-1, keepdims=True)
    return jnp.where(first, s0, s1)


def _shift_mix(u, mu):
    return u + mu * (0.5 * (_shifted(u, -1) + _shifted(u, 1)) - u)


def _rwkv_prep_kernel(ncb, rep, *refs):
    (r_ref, k_ref, v_ref, lo_ref, mur, muk, muv, mul, wupf, wupb, aup, gup,
     w0f, w0b, a0, kk_ref, ka_ref, ro, ko, vo, wo, alo, beo, go) = refs[:24]
    staged = dict(zip(("r", "k", "al", "be", "wf", "wb"), refs[24:]))
    q = pl.program_id(2)

    def put(name, cols, val):
        if staged:
            staged[name][:, cols] = val
        elif name in ("wf", "wb"):
            wo[0 if name == "wf" else 1, 0, 0, :, cols] = val
        else:
            {"r": ro, "k": ko, "al": alo, "be": beo}[name][0, 0, :, cols] = val

    def compute():
        lo = _shift_mix(lo_ref[0], mul[...])
        wdf = jnp.tanh(lo[:, 0:128]).astype(BF16)
        wdb = jnp.tanh(lo[:, 128:256]).astype(BF16)
        ad = lo[:, 256:384].astype(BF16)
        gd = jax.nn.sigmoid(lo[:, 384:640]).astype(BF16)
        for cb in range(ncb):
            cols = slice(cb * LANES, (cb + 1) * LANES)

            def mm(x, w_ref, cols=cols):
                return jnp.dot(x, w_ref[:, cols].astype(BF16), preferred_element_type=F32)

            r = _shift_mix(r_ref[0, :, cols], mur[:, cols])
            k = _shift_mix(k_ref[0, :, cols], muk[:, cols])
            v = _shift_mix(v_ref[0, :, cols], muv[:, cols])
            put("wf", cols, jnp.exp(-DECAY_SCALE * jax.nn.sigmoid(w0f[:, cols] + mm(wdf, wupf))))
            put("wb", cols, jnp.exp(-DECAY_SCALE * jax.nn.sigmoid(w0b[:, cols] + mm(wdb, wupb))))
            a = jax.nn.sigmoid(a0[:, cols] + mm(ad, aup))
            go[0, :, cols] = mm(gd, gup)
            kk = k * kk_ref[:, cols]
            kk = kk / jnp.maximum(jnp.sqrt(_seg_sum(kk * kk)), 1e-12)
            put("r", cols, r)
            put("k", cols, k * (1.0 + (a - 1.0) * ka_ref[:, cols]))
            vo[0, :, cols] = v
            put("al", cols, -kk)
            put("be", cols, kk * a)

    if staged:
        pl.when(q == 0)(compute)
        ro[0, 0] = staged["r"][...]
        ko[0, 0] = staged["k"][...]
        alo[0, 0] = staged["al"][...]
        beo[0, 0] = staged["be"][...]
        wo[0, 0, 0] = staged["wf"][...]
        wo[1, 0, 0] = staged["wb"][...]
    else:
        compute()


def rwkv_prep(proj, p, rep):
    bsz, seq, _ = proj.shape
    cw = min(RWKV_WIDTH, PREP_BLOCK_ELEMS // seq)
    nblk = RWKV_WIDTH // cw

    def col_spec(start):
        blk = start // cw
        return pl.BlockSpec((1, seq, cw), lambda b, j, q: (b, 0, blk + j))

    vec = pl.BlockSpec((1, cw), lambda b, j, q: (0, j))

    def mat(rows):
        return pl.BlockSpec((rows, cw), lambda b, j, q: (0, j))

    tok_spec = pl.BlockSpec((1, seq, cw), lambda b, j, q: (b, 0, j))
    rep_spec = pl.BlockSpec((1, 1, seq, cw), lambda b, j, q: (q, b, 0, j))
    w_spec = pl.BlockSpec((2, 1, 1, seq, cw), lambda b, j, q: (0, q, b, 0, j))
    tok_sh = jax.ShapeDtypeStruct((bsz, seq, RWKV_WIDTH), F32)
    rep_sh = jax.ShapeDtypeStruct((rep, bsz, seq, RWKV_WIDTH), F32)
    w_sh = jax.ShapeDtypeStruct((2, rep, bsz, seq, RWKV_WIDTH), F32)
    return pl.pallas_call(
        functools.partial(_rwkv_prep_kernel, cw // LANES, rep),
        grid=(bsz, nblk, rep),
        in_specs=[col_spec(COL_R), col_spec(COL_K), col_spec(COL_V),
                  pl.BlockSpec((1, seq, LORA_W), lambda b, j, q: (b, 0, COL_LORA // LORA_W)),
                  vec, vec, vec, pl.BlockSpec((1, LORA_W), lambda b, j, q: (0, 0)),
                  mat(128), mat(128), mat(128), mat(256),
                  vec, vec, vec, vec, vec],
        out_specs=[rep_spec, rep_spec, tok_spec, w_spec, rep_spec, rep_spec, tok_spec],
        out_shape=[rep_sh, rep_sh, tok_sh, w_sh, rep_sh, rep_sh, tok_sh],
        scratch_shapes=[pltpu.VMEM((seq, cw), F32)] * (6 if rep > 1 else 0),
        compiler_params=_cparams(("parallel", "arbitrary", "arbitrary")),
        name="rwkv_prep",
    )(proj, proj, proj, proj, p["mu_r"], p["mu_k"], p["mu_v"], p["mu_lora"],
      p["w_up_f"], p["w_up_b"], p["a_up"], p["g_up"],
      p["w0_f"], p["w0_b"], p["a0"], p["k_k"], p["k_a"])


def _rwkv_scan_kernel(tb, zero_init, *refs):
    if zero_init:
        r_ref, w_ref, k_ref, al_ref, be_ref, v_ref, y_ref, st_ref, s_s = refs
    else:
        r_ref, w_ref, k_ref, al_ref, be_ref, v_ref, s0_ref, y_ref, st_ref, s_s = refs
    kd = RWKV_HEAD_DIM
    d = pl.program_id(0)
    step_i = pl.program_id(2)

    def time_index(i):
        return i + d * (tb - 1 - 2 * i)

    @pl.when(step_i == 0)
    def _():
        if zero_init:
            s_s[...] = jnp.zeros_like(s_s)
        else:
            s_s[...] = s0_ref[...]

    t0 = time_index(0)
    sa0 = jnp.zeros(s_s.shape[1:], F32)
    sa1 = jnp.zeros(s_s.shape[1:], F32)
    for kq in range(0, kd, 2):
        sa0 = sa0 + s_s[kq] * al_ref[t0, kq:kq + 1, :]
        sa1 = sa1 + s_s[kq + 1] * al_ref[t0, kq + 1:kq + 2, :]

    def step(i, sa):
        t = time_index(i)
        tn = time_index(jnp.minimum(i + 1, tb - 1))
        vt = v_ref[t]
        y = jnp.zeros_like(sa)
        sa_next = jnp.zeros_like(sa)
        for kq in range(kd):
            row = slice(kq, kq + 1)
            sk = s_s[kq] * w_ref[t, row, :] + sa * be_ref[t, row, :] + vt * k_ref[t, row, :]
            s_s[kq] = sk
            y = y + sk * r_ref[t, row, :]
            sa_next = sa_next + sk * al_ref[tn, row, :]
        y_ref[t] = y
        return sa_next

    lax.fori_loop(0, tb, step, sa0 + sa1)

    @pl.when(step_i == pl.num_programs(2) - 1)
    def _():
        st_ref[...] = s_s[...]


def rwkv_scan(r, w, k, al, be, v, s0):
    seq, kd, nch = r.shape
    vv = v.shape[1]
    tb = 32
    nblk = seq // tb
    zero_init = s0 is None

    def tblk(d, i):
        return i + d * (nblk - 1 - 2 * i)

    kspec = pl.BlockSpec((tb, kd, LANES), lambda d, g, i: (tblk(d, i), 0, g))
    wspec = pl.BlockSpec((None, tb, kd, LANES), lambda d, g, i: (d, tblk(d, i), 0, g))
    vspec = pl.BlockSpec((tb, vv, LANES), lambda d, g, i: (tblk(d, i), 0, g))
    yspec = pl.BlockSpec((None, tb, vv, LANES), lambda d, g, i: (d, tblk(d, i), 0, g))
    sspec = pl.BlockSpec((None, kd, vv, LANES), lambda d, g, i: (d, 0, 0, g))
    in_specs = [kspec, wspec, kspec, kspec, kspec, vspec]
    args = [r, w, k, al, be, v]
    if not zero_init:
        in_specs.append(sspec)
        args.append(s0)
    return pl.pallas_call(
        functools.partial(_rwkv_scan_kernel, tb, zero_init),
        grid=(2, nch // LANES, nblk),
        in_specs=in_specs,
        out_specs=[yspec, sspec],
        out_shape=[jax.ShapeDtypeStruct((2, seq, vv, nch), F32),
                   jax.ShapeDtypeStruct((2, kd, vv, nch), F32)],
        scratch_shapes=[pltpu.VMEM((kd, vv, LANES), F32)],
        compiler_params=_cparams(("parallel", "parallel", "arbitrary")),
        name="rwkv_scan",
    )(*args)


def _rwkv_post_kernel(y_ref, r_ref, k_ref, v_ref, g_ref, lng, lnb, rk, o_ref):
    inv = 1.0 / RWKV_HEAD_DIM
    for cb in range(RWKV_WIDTH // LANES):
        cols = slice(cb * LANES, (cb + 1) * LANES)
        y = y_ref[0, 0, :, cols] + y_ref[1, 0, :, cols]
        mean = _seg_sum(y) * inv
        yc = y - mean
        var = _seg_sum(yc * yc) * inv
        yn = yc * lax.rsqrt(var + GN_EPS) * lng[:, cols] + lnb[:, cols]
        bonus = _seg_sum(r_ref[0, :, cols] * k_ref[0, :, cols] * rk[:, cols]) * v_ref[0, :, cols]
        o_ref[0, :, cols] = ((yn + bonus) * g_ref[0, :, cols]).astype(BF16)


def rwkv_post(y2, r, k, v, g, p):
    _, bsz, seq, wd = y2.shape
    tl = 256
    spec = pl.BlockSpec((1, tl, wd), lambda b, i: (b, i, 0))
    vec = pl.BlockSpec((1, wd), lambda b, i: (0, 0))
    return pl.pallas_call(
        _rwkv_post_kernel,
        grid=(bsz, seq // tl),
        in_specs=[pl.BlockSpec((2, 1, tl, wd), lambda b, i: (0, b, i, 0))] + [spec] * 4 + [vec] * 3,
        out_specs=spec,
        out_shape=jax.ShapeDtypeStruct((bsz, seq, wd), BF16),
        compiler_params=_cparams(("parallel", "parallel")),
        name="rwkv_post",
    )(y2, r, k, v, g, p["ln_g"], p["ln_b"], p["r_k"])


def _split_bf16(x):
    hi = x.astype(BF16)
    return hi, (x - hi.astype(F32)).astype(BF16)


def _out_proj_kernel(ys_ref, yr_ref, wo_ref, x_ref, gm_ref, ng_ref, sc_ref, sh_ref, rwh_ref, rwl_ref,
                     x1_ref, h2_ref, lg_ref):
    acc = jnp.dot(ys_ref[0], wo_ref[0:SSD_WIDTH, :], preferred_element_type=F32)
    acc = acc + jnp.dot(yr_ref[0], wo_ref[SSD_WIDTH:, :], preferred_element_type=F32)
    x1 = x_ref[0] + gm_ref[0] * acc
    x1_ref[0] = x1
    hn = x1 * lax.rsqrt(jnp.mean(x1 * x1, axis=-1, keepdims=True) + NORM_EPS) * ng_ref[...]
    h2 = hn * (1.0 + sc_ref[0]) + sh_ref[0]
    hi, lo = _split_bf16(h2)
    h2_ref[0] = hi
    lg_ref[0] = (jnp.dot(hi, rwh_ref[...], preferred_element_type=F32)
                 + jnp.dot(lo, rwh_ref[...], preferred_element_type=F32)
                 + jnp.dot(hi, rwl_ref[...], preferred_element_type=F32))


def out_proj(y_ssd, y_rwkv, w_out, x, gate_m, norm_g, scale_f, shift_f, router_w):
    bsz, seq, d = x.shape
    tm = min(512, seq)
    per_b = gate_m.shape[0] > 1
    midx = (lambda b, i: (b, 0, 0)) if per_b else (lambda b, i: (0, 0, 0))
    half = pl.BlockSpec((1, tm, d // 2), lambda b, i: (b, i, 0))
    full = pl.BlockSpec((1, tm, d), lambda b, i: (b, i, 0))
    mspec = pl.BlockSpec((1, 1, d), midx)
    return pl.pallas_call(
        _out_proj_kernel,
        grid=(bsz, seq // tm),
        in_specs=[half, half, pl.BlockSpec((d, d), lambda b, i: (0, 0)), full, mspec,
                  pl.BlockSpec((1, d), lambda b, i: (0, 0)), mspec, mspec,
                  pl.BlockSpec((d, LANES), lambda b, i: (0, 0)),
                  pl.BlockSpec((d, LANES), lambda b, i: (0, 0))],
        out_specs=[full, full, pl.BlockSpec((1, tm, LANES), lambda b, i: (b, i, 0))],
        out_shape=[jax.ShapeDtypeStruct((bsz, seq, d), F32), jax.ShapeDtypeStruct((bsz, seq, d), BF16),
                   jax.ShapeDtypeStruct((bsz, seq, LANES), F32)],
        compiler_params=_cparams(("parallel", "parallel")),
        name="out_proj",
    )(y_ssd, y_rwkv, w_out, x, gate_m, norm_g.reshape(1, d), scale_f, shift_f, *router_w)


def _route_kernel(nt, cap, lg_ref, selr_ref, posr_ref, selt_ref, post_ref, afft_ref, aff_s, pre_s):
    ne = N_EXPERTS
    lane = lax.broadcasted_iota(jnp.int32, (LANES, LANES), 1)
    sub = lax.broadcasted_iota(jnp.int32, (LANES, LANES), 0)
    upper_incl = (sub <= lane).astype(BF16)
    for i in range(nt):
        lg = jnp.where(lane < ne, lg_ref[i * LANES:(i + 1) * LANES, :], -jnp.inf)
        e = jnp.exp(lg - jnp.max(lg, axis=-1, keepdims=True))
        aff = e / jnp.sum(e, axis=-1, keepdims=True)
        afft_ref[i * LANES:(i + 1) * LANES, :] = aff
        aff_s[i] = aff.T[0:ne, :]
    aff3 = aff_s[...]

    def count(mask):
        s = jnp.sum(mask.astype(F32), axis=0, keepdims=True)
        return jnp.sum(s, axis=2, keepdims=True)

    def search(_, carry):
        lo, hi = carry
        mid = 0.5 * (lo + hi)
        ok = count(aff3 >= mid) >= cap
        return jnp.where(ok, mid, lo), jnp.where(ok, hi, mid)

    lo0 = jnp.zeros((1, ne, 1), F32)
    hi0 = jnp.full((1, ne, 1), 2.0, F32)
    lo, _ = lax.fori_loop(0, SEARCH_ITERS, search, (lo0, hi0))
    cand = jnp.where(aff3 >= lo, aff3, 4.0)
    thr = jnp.min(jnp.min(cand, axis=0, keepdims=True), axis=2, keepdims=True)
    gt = (aff3 > thr).astype(F32)
    eq = (aff3 == thr).astype(F32)
    need = cap - count(aff3 > thr)[0]

    def prefix_excl(m3):
        off = jnp.zeros((ne, 1), F32)
        for i in range(nt):
            inc = jnp.dot(m3[i].astype(BF16), upper_incl, preferred_element_type=F32)
            pre_s[i] = inc - m3[i] + off
            off = off + inc[:, LANES - 1:LANES]
        return pre_s[...]

    sel = jnp.maximum(gt, eq * (prefix_excl(eq) < need[None]).astype(F32))
    pos = prefix_excl(sel)
    selr_ref[...] = sel
    posr_ref[...] = pos
    zpad = jnp.zeros((LANES - ne, LANES), F32)
    for i in range(nt):
        rows = slice(i * LANES, (i + 1) * LANES)
        selt_ref[rows, :] = jnp.concatenate([sel[i], zpad], axis=0).T
        post_ref[rows, :] = jnp.concatenate([pos[i], zpad], axis=0).T


def route(logits, cap):
    n_tok = logits.shape[0]
    nt = n_tok // LANES
    row_sh = jax.ShapeDtypeStruct((nt, N_EXPERTS, LANES), F32)
    tm_sh = jax.ShapeDtypeStruct((n_tok, LANES), F32)
    return pl.pallas_call(
        functools.partial(_route_kernel, nt, cap),
        out_shape=[row_sh, row_sh, tm_sh, tm_sh, tm_sh],
        scratch_shapes=[pltpu.VMEM((nt, N_EXPERTS, LANES), F32), pltpu.VMEM((nt, N_EXPERTS, LANES), F32)],
        compiler_params=pltpu.CompilerParams(vmem_limit_bytes=VMEM_LIMIT),
        name="route",
    )(logits)


def _window_start(starts_ref, tile, e, cap, win, align=SUBLANES):
    ps = starts_ref[tile * N_EXPERTS + e]
    return pl.multiple_of(jnp.minimum((ps // align) * align, cap - win), align)


def _gather_kernel(cap, win, nsub, starts_ref, sel_ref, pos_ref, h_ref, o_ref):
    eg = pl.program_id(0)
    j = pl.program_id(1)

    @pl.when(j == 0)
    def _():
        o_ref[...] = jnp.zeros_like(o_ref)

    for el in range(GATHER_EXPERTS):
        e = eg * GATHER_EXPERTS + el
        for s in range(0, nsub, 2):
            tile = j * nsub + s
            s0 = _window_start(starts_ref, tile, e, cap, win, align=2 * SUBLANES)
            lo = starts_ref[tile * N_EXPERTS + e]
            hi = starts_ref[(tile + 2) * N_EXPERTS + e]
            prows = [pos_ref[s + u, pl.ds(e, 1), :] for u in range(2)]
            srows = [sel_ref[s + u, pl.ds(e, 1), :] for u in range(2)]
            for piece in range(win // GATHER_ROWS):
                p0 = pl.multiple_of(s0 + piece * GATHER_ROWS, 2 * SUBLANES)

                @pl.when(jnp.logical_and(p0 < hi, p0 + GATHER_ROWS > lo))
                def _(p0=p0, s=s, el=el, prows=prows, srows=srows):
                    slot = (lax.broadcasted_iota(jnp.int32, (GATHER_ROWS, LANES), 0) + p0).astype(F32)
                    onehot = jnp.concatenate(
                        [jnp.where(jnp.logical_and(slot == prows[u], srows[u] > 0.0), 1.0, 0.0).astype(BF16)
                         for u in range(2)], axis=1)
                    rows = pl.ds(p0, GATHER_ROWS)
                    got = jnp.dot(onehot, h_ref[s * LANES:(s + 2) * LANES, :], preferred_element_type=F32)
                    o_ref[el, rows, :] = (o_ref[el, rows, :].astype(F32) + got).astype(BF16)


def gather_tokens(starts, sel_r, pos_r, h, cap):
    n_tok, d = h.shape
    tt = min(1024, n_tok)
    nsub = tt // LANES
    win = min(-(-(2 * LANES + 2 * SUBLANES) // GATHER_ROWS) * GATHER_ROWS, cap)
    assert win % GATHER_ROWS == 0 and (cap - win) % (2 * SUBLANES) == 0
    rspec = pl.BlockSpec((nsub, N_EXPERTS, LANES), lambda e, j, st: (j, 0, 0))
    return pl.pallas_call(
        functools.partial(_gather_kernel, cap, win, nsub),
        grid_spec=pltpu.PrefetchScalarGridSpec(
            num_scalar_prefetch=1,
            grid=(N_EXPERTS // GATHER_EXPERTS, n_tok // tt),
            in_specs=[rspec, rspec, pl.BlockSpec((tt, d), lambda e, j, st: (j, 0))],
            out_specs=pl.BlockSpec((GATHER_EXPERTS, cap, d), lambda e, j, st: (e, 0, 0))),
        out_shape=jax.ShapeDtypeStruct((N_EXPERTS, cap, d), BF16),
        compiler_params=_cparams(("parallel", "arbitrary")),
        name="gather_tokens",
    )(starts, sel_r, pos_r, h)


def _ffn_kernel(n_sets, *refs):
    x_refs = refs[:n_sets]
    wg_ref, wu_ref, wd_ref = refs[n_sets:n_sets + 3]
    o_refs = refs[n_sets + 3:2 * n_sets + 3]
    accs = refs[2 * n_sets + 3:]
    f = pl.program_id(1)
    wg = wg_ref[0].astype(BF16)
    wu = wu_ref[0].astype(BF16)
    wd = wd_ref[0].astype(BF16)
    for x_ref, o_ref, acc in zip(x_refs, o_refs, accs):
        @pl.when(f == 0)
        def _():
            acc[...] = jnp.zeros_like(acc)

        x = x_ref[0]
        hg = jnp.dot(x, wg, preferred_element_type=F32)
        hu = jnp.dot(x, wu, preferred_element_type=F32)
        hid = (_silu(hg) * hu).astype(BF16)
        acc[...] += jnp.dot(hid, wd, preferred_element_type=F32)

        @pl.when(f == pl.num_programs(1) - 1)
        def _():
            o_ref[0] = acc[...].astype(BF16)


def expert_ffn(xs_sets, wg, wu, wd):
    ne, _, d = xs_sets[0].shape
    ff = wg.shape[2]
    tf = 256
    xspecs = [pl.BlockSpec((1, xs.shape[1], d), lambda e, f: (e, 0, 0)) for xs in xs_sets]
    return pl.pallas_call(
        functools.partial(_ffn_kernel, len(xs_sets)),
        grid=(ne, ff // tf),
        in_specs=xspecs + [pl.BlockSpec((1, d, tf), lambda e, f: (e, 0, f)),
                           pl.BlockSpec((1, d, tf), lambda e, f: (e, 0, f)),
                           pl.BlockSpec((1, tf, d), lambda e, f: (e, f, 0))],
        out_specs=xspecs,
        out_shape=[jax.ShapeDtypeStruct(xs.shape, BF16) for xs in xs_sets],
        scratch_shapes=[pltpu.VMEM(xs.shape[1:], F32) for xs in xs_sets],
        compiler_params=_cparams(("parallel", "arbitrary")),
        name="expert_ffn",
    )(*xs_sets, wg, wu, wd)


def _scatter_kernel(cap, win, nsub, starts_ref, sel_ref, pos_ref, aff_ref, ob_ref, y_ref):
    c = pl.program_id(0)
    e = pl.program_id(1)

    @pl.when(e == 0)
    def _():
        y_ref[...] = jnp.zeros_like(y_ref)

    mine = lax.broadcasted_iota(jnp.int32, (LANES, LANES), 1) == e
    for s in range(nsub):
        rows = slice(s * LANES, (s + 1) * LANES)

        def col(ref):
            return jnp.sum(jnp.where(mine, ref[rows, :], 0.0), axis=-1, keepdims=True)

        pcol, scol, gcol = col(pos_ref), col(sel_ref), col(aff_ref)
        s0 = _window_start(starts_ref, c * nsub + s, e, cap, win)
        slot = (lax.broadcasted_iota(jnp.int32, (LANES, win), 1) + s0).astype(F32)
        onehot = jnp.where(jnp.logical_and(slot == pcol, scol > 0.0), 1.0, 0.0).astype(BF16)
        y_ref[rows, :] += gcol * jnp.dot(onehot, ob_ref[0, pl.ds(s0, win), :],
                                         preferred_element_type=F32)


def scatter_combine(starts, sel_t, pos_t, aff_t, outbuf):
    n_tok = sel_t.shape[0]
    ne, cap, d = outbuf.shape
    chunk = min(2048, n_tok)
    nsub = chunk // LANES
    win = min(2 * LANES, cap)
    tspec = pl.BlockSpec((chunk, LANES), lambda c, e, st: (c, 0))
    return pl.pallas_call(
        functools.partial(_scatter_kernel, cap, win, nsub),
        grid_spec=pltpu.PrefetchScalarGridSpec(
            num_scalar_prefetch=1,
            grid=(n_tok // chunk, ne),
            in_specs=[tspec, tspec, tspec, pl.BlockSpec((1, cap, d), lambda c, e, st: (e, 0, 0))],
            out_specs=pl.BlockSpec((chunk, d), lambda c, e, st: (c, 0))),
        out_shape=jax.ShapeDtypeStruct((n_tok, d), F32),
        compiler_params=_cparams(("parallel", "arbitrary")),
        name="scatter_combine",
    )(starts, sel_t, pos_t, aff_t, outbuf)


def _final_kernel(x1_ref, y_ref, gf_ref, fg_ref, o_ref):
    x2 = x1_ref[0] + gf_ref[0] * y_ref[0]
    o_ref[0] = x2 * lax.rsqrt(jnp.mean(x2 * x2, axis=-1, keepdims=True) + NORM_EPS) * fg_ref[...]


def final_norm(x1, y, gate_f, final_g):
    bsz, seq, d = x1.shape
    tl = 256
    per_b = gate_f.shape[0] > 1
    midx = (lambda b, i: (b, 0, 0)) if per_b else (lambda b, i: (0, 0, 0))
    xspec = pl.BlockSpec((1, tl, d), lambda b, i: (b, i, 0))
    return pl.pallas_call(
        _final_kernel,
        grid=(bsz, seq // tl),
        in_specs=[xspec, xspec, pl.BlockSpec((1, 1, d), midx), pl.BlockSpec((1, d), lambda b, i: (0, 0))],
        out_specs=xspec,
        out_shape=jax.ShapeDtypeStruct((bsz, seq, d), F32),
        compiler_params=_cparams(("parallel", "parallel")),
        name="final_norm",
    )(x1, y, gate_f, final_g.reshape(1, d))


def _pad_cols(w, width):
    return jnp.pad(w, ((0, 0), (0, width - w.shape[1])))


def _pad_rows(w, rows):
    return jnp.pad(w, ((0, rows - w.shape[0]), (0, 0)))


def _relayout_columns(w):
    n_ssd = 2 * SSD_WIDTH + 2 * SSD_GROUPS * SSD_STATE + 2 * SSD_HEADS
    ssd, rw = w[:, :n_ssd], w[:, n_ssd:]
    o = 3 * RWKV_WIDTH
    zx = 2 * SSD_WIDTH
    bc = zx + 2 * SSD_GROUPS * SSD_STATE
    parts = [ssd[:, :zx], rw[:, :o], ssd[:, zx:bc], _pad_cols(ssd[:, bc:], LANES),
             _pad_cols(rw[:, o:o + DECAY_LORA], LANES),
             _pad_cols(rw[:, o + DECAY_LORA:o + 2 * DECAY_LORA], LANES),
             _pad_cols(rw[:, o + 2 * DECAY_LORA:o + 2 * DECAY_LORA + ICLR_LORA], LANES),
             _pad_cols(rw[:, o + 2 * DECAY_LORA + ICLR_LORA:], 2 * LANES)]
    return jnp.concatenate(parts, axis=1)


def _layer_params(l, w_in, w_out, ssd_conv_w, ssd_conv_b, ssd_A_log, ssd_dt_bias, ssd_D, ssd_norm_g,
                  rwkv_mu, rwkv_w0, rwkv_w_up, rwkv_a0, rwkv_a_up, rwkv_g_up, rwkv_k_k, rwkv_k_a,
                  rwkv_r_k, rwkv_ln_g, rwkv_ln_b, router_w):
    row = lambda a: a.reshape(1, -1)
    n_ssd = 2 * SSD_WIDTH + 2 * SSD_GROUPS * SSD_STATE + 2 * SSD_HEADS
    mu_full = _relayout_columns(jnp.concatenate([jnp.zeros((1, n_ssd), F32), row(rwkv_mu[l])], axis=1))
    cw, cb = ssd_conv_w[l], row(ssd_conv_b[l])
    xe, be = SSD_WIDTH, SSD_WIDTH + SSD_GROUPS * SSD_STATE
    return dict(
        w_in=_relayout_columns(w_in[l]).astype(BF16),
        w_out=w_out[l].astype(BF16),
        cw_x=cw[:, :xe], cb_x=cb[:, :xe], cw_b=cw[:, xe:be], cb_b=cb[:, xe:be],
        cw_c=cw[:, be:], cb_c=cb[:, be:],
        dt_bias=_pad_cols(row(ssd_dt_bias[l]), LANES), a_log=_pad_cols(row(ssd_A_log[l]), LANES),
        d_exp=row(jnp.repeat(ssd_D[l], SSD_HEAD_DIM)), ssd_ng=row(ssd_norm_g[l]),
        mu_r=mu_full[:, COL_R:COL_K], mu_k=mu_full[:, COL_K:COL_V], mu_v=mu_full[:, COL_V:COL_B],
        mu_lora=mu_full[:, COL_LORA:],
        w_up_f=_pad_rows(rwkv_w_up[l, 0], LANES), w_up_b=_pad_rows(rwkv_w_up[l, 1], LANES),
        a_up=_pad_rows(rwkv_a_up[l], LANES), g_up=_pad_rows(rwkv_g_up[l], 2 * LANES),
        w0_f=row(rwkv_w0[l, 0]), w0_b=row(rwkv_w0[l, 1]), a0=row(rwkv_a0[l]),
        k_k=row(rwkv_k_k[l]), k_a=row(rwkv_k_a[l]), r_k=row(rwkv_r_k[l]),
        ln_g=row(rwkv_ln_g[l]), ln_b=row(rwkv_ln_b[l]),
        router_w=tuple(_pad_cols(t, LANES) for t in _split_bf16(router_w[l])),
    )


def _to_chain(x):
    lead = x.shape[:-3]
    bsz, seq, _ = x.shape[-3:]
    n = len(lead)
    x = x.reshape(lead + (bsz, seq, RWKV_HEADS, RWKV_HEAD_DIM))
    perm = tuple(range(n)) + (n + 1, n + 3, n, n + 2)
    return jnp.transpose(x, perm).reshape(lead + (seq, RWKV_HEAD_DIM, bsz * RWKV_HEADS))


def _from_chain(y, bsz):
    lead = y.shape[:-3]
    seq = y.shape[-3]
    n = len(lead)
    y = y.reshape(lead + (seq, RWKV_HEAD_DIM, bsz, RWKV_HEADS))
    perm = tuple(range(n)) + (n + 2, n, n + 3, n + 1)
    return jnp.transpose(y, perm).reshape(lead + (bsz, seq, RWKV_WIDTH))


def _state_to_chain(s):
    return jnp.transpose(s, (0, 4, 3, 1, 2)).reshape(2, RWKV_HEAD_DIM, RWKV_HEAD_DIM, -1)


def _state_from_chain(s, bsz):
    return jnp.transpose(s.reshape(2, RWKV_HEAD_DIM, RWKV_HEAD_DIM, bsz, RWKV_HEADS), (0, 3, 4, 2, 1))


def _scan_replicas(bsz):
    nch = bsz * RWKV_HEADS
    rep = max(1, LANES // nch)
    assert (nch * rep) % LANES == 0
    return rep


def rwkv_mixer_scan(r, k, v, w2, al, be, s0):
    rep, bsz, seq, _ = r.shape
    nh, hd = RWKV_HEADS, RWKV_HEAD_DIM
    fold = lambda t: t.reshape(t.shape[:-4] + (rep * bsz, seq, RWKV_WIDTH))
    rc, kc, alc, bec, wc = (_to_chain(fold(t)) for t in (r, k, al, be, w2))
    if rep == 1:
        s0c = None if s0 is None else _state_to_chain(s0)
        y, st = rwkv_scan(rc, wc, kc, alc, bec, _to_chain(v), s0c)
        return _from_chain(y, bsz), _state_from_chain(st, bsz)

    vq = hd // rep
    vc = jnp.transpose(v.reshape(bsz, seq, nh, rep, vq), (1, 4, 3, 0, 2)).reshape(seq, vq, LANES)
    s0c = None
    if s0 is not None:
        s0c = jnp.transpose(s0.reshape(2, bsz, nh, rep, vq, hd), (0, 5, 4, 3, 1, 2)).reshape(
            2, hd, vq, LANES)
    y, st = rwkv_scan(rc, wc, kc, alc, bec, vc, s0c)
    y = jnp.transpose(y.reshape(2, seq, vq, rep, bsz, nh), (0, 4, 1, 5, 3, 2)).reshape(
        2, bsz, seq, RWKV_WIDTH)
    st = jnp.transpose(st.reshape(2, hd, vq, rep, bsz, nh), (0, 4, 5, 3, 2, 1)).reshape(
        2, bsz, nh, hd, hd)
    return y, st


def _grid_pos_embed(n_tokens, dim):
    rows = n_tokens // GRID_W
    row = jnp.repeat(jnp.arange(rows, dtype=F32), GRID_W)
    col = jnp.tile(jnp.arange(GRID_W, dtype=F32), rows)
    quarter = dim // 4
    freqs = jnp.exp(jnp.arange(quarter, dtype=F32) * (-math.log(POS_BASE) / quarter))

    def axis_embed(pos):
        ang = pos[:, None] * freqs[None, :]
        return jnp.concatenate([jnp.sin(ang), jnp.cos(ang)], axis=-1)

    return jnp.concatenate([axis_embed(row), axis_embed(col)], axis=-1)


def layer_to_dispatch(x, pos, mod, p, norm_mix_g, norm_ffn_g, states):
    bsz, seq, d = x.shape
    shift_m, scale_m, gate_m, shift_f, scale_f, gate_f = mod
    h, x0 = norm_modulate(x, pos, norm_mix_g, scale_m, shift_m)
    n_tok = bsz * seq
    proj = matmul_bf16(h.reshape(n_tok, d), p["w_in"], min(2048, n_tok), 640).reshape(bsz, seq, N_PROJ)
    h0f, h0b, s0f, s0b = states
    y_ssd, hf, hb = ssd_mixer(proj, p, h0f, h0b)
    r, k, v, w2, al, be, g = rwkv_prep(proj, p, _scan_replicas(bsz))
    s0 = None if s0f is None else jnp.stack([s0f, s0b], axis=0)
    y2, st = rwkv_mixer_scan(r, k, v, w2, al, be, s0)
    y_rwkv = rwkv_post(y2, r[0], k[0], v, g, p)
    x1, h2, logits = out_proj(y_ssd, y_rwkv, p["w_out"], x0, gate_m, norm_ffn_g, scale_f, shift_f,
                              p["router_w"])
    cap = EC_CAPACITY * n_tok // N_EXPERTS
    sel_r, pos_r, sel_t, pos_t, aff_t = route(logits.reshape(n_tok, LANES), cap)
    starts = jnp.concatenate([pos_r[:, :, 0], jnp.full((1, N_EXPERTS), cap, F32)], axis=0)
    starts = starts.astype(jnp.int32).reshape(-1)
    xs = gather_tokens(starts, sel_r, pos_r, h2.reshape(n_tok, d), cap)
    routed = dict(xs=xs, starts=starts, sel_t=sel_t, pos_t=pos_t, aff_t=aff_t, x1=x1, gate_f=gate_f)
    return routed, (hf, hb, st[0], st[1])


def combine_and_finish(routed, outbuf, final_g):
    x1 = routed["x1"]
    y_ffn = scatter_combine(routed["starts"], routed["sel_t"], routed["pos_t"], routed["aff_t"], outbuf)
    return final_norm(x1, y_ffn.reshape(x1.shape), routed["gate_f"], final_g)


def kernel(x_prompt, x_sample, state_ssd_fwd, state_ssd_bwd, state_rwkv_fwd, state_rwkv_bwd, c, c_ctx,
           w_ada, b_ada, norm_mix_g, norm_ffn_g, w_in, w_out, ssd_conv_w, ssd_conv_b, ssd_A_log,
           ssd_dt_bias, ssd_D, ssd_norm_g, rwkv_mu, rwkv_w0, rwkv_w_up, rwkv_a0, rwkv_a_up, rwkv_g_up,
           rwkv_k_k, rwkv_k_a, rwkv_r_k, rwkv_ln_g, rwkv_ln_b, router_w, exp_w_gate, exp_w_up,
           exp_w_down, final_norm_g):
    depth = w_in.shape[0]
    assert depth == 1, "the final norm runs right after the single layer's FFN residual"
    l = 0
    dec_b = x_sample.shape[0]
    p = _layer_params(l, w_in, w_out, ssd_conv_w, ssd_conv_b, ssd_A_log, ssd_dt_bias, ssd_D, ssd_norm_g,
                      rwkv_mu, rwkv_w0, rwkv_w_up, rwkv_a0, rwkv_a_up, rwkv_g_up, rwkv_k_k, rwkv_k_a,
                      rwkv_r_k, rwkv_ln_g, rwkv_ln_b, router_w)
    cond = jnp.concatenate([c_ctx[None, :], c, jnp.zeros((16 - 1 - dec_b, D_MODEL), F32)], axis=0)
    mod = ada_modulation(cond, w_ada[l], b_ada[l]).reshape(16, N_MOD, 1, D_MODEL)
    mod_ctx = [mod[0:1, i] for i in range(N_MOD)]
    mod_lat = [mod[1:1 + dec_b, i] for i in range(N_MOD)]

    routed_ctx, st = layer_to_dispatch(x_prompt, None, mod_ctx, p, norm_mix_g[l], norm_ffn_g[l],
                                       (None, None, None, None))
    pos = _grid_pos_embed(x_sample.shape[1], D_MODEL)
    cached = (state_ssd_fwd[:, l], state_ssd_bwd[:, l], state_rwkv_fwd[:, l], state_rwkv_bwd[:, l])
    routed_lat, _ = layer_to_dispatch(x_sample, pos, mod_lat, p, norm_mix_g[l], norm_ffn_g[l], cached)
    out_ctx, out_lat = expert_ffn([routed_ctx["xs"], routed_lat["xs"]],
                                  exp_w_gate[l], exp_w_up[l], exp_w_down[l])
    y_prompt = combine_and_finish(routed_ctx, out_ctx, final_norm_g)
    y_sample = combine_and_finish(routed_lat, out_lat, final_norm_g)
    return (y_prompt, y_sample, st[0][:, None], st[1][:, None], st[2][:, None], st[3][:, None])
```

```python
import functools
import math

import jax
import jax.numpy as jnp
from jax import lax
from jax.experimental import pallas as pl
from jax.experimental.pallas import tpu as pltpu

F32 = jnp.float32
BF16 = jnp.bfloat16
HIGHEST = lax.Precision.HIGHEST

D_MODEL = 2048
GRID_W = 64
SSD_WIDTH = 1024
SSD_HEAD_DIM = 64
SSD_HEADS = 16
SSD_GROUPS = 2
SSD_STATE = 128
SSD_CONV_W = 5
SSD_CHUNK = 128
RWKV_WIDTH = 1024
RWKV_HEAD_DIM = 64
RWKV_HEADS = 16
DECAY_LORA = 64
ICLR_LORA = 64
GATE_LORA = 160
N_EXPERTS = 16
EC_CAPACITY = 2
EXPERT_FF = 1024
N_MOD = 6
NORM_EPS = 1e-6
GN_EPS = 64e-5
DECAY_SCALE = 0.606531
POS_BASE = 10000.0
SEARCH_ITERS = 48

LANES = 128
SUBLANES = 8
GATHER_ROWS = 64
GATHER_EXPERTS = 4

COL_Z = 0
COL_X = 1024
COL_R = 2048
COL_K = 3072
COL_V = 4096
COL_B = 5120
COL_C = 5376
COL_DT = 5632
COL_LORA = 5760
PREP_BLOCK_ELEMS = 256 * 1024
LORA_W = 640
N_PROJ = COL_LORA + LORA_W

VMEM_LIMIT = 56 * 1024 * 1024


def _cparams(sem):
    return pltpu.CompilerParams(dimension_semantics=sem, vmem_limit_bytes=VMEM_LIMIT)


def _silu(x):
    return x * jax.nn.sigmoid(x)


def _ada_kernel(c_ref, w_ref, b_ref, o_ref):
    s = _silu(c_ref[...]).astype(BF16)
    o_ref[...] = jnp.dot(s, w_ref[...].astype(BF16), preferred_element_type=F32) + b_ref[...]


def ada_modulation(cond, w, b):
    m, d = cond.shape
    n = w.shape[1]
    tn = 1024
    return pl.pallas_call(
        _ada_kernel,
        grid=(n // tn,),
        in_specs=[pl.BlockSpec((m, d), lambda j: (0, 0)),
                  pl.BlockSpec((d, tn), lambda j: (0, j)),
                  pl.BlockSpec((1, tn), lambda j: (0, j))],
        out_specs=pl.BlockSpec((m, tn), lambda j: (0, j)),
        out_shape=jax.ShapeDtypeStruct((m, n), F32),
        compiler_params=_cparams(("parallel",)),
        name="ada_modulation",
    )(cond, w, b.reshape(1, n))


def _norm_mod_kernel(has_pos, *refs):
    if has_pos:
        x_ref, pos_ref, g_ref, sc_ref, sh_ref, h_ref, x0_ref = refs
    else:
        x_ref, g_ref, sc_ref, sh_ref, h_ref = refs
    x = x_ref[0]
    if has_pos:
        x = x + pos_ref[...]
        x0_ref[0] = x
    y = x * lax.rsqrt(jnp.mean(x * x, axis=-1, keepdims=True) + NORM_EPS) * g_ref[...]
    h_ref[0] = (y * (1.0 + sc_ref[0]) + sh_ref[0]).astype(BF16)


def norm_modulate(x, pos, g, scale, shift):
    bsz, seq, d = x.shape
    tl = 256
    per_b = scale.shape[0] > 1
    midx = (lambda b, i: (b, 0, 0)) if per_b else (lambda b, i: (0, 0, 0))
    xspec = pl.BlockSpec((1, tl, d), lambda b, i: (b, i, 0))
    in_specs = [xspec]
    args = [x]
    if pos is not None:
        in_specs.append(pl.BlockSpec((tl, d), lambda b, i: (i, 0)))
        args.append(pos)
    in_specs += [pl.BlockSpec((1, d), lambda b, i: (0, 0)),
                 pl.BlockSpec((1, 1, d), midx), pl.BlockSpec((1, 1, d), midx)]
    args += [g.reshape(1, d), scale, shift]
    out_shape = [jax.ShapeDtypeStruct((bsz, seq, d), BF16)]
    out_specs = [xspec]
    if pos is not None:
        out_shape.append(jax.ShapeDtypeStruct((bsz, seq, d), F32))
        out_specs.append(xspec)
    res = pl.pallas_call(
        functools.partial(_norm_mod_kernel, pos is not None),
        grid=(bsz, seq // tl),
        in_specs=in_specs, out_specs=out_specs, out_shape=out_shape,
        compiler_params=_cparams(("parallel", "parallel")),
        name="norm_modulate",
    )(*args)
    return (res[0], res[1]) if pos is not None else (res[0], x)


def _mm_kernel(a_ref, b_ref, o_ref):
    o_ref[...] = jnp.dot(a_ref[...], b_ref[...], preferred_element_type=F32)


def matmul_bf16(a, b, tm, tn):
    m, k = a.shape
    n = b.shape[1]
    return pl.pallas_call(
        _mm_kernel,
        grid=(m // tm, n // tn),
        in_specs=[pl.BlockSpec((tm, k), lambda i, j: (i, 0)),
                  pl.BlockSpec((k, tn), lambda i, j: (0, j))],
        out_specs=pl.BlockSpec((tm, tn), lambda i, j: (i, j)),
        out_shape=jax.ShapeDtypeStruct((m, n), F32),
        compiler_params=_cparams(("parallel", "arbitrary")),
        name="in_proj",
    )(a, b)


def _shifted(u, off):
    n = u.shape[0]
    row = lax.broadcasted_iota(jnp.int32, (n, 1), 0)
    rolled = pltpu.roll(u, (-off) % n, axis=0)
    valid = jnp.logical_and(row + off >= 0, row + off < n)
    return jnp.where(valid, rolled, 0.0)


def _conv_silu(u, w_ref, b_ref, cols):
    pad = SSD_CONV_W // 2
    acc = u * w_ref[pad:pad + 1, cols] + b_ref[:, cols]
    for j in range(SSD_CONV_W):
        if j != pad:
            acc = acc + _shifted(u, j - pad) * w_ref[j:j + 1, cols]
    return _silu(acc)


def _softplus(x):
    return jnp.maximum(x, 0.0) + jnp.log1p(jnp.exp(-jnp.abs(x)))


def _ssd_kernel(nc, zero_init, *refs):
    q = SSD_CHUNK
    (z_ref, x_ref, b_ref, c_ref, dt_ref, cwx, cbx, cwb, cbb, cwc, cbc,
     dtb_ref, alog_ref, d_ref, ng_ref) = refs[:15]
    refs = refs[15:]
    if not zero_init:
        h0f_ref, h0b_ref = refs[:2]
        refs = refs[2:]
    y_ref, hf_ref, hb_ref, xh_s, bm_s, cm_s, dt_s, a_s, at_s, y_s, h_s = refs

    slab = 256
    for cb in range(SSD_WIDTH // slab):
        cols = slice(cb * slab, (cb + 1) * slab)
        xh = _conv_silu(x_ref[0, :, cols], cwx, cbx, cols)
        for c in range(nc):
            rows = slice(c * q, (c + 1) * q)
            xh_s[c, :, cols] = xh[rows]
            y_s[c, :, cols] = xh[rows] * d_ref[:, cols]
    allc = slice(0, SSD_GROUPS * SSD_STATE)
    bm = _conv_silu(b_ref[0], cwb, cbb, allc)
    cm = _conv_silu(c_ref[0], cwc, cbc, allc)
    dt = _softplus(dt_ref[0] + dtb_ref[...])
    a = dt * (-jnp.exp(alog_ref[...]))
    ii = lax.broadcasted_iota(jnp.int32, (q, q), 0)
    jj = lax.broadcasted_iota(jnp.int32, (q, q), 1)
    eye = (ii == jj).astype(F32)
    lower = ii >= jj
    upper = ii <= jj
    first_half = jj < SSD_HEAD_DIM
    for c in range(nc):
        rows = slice(c * q, (c + 1) * q)
        for g in range(SSD_GROUPS):
            gcols = slice(g * SSD_STATE, (g + 1) * SSD_STATE)
            bm_s[c, gcols, :] = bm[rows, gcols].T
        cm_s[c] = cm[rows]
        dt_s[c] = dt[rows]
        a_s[c] = a[rows]
        at_s[c] = lax.dot_general(eye, a[rows], (((1,), (1,)), ((), ())),
                                  precision=HIGHEST, preferred_element_type=F32)

    for d in range(2):
        mask = lower if d == 0 else upper
        tri = mask.astype(F32)
        tri_t = (upper if d == 0 else lower).astype(F32)
        if zero_init:
            h_s[...] = jnp.zeros_like(h_s)
        else:
            h0_ref = h0f_ref if d == 0 else h0b_ref
            for p in range(SSD_HEADS // 2):
                h_s[p] = jnp.concatenate([h0_ref[0, 2 * p], h0_ref[0, 2 * p + 1]], axis=0).T

        def chunk_body(ci, carry, d=d, mask=mask, tri=tri, tri_t=tri_t):
            c = ci if d == 0 else nc - 1 - ci
            a_c = a_s[c]
            cum = jnp.dot(tri, a_c, precision=HIGHEST, preferred_element_type=F32)
            cum_t = jnp.dot(at_s[c], tri_t, precision=HIGHEST, preferred_element_type=F32)
            dt_c = dt_s[c]
            for g in range(SSD_GROUPS):
                gcols = slice(g * SSD_STATE, (g + 1) * SSD_STATE)
                bt = bm_s[c, gcols, :]
                cg = cm_s[c, :, gcols]
                gram = jnp.dot(cg.astype(BF16), bt.astype(BF16), preferred_element_type=F32)
                for pp in range(SSD_HEADS // SSD_GROUPS // 2):
                    p = g * (SSD_HEADS // SSD_GROUPS // 2) + pp
                    pcols = slice(p * LANES, (p + 1) * LANES)
                    xh_pair = xh_s[c, :, pcols]
                    hprev = h_s[p]
                    hprev_b = hprev.astype(BF16)
                    parts = []
                    for e in range(2):
                        col = d * SSD_HEADS + 2 * p + e
                        mine = first_half if e == 0 else jnp.logical_not(first_half)
                        cc = jnp.broadcast_to(cum[:, col:col + 1], (q, q))
                        cr = cum_t[col:col + 1, :]
                        dec = jnp.exp(jnp.where(mask, cc - cr, -jnp.inf))
                        m = (gram * dec).astype(BF16)
                        xdt = jnp.where(mine, xh_pair * dt_c[:, col:col + 1], 0.0).astype(BF16)
                        tot = cum[q - 1:q, col:col + 1] if d == 0 else cum[0:1, col:col + 1]
                        bd_t = (bt * jnp.exp(tot - cr)).astype(BF16)
                        cd = (cg * jnp.exp(cc)).astype(BF16)
                        s_c = jnp.dot(bd_t, xdt, preferred_element_type=F32)
                        y_off = jnp.dot(cd, hprev_b, preferred_element_type=F32)
                        y_diag = jnp.dot(m, xdt, preferred_element_type=F32)
                        parts.append((s_c, y_off, y_diag, jnp.exp(tot)))
                    (s0, yo0, yd0, g0), (s1, yo1, yd1, g1) = parts
                    h_s[p] = hprev * jnp.where(first_half, g0, g1) + (s0 + s1)
                    y_s[c, :, pcols] += (yd0 + yd1) + jnp.where(first_half, yo0, yo1)
            return carry

        lax.fori_loop(0, nc, chunk_body, 0)
        out_ref = hf_ref if d == 0 else hb_ref
        for p in range(SSD_HEADS // 2):
            tr = h_s[p].T
            out_ref[0, 2 * p] = tr[0:SSD_HEAD_DIM]
            out_ref[0, 2 * p + 1] = tr[SSD_HEAD_DIM:]

    gw = SSD_WIDTH // SSD_GROUPS
    for c in range(nc):
        rows = slice(c * q, (c + 1) * q)
        for g in range(SSD_GROUPS):
            cols = slice(g * gw, (g + 1) * gw)
            y = y_s[c, :, cols] * _silu(z_ref[0, rows, cols])
            y = y * lax.rsqrt(jnp.mean(y * y, axis=-1, keepdims=True) + NORM_EPS)
            y_ref[0, rows, cols] = (y * ng_ref[:, cols]).astype(BF16)


def ssd_mixer(proj, p, h0f, h0b):
    bsz, seq, _ = proj.shape
    nc = seq // SSD_CHUNK
    zero_init = h0f is None

    def col_spec(width, start):
        blk = start // width
        return pl.BlockSpec((1, seq, width), lambda b: (b, 0, blk))

    def full2(arr):
        return pl.BlockSpec(arr.shape, lambda b: (0, 0))

    st_spec = pl.BlockSpec((1, SSD_HEADS, SSD_HEAD_DIM, SSD_STATE), lambda b: (b, 0, 0, 0))
    bc = SSD_GROUPS * SSD_STATE
    small = [p["cw_x"], p["cb_x"], p["cw_b"], p["cb_b"], p["cw_c"], p["cb_c"],
             p["dt_bias"], p["a_log"], p["d_exp"], p["ssd_ng"]]
    in_specs = [col_spec(SSD_WIDTH, COL_Z), col_spec(SSD_WIDTH, COL_X), col_spec(bc, COL_B),
                col_spec(bc, COL_C), col_spec(LANES, COL_DT)] + [full2(s) for s in small]
    args = [proj] * 5 + small
    if not zero_init:
        in_specs += [st_spec, st_spec]
        args += [h0f, h0b]
    st_shape = jax.ShapeDtypeStruct((bsz, SSD_HEADS, SSD_HEAD_DIM, SSD_STATE), F32)
    q = SSD_CHUNK
    return pl.pallas_call(
        functools.partial(_ssd_kernel, nc, zero_init),
        grid=(bsz,),
        in_specs=in_specs,
        out_specs=[pl.BlockSpec((1, seq, SSD_WIDTH), lambda b: (b, 0, 0)), st_spec, st_spec],
        out_shape=[jax.ShapeDtypeStruct((bsz, seq, SSD_WIDTH), BF16), st_shape, st_shape],
        scratch_shapes=[pltpu.VMEM((nc, q, SSD_WIDTH), F32), pltpu.VMEM((nc, bc, q), F32),
                        pltpu.VMEM((nc, q, bc), F32), pltpu.VMEM((nc, q, LANES), F32),
                        pltpu.VMEM((nc, q, LANES), F32), pltpu.VMEM((nc, LANES, q), F32),
                        pltpu.VMEM((nc, q, SSD_WIDTH), F32),
                        pltpu.VMEM((SSD_HEADS // 2, SSD_STATE, 2 * SSD_HEAD_DIM), F32)],
        compiler_params=_cparams(("parallel",)),
        name="ssd_mixer",
    )(*args)


def _seg_sum(x):
    lane = lax.broadcasted_iota(jnp.int32, x.shape, 1)
    first = lane < RWKV_HEAD_DIM
    s0 = jnp.sum(jnp.where(first, x, 0.0), axis=-1, keepdims=True)
    s1 = jnp.sum(jnp.where(first, 0.0, x), axis=-1, keepdims=True)
    return jnp.where(first, s0, s1)


def _shift_mix(u, mu):
    return u + mu * (0.5 * (_shifted(u, -1) + _shifted(u, 1)) - u)


def _rwkv_prep_kernel(ncb, rep, *refs):
    (r_ref, k_ref, v_ref, lo_ref, mur, muk, muv, mul, wupf, wupb, aup, gup,
     w0f, w0b, a0, ro, ko, vo, wo, ao, go) = refs[:21]
    staged = dict(zip(("r", "k", "a", "wf", "wb"), refs[21:]))
    q = pl.program_id(2)

    def put(name, cols, val):
        if staged:
            staged[name][:, cols] = val
        elif name in ("wf", "wb"):
            wo[0 if name == "wf" else 1, 0, 0, :, cols] = val
        else:
            {"r": ro, "k": ko, "a": ao}[name][0, 0, :, cols] = val

    def compute():
        lo = _shift_mix(lo_ref[0], mul[...])
        wdf = jnp.tanh(lo[:, 0:128]).astype(BF16)
        wdb = jnp.tanh(lo[:, 128:256]).astype(BF16)
        ad = lo[:, 256:384].astype(BF16)
        gd = jax.nn.sigmoid(lo[:, 384:640]).astype(BF16)
        for cb in range(ncb):
            cols = slice(cb * LANES, (cb + 1) * LANES)

            def mm(x, w_ref, cols=cols):
                return jnp.dot(x, w_ref[:, cols].astype(BF16), preferred_element_type=F32)

            r = _shift_mix(r_ref[0, :, cols], mur[:, cols])
            k = _shift_mix(k_ref[0, :, cols], muk[:, cols])
            v = _shift_mix(v_ref[0, :, cols], muv[:, cols])
            put("wf", cols, jnp.exp(-DECAY_SCALE * jax.nn.sigmoid(w0f[:, cols] + mm(wdf, wupf))))
            put("wb", cols, jnp.exp(-DECAY_SCALE * jax.nn.sigmoid(w0b[:, cols] + mm(wdb, wupb))))
            put("a", cols, jax.nn.sigmoid(a0[:, cols] + mm(ad, aup)))
            go[0, :, cols] = mm(gd, gup)
            put("r", cols, r)
            put("k", cols, k)
            vo[0, :, cols] = v

    if staged:
        pl.when(q == 0)(compute)
        ro[0, 0] = staged["r"][...]
        ko[0, 0] = staged["k"][...]
        ao[0, 0] = staged["a"][...]
        wo[0, 0, 0] = staged["wf"][...]
        wo[1, 0, 0] = staged["wb"][...]
    else:
        compute()


def rwkv_prep(proj, p, rep):
    bsz, seq, _ = proj.shape
    cw = min(RWKV_WIDTH, PREP_BLOCK_ELEMS // seq)
    nblk = RWKV_WIDTH // cw

    def col_spec(start):
        blk = start // cw
        return pl.BlockSpec((1, seq, cw), lambda b, j, q: (b, 0, blk + j))

    vec = pl.BlockSpec((1, cw), lambda b, j, q: (0, j))

    def mat(rows):
        return pl.BlockSpec((rows, cw), lambda b, j, q: (0, j))

    tok_spec = pl.BlockSpec((1, seq, cw), lambda b, j, q: (b, 0, j))
    rep_spec = pl.BlockSpec((1, 1, seq, cw), lambda b, j, q: (q, b, 0, j))
    w_spec = pl.BlockSpec((2, 1, 1, seq, cw), lambda b, j, q: (0, q, b, 0, j))
    tok_sh = jax.ShapeDtypeStruct((bsz, seq, RWKV_WIDTH), F32)
    rep_sh = jax.ShapeDtypeStruct((rep, bsz, seq, RWKV_WIDTH), F32)
    w_sh = jax.ShapeDtypeStruct((2, rep, bsz, seq, RWKV_WIDTH), F32)
    return pl.pallas_call(
        functools.partial(_rwkv_prep_kernel, cw // LANES, rep),
        grid=(bsz, nblk, rep),
        in_specs=[col_spec(COL_R), col_spec(COL_K), col_spec(COL_V),
                  pl.BlockSpec((1, seq, LORA_W), lambda b, j, q: (b, 0, COL_LORA // LORA_W)),
                  vec, vec, vec, pl.BlockSpec((1, LORA_W), lambda b, j, q: (0, 0)),
                  mat(128), mat(128), mat(128), mat(256),
                  vec, vec, vec],
        out_specs=[rep_spec, rep_spec, tok_spec, w_spec, rep_spec, tok_spec],
        out_shape=[rep_sh, rep_sh, tok_sh, w_sh, rep_sh, tok_sh],
        scratch_shapes=[pltpu.VMEM((seq, cw), F32)] * (5 if rep > 1 else 0),
        compiler_params=_cparams(("parallel", "arbitrary", "arbitrary")),
        name="rwkv_prep",
    )(proj, proj, proj, proj, p["mu_r"], p["mu_k"], p["mu_v"], p["mu_lora"],
      p["w_up_f"], p["w_up_b"], p["a_up"], p["g_up"],
      p["w0_f"], p["w0_b"], p["a0"])


def _rwkv_scan_kernel(tb, zero_init, *refs):
    if zero_init:
        r_ref, w_ref, kr_ref, a_ref, v_ref, kk_ref, ka_ref, y_ref, st_ref, s_s, k_ref, al_ref, be_ref = refs
    else:
        (r_ref, w_ref, kr_ref, a_ref, v_ref, kk_ref, ka_ref, s0_ref, y_ref, st_ref,
         s_s, k_ref, al_ref, be_ref) = refs
    kd = RWKV_HEAD_DIM
    d = pl.program_id(0)
    step_i = pl.program_id(2)

    def time_index(i):
        return i + d * (tb - 1 - 2 * i)

    @pl.when(step_i == 0)
    def _():
        if zero_init:
            s_s[...] = jnp.zeros_like(s_s)
        else:
            s_s[...] = s0_ref[...]

    def derive(t, carry):
        kraw = kr_ref[t]
        a = a_ref[t]
        kk = kraw * kk_ref[...]
        kk = kk / jnp.maximum(jnp.sqrt(jnp.sum(kk * kk, axis=0, keepdims=True)), 1e-12)
        al_ref[t] = -kk
        be_ref[t] = kk * a
        k_ref[t] = kraw * (1.0 + (a - 1.0) * ka_ref[...])
        return carry

    lax.fori_loop(0, tb, derive, 0)

    t0 = time_index(0)
    sa0 = jnp.zeros(s_s.shape[1:], F32)
    sa1 = jnp.zeros(s_s.shape[1:], F32)
    for kq in range(0, kd, 2):
        sa0 = sa0 + s_s[kq] * al_ref[t0, kq:kq + 1, :]
        sa1 = sa1 + s_s[kq + 1] * al_ref[t0, kq + 1:kq + 2, :]

    def step(i, sa):
        t = time_index(i)
        tn = time_index(jnp.minimum(i + 1, tb - 1))
        vt = v_ref[t]
        y = jnp.zeros_like(sa)
        sa_next = jnp.zeros_like(sa)
        for kq in range(kd):
            row = slice(kq, kq + 1)
            sk = s_s[kq] * w_ref[t, row, :] + sa * be_ref[t, row, :] + vt * k_ref[t, row, :]
            s_s[kq] = sk
            y = y + sk * r_ref[t, row, :]
            sa_next = sa_next + sk * al_ref[tn, row, :]
        y_ref[t] = y
        return sa_next

    lax.fori_loop(0, tb, step, sa0 + sa1)

    @pl.when(step_i == pl.num_programs(2) - 1)
    def _():
        st_ref[...] = s_s[...]


def rwkv_scan(r, w, k, a, v, kk_tab, ka_tab, s0):
    seq, kd, nch = r.shape
    vv = v.shape[1]
    tb = 32
    nblk = seq // tb
    zero_init = s0 is None

    def tblk(d, i):
        return i + d * (nblk - 1 - 2 * i)

    kspec = pl.BlockSpec((tb, kd, LANES), lambda d, g, i: (tblk(d, i), 0, g))
    wspec = pl.BlockSpec((None, tb, kd, LANES), lambda d, g, i: (d, tblk(d, i), 0, g))
    vspec = pl.BlockSpec((tb, vv, LANES), lambda d, g, i: (tblk(d, i), 0, g))
    yspec = pl.BlockSpec((None, tb, vv, LANES), lambda d, g, i: (d, tblk(d, i), 0, g))
    sspec = pl.BlockSpec((None, kd, vv, LANES), lambda d, g, i: (d, 0, 0, g))
    tabspec = pl.BlockSpec((kd, LANES), lambda d, g, i: (0, 0))
    in_specs = [kspec, wspec, kspec, kspec, vspec, tabspec, tabspec]
    args = [r, w, k, a, v, kk_tab, ka_tab]
    if not zero_init:
        in_specs.append(sspec)
        args.append(s0)
    return pl.pallas_call(
        functools.partial(_rwkv_scan_kernel, tb, zero_init),
        grid=(2, nch // LANES, nblk),
        in_specs=in_specs,
        out_specs=[yspec, sspec],
        out_shape=[jax.ShapeDtypeStruct((2, seq, vv, nch), F32),
                   jax.ShapeDtypeStruct((2, kd, vv, nch), F32)],
        scratch_shapes=[pltpu.VMEM((kd, vv, LANES), F32)] + [pltpu.VMEM((tb, kd, LANES), F32)] * 3,
        compiler_params=_cparams(("parallel", "parallel", "arbitrary")),
        name="rwkv_scan",
    )(*args)


def _rwkv_post_kernel(y_ref, r_ref, k_ref, a_ref, v_ref, g_ref, lng, lnb, rk, ka, o_ref):
    inv = 1.0 / RWKV_HEAD_DIM
    for cb in range(RWKV_WIDTH // LANES):
        cols = slice(cb * LANES, (cb + 1) * LANES)
        y = y_ref[0, 0, :, cols] + y_ref[1, 0, :, cols]
        mean = _seg_sum(y) * inv
        yc = y - mean
        var = _seg_sum(yc * yc) * inv
        yn = yc * lax.rsqrt(var + GN_EPS) * lng[:, cols] + lnb[:, cols]
        k = k_ref[0, :, cols] * (1.0 + (a_ref[0, :, cols] - 1.0) * ka[:, cols])
        bonus = _seg_sum(r_ref[0, :, cols] * k * rk[:, cols]) * v_ref[0, :, cols]
        o_ref[0, :, cols] = ((yn + bonus) * g_ref[0, :, cols]).astype(BF16)


def rwkv_post(y2, r, k, a, v, g, p):
    _, bsz, seq, wd = y2.shape
    tl = 256
    spec = pl.BlockSpec((1, tl, wd), lambda b, i: (b, i, 0))
    vec = pl.BlockSpec((1, wd), lambda b, i: (0, 0))
    return pl.pallas_call(
        _rwkv_post_kernel,
        grid=(bsz, seq // tl),
        in_specs=[pl.BlockSpec((2, 1, tl, wd), lambda b, i: (0, b, i, 0))] + [spec] * 5 + [vec] * 4,
        out_specs=spec,
        out_shape=jax.ShapeDtypeStruct((bsz, seq, wd), BF16),
        compiler_params=_cparams(("parallel", "parallel")),
        name="rwkv_post",
    )(y2, r, k, a, v, g, p["ln_g"], p["ln_b"], p["r_k"], p["k_a"])


def _split_bf16(x):
    hi = x.astype(BF16)
    return hi, (x - hi.astype(F32)).astype(BF16)


def _out_proj_kernel(ys_ref, yr_ref, wo_ref, x_ref, gm_ref, ng_ref, sc_ref, sh_ref, rwh_ref, rwl_ref,
                     x1_ref, h2_ref, lg_ref):
    acc = jnp.dot(ys_ref[0], wo_ref[0:SSD_WIDTH, :], preferred_element_type=F32)
    acc = acc + jnp.dot(yr_ref[0], wo_ref[SSD_WIDTH:, :], preferred_element_type=F32)
    x1 = x_ref[0] + gm_ref[0] * acc
    x1_ref[0] = x1
    hn = x1 * lax.rsqrt(jnp.mean(x1 * x1, axis=-1, keepdims=True) + NORM_EPS) * ng_ref[...]
    h2 = hn * (1.0 + sc_ref[0]) + sh_ref[0]
    hi, lo = _split_bf16(h2)
    h2_ref[0] = hi
    lg_ref[0] = (jnp.dot(hi, rwh_ref[...], preferred_element_type=F32)
                 + jnp.dot(lo, rwh_ref[...], preferred_element_type=F32)
                 + jnp.dot(hi, rwl_ref[...], preferred_element_type=F32))


def out_proj(y_ssd, y_rwkv, w_out, x, gate_m, norm_g, scale_f, shift_f, router_w):
    bsz, seq, d = x.shape
    tm = min(512, seq)
    per_b = gate_m.shape[0] > 1
    midx = (lambda b, i: (b, 0, 0)) if per_b else (lambda b, i: (0, 0, 0))
    half = pl.BlockSpec((1, tm, d // 2), lambda b, i: (b, i, 0))
    full = pl.BlockSpec((1, tm, d), lambda b, i: (b, i, 0))
    mspec = pl.BlockSpec((1, 1, d), midx)
    return pl.pallas_call(
        _out_proj_kernel,
        grid=(bsz, seq // tm),
        in_specs=[half, half, pl.BlockSpec((d, d), lambda b, i: (0, 0)), full, mspec,
                  pl.BlockSpec((1, d), lambda b, i: (0, 0)), mspec, mspec,
                  pl.BlockSpec((d, LANES), lambda b, i: (0, 0)),
                  pl.BlockSpec((d, LANES), lambda b, i: (0, 0))],
        out_specs=[full, full, pl.BlockSpec((1, tm, LANES), lambda b, i: (b, i, 0))],
        out_shape=[jax.ShapeDtypeStruct((bsz, seq, d), F32), jax.ShapeDtypeStruct((bsz, seq, d), BF16),
                   jax.ShapeDtypeStruct((bsz, seq, LANES), F32)],
        compiler_params=_cparams(("parallel", "parallel")),
        name="out_proj",
    )(y_ssd, y_rwkv, w_out, x, gate_m, norm_g.reshape(1, d), scale_f, shift_f, *router_w)


def _route_kernel(nt, cap, lg_ref, selr_ref, posr_ref, selt_ref, post_ref, afft_ref, aff_s, pre_s):
    ne = N_EXPERTS
    lane = lax.broadcasted_iota(jnp.int32, (LANES, LANES), 1)
    sub = lax.broadcasted_iota(jnp.int32, (LANES, LANES), 0)
    upper_incl = (sub <= lane).astype(BF16)
    for i in range(nt):
        lg = jnp.where(lane < ne, lg_ref[i * LANES:(i + 1) * LANES, :], -jnp.inf)
        e = jnp.exp(lg - jnp.max(lg, axis=-1, keepdims=True))
        aff = e / jnp.sum(e, axis=-1, keepdims=True)
        afft_ref[i * LANES:(i + 1) * LANES, :] = aff
        aff_s[i] = aff.T[0:ne, :]
    aff3 = aff_s[...]

    def count(mask):
        s = jnp.sum(mask.astype(F32), axis=0, keepdims=True)
        return jnp.sum(s, axis=2, keepdims=True)

    def search(_, carry):
        lo, hi = carry
        mid = 0.5 * (lo + hi)
        ok = count(aff3 >= mid) >= cap
        return jnp.where(ok, mid, lo), jnp.where(ok, hi, mid)

    lo0 = jnp.zeros((1, ne, 1), F32)
    hi0 = jnp.full((1, ne, 1), 2.0, F32)
    lo, _ = lax.fori_loop(0, SEARCH_ITERS, search, (lo0, hi0))
    cand = jnp.where(aff3 >= lo, aff3, 4.0)
    thr = jnp.min(jnp.min(cand, axis=0, keepdims=True), axis=2, keepdims=True)
    gt = (aff3 > thr).astype(F32)
    eq = (aff3 == thr).astype(F32)
    need = cap - count(aff3 > thr)[0]

    def prefix_excl(m3):
        off = jnp.zeros((ne, 1), F32)
        for i in range(nt):
            inc = jnp.dot(m3[i].astype(BF16), upper_incl, preferred_element_type=F32)
            pre_s[i] = inc - m3[i] + off
            off = off + inc[:, LANES - 1:LANES]
        return pre_s[...]

    sel = jnp.maximum(gt, eq * (prefix_excl(eq) < need[None]).astype(F32))
    pos = prefix_excl(sel)
    selr_ref[...] = sel
    posr_ref[...] = pos
    zpad = jnp.zeros((LANES - ne, LANES), F32)
    for i in range(nt):
        rows = slice(i * LANES, (i + 1) * LANES)
        selt_ref[rows, :] = jnp.concatenate([sel[i], zpad], axis=0).T
        post_ref[rows, :] = jnp.concatenate([pos[i], zpad], axis=0).T


def route(logits, cap):
    n_tok = logits.shape[0]
    nt = n_tok // LANES
    row_sh = jax.ShapeDtypeStruct((nt, N_EXPERTS, LANES), F32)
    tm_sh = jax.ShapeDtypeStruct((n_tok, LANES), F32)
    return pl.pallas_call(
        functools.partial(_route_kernel, nt, cap),
        out_shape=[row_sh, row_sh, tm_sh, tm_sh, tm_sh],
        scratch_shapes=[pltpu.VMEM((nt, N_EXPERTS, LANES), F32), pltpu.VMEM((nt, N_EXPERTS, LANES), F32)],
        compiler_params=pltpu.CompilerParams(vmem_limit_bytes=VMEM_LIMIT),
        name="route",
    )(logits)


def _window_start(starts_ref, tile, e, cap, win, align=SUBLANES):
    ps = starts_ref[tile * N_EXPERTS + e]
    return pl.multiple_of(jnp.minimum((ps // align) * align, cap - win), align)


def _gather_kernel(cap, win, nsub, starts_ref, sel_ref, pos_ref, h_ref, o_ref):
    eg = pl.program_id(0)
    j = pl.program_id(1)

    @pl.when(j == 0)
    def _():
        o_ref[...] = jnp.zeros_like(o_ref)

    for el in range(GATHER_EXPERTS):
        e = eg * GATHER_EXPERTS + el
        for s in range(0, nsub, 2):
            tile = j * nsub + s
            s0 = _window_start(starts_ref, tile, e, cap, win, align=2 * SUBLANES)
            lo = starts_ref[tile * N_EXPERTS + e]
            hi = starts_ref[(tile + 2) * N_EXPERTS + e]
            prows = [pos_ref[s + u, pl.ds(e, 1), :] for u in range(2)]
            srows = [sel_ref[s + u, pl.ds(e, 1), :] for u in range(2)]
            for piece in range(win // GATHER_ROWS):
                p0 = pl.multiple_of(s0 + piece * GATHER_ROWS, 2 * SUBLANES)

                @pl.when(jnp.logical_and(p0 < hi, p0 + GATHER_ROWS > lo))
                def _(p0=p0, s=s, el=el, prows=prows, srows=srows):
                    slot = (lax.broadcasted_iota(jnp.int32, (GATHER_ROWS, LANES), 0) + p0).astype(F32)
                    onehot = jnp.concatenate(
                        [jnp.where(jnp.logical_and(slot == prows[u], srows[u] > 0.0), 1.0, 0.0).astype(BF16)
                         for u in range(2)], axis=1)
                    rows = pl.ds(p0, GATHER_ROWS)
                    got = jnp.dot(onehot, h_ref[s * LANES:(s + 2) * LANES, :], preferred_element_type=F32)
                    o_ref[el, rows, :] = (o_ref[el, rows, :].astype(F32) + got).astype(BF16)


def gather_tokens(starts, sel_r, pos_r, h, cap):
    n_tok, d = h.shape
    tt = min(1024, n_tok)
    nsub = tt // LANES
    win = min(-(-(2 * LANES + 2 * SUBLANES) // GATHER_ROWS) * GATHER_ROWS, cap)
    assert win % GATHER_ROWS == 0 and (cap - win) % (2 * SUBLANES) == 0
    rspec = pl.BlockSpec((nsub, N_EXPERTS, LANES), lambda e, j, st: (j, 0, 0))
    return pl.pallas_call(
        functools.partial(_gather_kernel, cap, win, nsub),
        grid_spec=pltpu.PrefetchScalarGridSpec(
            num_scalar_prefetch=1,
            grid=(N_EXPERTS // GATHER_EXPERTS, n_tok // tt),
            in_specs=[rspec, rspec, pl.BlockSpec((tt, d), lambda e, j, st: (j, 0))],
            out_specs=pl.BlockSpec((GATHER_EXPERTS, cap, d), lambda e, j, st: (e, 0, 0))),
        out_shape=jax.ShapeDtypeStruct((N_EXPERTS, cap, d), BF16),
        compiler_params=_cparams(("parallel", "arbitrary")),
        name="gather_tokens",
    )(starts, sel_r, pos_r, h)


def _ffn_kernel(n_sets, *refs):
    x_refs = refs[:n_sets]
    wg_ref, wu_ref, wd_ref = refs[n_sets:n_sets + 3]
    o_refs = refs[n_sets + 3:2 * n_sets + 3]
    accs = refs[2 * n_sets + 3:]
    f = pl.program_id(1)
    wg = wg_ref[0].astype(BF16)
    wu = wu_ref[0].astype(BF16)
    wd = wd_ref[0].astype(BF16)
    for x_ref, o_ref, acc in zip(x_refs, o_refs, accs):
        @pl.when(f == 0)
        def _():
            acc[...] = jnp.zeros_like(acc)

        x = x_ref[0]
        hg = jnp.dot(x, wg, preferred_element_type=F32)
        hu = jnp.dot(x, wu, preferred_element_type=F32)
        hid = (_silu(hg) * hu).astype(BF16)
        acc[...] += jnp.dot(hid, wd, preferred_element_type=F32)

        @pl.when(f == pl.num_programs(1) - 1)
        def _():
            o_ref[0] = acc[...].astype(BF16)


def expert_ffn(xs_sets, wg, wu, wd):
    ne, _, d = xs_sets[0].shape
    ff = wg.shape[2]
    tf = 256
    xspecs = [pl.BlockSpec((1, xs.shape[1], d), lambda e, f: (e, 0, 0)) for xs in xs_sets]
    return pl.pallas_call(
        functools.partial(_ffn_kernel, len(xs_sets)),
        grid=(ne, ff // tf),
        in_specs=xspecs + [pl.BlockSpec((1, d, tf), lambda e, f: (e, 0, f)),
                           pl.BlockSpec((1, d, tf), lambda e, f: (e, 0, f)),
                           pl.BlockSpec((1, tf, d), lambda e, f: (e, f, 0))],
        out_specs=xspecs,
        out_shape=[jax.ShapeDtypeStruct(xs.shape, BF16) for xs in xs_sets],
        scratch_shapes=[pltpu.VMEM(xs.shape[1:], F32) for xs in xs_sets],
        compiler_params=_cparams(("parallel", "arbitrary")),
        name="expert_ffn",
    )(*xs_sets, wg, wu, wd)


def _scatter_kernel(cap, win, nsub, starts_ref, sel_ref, pos_ref, aff_ref, ob_ref, y_ref):
    c = pl.program_id(0)
    e = pl.program_id(1)

    @pl.when(e == 0)
    def _():
        y_ref[...] = jnp.zeros_like(y_ref)

    mine = lax.broadcasted_iota(jnp.int32, (LANES, LANES), 1) == e
    for s in range(nsub):
        rows = slice(s * LANES, (s + 1) * LANES)

        def col(ref):
            return jnp.sum(jnp.where(mine, ref[rows, :], 0.0), axis=-1, keepdims=True)

        pcol, scol, gcol = col(pos_ref), col(sel_ref), col(aff_ref)
        s0 = _window_start(starts_ref, c * nsub + s, e, cap, win)
        slot = (lax.broadcasted_iota(jnp.int32, (LANES, win), 1) + s0).astype(F32)
        onehot = jnp.where(jnp.logical_and(slot == pcol, scol > 0.0), 1.0, 0.0).astype(BF16)
        y_ref[rows, :] += gcol * jnp.dot(onehot, ob_ref[0, pl.ds(s0, win), :],
                                         preferred_element_type=F32)


def scatter_combine(starts, sel_t, pos_t, aff_t, outbuf):
    n_tok = sel_t.shape[0]
    ne, cap, d = outbuf.shape
    chunk = min(2048, n_tok)
    nsub = chunk // LANES
    win = min(2 * LANES, cap)
    tspec = pl.BlockSpec((chunk, LANES), lambda c, e, st: (c, 0))
    return pl.pallas_call(
        functools.partial(_scatter_kernel, cap, win, nsub),
        grid_spec=pltpu.PrefetchScalarGridSpec(
            num_scalar_prefetch=1,
            grid=(n_tok // chunk, ne),
            in_specs=[tspec, tspec, tspec, pl.BlockSpec((1, cap, d), lambda c, e, st: (e, 0, 0))],
            out_specs=pl.BlockSpec((chunk, d), lambda c, e, st: (c, 0))),
        out_shape=jax.ShapeDtypeStruct((n_tok, d), F32),
        compiler_params=_cparams(("parallel", "arbitrary")),
        name="scatter_combine",
    )(starts, sel_t, pos_t, aff_t, outbuf)


def _final_kernel(x1_ref, y_ref, gf_ref, fg_ref, o_ref):
    x2 = x1_ref[0] + gf_ref[0] * y_ref[0]
    o_ref[0] = x2 * lax.rsqrt(jnp.mean(x2 * x2, axis=-1, keepdims=True) + NORM_EPS) * fg_ref[...]


def final_norm(x1, y, gate_f, final_g):
    bsz, seq, d = x1.shape
    tl = 256
    per_b = gate_f.shape[0] > 1
    midx = (lambda b, i: (b, 0, 0)) if per_b else (lambda b, i: (0, 0, 0))
    xspec = pl.BlockSpec((1, tl, d), lambda b, i: (b, i, 0))
    return pl.pallas_call(
        _final_kernel,
        grid=(bsz, seq // tl),
        in_specs=[xspec, xspec, pl.BlockSpec((1, 1, d), midx), pl.BlockSpec((1, d), lambda b, i: (0, 0))],
        out_specs=xspec,
        out_shape=jax.ShapeDtypeStruct((bsz, seq, d), F32),
        compiler_params=_cparams(("parallel", "parallel")),
        name="final_norm",
    )(x1, y, gate_f, final_g.reshape(1, d))


def _pad_cols(w, width):
    return jnp.pad(w, ((0, 0), (0, width - w.shape[1])))


def _pad_rows(w, rows):
    return jnp.pad(w, ((0, rows - w.shape[0]), (0, 0)))


def _relayout_columns(w):
    n_ssd = 2 * SSD_WIDTH + 2 * SSD_GROUPS * SSD_STATE + 2 * SSD_HEADS
    ssd, rw = w[:, :n_ssd], w[:, n_ssd:]
    o = 3 * RWKV_WIDTH
    zx = 2 * SSD_WIDTH
    bc = zx + 2 * SSD_GROUPS * SSD_STATE
    parts = [ssd[:, :zx], rw[:, :o], ssd[:, zx:bc], _pad_cols(ssd[:, bc:], LANES),
             _pad_cols(rw[:, o:o + DECAY_LORA], LANES),
             _pad_cols(rw[:, o + DECAY_LORA:o + 2 * DECAY_LORA], LANES),
             _pad_cols(rw[:, o + 2 * DECAY_LORA:o + 2 * DECAY_LORA + ICLR_LORA], LANES),
             _pad_cols(rw[:, o + 2 * DECAY_LORA + ICLR_LORA:], 2 * LANES)]
    return jnp.concatenate(parts, axis=1)


def _layer_params(l, w_in, w_out, ssd_conv_w, ssd_conv_b, ssd_A_log, ssd_dt_bias, ssd_D, ssd_norm_g,
                  rwkv_mu, rwkv_w0, rwkv_w_up, rwkv_a0, rwkv_a_up, rwkv_g_up, rwkv_k_k, rwkv_k_a,
                  rwkv_r_k, rwkv_ln_g, rwkv_ln_b, router_w):
    row = lambda a: a.reshape(1, -1)
    n_ssd = 2 * SSD_WIDTH + 2 * SSD_GROUPS * SSD_STATE + 2 * SSD_HEADS
    mu_full = _relayout_columns(jnp.concatenate([jnp.zeros((1, n_ssd), F32), row(rwkv_mu[l])], axis=1))
    cw, cb = ssd_conv_w[l], row(ssd_conv_b[l])
    xe, be = SSD_WIDTH, SSD_WIDTH + SSD_GROUPS * SSD_STATE
    return dict(
        w_in=_relayout_columns(w_in[l]).astype(BF16),
        w_out=w_out[l].astype(BF16),
        cw_x=cw[:, :xe], cb_x=cb[:, :xe], cw_b=cw[:, xe:be], cb_b=cb[:, xe:be],
        cw_c=cw[:, be:], cb_c=cb[:, be:],
        dt_bias=_pad_cols(row(ssd_dt_bias[l]), LANES), a_log=_pad_cols(row(ssd_A_log[l]), LANES),
        d_exp=row(jnp.repeat(ssd_D[l], SSD_HEAD_DIM)), ssd_ng=row(ssd_norm_g[l]),
        mu_r=mu_full[:, COL_R:COL_K], mu_k=mu_full[:, COL_K:COL_V], mu_v=mu_full[:, COL_V:COL_B],
        mu_lora=mu_full[:, COL_LORA:],
        w_up_f=_pad_rows(rwkv_w_up[l, 0], LANES), w_up_b=_pad_rows(rwkv_w_up[l, 1], LANES),
        a_up=_pad_rows(rwkv_a_up[l], LANES), g_up=_pad_rows(rwkv_g_up[l], 2 * LANES),
        w0_f=row(rwkv_w0[l, 0]), w0_b=row(rwkv_w0[l, 1]), a0=row(rwkv_a0[l]),
        k_k=row(rwkv_k_k[l]), k_a=row(rwkv_k_a[l]), r_k=row(rwkv_r_k[l]),
        ln_g=row(rwkv_ln_g[l]), ln_b=row(rwkv_ln_b[l]),
        router_w=tuple(_pad_cols(t, LANES) for t in _split_bf16(router_w[l])),
    )


def _to_chain(x):
    lead = x.shape[:-3]
    bsz, seq, _ = x.shape[-3:]
    n = len(lead)
    x = x.reshape(lead + (bsz, seq, RWKV_HEADS, RWKV_HEAD_DIM))
    perm = tuple(range(n)) + (n + 1, n + 3, n, n + 2)
    return jnp.transpose(x, perm).reshape(lead + (seq, RWKV_HEAD_DIM, bsz * RWKV_HEADS))


def _from_chain(y, bsz):
    lead = y.shape[:-3]
    seq = y.shape[-3]
    n = len(lead)
    y = y.reshape(lead + (seq, RWKV_HEAD_DIM, bsz, RWKV_HEADS))
    perm = tuple(range(n)) + (n + 2, n, n + 3, n + 1)
    return jnp.transpose(y, perm).reshape(lead + (bsz, seq, RWKV_WIDTH))


def _state_to_chain(s):
    return jnp.transpose(s, (0, 4, 3, 1, 2)).reshape(2, RWKV_HEAD_DIM, RWKV_HEAD_DIM, -1)


def _state_from_chain(s, bsz):
    return jnp.transpose(s.reshape(2, RWKV_HEAD_DIM, RWKV_HEAD_DIM, bsz, RWKV_HEADS), (0, 3, 4, 2, 1))


def _scan_replicas(bsz):
    nch = bsz * RWKV_HEADS
    rep = max(1, LANES // nch)
    assert (nch * rep) % LANES == 0
    return rep


def rwkv_mixer_scan(r, k, v, w2, a, s0, p):
    rep, bsz, seq, _ = r.shape
    nh, hd = RWKV_HEADS, RWKV_HEAD_DIM
    fold = lambda t: t.reshape(t.shape[:-4] + (rep * bsz, seq, RWKV_WIDTH))
    rc, kc, ac, wc = (_to_chain(fold(t)) for t in (r, k, a, w2))
    tabs = [jnp.tile(p[n].reshape(nh, hd).T, (1, LANES // nh)) for n in ("k_k", "k_a")]
    if rep == 1:
        s0c = None if s0 is None else _state_to_chain(s0)
        y, st = rwkv_scan(rc, wc, kc, ac, _to_chain(v), *tabs, s0c)
        return _from_chain(y, bsz), _state_from_chain(st, bsz)

    vq = hd // rep
    vc = jnp.transpose(v.reshape(bsz, seq, nh, rep, vq), (1, 4, 3, 0, 2)).reshape(seq, vq, LANES)
    s0c = None
    if s0 is not None:
        s0c = jnp.transpose(s0.reshape(2, bsz, nh, rep, vq, hd), (0, 5, 4, 3, 1, 2)).reshape(
            2, hd, vq, LANES)
    y, st = rwkv_scan(rc, wc, kc, ac, vc, *tabs, s0c)
    y = jnp.transpose(y.reshape(2, seq, vq, rep, bsz, nh), (0, 4, 1, 5, 3, 2)).reshape(
        2, bsz, seq, RWKV_WIDTH)
    st = jnp.transpose(st.reshape(2, hd, vq, rep, bsz, nh), (0, 4, 5, 3, 2, 1)).reshape(
        2, bsz, nh, hd, hd)
    return y, st


def _grid_pos_embed(n_tokens, dim):
    rows = n_tokens // GRID_W
    row = jnp.repeat(jnp.arange(rows, dtype=F32), GRID_W)
    col = jnp.tile(jnp.arange(GRID_W, dtype=F32), rows)
    quarter = dim // 4
    freqs = jnp.exp(jnp.arange(quarter, dtype=F32) * (-math.log(POS_BASE) / quarter))

    def axis_embed(pos):
        ang = pos[:, None] * freqs[None, :]
        return jnp.concatenate([jnp.sin(ang), jnp.cos(ang)], axis=-1)

    return jnp.concatenate([axis_embed(row), axis_embed(col)], axis=-1)


def layer_to_dispatch(x, pos, mod, p, norm_mix_g, norm_ffn_g, states):
    bsz, seq, d = x.shape
    shift_m, scale_m, gate_m, shift_f, scale_f, gate_f = mod
    h, x0 = norm_modulate(x, pos, norm_mix_g, scale_m, shift_m)
    n_tok = bsz * seq
    proj = matmul_bf16(h.reshape(n_tok, d), p["w_in"], min(2048, n_tok), 640).reshape(bsz, seq, N_PROJ)
    h0f, h0b, s0f, s0b = states
    y_ssd, hf, hb = ssd_mixer(proj, p, h0f, h0b)
    r, k, v, w2, a, g = rwkv_prep(proj, p, _scan_replicas(bsz))
    s0 = None if s0f is None else jnp.stack([s0f, s0b], axis=0)
    y2, st = rwkv_mixer_scan(r, k, v, w2, a, s0, p)
    y_rwkv = rwkv_post(y2, r[0], k[0], a[0], v, g, p)
    x1, h2, logits = out_proj(y_ssd, y_rwkv, p["w_out"], x0, gate_m, norm_ffn_g, scale_f, shift_f,
                              p["router_w"])
    cap = EC_CAPACITY * n_tok // N_EXPERTS
    sel_r, pos_r, sel_t, pos_t, aff_t = route(logits.reshape(n_tok, LANES), cap)
    starts = jnp.concatenate([pos_r[:, :, 0], jnp.full((1, N_EXPERTS), cap, F32)], axis=0)
    starts = starts.astype(jnp.int32).reshape(-1)
    xs = gather_tokens(starts, sel_r, pos_r, h2.reshape(n_tok, d), cap)
    routed = dict(xs=xs, starts=starts, sel_t=sel_t, pos_t=pos_t, aff_t=aff_t, x1=x1, gate_f=gate_f)
    return routed, (hf, hb, st[0], st[1])


def combine_and_finish(routed, outbuf, final_g):
    x1 = routed["x1"]
    y_ffn = scatter_combine(routed["starts"], routed["sel_t"], routed["pos_t"], routed["aff_t"], outbuf)
    return final_norm(x1, y_ffn.reshape(x1.shape), routed["gate_f"], final_g)


def kernel(x_prompt, x_sample, state_ssd_fwd, state_ssd_bwd, state_rwkv_fwd, state_rwkv_bwd, c, c_ctx,
           w_ada, b_ada, norm_mix_g, norm_ffn_g, w_in, w_out, ssd_conv_w, ssd_conv_b, ssd_A_log,
           ssd_dt_bias, ssd_D, ssd_norm_g, rwkv_mu, rwkv_w0, rwkv_w_up, rwkv_a0, rwkv_a_up, rwkv_g_up,
           rwkv_k_k, rwkv_k_a, rwkv_r_k, rwkv_ln_g, rwkv_ln_b, router_w, exp_w_gate, exp_w_up,
           exp_w_down, final_norm_g):
    depth = w_in.shape[0]
    assert depth == 1, "the final norm runs right after the single layer's FFN residual"
    l = 0
    dec_b = x_sample.shape[0]
    p = _layer_params(l, w_in, w_out, ssd_conv_w, ssd_conv_b, ssd_A_log, ssd_dt_bias, ssd_D, ssd_norm_g,
                      rwkv_mu, rwkv_w0, rwkv_w_up, rwkv_a0, rwkv_a_up, rwkv_g_up, rwkv_k_k, rwkv_k_a,
                      rwkv_r_k, rwkv_ln_g, rwkv_ln_b, router_w)
    cond = jnp.concatenate([c_ctx[None, :], c, jnp.zeros((16 - 1 - dec_b, D_MODEL), F32)], axis=0)
    mod = ada_modulation(cond, w_ada[l], b_ada[l]).reshape(16, N_MOD, 1, D_MODEL)
    mod_ctx = [mod[0:1, i] for i in range(N_MOD)]
    mod_lat = [mod[1:1 + dec_b, i] for i in range(N_MOD)]

    routed_ctx, st = layer_to_dispatch(x_prompt, None, mod_ctx, p, norm_mix_g[l], norm_ffn_g[l],
                                       (None, None, None, None))
    pos = _grid_pos_embed(x_sample.shape[1], D_MODEL)
    cached = (state_ssd_fwd[:, l], state_ssd_bwd[:, l], state_rwkv_fwd[:, l], state_rwkv_bwd[:, l])
    routed_lat, _ = layer_to_dispatch(x_sample, pos, mod_lat, p, norm_mix_g[l], norm_ffn_g[l], cached)
    out_ctx, out_lat = expert_ffn([routed_ctx["xs"], routed_lat["xs"]],
                                  exp_w_gate[l], exp_w_up[l], exp_w_down[l])
    y_prompt = combine_and_finish(routed_ctx, out_ctx, final_norm_g)
    y_sample = combine_and_finish(routed_lat, out_lat, final_norm_g)
    return (y_prompt, y_sample, st[0][:, None], st[1][:, None], st[2][:, None], st[3][:, None])
```

```python
import functools
import math

import jax
import jax.numpy as jnp
from jax import lax
from jax.experimental import pallas as pl
from jax.experimental.pallas import tpu as pltpu

F32 = jnp.float32
BF16 = jnp.bfloat16
HIGHEST = lax.Precision.HIGHEST

D_MODEL = 2048
GRID_W = 64
SSD_WIDTH = 1024
SSD_HEAD_DIM = 64
SSD_HEADS = 16
SSD_GROUPS = 2
SSD_STATE = 128
SSD_CONV_W = 5
SSD_CHUNK = 128
RWKV_WIDTH = 1024
RWKV_HEAD_DIM = 64
RWKV_HEADS = 16
DECAY_LORA = 64
ICLR_LORA = 64
GATE_LORA = 160
N_EXPERTS = 16
EC_CAPACITY = 2
EXPERT_FF = 1024
N_MOD = 6
NORM_EPS = 1e-6
GN_EPS = 64e-5
DECAY_SCALE = 0.606531
POS_BASE = 10000.0
SEARCH_ITERS = 48

LANES = 128
SUBLANES = 8
GATHER_ROWS = 64
GATHER_EXPERTS = 4

COL_Z = 0
COL_X = 1024
COL_R = 2048
COL_K = 3072
COL_V = 4096
COL_B = 5120
COL_C = 5376
COL_DT = 5632
COL_LORA = 5760
PREP_BLOCK_ELEMS = 256 * 1024
LORA_W = 640
N_PROJ = COL_LORA + LORA_W

VMEM_LIMIT = 56 * 1024 * 1024


def _cparams(sem):
    return pltpu.CompilerParams(dimension_semantics=sem, vmem_limit_bytes=VMEM_LIMIT)


def _silu(x):
    return x * jax.nn.sigmoid(x)


def _ada_kernel(c_ref, w_ref, b_ref, o_ref):
    s = _silu(c_ref[...]).astype(BF16)
    o_ref[...] = jnp.dot(s, w_ref[...].astype(BF16), preferred_element_type=F32) + b_ref[...]


def ada_modulation(cond, w, b):
    m, d = cond.shape
    n = w.shape[1]
    tn = 1024
    return pl.pallas_call(
        _ada_kernel,
        grid=(n // tn,),
        in_specs=[pl.BlockSpec((m, d), lambda j: (0, 0)),
                  pl.BlockSpec((d, tn), lambda j: (0, j)),
                  pl.BlockSpec((1, tn), lambda j: (0, j))],
        out_specs=pl.BlockSpec((m, tn), lambda j: (0, j)),
        out_shape=jax.ShapeDtypeStruct((m, n), F32),
        compiler_params=_cparams(("parallel",)),
        name="ada_modulation",
    )(cond, w, b.reshape(1, n))


def _norm_mod_kernel(has_pos, *refs):
    if has_pos:
        x_ref, pos_ref, g_ref, sc_ref, sh_ref, h_ref, x0_ref = refs
    else:
        x_ref, g_ref, sc_ref, sh_ref, h_ref = refs
    x = x_ref[0]
    if has_pos:
        x = x + pos_ref[...]
        x0_ref[0] = x
    y = x * lax.rsqrt(jnp.mean(x * x, axis=-1, keepdims=True) + NORM_EPS) * g_ref[...]
    h_ref[0] = (y * (1.0 + sc_ref[0]) + sh_ref[0]).astype(BF16)


def norm_modulate(x, pos, g, scale, shift):
    bsz, seq, d = x.shape
    tl = 256
    per_b = scale.shape[0] > 1
    midx = (lambda b, i: (b, 0, 0)) if per_b else (lambda b, i: (0, 0, 0))
    xspec = pl.BlockSpec((1, tl, d), lambda b, i: (b, i, 0))
    in_specs = [xspec]
    args = [x]
    if pos is not None:
        in_specs.append(pl.BlockSpec((tl, d), lambda b, i: (i, 0)))
        args.append(pos)
    in_specs += [pl.BlockSpec((1, d), lambda b, i: (0, 0)),
                 pl.BlockSpec((1, 1, d), midx), pl.BlockSpec((1, 1, d), midx)]
    args += [g.reshape(1, d), scale, shift]
    out_shape = [jax.ShapeDtypeStruct((bsz, seq, d), BF16)]
    out_specs = [xspec]
    if pos is not None:
        out_shape.append(jax.ShapeDtypeStruct((bsz, seq, d), F32))
        out_specs.append(xspec)
    res = pl.pallas_call(
        functools.partial(_norm_mod_kernel, pos is not None),
        grid=(bsz, seq // tl),
        in_specs=in_specs, out_specs=out_specs, out_shape=out_shape,
        compiler_params=_cparams(("parallel", "parallel")),
        name="norm_modulate",
    )(*args)
    return (res[0], res[1]) if pos is not None else (res[0], x)


def _mm_kernel(a_ref, b_ref, o_ref):
    o_ref[...] = jnp.dot(a_ref[...], b_ref[...], preferred_element_type=F32)


def matmul_bf16(a, b, tm, tn):
    m, k = a.shape
    n = b.shape[1]
    return pl.pallas_call(
        _mm_kernel,
        grid=(m // tm, n // tn),
        in_specs=[pl.BlockSpec((tm, k), lambda i, j: (i, 0)),
                  pl.BlockSpec((k, tn), lambda i, j: (0, j))],
        out_specs=pl.BlockSpec((tm, tn), lambda i, j: (i, j)),
        out_shape=jax.ShapeDtypeStruct((m, n), F32),
        compiler_params=_cparams(("parallel", "arbitrary")),
        name="in_proj",
    )(a, b)


def _shifted(u, off):
    n = u.shape[0]
    row = lax.broadcasted_iota(jnp.int32, (n, 1), 0)
    rolled = pltpu.roll(u, (-off) % n, axis=0)
    valid = jnp.logical_and(row + off >= 0, row + off < n)
    return jnp.where(valid, rolled, 0.0)


def _conv_silu(u, w_ref, b_ref, cols):
    pad = SSD_CONV_W // 2
    acc = u * w_ref[pad:pad + 1, cols] + b_ref[:, cols]
    for j in range(SSD_CONV_W):
        if j != pad:
            acc = acc + _shifted(u, j - pad) * w_ref[j:j + 1, cols]
    return _silu(acc)


def _softplus(x):
    return jnp.maximum(x, 0.0) + jnp.log1p(jnp.exp(-jnp.abs(x)))


def _ssd_kernel(nc, zero_init, *refs):
    q = SSD_CHUNK
    (z_ref, x_ref, b_ref, c_ref, dt_ref, cwx, cbx, cwb, cbb, cwc, cbc,
     dtb_ref, alog_ref, d_ref, ng_ref) = refs[:15]
    refs = refs[15:]
    if not zero_init:
        h0f_ref, h0b_ref = refs[:2]
        refs = refs[2:]
    y_ref, hf_ref, hb_ref, xh_s, bm_s, cm_s, dt_s, a_s, at_s, y_s, h_s = refs

    slab = 256
    for cb in range(SSD_WIDTH // slab):
        cols = slice(cb * slab, (cb + 1) * slab)
        xh = _conv_silu(x_ref[0, :, cols], cwx, cbx, cols)
        for c in range(nc):
            rows = slice(c * q, (c + 1) * q)
            xh_s[c, :, cols] = xh[rows]
            y_s[c, :, cols] = xh[rows] * d_ref[:, cols]
    allc = slice(0, SSD_GROUPS * SSD_STATE)
    bm = _conv_silu(b_ref[0], cwb, cbb, allc)
    cm = _conv_silu(c_ref[0], cwc, cbc, allc)
    dt = _softplus(dt_ref[0] + dtb_ref[...])
    a = dt * (-jnp.exp(alog_ref[...]))
    ii = lax.broadcasted_iota(jnp.int32, (q, q), 0)
    jj = lax.broadcasted_iota(jnp.int32, (q, q), 1)
    eye = (ii == jj).astype(F32)
    lower = ii >= jj
    upper = ii <= jj
    first_half = jj < SSD_HEAD_DIM
    for c in range(nc):
        rows = slice(c * q, (c + 1) * q)
        for g in range(SSD_GROUPS):
            gcols = slice(g * SSD_STATE, (g + 1) * SSD_STATE)
            bm_s[c, gcols, :] = bm[rows, gcols].T
        cm_s[c] = cm[rows]
        dt_s[c] = dt[rows]
        a_s[c] = a[rows]
        at_s[c] = lax.dot_general(eye, a[rows], (((1,), (1,)), ((), ())),
                                  precision=HIGHEST, preferred_element_type=F32)

    for d in range(2):
        mask = lower if d == 0 else upper
        tri = mask.astype(F32)
        tri_t = (upper if d == 0 else lower).astype(F32)
        if zero_init:
            h_s[...] = jnp.zeros_like(h_s)
        else:
            h0_ref = h0f_ref if d == 0 else h0b_ref
            for p in range(SSD_HEADS // 2):
                h_s[p] = jnp.concatenate([h0_ref[0, 2 * p], h0_ref[0, 2 * p + 1]], axis=0).T

        def chunk_body(ci, carry, d=d, mask=mask, tri=tri, tri_t=tri_t):
            c = ci if d == 0 else nc - 1 - ci
            a_c = a_s[c]
            cum = jnp.dot(tri, a_c, precision=HIGHEST, preferred_element_type=F32)
            cum_t = jnp.dot(at_s[c], tri_t, precision=HIGHEST, preferred_element_type=F32)
            dt_c = dt_s[c]
            for g in range(SSD_GROUPS):
                gcols = slice(g * SSD_STATE, (g + 1) * SSD_STATE)
                bt = bm_s[c, gcols, :]
                cg = cm_s[c, :, gcols]
                gram = jnp.dot(cg.astype(BF16), bt.astype(BF16), preferred_element_type=F32)
                for pp in range(SSD_HEADS // SSD_GROUPS // 2):
                    p = g * (SSD_HEADS // SSD_GROUPS // 2) + pp
                    pcols = slice(p * LANES, (p + 1) * LANES)
                    xh_pair = xh_s[c, :, pcols]
                    hprev = h_s[p]
                    hprev_b = hprev.astype(BF16)
                    parts = []
                    for e in range(2):
                        col = d * SSD_HEADS + 2 * p + e
                        mine = first_half if e == 0 else jnp.logical_not(first_half)
                        cc = jnp.broadcast_to(cum[:, col:col + 1], (q, q))
                        cr = cum_t[col:col + 1, :]
                        dec = jnp.exp(jnp.where(mask, cc - cr, -jnp.inf))
                        m = (gram * dec).astype(BF16)
                        xdt = jnp.where(mine, xh_pair * dt_c[:, col:col + 1], 0.0).astype(BF16)
                        tot = cum[q - 1:q, col:col + 1] if d == 0 else cum[0:1, col:col + 1]
                        bd_t = (bt * jnp.exp(tot - cr)).astype(BF16)
                        cd = (cg * jnp.exp(cc)).astype(BF16)
                        s_c = jnp.dot(bd_t, xdt, preferred_element_type=F32)
                        y_off = jnp.dot(cd, hprev_b, preferred_element_type=F32)
                        y_diag = jnp.dot(m, xdt, preferred_element_type=F32)
                        parts.append((s_c, y_off, y_diag, jnp.exp(tot)))
                    (s0, yo0, yd0, g0), (s1, yo1, yd1, g1) = parts
                    h_s[p] = hprev * jnp.where(first_half, g0, g1) + (s0 + s1)
                    y_s[c, :, pcols] += (yd0 + yd1) + jnp.where(first_half, yo0, yo1)
            return carry

        lax.fori_loop(0, nc, chunk_body, 0)
        out_ref = hf_ref if d == 0 else hb_ref
        for p in range(SSD_HEADS // 2):
            tr = h_s[p].T
            out_ref[0, 2 * p] = tr[0:SSD_HEAD_DIM]
            out_ref[0, 2 * p + 1] = tr[SSD_HEAD_DIM:]

    gw = SSD_WIDTH // SSD_GROUPS
    for c in range(nc):
        rows = slice(c * q, (c + 1) * q)
        for g in range(SSD_GROUPS):
            cols = slice(g * gw, (g + 1) * gw)
            y = y_s[c, :, cols] * _silu(z_ref[0, rows, cols])
            y = y * lax.rsqrt(jnp.mean(y * y, axis=-1, keepdims=True) + NORM_EPS)
            y_ref[0, rows, cols] = (y * ng_ref[:, cols]).astype(BF16)


def ssd_mixer(proj, p, h0f, h0b):
    bsz, seq, _ = proj.shape
    nc = seq // SSD_CHUNK
    zero_init = h0f is None

    def col_spec(width, start):
        blk = start // width
        return pl.BlockSpec((1, seq, width), lambda b: (b, 0, blk))

    def full2(arr):
        return pl.BlockSpec(arr.shape, lambda b: (0, 0))

    st_spec = pl.BlockSpec((1, SSD_HEADS, SSD_HEAD_DIM, SSD_STATE), lambda b: (b, 0, 0, 0))
    bc = SSD_GROUPS * SSD_STATE
    small = [p["cw_x"], p["cb_x"], p["cw_b"], p["cb_b"], p["cw_c"], p["cb_c"],
             p["dt_bias"], p["a_log"], p["d_exp"], p["ssd_ng"]]
    in_specs = [col_spec(SSD_WIDTH, COL_Z), col_spec(SSD_WIDTH, COL_X), col_spec(bc, COL_B),
                col_spec(bc, COL_C), col_spec(LANES, COL_DT)] + [full2(s) for s in small]
    args = [proj] * 5 + small
    if not zero_init:
        in_specs += [st_spec, st_spec]
        args += [h0f, h0b]
    st_shape = jax.ShapeDtypeStruct((bsz, SSD_HEADS, SSD_HEAD_DIM, SSD_STATE), F32)
    q = SSD_CHUNK
    return pl.pallas_call(
        functools.partial(_ssd_kernel, nc, zero_init),
        grid=(bsz,),
        in_specs=in_specs,
        out_specs=[pl.BlockSpec((1, seq, SSD_WIDTH), lambda b: (b, 0, 0)), st_spec, st_spec],
        out_shape=[jax.ShapeDtypeStruct((bsz, seq, SSD_WIDTH), BF16), st_shape, st_shape],
        scratch_shapes=[pltpu.VMEM((nc, q, SSD_WIDTH), F32), pltpu.VMEM((nc, bc, q), F32),
                        pltpu.VMEM((nc, q, bc), F32), pltpu.VMEM((nc, q, LANES), F32),
                        pltpu.VMEM((nc, q, LANES), F32), pltpu.VMEM((nc, LANES, q), F32),
                        pltpu.VMEM((nc, q, SSD_WIDTH), F32),
                        pltpu.VMEM((SSD_HEADS // 2, SSD_STATE, 2 * SSD_HEAD_DIM), F32)],
        compiler_params=_cparams(("parallel",)),
        name="ssd_mixer",
    )(*args)


def _seg_sum(x):
    lane = lax.broadcasted_iota(jnp.int32, x.shape, 1)
    first = lane < RWKV_HEAD_DIM
    s0 = jnp.sum(jnp.where(first, x, 0.0), axis=-1, keepdims=True)
    s1 = jnp.sum(jnp.where(first, 0.0, x), axis=-1, keepdims=True)
    return jnp.where(first, s0, s1)


def _shift_mix(u, mu):
    return u + mu * (0.5 * (_shifted(u, -1) + _shifted(u, 1)) - u)


def _rwkv_prep_kernel(ncb, rep, *refs):
    (r_ref, k_ref, v_ref, lo_ref, mur, muk, muv, mul, wupf, wupb, aup, gup,
     w0f, w0b, a0, ro, ko, vo, wo, ao, go) = refs[:21]
    staged = dict(zip(("r", "k", "a", "wf", "wb"), refs[21:]))
    q = pl.program_id(2)

    def put(name, cols, val):
        if staged:
            staged[name][:, cols] = val
        elif name in ("wf", "wb"):
            wo[0 if name == "wf" else 1, 0, 0, :, cols] = val
        else:
            {"r": ro, "k": ko, "a": ao}[name][0, 0, :, cols] = val

    def compute():
        lo = _shift_mix(lo_ref[0], mul[...])
        wdf = jnp.tanh(lo[:, 0:128]).astype(BF16)
        wdb = jnp.tanh(lo[:, 128:256]).astype(BF16)
        ad = lo[:, 256:384].astype(BF16)
        gd = jax.nn.sigmoid(lo[:, 384:640]).astype(BF16)
        for cb in range(ncb):
            cols = slice(cb * LANES, (cb + 1) * LANES)

            def mm(x, w_ref, cols=cols):
                return jnp.dot(x, w_ref[:, cols].astype(BF16), preferred_element_type=F32)

            r = _shift_mix(r_ref[0, :, cols], mur[:, cols])
            k = _shift_mix(k_ref[0, :, cols], muk[:, cols])
            v = _shift_mix(v_ref[0, :, cols], muv[:, cols])
            put("wf", cols, jnp.exp(-DECAY_SCALE * jax.nn.sigmoid(w0f[:, cols] + mm(wdf, wupf))))
            put("wb", cols, jnp.exp(-DECAY_SCALE * jax.nn.sigmoid(w0b[:, cols] + mm(wdb, wupb))))
            put("a", cols, jax.nn.sigmoid(a0[:, cols] + mm(ad, aup)))
            go[0, :, cols] = mm(gd, gup)
            put("r", cols, r)
            put("k", cols, k)
            vo[0, :, cols] = v

    if staged:
        pl.when(q == 0)(compute)
        ro[0, 0] = staged["r"][...]
        ko[0, 0] = staged["k"][...]
        ao[0, 0] = staged["a"][...]
        wo[0, 0, 0] = staged["wf"][...]
        wo[1, 0, 0] = staged["wb"][...]
    else:
        compute()


def rwkv_prep(proj, p, rep):
    bsz, seq, _ = proj.shape
    cw = min(RWKV_WIDTH, PREP_BLOCK_ELEMS // seq)
    nblk = RWKV_WIDTH // cw

    def col_spec(start):
        blk = start // cw
        return pl.BlockSpec((1, seq, cw), lambda b, j, q: (b, 0, blk + j))

    vec = pl.BlockSpec((1, cw), lambda b, j, q: (0, j))

    def mat(rows):
        return pl.BlockSpec((rows, cw), lambda b, j, q: (0, j))

    tok_spec = pl.BlockSpec((1, seq, cw), lambda b, j, q: (b, 0, j))
    rep_spec = pl.BlockSpec((1, 1, seq, cw), lambda b, j, q: (q, b, 0, j))
    w_spec = pl.BlockSpec((2, 1, 1, seq, cw), lambda b, j, q: (0, q, b, 0, j))
    tok_sh = jax.ShapeDtypeStruct((bsz, seq, RWKV_WIDTH), F32)
    rep_sh = jax.ShapeDtypeStruct((rep, bsz, seq, RWKV_WIDTH), F32)
    w_sh = jax.ShapeDtypeStruct((2, rep, bsz, seq, RWKV_WIDTH), F32)
    return pl.pallas_call(
        functools.partial(_rwkv_prep_kernel, cw // LANES, rep),
        grid=(bsz, nblk, rep),
        in_specs=[col_spec(COL_R), col_spec(COL_K), col_spec(COL_V),
                  pl.BlockSpec((1, seq, LORA_W), lambda b, j, q: (b, 0, COL_LORA // LORA_W)),
                  vec, vec, vec, pl.BlockSpec((1, LORA_W), lambda b, j, q: (0, 0)),
                  mat(128), mat(128), mat(128), mat(256),
                  vec, vec, vec],
        out_specs=[rep_spec, rep_spec, tok_spec, w_spec, rep_spec, tok_spec],
        out_shape=[rep_sh, rep_sh, tok_sh, w_sh, rep_sh, tok_sh],
        scratch_shapes=[pltpu.VMEM((seq, cw), F32)] * (5 if rep > 1 else 0),
        compiler_params=_cparams(("parallel", "arbitrary", "arbitrary")),
        name="rwkv_prep",
    )(proj, proj, proj, proj, p["mu_r"], p["mu_k"], p["mu_v"], p["mu_lora"],
      p["w_up_f"], p["w_up_b"], p["a_up"], p["g_up"],
      p["w0_f"], p["w0_b"], p["a0"])


def _rwkv_scan_kernel(tb, zero_init, *refs):
    if zero_init:
        r_ref, w_ref, kr_ref, a_ref, v_ref, kk_ref, ka_ref, y_ref, st_ref, s_s, k_ref, al_ref, be_ref = refs
    else:
        (r_ref, w_ref, kr_ref, a_ref, v_ref, kk_ref, ka_ref, s0_ref, y_ref, st_ref,
         s_s, k_ref, al_ref, be_ref) = refs
    kd = RWKV_HEAD_DIM
    d = pl.program_id(0)
    step_i = pl.program_id(2)

    def time_index(i):
        return i + d * (tb - 1 - 2 * i)

    @pl.when(step_i == 0)
    def _():
        if zero_init:
            s_s[...] = jnp.zeros_like(s_s)
        else:
            s_s[...] = s0_ref[...]

    def derive(t, carry):
        kraw = kr_ref[t]
        a = a_ref[t]
        kk = kraw * kk_ref[...]
        kk = kk * (1.0 / jnp.maximum(jnp.sqrt(jnp.sum(kk * kk, axis=0, keepdims=True)), 1e-12))
        al_ref[t] = -kk
        be_ref[t] = kk * a
        k_ref[t] = kraw * (1.0 + (a - 1.0) * ka_ref[...])
        return carry

    lax.fori_loop(0, tb, derive, 0, unroll=4)

    t0 = time_index(0)
    sa0 = jnp.zeros(s_s.shape[1:], F32)
    sa1 = jnp.zeros(s_s.shape[1:], F32)
    for kq in range(0, kd, 2):
        sa0 = sa0 + s_s[kq] * al_ref[t0, kq:kq + 1, :]
        sa1 = sa1 + s_s[kq + 1] * al_ref[t0, kq + 1:kq + 2, :]

    def step(i, sa):
        t = time_index(i)
        tn = time_index(jnp.minimum(i + 1, tb - 1))
        vt = v_ref[t]
        y = jnp.zeros_like(sa)
        sa_next = jnp.zeros_like(sa)
        for kq in range(kd):
            row = slice(kq, kq + 1)
            sk = s_s[kq] * w_ref[t, row, :] + sa * be_ref[t, row, :] + vt * k_ref[t, row, :]
            s_s[kq] = sk
            y = y + sk * r_ref[t, row, :]
            sa_next = sa_next + sk * al_ref[tn, row, :]
        y_ref[t] = y
        return sa_next

    lax.fori_loop(0, tb, step, sa0 + sa1)

    @pl.when(step_i == pl.num_programs(2) - 1)
    def _():
        st_ref[...] = s_s[...]


def rwkv_scan(r, w, k, a, v, kk_tab, ka_tab, s0):
    seq, kd, nch = r.shape
    vv = v.shape[1]
    tb = 32
    nblk = seq // tb
    zero_init = s0 is None

    def tblk(d, i):
        return i + d * (nblk - 1 - 2 * i)

    kspec = pl.BlockSpec((tb, kd, LANES), lambda d, g, i: (tblk(d, i), 0, g))
    wspec = pl.BlockSpec((None, tb, kd, LANES), lambda d, g, i: (d, tblk(d, i), 0, g))
    vspec = pl.BlockSpec((tb, vv, LANES), lambda d, g, i: (tblk(d, i), 0, g))
    yspec = pl.BlockSpec((None, tb, vv, LANES), lambda d, g, i: (d, tblk(d, i), 0, g))
    sspec = pl.BlockSpec((None, kd, vv, LANES), lambda d, g, i: (d, 0, 0, g))
    tabspec = pl.BlockSpec((kd, LANES), lambda d, g, i: (0, 0))
    in_specs = [kspec, wspec, kspec, kspec, vspec, tabspec, tabspec]
    args = [r, w, k, a, v, kk_tab, ka_tab]
    if not zero_init:
        in_specs.append(sspec)
        args.append(s0)
    return pl.pallas_call(
        functools.partial(_rwkv_scan_kernel, tb, zero_init),
        grid=(2, nch // LANES, nblk),
        in_specs=in_specs,
        out_specs=[yspec, sspec],
        out_shape=[jax.ShapeDtypeStruct((2, seq, vv, nch), F32),
                   jax.ShapeDtypeStruct((2, kd, vv, nch), F32)],
        scratch_shapes=[pltpu.VMEM((kd, vv, LANES), F32)] + [pltpu.VMEM((tb, kd, LANES), F32)] * 3,
        compiler_params=_cparams(("parallel", "parallel", "arbitrary")),
        name="rwkv_scan",
    )(*args)


def _rwkv_post_kernel(y_ref, r_ref, k_ref, a_ref, v_ref, g_ref, lng, lnb, rk, ka, o_ref):
    inv = 1.0 / RWKV_HEAD_DIM
    for cb in range(RWKV_WIDTH // LANES):
        cols = slice(cb * LANES, (cb + 1) * LANES)
        y = y_ref[0, 0, :, cols] + y_ref[1, 0, :, cols]
        mean = _seg_sum(y) * inv
        yc = y - mean
        var = _seg_sum(yc * yc) * inv
        yn = yc * lax.rsqrt(var + GN_EPS) * lng[:, cols] + lnb[:, cols]
        k = k_ref[0, :, cols] * (1.0 + (a_ref[0, :, cols] - 1.0) * ka[:, cols])
        bonus = _seg_sum(r_ref[0, :, cols] * k * rk[:, cols]) * v_ref[0, :, cols]
        o_ref[0, :, cols] = ((yn + bonus) * g_ref[0, :, cols]).astype(BF16)


def rwkv_post(y2, r, k, a, v, g, p):
    _, bsz, seq, wd = y2.shape
    tl = 256
    spec = pl.BlockSpec((1, tl, wd), lambda b, i: (b, i, 0))
    vec = pl.BlockSpec((1, wd), lambda b, i: (0, 0))
    return pl.pallas_call(
        _rwkv_post_kernel,
        grid=(bsz, seq // tl),
        in_specs=[pl.BlockSpec((2, 1, tl, wd), lambda b, i: (0, b, i, 0))] + [spec] * 5 + [vec] * 4,
        out_specs=spec,
        out_shape=jax.ShapeDtypeStruct((bsz, seq, wd), BF16),
        compiler_params=_cparams(("parallel", "parallel")),
        name="rwkv_post",
    )(y2, r, k, a, v, g, p["ln_g"], p["ln_b"], p["r_k"], p["k_a"])


def _split_bf16(x):
    hi = x.astype(BF16)
    return hi, (x - hi.astype(F32)).astype(BF16)


def _out_proj_kernel(ys_ref, yr_ref, wo_ref, x_ref, gm_ref, ng_ref, sc_ref, sh_ref, rwh_ref, rwl_ref,
                     x1_ref, h2_ref, lg_ref):
    acc = jnp.dot(ys_ref[0], wo_ref[0:SSD_WIDTH, :], preferred_element_type=F32)
    acc = acc + jnp.dot(yr_ref[0], wo_ref[SSD_WIDTH:, :], preferred_element_type=F32)
    x1 = x_ref[0] + gm_ref[0] * acc
    x1_ref[0] = x1
    hn = x1 * lax.rsqrt(jnp.mean(x1 * x1, axis=-1, keepdims=True) + NORM_EPS) * ng_ref[...]
    h2 = hn * (1.0 + sc_ref[0]) + sh_ref[0]
    hi, lo = _split_bf16(h2)
    h2_ref[0] = hi
    lg_ref[0] = (jnp.dot(hi, rwh_ref[...], preferred_element_type=F32)
                 + jnp.dot(lo, rwh_ref[...], preferred_element_type=F32)
                 + jnp.dot(hi, rwl_ref[...], preferred_element_type=F32))


def out_proj(y_ssd, y_rwkv, w_out, x, gate_m, norm_g, scale_f, shift_f, router_w):
    bsz, seq, d = x.shape
    tm = min(512, seq)
    per_b = gate_m.shape[0] > 1
    midx = (lambda b, i: (b, 0, 0)) if per_b else (lambda b, i: (0, 0, 0))
    half = pl.BlockSpec((1, tm, d // 2), lambda b, i: (b, i, 0))
    full = pl.BlockSpec((1, tm, d), lambda b, i: (b, i, 0))
    mspec = pl.BlockSpec((1, 1, d), midx)
    return pl.pallas_call(
        _out_proj_kernel,
        grid=(bsz, seq // tm),
        in_specs=[half, half, pl.BlockSpec((d, d), lambda b, i: (0, 0)), full, mspec,
                  pl.BlockSpec((1, d), lambda b, i: (0, 0)), mspec, mspec,
                  pl.BlockSpec((d, LANES), lambda b, i: (0, 0)),
                  pl.BlockSpec((d, LANES), lambda b, i: (0, 0))],
        out_specs=[full, full, pl.BlockSpec((1, tm, LANES), lambda b, i: (b, i, 0))],
        out_shape=[jax.ShapeDtypeStruct((bsz, seq, d), F32), jax.ShapeDtypeStruct((bsz, seq, d), BF16),
                   jax.ShapeDtypeStruct((bsz, seq, LANES), F32)],
        compiler_params=_cparams(("parallel", "parallel")),
        name="out_proj",
    )(y_ssd, y_rwkv, w_out, x, gate_m, norm_g.reshape(1, d), scale_f, shift_f, *router_w)


def _route_kernel(nt, cap, lg_ref, selr_ref, posr_ref, selt_ref, post_ref, afft_ref, aff_s, pre_s):
    ne = N_EXPERTS
    lane = lax.broadcasted_iota(jnp.int32, (LANES, LANES), 1)
    sub = lax.broadcasted_iota(jnp.int32, (LANES, LANES), 0)
    upper_incl = (sub <= lane).astype(BF16)
    for i in range(nt):
        lg = jnp.where(lane < ne, lg_ref[i * LANES:(i + 1) * LANES, :], -jnp.inf)
        e = jnp.exp(lg - jnp.max(lg, axis=-1, keepdims=True))
        aff = e / jnp.sum(e, axis=-1, keepdims=True)
        afft_ref[i * LANES:(i + 1) * LANES, :] = aff
        aff_s[i] = aff.T[0:ne, :]
    aff3 = aff_s[...]

    def count(mask):
        s = jnp.sum(mask.astype(F32), axis=0, keepdims=True)
        return jnp.sum(s, axis=2, keepdims=True)

    def search(_, carry):
        lo, hi = carry
        mid = 0.5 * (lo + hi)
        ok = count(aff3 >= mid) >= cap
        return jnp.where(ok, mid, lo), jnp.where(ok, hi, mid)

    lo0 = jnp.zeros((1, ne, 1), F32)
    hi0 = jnp.full((1, ne, 1), 2.0, F32)
    lo, _ = lax.fori_loop(0, SEARCH_ITERS, search, (lo0, hi0))
    cand = jnp.where(aff3 >= lo, aff3, 4.0)
    thr = jnp.min(jnp.min(cand, axis=0, keepdims=True), axis=2, keepdims=True)
    gt = (aff3 > thr).astype(F32)
    eq = (aff3 == thr).astype(F32)
    need = cap - count(aff3 > thr)[0]

    def prefix_excl(m3):
        off = jnp.zeros((ne, 1), F32)
        for i in range(nt):
            inc = jnp.dot(m3[i].astype(BF16), upper_incl, preferred_element_type=F32)
            pre_s[i] = inc - m3[i] + off
            off = off + inc[:, LANES - 1:LANES]
        return pre_s[...]

    sel = jnp.maximum(gt, eq * (prefix_excl(eq) < need[None]).astype(F32))
    pos = prefix_excl(sel)
    selr_ref[...] = sel
    posr_ref[...] = pos
    zpad = jnp.zeros((LANES - ne, LANES), F32)
    for i in range(nt):
        rows = slice(i * LANES, (i + 1) * LANES)
        selt_ref[rows, :] = jnp.concatenate([sel[i], zpad], axis=0).T
        post_ref[rows, :] = jnp.concatenate([pos[i], zpad], axis=0).T


def route(logits, cap):
    n_tok = logits.shape[0]
    nt = n_tok // LANES
    row_sh = jax.ShapeDtypeStruct((nt, N_EXPERTS, LANES), F32)
    tm_sh = jax.ShapeDtypeStruct((n_tok, LANES), F32)
    return pl.pallas_call(
        functools.partial(_route_kernel, nt, cap),
        out_shape=[row_sh, row_sh, tm_sh, tm_sh, tm_sh],
        scratch_shapes=[pltpu.VMEM((nt, N_EXPERTS, LANES), F32), pltpu.VMEM((nt, N_EXPERTS, LANES), F32)],
        compiler_params=pltpu.CompilerParams(vmem_limit_bytes=VMEM_LIMIT),
        name="route",
    )(logits)


def _window_start(starts_ref, tile, e, cap, win, align=SUBLANES):
    ps = starts_ref[tile * N_EXPERTS + e]
    return pl.multiple_of(jnp.minimum((ps // align) * align, cap - win), align)


def _gather_kernel(cap, win, nsub, starts_ref, sel_ref, pos_ref, h_ref, o_ref):
    eg = pl.program_id(0)
    j = pl.program_id(1)

    @pl.when(j == 0)
    def _():
        o_ref[...] = jnp.zeros_like(o_ref)

    for el in range(GATHER_EXPERTS):
        e = eg * GATHER_EXPERTS + el
        for s in range(0, nsub, 2):
            tile = j * nsub + s
            s0 = _window_start(starts_ref, tile, e, cap, win, align=2 * SUBLANES)
            lo = starts_ref[tile * N_EXPERTS + e]
            hi = starts_ref[(tile + 2) * N_EXPERTS + e]
            prows = [pos_ref[s + u, pl.ds(e, 1), :] for u in range(2)]
            srows = [sel_ref[s + u, pl.ds(e, 1), :] for u in range(2)]
            for piece in range(win // GATHER_ROWS):
                p0 = pl.multiple_of(s0 + piece * GATHER_ROWS, 2 * SUBLANES)

                @pl.when(jnp.logical_and(p0 < hi, p0 + GATHER_ROWS > lo))
                def _(p0=p0, s=s, el=el, prows=prows, srows=srows):
                    slot = (lax.broadcasted_iota(jnp.int32, (GATHER_ROWS, LANES), 0) + p0).astype(F32)
                    onehot = jnp.concatenate(
                        [jnp.where(jnp.logical_and(slot == prows[u], srows[u] > 0.0), 1.0, 0.0).astype(BF16)
                         for u in range(2)], axis=1)
                    rows = pl.ds(p0, GATHER_ROWS)
                    got = jnp.dot(onehot, h_ref[s * LANES:(s + 2) * LANES, :], preferred_element_type=F32)
                    o_ref[el, rows, :] = (o_ref[el, rows, :].astype(F32) + got).astype(BF16)


def gather_tokens(starts, sel_r, pos_r, h, cap):
    n_tok, d = h.shape
    tt = min(1024, n_tok)
    nsub = tt // LANES
    win = min(-(-(2 * LANES + 2 * SUBLANES) // GATHER_ROWS) * GATHER_ROWS, cap)
    assert win % GATHER_ROWS == 0 and (cap - win) % (2 * SUBLANES) == 0
    rspec = pl.BlockSpec((nsub, N_EXPERTS, LANES), lambda e, j, st: (j, 0, 0))
    return pl.pallas_call(
        functools.partial(_gather_kernel, cap, win, nsub),
        grid_spec=pltpu.PrefetchScalarGridSpec(
            num_scalar_prefetch=1,
            grid=(N_EXPERTS // GATHER_EXPERTS, n_tok // tt),
            in_specs=[rspec, rspec, pl.BlockSpec((tt, d), lambda e, j, st: (j, 0))],
            out_specs=pl.BlockSpec((GATHER_EXPERTS, cap, d), lambda e, j, st: (e, 0, 0))),
        out_shape=jax.ShapeDtypeStruct((N_EXPERTS, cap, d), BF16),
        compiler_params=_cparams(("parallel", "arbitrary")),
        name="gather_tokens",
    )(starts, sel_r, pos_r, h)


def _ffn_kernel(n_sets, *refs):
    x_refs = refs[:n_sets]
    wg_ref, wu_ref, wd_ref = refs[n_sets:n_sets + 3]
    o_refs = refs[n_sets + 3:2 * n_sets + 3]
    accs = refs[2 * n_sets + 3:]
    f = pl.program_id(1)
    wg = wg_ref[0].astype(BF16)
    wu = wu_ref[0].astype(BF16)
    wd = wd_ref[0].astype(BF16)
    for x_ref, o_ref, acc in zip(x_refs, o_refs, accs):
        @pl.when(f == 0)
        def _():
            acc[...] = jnp.zeros_like(acc)

        x = x_ref[0]
        hg = jnp.dot(x, wg, preferred_element_type=F32)
        hu = jnp.dot(x, wu, preferred_element_type=F32)
        hid = (_silu(hg) * hu).astype(BF16)
        acc[...] += jnp.dot(hid, wd, preferred_element_type=F32)

        @pl.when(f == pl.num_programs(1) - 1)
        def _():
            o_ref[0] = acc[...].astype(BF16)


def expert_ffn(xs_sets, wg, wu, wd):
    ne, _, d = xs_sets[0].shape
    ff = wg.shape[2]
    tf = 256
    xspecs = [pl.BlockSpec((1, xs.shape[1], d), lambda e, f: (e, 0, 0)) for xs in xs_sets]
    return pl.pallas_call(
        functools.partial(_ffn_kernel, len(xs_sets)),
        grid=(ne, ff // tf),
        in_specs=xspecs + [pl.BlockSpec((1, d, tf), lambda e, f: (e, 0, f)),
                           pl.BlockSpec((1, d, tf), lambda e, f: (e, 0, f)),
                           pl.BlockSpec((1, tf, d), lambda e, f: (e, f, 0))],
        out_specs=xspecs,
        out_shape=[jax.ShapeDtypeStruct(xs.shape, BF16) for xs in xs_sets],
        scratch_shapes=[pltpu.VMEM(xs.shape[1:], F32) for xs in xs_sets],
        compiler_params=_cparams(("parallel", "arbitrary")),
        name="expert_ffn",
    )(*xs_sets, wg, wu, wd)


def _scatter_kernel(cap, win, nsub, starts_ref, sel_ref, pos_ref, aff_ref, ob_ref, y_ref):
    c = pl.program_id(0)
    e = pl.program_id(1)

    @pl.when(e == 0)
    def _():
        y_ref[...] = jnp.zeros_like(y_ref)

    mine = lax.broadcasted_iota(jnp.int32, (LANES, LANES), 1) == e
    for s in range(nsub):
        rows = slice(s * LANES, (s + 1) * LANES)

        def col(ref):
            return jnp.sum(jnp.where(mine, ref[rows, :], 0.0), axis=-1, keepdims=True)

        pcol, scol, gcol = col(pos_ref), col(sel_ref), col(aff_ref)
        s0 = _window_start(starts_ref, c * nsub + s, e, cap, win)
        slot = (lax.broadcasted_iota(jnp.int32, (LANES, win), 1) + s0).astype(F32)
        onehot = jnp.where(jnp.logical_and(slot == pcol, scol > 0.0), 1.0, 0.0).astype(BF16)
        y_ref[rows, :] += gcol * jnp.dot(onehot, ob_ref[0, pl.ds(s0, win), :],
                                         preferred_element_type=F32)


def scatter_combine(starts, sel_t, pos_t, aff_t, outbuf):
    n_tok = sel_t.shape[0]
    ne, cap, d = outbuf.shape
    chunk = min(2048, n_tok)
    nsub = chunk // LANES
    win = min(2 * LANES, cap)
    tspec = pl.BlockSpec((chunk, LANES), lambda c, e, st: (c, 0))
    return pl.pallas_call(
        functools.partial(_scatter_kernel, cap, win, nsub),
        grid_spec=pltpu.PrefetchScalarGridSpec(
            num_scalar_prefetch=1,
            grid=(n_tok // chunk, ne),
            in_specs=[tspec, tspec, tspec, pl.BlockSpec((1, cap, d), lambda c, e, st: (e, 0, 0))],
            out_specs=pl.BlockSpec((chunk, d), lambda c, e, st: (c, 0))),
        out_shape=jax.ShapeDtypeStruct((n_tok, d), F32),
        compiler_params=_cparams(("parallel", "arbitrary")),
        name="scatter_combine",
    )(starts, sel_t, pos_t, aff_t, outbuf)


def _final_kernel(x1_ref, y_ref, gf_ref, fg_ref, o_ref):
    x2 = x1_ref[0] + gf_ref[0] * y_ref[0]
    o_ref[0] = x2 * lax.rsqrt(jnp.mean(x2 * x2, axis=-1, keepdims=True) + NORM_EPS) * fg_ref[...]


def final_norm(x1, y, gate_f, final_g):
    bsz, seq, d = x1.shape
    tl = 256
    per_b = gate_f.shape[0] > 1
    midx = (lambda b, i: (b, 0, 0)) if per_b else (lambda b, i: (0, 0, 0))
    xspec = pl.BlockSpec((1, tl, d), lambda b, i: (b, i, 0))
    return pl.pallas_call(
        _final_kernel,
        grid=(bsz, seq // tl),
        in_specs=[xspec, xspec, pl.BlockSpec((1, 1, d), midx), pl.BlockSpec((1, d), lambda b, i: (0, 0))],
        out_specs=xspec,
        out_shape=jax.ShapeDtypeStruct((bsz, seq, d), F32),
        compiler_params=_cparams(("parallel", "parallel")),
        name="final_norm",
    )(x1, y, gate_f, final_g.reshape(1, d))


def _pad_cols(w, width):
    return jnp.pad(w, ((0, 0), (0, width - w.shape[1])))


def _pad_rows(w, rows):
    return jnp.pad(w, ((0, rows - w.shape[0]), (0, 0)))


def _relayout_columns(w):
    n_ssd = 2 * SSD_WIDTH + 2 * SSD_GROUPS * SSD_STATE + 2 * SSD_HEADS
    ssd, rw = w[:, :n_ssd], w[:, n_ssd:]
    o = 3 * RWKV_WIDTH
    zx = 2 * SSD_WIDTH
    bc = zx + 2 * SSD_GROUPS * SSD_STATE
    parts = [ssd[:, :zx], rw[:, :o], ssd[:, zx:bc], _pad_cols(ssd[:, bc:], LANES),
             _pad_cols(rw[:, o:o + DECAY_LORA], LANES),
             _pad_cols(rw[:, o + DECAY_LORA:o + 2 * DECAY_LORA], LANES),
             _pad_cols(rw[:, o + 2 * DECAY_LORA:o + 2 * DECAY_LORA + ICLR_LORA], LANES),
             _pad_cols(rw[:, o + 2 * DECAY_LORA + ICLR_LORA:], 2 * LANES)]
    return jnp.concatenate(parts, axis=1)


def _layer_params(l, w_in, w_out, ssd_conv_w, ssd_conv_b, ssd_A_log, ssd_dt_bias, ssd_D, ssd_norm_g,
                  rwkv_mu, rwkv_w0, rwkv_w_up, rwkv_a0, rwkv_a_up, rwkv_g_up, rwkv_k_k, rwkv_k_a,
                  rwkv_r_k, rwkv_ln_g, rwkv_ln_b, router_w):
    row = lambda a: a.reshape(1, -1)
    n_ssd = 2 * SSD_WIDTH + 2 * SSD_GROUPS * SSD_STATE + 2 * SSD_HEADS
    mu_full = _relayout_columns(jnp.concatenate([jnp.zeros((1, n_ssd), F32), row(rwkv_mu[l])], axis=1))
    cw, cb = ssd_conv_w[l], row(ssd_conv_b[l])
    xe, be = SSD_WIDTH, SSD_WIDTH + SSD_GROUPS * SSD_STATE
    return dict(
        w_in=_relayout_columns(w_in[l]).astype(BF16),
        w_out=w_out[l].astype(BF16),
        cw_x=cw[:, :xe], cb_x=cb[:, :xe], cw_b=cw[:, xe:be], cb_b=cb[:, xe:be],
        cw_c=cw[:, be:], cb_c=cb[:, be:],
        dt_bias=_pad_cols(row(ssd_dt_bias[l]), LANES), a_log=_pad_cols(row(ssd_A_log[l]), LANES),
        d_exp=row(jnp.repeat(ssd_D[l], SSD_HEAD_DIM)), ssd_ng=row(ssd_norm_g[l]),
        mu_r=mu_full[:, COL_R:COL_K], mu_k=mu_full[:, COL_K:COL_V], mu_v=mu_full[:, COL_V:COL_B],
        mu_lora=mu_full[:, COL_LORA:],
        w_up_f=_pad_rows(rwkv_w_up[l, 0], LANES), w_up_b=_pad_rows(rwkv_w_up[l, 1], LANES),
        a_up=_pad_rows(rwkv_a_up[l], LANES), g_up=_pad_rows(rwkv_g_up[l], 2 * LANES),
        w0_f=row(rwkv_w0[l, 0]), w0_b=row(rwkv_w0[l, 1]), a0=row(rwkv_a0[l]),
        k_k=row(rwkv_k_k[l]), k_a=row(rwkv_k_a[l]), r_k=row(rwkv_r_k[l]),
        ln_g=row(rwkv_ln_g[l]), ln_b=row(rwkv_ln_b[l]),
        router_w=tuple(_pad_cols(t, LANES) for t in _split_bf16(router_w[l])),
    )


def _to_chain(x):
    lead = x.shape[:-3]
    bsz, seq, _ = x.shape[-3:]
    n = len(lead)
    x = x.reshape(lead + (bsz, seq, RWKV_HEADS, RWKV_HEAD_DIM))
    perm = tuple(range(n)) + (n + 1, n + 3, n, n + 2)
    return jnp.transpose(x, perm).reshape(lead + (seq, RWKV_HEAD_DIM, bsz * RWKV_HEADS))


def _from_chain(y, bsz):
    lead = y.shape[:-3]
    seq = y.shape[-3]
    n = len(lead)
    y = y.reshape(lead + (seq, RWKV_HEAD_DIM, bsz, RWKV_HEADS))
    perm = tuple(range(n)) + (n + 2, n, n + 3, n + 1)
    return jnp.transpose(y, perm).reshape(lead + (bsz, seq, RWKV_WIDTH))


def _state_to_chain(s):
    return jnp.transpose(s, (0, 4, 3, 1, 2)).reshape(2, RWKV_HEAD_DIM, RWKV_HEAD_DIM, -1)


def _state_from_chain(s, bsz):
    return jnp.transpose(s.reshape(2, RWKV_HEAD_DIM, RWKV_HEAD_DIM, bsz, RWKV_HEADS), (0, 3, 4, 2, 1))


def _scan_replicas(bsz):
    nch = bsz * RWKV_HEADS
    rep = max(1, LANES // nch)
    assert (nch * rep) % LANES == 0
    return rep


def rwkv_mixer_scan(r, k, v, w2, a, s0, p):
    rep, bsz, seq, _ = r.shape
    nh, hd = RWKV_HEADS, RWKV_HEAD_DIM
    fold = lambda t: t.reshape(t.shape[:-4] + (rep * bsz, seq, RWKV_WIDTH))
    rc, kc, ac, wc = (_to_chain(fold(t)) for t in (r, k, a, w2))
    tabs = [jnp.tile(p[n].reshape(nh, hd).T, (1, LANES // nh)) for n in ("k_k", "k_a")]
    if rep == 1:
        s0c = None if s0 is None else _state_to_chain(s0)
        y, st = rwkv_scan(rc, wc, kc, ac, _to_chain(v), *tabs, s0c)
        return _from_chain(y, bsz), _state_from_chain(st, bsz)

    vq = hd // rep
    vc = jnp.transpose(v.reshape(bsz, seq, nh, rep, vq), (1, 4, 3, 0, 2)).reshape(seq, vq, LANES)
    s0c = None
    if s0 is not None:
        s0c = jnp.transpose(s0.reshape(2, bsz, nh, rep, vq, hd), (0, 5, 4, 3, 1, 2)).reshape(
            2, hd, vq, LANES)
    y, st = rwkv_scan(rc, wc, kc, ac, vc, *tabs, s0c)
    y = jnp.transpose(y.reshape(2, seq, vq, rep, bsz, nh), (0, 4, 1, 5, 3, 2)).reshape(
        2, bsz, seq, RWKV_WIDTH)
    st = jnp.transpose(st.reshape(2, hd, vq, rep, bsz, nh), (0, 4, 5, 3, 2, 1)).reshape(
        2, bsz, nh, hd, hd)
    return y, st


def _grid_pos_embed(n_tokens, dim):
    rows = n_tokens // GRID_W
    row = jnp.repeat(jnp.arange(rows, dtype=F32), GRID_W)
    col = jnp.tile(jnp.arange(GRID_W, dtype=F32), rows)
    quarter = dim // 4
    freqs = jnp.exp(jnp.arange(quarter, dtype=F32) * (-math.log(POS_BASE) / quarter))

    def axis_embed(pos):
        ang = pos[:, None] * freqs[None, :]
        return jnp.concatenate([jnp.sin(ang), jnp.cos(ang)], axis=-1)

    return jnp.concatenate([axis_embed(row), axis_embed(col)], axis=-1)


def layer_to_dispatch(x, pos, mod, p, norm_mix_g, norm_ffn_g, states):
    bsz, seq, d = x.shape
    shift_m, scale_m, gate_m, shift_f, scale_f, gate_f = mod
    h, x0 = norm_modulate(x, pos, norm_mix_g, scale_m, shift_m)
    n_tok = bsz * seq
    proj = matmul_bf16(h.reshape(n_tok, d), p["w_in"], min(2048, n_tok), 640).reshape(bsz, seq, N_PROJ)
    h0f, h0b, s0f, s0b = states
    y_ssd, hf, hb = ssd_mixer(proj, p, h0f, h0b)
    r, k, v, w2, a, g = rwkv_prep(proj, p, _scan_replicas(bsz))
    s0 = None if s0f is None else jnp.stack([s0f, s0b], axis=0)
    y2, st = rwkv_mixer_scan(r, k, v, w2, a, s0, p)
    y_rwkv = rwkv_post(y2, r[0], k[0], a[0], v, g, p)
    x1, h2, logits = out_proj(y_ssd, y_rwkv, p["w_out"], x0, gate_m, norm_ffn_g, scale_f, shift_f,
                              p["router_w"])
    cap = EC_CAPACITY * n_tok // N_EXPERTS
    sel_r, pos_r, sel_t, pos_t, aff_t = route(logits.reshape(n_tok, LANES), cap)
    starts = jnp.concatenate([pos_r[:, :, 0], jnp.full((1, N_EXPERTS), cap, F32)], axis=0)
    starts = starts.astype(jnp.int32).reshape(-1)
    xs = gather_tokens(starts, sel_r, pos_r, h2.reshape(n_tok, d), cap)
    routed = dict(xs=xs, starts=starts, sel_t=sel_t, pos_t=pos_t, aff_t=aff_t, x1=x1, gate_f=gate_f)
    return routed, (hf, hb, st[0], st[1])


def combine_and_finish(routed, outbuf, final_g):
    x1 = routed["x1"]
    y_ffn = scatter_combine(routed["starts"], routed["sel_t"], routed["pos_t"], routed["aff_t"], outbuf)
    return final_norm(x1, y_ffn.reshape(x1.shape), routed["gate_f"], final_g)


def kernel(x_prompt, x_sample, state_ssd_fwd, state_ssd_bwd, state_rwkv_fwd, state_rwkv_bwd, c, c_ctx,
           w_ada, b_ada, norm_mix_g, norm_ffn_g, w_in, w_out, ssd_conv_w, ssd_conv_b, ssd_A_log,
           ssd_dt_bias, ssd_D, ssd_norm_g, rwkv_mu, rwkv_w0, rwkv_w_up, rwkv_a0, rwkv_a_up, rwkv_g_up,
           rwkv_k_k, rwkv_k_a, rwkv_r_k, rwkv_ln_g, rwkv_ln_b, router_w, exp_w_gate, exp_w_up,
           exp_w_down, final_norm_g):
    depth = w_in.shape[0]
    assert depth == 1, "the final norm runs right after the single layer's FFN residual"
    l = 0
    dec_b = x_sample.shape[0]
    p = _layer_params(l, w_in, w_out, ssd_conv_w, ssd_conv_b, ssd_A_log, ssd_dt_bias, ssd_D, ssd_norm_g,
                      rwkv_mu, rwkv_w0, rwkv_w_up, rwkv_a0, rwkv_a_up, rwkv_g_up, rwkv_k_k, rwkv_k_a,
                      rwkv_r_k, rwkv_ln_g, rwkv_ln_b, router_w)
    cond = jnp.concatenate([c_ctx[None, :], c, jnp.zeros((16 - 1 - dec_b, D_MODEL), F32)], axis=0)
    mod = ada_modulation(cond, w_ada[l], b_ada[l]).reshape(16, N_MOD, 1, D_MODEL)
    mod_ctx = [mod[0:1, i] for i in range(N_MOD)]
    mod_lat = [mod[1:1 + dec_b, i] for i in range(N_MOD)]

    routed_ctx, st = layer_to_dispatch(x_prompt, None, mod_ctx, p, norm_mix_g[l], norm_ffn_g[l],
                                       (None, None, None, None))
    pos = _grid_pos_embed(x_sample.shape[1], D_MODEL)
    cached = (state_ssd_fwd[:, l], state_ssd_bwd[:, l], state_rwkv_fwd[:, l], state_rwkv_bwd[:, l])
    routed_lat, _ = layer_to_dispatch(x_sample, pos, mod_lat, p, norm_mix_g[l], norm_ffn_g[l], cached)
    out_ctx, out_lat = expert_ffn([routed_ctx["xs"], routed_lat["xs"]],
                                  exp_w_gate[l], exp_w_up[l], exp_w_down[l])
    y_prompt = combine_and_finish(routed_ctx, out_ctx, final_norm_g)
    y_sample = combine_and_finish(routed_lat, out_lat, final_norm_g)
    return (y_prompt, y_sample, st[0][:, None], st[1][:, None], st[2][:, None], st[3][:, None])
```

```python
import functools
import math

import jax
import jax.numpy as jnp
from jax import lax
from jax.experimental import pallas as pl
from jax.experimental.pallas import tpu as pltpu

F32 = jnp.float32
BF16 = jnp.bfloat16
HIGHEST = lax.Precision.HIGHEST

D_MODEL = 2048
GRID_W = 64
SSD_WIDTH = 1024
SSD_HEAD_DIM = 64
SSD_HEADS = 16
SSD_GROUPS = 2
SSD_STATE = 128
SSD_CONV_W = 5
SSD_CHUNK = 128
RWKV_WIDTH = 1024
RWKV_HEAD_DIM = 64
RWKV_HEADS = 16
DECAY_LORA = 64
ICLR_LORA = 64
GATE_LORA = 160
N_EXPERTS = 16
EC_CAPACITY = 2
EXPERT_FF = 1024
N_MOD = 6
NORM_EPS = 1e-6
GN_EPS = 64e-5
DECAY_SCALE = 0.606531
POS_BASE = 10000.0
SEARCH_ITERS = 48

LANES = 128
SUBLANES = 8
GATHER_ROWS = 64
GATHER_EXPERTS = 4

COL_Z = 0
COL_X = 1024
COL_R = 2048
COL_K = 3072
COL_V = 4096
COL_B = 5120
COL_C = 5376
COL_DT = 5632
COL_LORA = 5760
PREP_BLOCK_ELEMS = 256 * 1024
LORA_W = 640
N_PROJ = COL_LORA + LORA_W

VMEM_LIMIT = 56 * 1024 * 1024


def _cparams(sem):
    return pltpu.CompilerParams(dimension_semantics=sem, vmem_limit_bytes=VMEM_LIMIT)


def _silu(x):
    return x * jax.nn.sigmoid(x)


def _ada_kernel(c_ref, w_ref, b_ref, o_ref):
    s = _silu(c_ref[...]).astype(BF16)
    o_ref[...] = jnp.dot(s, w_ref[...].astype(BF16), preferred_element_type=F32) + b_ref[...]


def ada_modulation(cond, w, b):
    m, d = cond.shape
    n = w.shape[1]
    tn = 1024
    return pl.pallas_call(
        _ada_kernel,
        grid=(n // tn,),
        in_specs=[pl.BlockSpec((m, d), lambda j: (0, 0)),
                  pl.BlockSpec((d, tn), lambda j: (0, j)),
                  pl.BlockSpec((1, tn), lambda j: (0, j))],
        out_specs=pl.BlockSpec((m, tn), lambda j: (0, j)),
        out_shape=jax.ShapeDtypeStruct((m, n), F32),
        compiler_params=_cparams(("parallel",)),
        name="ada_modulation",
    )(cond, w, b.reshape(1, n))


def _norm_mod_kernel(has_pos, *refs):
    if has_pos:
        x_ref, pos_ref, g_ref, sc_ref, sh_ref, h_ref, x0_ref = refs
    else:
        x_ref, g_ref, sc_ref, sh_ref, h_ref = refs
    x = x_ref[0]
    if has_pos:
        x = x + pos_ref[...]
        x0_ref[0] = x
    y = x * lax.rsqrt(jnp.mean(x * x, axis=-1, keepdims=True) + NORM_EPS) * g_ref[...]
    h_ref[0] = (y * (1.0 + sc_ref[0]) + sh_ref[0]).astype(BF16)


def norm_modulate(x, pos, g, scale, shift):
    bsz, seq, d = x.shape
    tl = 256
    per_b = scale.shape[0] > 1
    midx = (lambda b, i: (b, 0, 0)) if per_b else (lambda b, i: (0, 0, 0))
    xspec = pl.BlockSpec((1, tl, d), lambda b, i: (b, i, 0))
    in_specs = [xspec]
    args = [x]
    if pos is not None:
        in_specs.append(pl.BlockSpec((tl, d), lambda b, i: (i, 0)))
        args.append(pos)
    in_specs += [pl.BlockSpec((1, d), lambda b, i: (0, 0)),
                 pl.BlockSpec((1, 1, d), midx), pl.BlockSpec((1, 1, d), midx)]
    args += [g.reshape(1, d), scale, shift]
    out_shape = [jax.ShapeDtypeStruct((bsz, seq, d), BF16)]
    out_specs = [xspec]
    if pos is not None:
        out_shape.append(jax.ShapeDtypeStruct((bsz, seq, d), F32))
        out_specs.append(xspec)
    res = pl.pallas_call(
        functools.partial(_norm_mod_kernel, pos is not None),
        grid=(bsz, seq // tl),
        in_specs=in_specs, out_specs=out_specs, out_shape=out_shape,
        compiler_params=_cparams(("parallel", "parallel")),
        name="norm_modulate",
    )(*args)
    return (res[0], res[1]) if pos is not None else (res[0], x)


def _norm_proj_kernel(has_pos, rows, *refs):
    if has_pos:
        x_ref, pos_ref, g_ref, sc_ref, sh_ref, w_ref, o_ref, x0_ref, h_s = refs
    else:
        x_ref, g_ref, sc_ref, sh_ref, w_ref, o_ref, h_s = refs

    @pl.when(pl.program_id(1) == 0)
    def _():
        for c in range(x_ref.shape[0] // rows):
            rs = slice(c * rows, (c + 1) * rows)
            x = x_ref[rs, :]
            if has_pos:
                x = x + pos_ref[rs, :]
                x0_ref[rs, :] = x
            y = x * lax.rsqrt(jnp.mean(x * x, axis=-1, keepdims=True) + NORM_EPS) * g_ref[...]
            h_s[rs, :] = (y * (1.0 + sc_ref[0]) + sh_ref[0]).astype(BF16)

    o_ref[...] = jnp.dot(h_s[...], w_ref[...], preferred_element_type=F32)


def norm_in_proj(x, pos, g, scale, shift, w, tn):
    bsz, seq, d = x.shape
    n_tok = bsz * seq
    n = w.shape[1]
    per_b = scale.shape[0] > 1
    tm = min(512 if pos is not None else 1024, n_tok)
    if per_b or pos is not None:
        tm = min(tm, seq)
    tps = max(seq // tm, 1)
    midx = (lambda i, j: (i // tps, 0, 0)) if per_b else (lambda i, j: (0, 0, 0))
    rspec = pl.BlockSpec((tm, d), lambda i, j: (i, 0))
    in_specs = [rspec]
    args = [x.reshape(n_tok, d)]
    if pos is not None:
        in_specs.append(pl.BlockSpec((tm, d), lambda i, j: (i % tps, 0)))
        args.append(pos)
    in_specs += [pl.BlockSpec((1, d), lambda i, j: (0, 0)), pl.BlockSpec((1, 1, d), midx),
                 pl.BlockSpec((1, 1, d), midx), pl.BlockSpec((d, tn), lambda i, j: (0, j))]
    args += [g.reshape(1, d), scale, shift, w]
    out_specs = [pl.BlockSpec((tm, tn), lambda i, j: (i, j))]
    out_shape = [jax.ShapeDtypeStruct((n_tok, n), F32)]
    if pos is not None:
        out_specs.append(rspec)
        out_shape.append(jax.ShapeDtypeStruct((n_tok, d), F32))
    res = pl.pallas_call(
        functools.partial(_norm_proj_kernel, pos is not None, min(256, tm)),
        grid=(n_tok // tm, n // tn),
        in_specs=in_specs, out_specs=out_specs, out_shape=out_shape,
        scratch_shapes=[pltpu.VMEM((tm, d), BF16)],
        compiler_params=_cparams(("parallel", "arbitrary")),
        name="norm_in_proj",
    )(*args)
    return res[0], (res[1].reshape(bsz, seq, d) if pos is not None else x)


def _mm_kernel(a_ref, b_ref, o_ref):
    o_ref[...] = jnp.dot(a_ref[...], b_ref[...], preferred_element_type=F32)


def matmul_bf16(a, b, tm, tn):
    m, k = a.shape
    n = b.shape[1]
    return pl.pallas_call(
        _mm_kernel,
        grid=(m // tm, n // tn),
        in_specs=[pl.BlockSpec((tm, k), lambda i, j: (i, 0)),
                  pl.BlockSpec((k, tn), lambda i, j: (0, j))],
        out_specs=pl.BlockSpec((tm, tn), lambda i, j: (i, j)),
        out_shape=jax.ShapeDtypeStruct((m, n), F32),
        compiler_params=_cparams(("parallel", "arbitrary")),
        name="in_proj",
    )(a, b)


def _shifted(u, off):
    n = u.shape[0]
    row = lax.broadcasted_iota(jnp.int32, (n, 1), 0)
    rolled = pltpu.roll(u, (-off) % n, axis=0)
    valid = jnp.logical_and(row + off >= 0, row + off < n)
    return jnp.where(valid, rolled, 0.0)


def _conv_silu(u, w_ref, b_ref, cols):
    pad = SSD_CONV_W // 2
    acc = u * w_ref[pad:pad + 1, cols] + b_ref[:, cols]
    for j in range(SSD_CONV_W):
        if j != pad:
            acc = acc + _shifted(u, j - pad) * w_ref[j:j + 1, cols]
    return _silu(acc)


def _softplus(x):
    return jnp.maximum(x, 0.0) + jnp.log1p(jnp.exp(-jnp.abs(x)))


def _ssd_kernel(nc, zero_init, *refs):
    q = SSD_CHUNK
    (z_ref, x_ref, b_ref, c_ref, dt_ref, cwx, cbx, cwb, cbb, cwc, cbc,
     dtb_ref, alog_ref, d_ref, ng_ref) = refs[:15]
    refs = refs[15:]
    if not zero_init:
        h0f_ref, h0b_ref = refs[:2]
        refs = refs[2:]
    y_ref, hf_ref, hb_ref, xh_s, bm_s, cm_s, dt_s, a_s, at_s, y_s, h_s = refs

    slab = 256
    for cb in range(SSD_WIDTH // slab):
        cols = slice(cb * slab, (cb + 1) * slab)
        xh = _conv_silu(x_ref[0, :, cols], cwx, cbx, cols)
        for c in range(nc):
            rows = slice(c * q, (c + 1) * q)
            xh_s[c, :, cols] = xh[rows]
            y_s[c, :, cols] = xh[rows] * d_ref[:, cols]
    allc = slice(0, SSD_GROUPS * SSD_STATE)
    bm = _conv_silu(b_ref[0], cwb, cbb, allc)
    cm = _conv_silu(c_ref[0], cwc, cbc, allc)
    dt = _softplus(dt_ref[0] + dtb_ref[...])
    a = dt * (-jnp.exp(alog_ref[...]))
    ii = lax.broadcasted_iota(jnp.int32, (q, q), 0)
    jj = lax.broadcasted_iota(jnp.int32, (q, q), 1)
    eye = (ii == jj).astype(F32)
    lower = ii >= jj
    upper = ii <= jj
    first_half = jj < SSD_HEAD_DIM
    for c in range(nc):
        rows = slice(c * q, (c + 1) * q)
        for g in range(SSD_GROUPS):
            gcols = slice(g * SSD_STATE, (g + 1) * SSD_STATE)
            bm_s[c, gcols, :] = bm[rows, gcols].T
        cm_s[c] = cm[rows]
        dt_s[c] = dt[rows]
        a_s[c] = a[rows]
        at_s[c] = lax.dot_general(eye, a[rows], (((1,), (1,)), ((), ())),
                                  precision=HIGHEST, preferred_element_type=F32)

    for d in range(2):
        mask = lower if d == 0 else upper
        tri = mask.astype(F32)
        tri_t = (upper if d == 0 else lower).astype(F32)
        if zero_init:
            h_s[...] = jnp.zeros_like(h_s)
        else:
            h0_ref = h0f_ref if d == 0 else h0b_ref
            for p in range(SSD_HEADS // 2):
                h_s[p] = jnp.concatenate([h0_ref[0, 2 * p], h0_ref[0, 2 * p + 1]], axis=0).T

        def chunk_body(ci, carry, d=d, mask=mask, tri=tri, tri_t=tri_t):
            c = ci if d == 0 else nc - 1 - ci
            a_c = a_s[c]
            cum = jnp.dot(tri, a_c, precision=HIGHEST, preferred_element_type=F32)
            cum_t = jnp.dot(at_s[c], tri_t, precision=HIGHEST, preferred_element_type=F32)
            dt_c = dt_s[c]
            for g in range(SSD_GROUPS):
                gcols = slice(g * SSD_STATE, (g + 1) * SSD_STATE)
                bt = bm_s[c, gcols, :]
                cg = cm_s[c, :, gcols]
                gram = jnp.dot(cg.astype(BF16), bt.astype(BF16), preferred_element_type=F32)
                for pp in range(SSD_HEADS // SSD_GROUPS // 2):
                    p = g * (SSD_HEADS // SSD_GROUPS // 2) + pp
                    pcols = slice(p * LANES, (p + 1) * LANES)
                    xh_pair = xh_s[c, :, pcols]
                    hprev = h_s[p]
                    hprev_b = hprev.astype(BF16)
                    parts = []
                    for e in range(2):
                        col = d * SSD_HEADS + 2 * p + e
                        mine = first_half if e == 0 else jnp.logical_not(first_half)
                        cc = jnp.broadcast_to(cum[:, col:col + 1], (q, q))
                        cr = cum_t[col:col + 1, :]
                        dec = jnp.exp(jnp.where(mask, cc - cr, -jnp.inf))
                        m = (gram * dec).astype(BF16)
                        xdt = jnp.where(mine, xh_pair * dt_c[:, col:col + 1], 0.0).astype(BF16)
                        tot = cum[q - 1:q, col:col + 1] if d == 0 else cum[0:1, col:col + 1]
                        bd_t = (bt * jnp.exp(tot - cr)).astype(BF16)
                        cd = (cg * jnp.exp(cc)).astype(BF16)
                        s_c = jnp.dot(bd_t, xdt, preferred_element_type=F32)
                        y_off = jnp.dot(cd, hprev_b, preferred_element_type=F32)
                        y_diag = jnp.dot(m, xdt, preferred_element_type=F32)
                        parts.append((s_c, y_off, y_diag, jnp.exp(tot)))
                    (s0, yo0, yd0, g0), (s1, yo1, yd1, g1) = parts
                    h_s[p] = hprev * jnp.where(first_half, g0, g1) + (s0 + s1)
                    y_s[c, :, pcols] += (yd0 + yd1) + jnp.where(first_half, yo0, yo1)
            return carry

        lax.fori_loop(0, nc, chunk_body, 0)
        out_ref = hf_ref if d == 0 else hb_ref
        for p in range(SSD_HEADS // 2):
            tr = h_s[p].T
            out_ref[0, 2 * p] = tr[0:SSD_HEAD_DIM]
            out_ref[0, 2 * p + 1] = tr[SSD_HEAD_DIM:]

    gw = SSD_WIDTH // SSD_GROUPS
    for c in range(nc):
        rows = slice(c * q, (c + 1) * q)
        for g in range(SSD_GROUPS):
            cols = slice(g * gw, (g + 1) * gw)
            y = y_s[c, :, cols] * _silu(z_ref[0, rows, cols])
            y = y * lax.rsqrt(jnp.mean(y * y, axis=-1, keepdims=True) + NORM_EPS)
            y_ref[0, rows, cols] = (y * ng_ref[:, cols]).astype(BF16)


def ssd_mixer(proj, p, h0f, h0b):
    bsz, seq, _ = proj.shape
    nc = seq // SSD_CHUNK
    zero_init = h0f is None

    def col_spec(width, start):
        blk = start // width
        return pl.BlockSpec((1, seq, width), lambda b: (b, 0, blk))

    def full2(arr):
        return pl.BlockSpec(arr.shape, lambda b: (0, 0))

    st_spec = pl.BlockSpec((1, SSD_HEADS, SSD_HEAD_DIM, SSD_STATE), lambda b: (b, 0, 0, 0))
    bc = SSD_GROUPS * SSD_STATE
    small = [p["cw_x"], p["cb_x"], p["cw_b"], p["cb_b"], p["cw_c"], p["cb_c"],
             p["dt_bias"], p["a_log"], p["d_exp"], p["ssd_ng"]]
    in_specs = [col_spec(SSD_WIDTH, COL_Z), col_spec(SSD_WIDTH, COL_X), col_spec(bc, COL_B),
                col_spec(bc, COL_C), col_spec(LANES, COL_DT)] + [full2(s) for s in small]
    args = [proj] * 5 + small
    if not zero_init:
        in_specs += [st_spec, st_spec]
        args += [h0f, h0b]
    st_shape = jax.ShapeDtypeStruct((bsz, SSD_HEADS, SSD_HEAD_DIM, SSD_STATE), F32)
    q = SSD_CHUNK
    return pl.pallas_call(
        functools.partial(_ssd_kernel, nc, zero_init),
        grid=(bsz,),
        in_specs=in_specs,
        out_specs=[pl.BlockSpec((1, seq, SSD_WIDTH), lambda b: (b, 0, 0)), st_spec, st_spec],
        out_shape=[jax.ShapeDtypeStruct((bsz, seq, SSD_WIDTH), BF16), st_shape, st_shape],
        scratch_shapes=[pltpu.VMEM((nc, q, SSD_WIDTH), F32), pltpu.VMEM((nc, bc, q), F32),
                        pltpu.VMEM((nc, q, bc), F32), pltpu.VMEM((nc, q, LANES), F32),
                        pltpu.VMEM((nc, q, LANES), F32), pltpu.VMEM((nc, LANES, q), F32),
                        pltpu.VMEM((nc, q, SSD_WIDTH), F32),
                        pltpu.VMEM((SSD_HEADS // 2, SSD_STATE, 2 * SSD_HEAD_DIM), F32)],
        compiler_params=_cparams(("parallel",)),
        name="ssd_mixer",
    )(*args)


def _seg_sum(x):
    lane = lax.broadcasted_iota(jnp.int32, x.shape, 1)
    first = lane < RWKV_HEAD_DIM
    s0 = jnp.sum(jnp.where(first, x, 0.0), axis=-1, keepdims=True)
    s1 = jnp.sum(jnp.where(first, 0.0, x), axis=-1, keepdims=True)
    return jnp.where(first, s0, s1)


def _shift_mix(u, mu):
    return u + mu * (0.5 * (_shifted(u, -1) + _shifted(u, 1)) - u)


def _rwkv_prep_kernel(ncb, rep, *refs):
    (r_ref, k_ref, v_ref, lo_ref, mur, muk, muv, mul, wupf, wupb, aup, gup,
     w0f, w0b, a0, ro, ko, vo, wo, ao, go) = refs[:21]
    staged = dict(zip(("r", "k", "a", "wf", "wb"), refs[21:]))
    q = pl.program_id(2)

    def put(name, cols, val):
        if staged:
            staged[name][:, cols] = val
        elif name in ("wf", "wb"):
            wo[0 if name == "wf" else 1, 0, 0, :, cols] = val
        else:
            {"r": ro, "k": ko, "a": ao}[name][0, 0, :, cols] = val

    def compute():
        lo = _shift_mix(lo_ref[0], mul[...])
        wdf = jnp.tanh(lo[:, 0:128]).astype(BF16)
        wdb = jnp.tanh(lo[:, 128:256]).astype(BF16)
        ad = lo[:, 256:384].astype(BF16)
        gd = jax.nn.sigmoid(lo[:, 384:640]).astype(BF16)
        for cb in range(ncb):
            cols = slice(cb * LANES, (cb + 1) * LANES)

            def mm(x, w_ref, cols=cols):
                return jnp.dot(x, w_ref[:, cols].astype(BF16), preferred_element_type=F32)

            r = _shift_mix(r_ref[0, :, cols], mur[:, cols])
            k = _shift_mix(k_ref[0, :, cols], muk[:, cols])
            v = _shift_mix(v_ref[0, :, cols], muv[:, cols])
            put("wf", cols, jnp.exp(-DECAY_SCALE * jax.nn.sigmoid(w0f[:, cols] + mm(wdf, wupf))))
            put("wb", cols, jnp.exp(-DECAY_SCALE * jax.nn.sigmoid(w0b[:, cols] + mm(wdb, wupb))))
            put("a", cols, jax.nn.sigmoid(a0[:, cols] + mm(ad, aup)))
            go[0, :, cols] = mm(gd, gup)
            put("r", cols, r)
            put("k", cols, k)
            vo[0, :, cols] = v

    if staged:
        pl.when(q == 0)(compute)
        ro[0, 0] = staged["r"][...]
        ko[0, 0] = staged["k"][...]
        ao[0, 0] = staged["a"][...]
        wo[0, 0, 0] = staged["wf"][...]
        wo[1, 0, 0] = staged["wb"][...]
    else:
        compute()


def rwkv_prep(proj, p, rep):
    bsz, seq, _ = proj.shape
    cw = min(RWKV_WIDTH, PREP_BLOCK_ELEMS // seq)
    nblk = RWKV_WIDTH // cw

    def col_spec(start):
        blk = start // cw
        return pl.BlockSpec((1, seq, cw), lambda b, j, q: (b, 0, blk + j))

    vec = pl.BlockSpec((1, cw), lambda b, j, q: (0, j))

    def mat(rows):
        return pl.BlockSpec((rows, cw), lambda b, j, q: (0, j))

    tok_spec = pl.BlockSpec((1, seq, cw), lambda b, j, q: (b, 0, j))
    rep_spec = pl.BlockSpec((1, 1, seq, cw), lambda b, j, q: (q, b, 0, j))
    w_spec = pl.BlockSpec((2, 1, 1, seq, cw), lambda b, j, q: (0, q, b, 0, j))
    tok_sh = jax.ShapeDtypeStruct((bsz, seq, RWKV_WIDTH), F32)
    rep_sh = jax.ShapeDtypeStruct((rep, bsz, seq, RWKV_WIDTH), F32)
    w_sh = jax.ShapeDtypeStruct((2, rep, bsz, seq, RWKV_WIDTH), F32)
    return pl.pallas_call(
        functools.partial(_rwkv_prep_kernel, cw // LANES, rep),
        grid=(bsz, nblk, rep),
        in_specs=[col_spec(COL_R), col_spec(COL_K), col_spec(COL_V),
                  pl.BlockSpec((1, seq, LORA_W), lambda b, j, q: (b, 0, COL_LORA // LORA_W)),
                  vec, vec, vec, pl.BlockSpec((1, LORA_W), lambda b, j, q: (0, 0)),
                  mat(128), mat(128), mat(128), mat(256),
                  vec, vec, vec],
        out_specs=[rep_spec, rep_spec, tok_spec, w_spec, rep_spec, tok_spec],
        out_shape=[rep_sh, rep_sh, tok_sh, w_sh, rep_sh, tok_sh],
        scratch_shapes=[pltpu.VMEM((seq, cw), F32)] * (5 if rep > 1 else 0),
        compiler_params=_cparams(("parallel", "arbitrary", "arbitrary")),
        name="rwkv_prep",
    )(proj, proj, proj, proj, p["mu_r"], p["mu_k"], p["mu_v"], p["mu_lora"],
      p["w_up_f"], p["w_up_b"], p["a_up"], p["g_up"],
      p["w0_f"], p["w0_b"], p["a0"])


def _rwkv_scan_kernel(tb, zero_init, *refs):
    if zero_init:
        r_ref, w_ref, kr_ref, a_ref, v_ref, kk_ref, ka_ref, y_ref, st_ref, s_s, k_ref, al_ref, be_ref = refs
    else:
        (r_ref, w_ref, kr_ref, a_ref, v_ref, kk_ref, ka_ref, s0_ref, y_ref, st_ref,
         s_s, k_ref, al_ref, be_ref) = refs
    kd = RWKV_HEAD_DIM
    d = pl.program_id(0)
    step_i = pl.program_id(2)

    def time_index(i):
        return i + d * (tb - 1 - 2 * i)

    @pl.when(step_i == 0)
    def _():
        if zero_init:
            s_s[...] = jnp.zeros_like(s_s)
        else:
            s_s[...] = s0_ref[...]

    def derive(t, carry):
        kraw = kr_ref[t]
        a = a_ref[t]
        kk = kraw * kk_ref[...]
        kk = kk * (1.0 / jnp.maximum(jnp.sqrt(jnp.sum(kk * kk, axis=0, keepdims=True)), 1e-12))
        al_ref[t] = -kk
        be_ref[t] = kk * a
        k_ref[t] = kraw * (1.0 + (a - 1.0) * ka_ref[...])
        return carry

    lax.fori_loop(0, tb, derive, 0, unroll=4)

    t0 = time_index(0)
    sa0 = jnp.zeros(s_s.shape[1:], F32)
    sa1 = jnp.zeros(s_s.shape[1:], F32)
    for kq in range(0, kd, 2):
        sa0 = sa0 + s_s[kq] * al_ref[t0, kq:kq + 1, :]
        sa1 = sa1 + s_s[kq + 1] * al_ref[t0, kq + 1:kq + 2, :]

    def step(i, sa):
        t = time_index(i)
        tn = time_index(jnp.minimum(i + 1, tb - 1))
        vt = v_ref[t]
        y = jnp.zeros_like(sa)
        sa_next = jnp.zeros_like(sa)
        for kq in range(kd):
            row = slice(kq, kq + 1)
            sk = s_s[kq] * w_ref[t, row, :] + sa * be_ref[t, row, :] + vt * k_ref[t, row, :]
            s_s[kq] = sk
            y = y + sk * r_ref[t, row, :]
            sa_next = sa_next + sk * al_ref[tn, row, :]
        y_ref[t] = y
        return sa_next

    lax.fori_loop(0, tb, step, sa0 + sa1)

    @pl.when(step_i == pl.num_programs(2) - 1)
    def _():
        st_ref[...] = s_s[...]


def rwkv_scan(r, w, k, a, v, kk_tab, ka_tab, s0):
    seq, kd, nch = r.shape
    vv = v.shape[1]
    tb = 32
    nblk = seq // tb
    zero_init = s0 is None

    def tblk(d, i):
        return i + d * (nblk - 1 - 2 * i)

    kspec = pl.BlockSpec((tb, kd, LANES), lambda d, g, i: (tblk(d, i), 0, g))
    wspec = pl.BlockSpec((None, tb, kd, LANES), lambda d, g, i: (d, tblk(d, i), 0, g))
    vspec = pl.BlockSpec((tb, vv, LANES), lambda d, g, i: (tblk(d, i), 0, g))
    yspec = pl.BlockSpec((None, tb, vv, LANES), lambda d, g, i: (d, tblk(d, i), 0, g))
    sspec = pl.BlockSpec((None, kd, vv, LANES), lambda d, g, i: (d, 0, 0, g))
    tabspec = pl.BlockSpec((kd, LANES), lambda d, g, i: (0, 0))
    in_specs = [kspec, wspec, kspec, kspec, vspec, tabspec, tabspec]
    args = [r, w, k, a, v, kk_tab, ka_tab]
    if not zero_init:
        in_specs.append(sspec)
        args.append(s0)
    return pl.pallas_call(
        functools.partial(_rwkv_scan_kernel, tb, zero_init),
        grid=(2, nch // LANES, nblk),
        in_specs=in_specs,
        out_specs=[yspec, sspec],
        out_shape=[jax.ShapeDtypeStruct((2, seq, vv, nch), F32),
                   jax.ShapeDtypeStruct((2, kd, vv, nch), F32)],
        scratch_shapes=[pltpu.VMEM((kd, vv, LANES), F32)] + [pltpu.VMEM((tb, kd, LANES), F32)] * 3,
        compiler_params=_cparams(("parallel", "parallel", "arbitrary")),
        name="rwkv_scan",
    )(*args)


def _rwkv_post_kernel(y_ref, r_ref, k_ref, a_ref, v_ref, g_ref, lng, lnb, rk, ka, o_ref):
    inv = 1.0 / RWKV_HEAD_DIM
    for cb in range(RWKV_WIDTH // LANES):
        cols = slice(cb * LANES, (cb + 1) * LANES)
        y = y_ref[0, 0, :, cols] + y_ref[1, 0, :, cols]
        mean = _seg_sum(y) * inv
        yc = y - mean
        var = _seg_sum(yc * yc) * inv
        yn = yc * lax.rsqrt(var + GN_EPS) * lng[:, cols] + lnb[:, cols]
        k = k_ref[0, :, cols] * (1.0 + (a_ref[0, :, cols] - 1.0) * ka[:, cols])
        bonus = _seg_sum(r_ref[0, :, cols] * k * rk[:, cols]) * v_ref[0, :, cols]
        o_ref[0, :, cols] = ((yn + bonus) * g_ref[0, :, cols]).astype(BF16)


def rwkv_post(y2, r, k, a, v, g, p):
    _, bsz, seq, wd = y2.shape
    tl = 256
    spec = pl.BlockSpec((1, tl, wd), lambda b, i: (b, i, 0))
    vec = pl.BlockSpec((1, wd), lambda b, i: (0, 0))
    return pl.pallas_call(
        _rwkv_post_kernel,
        grid=(bsz, seq // tl),
        in_specs=[pl.BlockSpec((2, 1, tl, wd), lambda b, i: (0, b, i, 0))] + [spec] * 5 + [vec] * 4,
        out_specs=spec,
        out_shape=jax.ShapeDtypeStruct((bsz, seq, wd), BF16),
        compiler_params=_cparams(("parallel", "parallel")),
        name="rwkv_post",
    )(y2, r, k, a, v, g, p["ln_g"], p["ln_b"], p["r_k"], p["k_a"])


def _split_bf16(x):
    hi = x.astype(BF16)
    return hi, (x - hi.astype(F32)).astype(BF16)


def _out_proj_kernel(ys_ref, yr_ref, wo_ref, x_ref, gm_ref, ng_ref, sc_ref, sh_ref, rwh_ref, rwl_ref,
                     x1_ref, h2_ref, lg_ref):
    acc = jnp.dot(ys_ref[0], wo_ref[0:SSD_WIDTH, :], preferred_element_type=F32)
    acc = acc + jnp.dot(yr_ref[0], wo_ref[SSD_WIDTH:, :], preferred_element_type=F32)
    x1 = x_ref[0] + gm_ref[0] * acc
    x1_ref[0] = x1
    hn = x1 * lax.rsqrt(jnp.mean(x1 * x1, axis=-1, keepdims=True) + NORM_EPS) * ng_ref[...]
    h2 = hn * (1.0 + sc_ref[0]) + sh_ref[0]
    hi, lo = _split_bf16(h2)
    h2_ref[0] = hi
    lg_ref[0] = (jnp.dot(hi, rwh_ref[...], preferred_element_type=F32)
                 + jnp.dot(lo, rwh_ref[...], preferred_element_type=F32)
                 + jnp.dot(hi, rwl_ref[...], preferred_element_type=F32))


def out_proj(y_ssd, y_rwkv, w_out, x, gate_m, norm_g, scale_f, shift_f, router_w):
    bsz, seq, d = x.shape
    tm = min(512, seq)
    per_b = gate_m.shape[0] > 1
    midx = (lambda b, i: (b, 0, 0)) if per_b else (lambda b, i: (0, 0, 0))
    half = pl.BlockSpec((1, tm, d // 2), lambda b, i: (b, i, 0))
    full = pl.BlockSpec((1, tm, d), lambda b, i: (b, i, 0))
    mspec = pl.BlockSpec((1, 1, d), midx)
    return pl.pallas_call(
        _out_proj_kernel,
        grid=(bsz, seq // tm),
        in_specs=[half, half, pl.BlockSpec((d, d), lambda b, i: (0, 0)), full, mspec,
                  pl.BlockSpec((1, d), lambda b, i: (0, 0)), mspec, mspec,
                  pl.BlockSpec((d, LANES), lambda b, i: (0, 0)),
                  pl.BlockSpec((d, LANES), lambda b, i: (0, 0))],
        out_specs=[full, full, pl.BlockSpec((1, tm, LANES), lambda b, i: (b, i, 0))],
        out_shape=[jax.ShapeDtypeStruct((bsz, seq, d), F32), jax.ShapeDtypeStruct((bsz, seq, d), BF16),
                   jax.ShapeDtypeStruct((bsz, seq, LANES), F32)],
        compiler_params=_cparams(("parallel", "parallel")),
        name="out_proj",
    )(y_ssd, y_rwkv, w_out, x, gate_m, norm_g.reshape(1, d), scale_f, shift_f, *router_w)


def _route_kernel(nt, cap, lg_ref, selr_ref, posr_ref, selt_ref, post_ref, afft_ref, aff_s, pre_s):
    ne = N_EXPERTS
    lane = lax.broadcasted_iota(jnp.int32, (LANES, LANES), 1)
    sub = lax.broadcasted_iota(jnp.int32, (LANES, LANES), 0)
    upper_incl = (sub <= lane).astype(BF16)
    for i in range(nt):
        lg = jnp.where(lane < ne, lg_ref[i * LANES:(i + 1) * LANES, :], -jnp.inf)
        e = jnp.exp(lg - jnp.max(lg, axis=-1, keepdims=True))
        aff = e / jnp.sum(e, axis=-1, keepdims=True)
        afft_ref[i * LANES:(i + 1) * LANES, :] = aff
        aff_s[i] = aff.T[0:ne, :]
    aff3 = aff_s[...]

    def count(mask):
        s = jnp.sum(mask.astype(F32), axis=0, keepdims=True)
        return jnp.sum(s, axis=2, keepdims=True)

    def search(_, carry):
        lo, hi = carry
        mid = 0.5 * (lo + hi)
        ok = count(aff3 >= mid) >= cap
        return jnp.where(ok, mid, lo), jnp.where(ok, hi, mid)

    lo0 = jnp.zeros((1, ne, 1), F32)
    hi0 = jnp.full((1, ne, 1), 2.0, F32)
    lo, _ = lax.fori_loop(0, SEARCH_ITERS, search, (lo0, hi0))
    cand = jnp.where(aff3 >= lo, aff3, 4.0)
    thr = jnp.min(jnp.min(cand, axis=0, keepdims=True), axis=2, keepdims=True)
    gt = (aff3 > thr).astype(F32)
    eq = (aff3 == thr).astype(F32)
    need = cap - count(aff3 > thr)[0]

    def prefix_excl(m3):
        off = jnp.zeros((ne, 1), F32)
        for i in range(nt):
            inc = jnp.dot(m3[i].astype(BF16), upper_incl, preferred_element_type=F32)
            pre_s[i] = inc - m3[i] + off
            off = off + inc[:, LANES - 1:LANES]
        return pre_s[...]

    sel = jnp.maximum(gt, eq * (prefix_excl(eq) < need[None]).astype(F32))
    pos = prefix_excl(sel)
    selr_ref[...] = sel
    posr_ref[...] = pos
    zpad = jnp.zeros((LANES - ne, LANES), F32)
    for i in range(nt):
        rows = slice(i * LANES, (i + 1) * LANES)
        selt_ref[rows, :] = jnp.concatenate([sel[i], zpad], axis=0).T
        post_ref[rows, :] = jnp.concatenate([pos[i], zpad], axis=0).T


def route(logits, cap):
    n_tok = logits.shape[0]
    nt = n_tok // LANES
    row_sh = jax.ShapeDtypeStruct((nt, N_EXPERTS, LANES), F32)
    tm_sh = jax.ShapeDtypeStruct((n_tok, LANES), F32)
    return pl.pallas_call(
        functools.partial(_route_kernel, nt, cap),
        out_shape=[row_sh, row_sh, tm_sh, tm_sh, tm_sh],
        scratch_shapes=[pltpu.VMEM((nt, N_EXPERTS, LANES), F32), pltpu.VMEM((nt, N_EXPERTS, LANES), F32)],
        compiler_params=pltpu.CompilerParams(vmem_limit_bytes=VMEM_LIMIT),
        name="route",
    )(logits)


def _window_start(starts_ref, tile, e, cap, win, align=SUBLANES):
    ps = starts_ref[tile * N_EXPERTS + e]
    return pl.multiple_of(jnp.minimum((ps // align) * align, cap - win), align)


def _gather_kernel(cap, win, nsub, starts_ref, sel_ref, pos_ref, h_ref, o_ref):
    eg = pl.program_id(0)
    j = pl.program_id(1)

    @pl.when(j == 0)
    def _():
        o_ref[...] = jnp.zeros_like(o_ref)

    for el in range(GATHER_EXPERTS):
        e = eg * GATHER_EXPERTS + el
        for s in range(0, nsub, 2):
            tile = j * nsub + s
            s0 = _window_start(starts_ref, tile, e, cap, win, align=2 * SUBLANES)
            lo = starts_ref[tile * N_EXPERTS + e]
            hi = starts_ref[(tile + 2) * N_EXPERTS + e]
            prows = [pos_ref[s + u, pl.ds(e, 1), :] for u in range(2)]
            srows = [sel_ref[s + u, pl.ds(e, 1), :] for u in range(2)]
            for piece in range(win // GATHER_ROWS):
                p0 = pl.multiple_of(s0 + piece * GATHER_ROWS, 2 * SUBLANES)

                @pl.when(jnp.logical_and(p0 < hi, p0 + GATHER_ROWS > lo))
                def _(p0=p0, s=s, el=el, prows=prows, srows=srows):
                    slot = (lax.broadcasted_iota(jnp.int32, (GATHER_ROWS, LANES), 0) + p0).astype(F32)
                    onehot = jnp.concatenate(
                        [jnp.where(jnp.logical_and(slot == prows[u], srows[u] > 0.0), 1.0, 0.0).astype(BF16)
                         for u in range(2)], axis=1)
                    rows = pl.ds(p0, GATHER_ROWS)
                    got = jnp.dot(onehot, h_ref[s * LANES:(s + 2) * LANES, :], preferred_element_type=F32)
                    o_ref[el, rows, :] = (o_ref[el, rows, :].astype(F32) + got).astype(BF16)


def gather_tokens(starts, sel_r, pos_r, h, cap):
    n_tok, d = h.shape
    tt = min(1024, n_tok)
    nsub = tt // LANES
    win = min(-(-(2 * LANES + 2 * SUBLANES) // GATHER_ROWS) * GATHER_ROWS, cap)
    assert win % GATHER_ROWS == 0 and (cap - win) % (2 * SUBLANES) == 0
    rspec = pl.BlockSpec((nsub, N_EXPERTS, LANES), lambda e, j, st: (j, 0, 0))
    return pl.pallas_call(
        functools.partial(_gather_kernel, cap, win, nsub),
        grid_spec=pltpu.PrefetchScalarGridSpec(
            num_scalar_prefetch=1,
            grid=(N_EXPERTS // GATHER_EXPERTS, n_tok // tt),
            in_specs=[rspec, rspec, pl.BlockSpec((tt, d), lambda e, j, st: (j, 0))],
            out_specs=pl.BlockSpec((GATHER_EXPERTS, cap, d), lambda e, j, st: (e, 0, 0))),
        out_shape=jax.ShapeDtypeStruct((N_EXPERTS, cap, d), BF16),
        compiler_params=_cparams(("parallel", "arbitrary")),
        name="gather_tokens",
    )(starts, sel_r, pos_r, h)


def _ffn_kernel(n_sets, *refs):
    x_refs = refs[:n_sets]
    wg_ref, wu_ref, wd_ref = refs[n_sets:n_sets + 3]
    o_refs = refs[n_sets + 3:2 * n_sets + 3]
    accs = refs[2 * n_sets + 3:]
    f = pl.program_id(1)
    wg = wg_ref[0].astype(BF16)
    wu = wu_ref[0].astype(BF16)
    wd = wd_ref[0].astype(BF16)
    for x_ref, o_ref, acc in zip(x_refs, o_refs, accs):
        @pl.when(f == 0)
        def _():
            acc[...] = jnp.zeros_like(acc)

        x = x_ref[0]
        hg = jnp.dot(x, wg, preferred_element_type=F32)
        hu = jnp.dot(x, wu, preferred_element_type=F32)
        hid = (_silu(hg) * hu).astype(BF16)
        acc[...] += jnp.dot(hid, wd, preferred_element_type=F32)

        @pl.when(f == pl.num_programs(1) - 1)
        def _():
            o_ref[0] = acc[...].astype(BF16)


def expert_ffn(xs_sets, wg, wu, wd):
    ne, _, d = xs_sets[0].shape
    ff = wg.shape[2]
    tf = 256
    xspecs = [pl.BlockSpec((1, xs.shape[1], d), lambda e, f: (e, 0, 0)) for xs in xs_sets]
    return pl.pallas_call(
        functools.partial(_ffn_kernel, len(xs_sets)),
        grid=(ne, ff // tf),
        in_specs=xspecs + [pl.BlockSpec((1, d, tf), lambda e, f: (e, 0, f)),
                           pl.BlockSpec((1, d, tf), lambda e, f: (e, 0, f)),
                           pl.BlockSpec((1, tf, d), lambda e, f: (e, f, 0))],
        out_specs=xspecs,
        out_shape=[jax.ShapeDtypeStruct(xs.shape, BF16) for xs in xs_sets],
        scratch_shapes=[pltpu.VMEM(xs.shape[1:], F32) for xs in xs_sets],
        compiler_params=_cparams(("parallel", "arbitrary")),
        name="expert_ffn",
    )(*xs_sets, wg, wu, wd)


def _scatter_kernel(cap, win, nsub, starts_ref, sel_ref, pos_ref, aff_ref, ob_ref, y_ref):
    c = pl.program_id(0)
    e = pl.program_id(1)

    @pl.when(e == 0)
    def _():
        y_ref[...] = jnp.zeros_like(y_ref)

    mine = lax.broadcasted_iota(jnp.int32, (LANES, LANES), 1) == e
    for s in range(nsub):
        rows = slice(s * LANES, (s + 1) * LANES)

        def col(ref):
            return jnp.sum(jnp.where(mine, ref[rows, :], 0.0), axis=-1, keepdims=True)

        pcol, scol, gcol = col(pos_ref), col(sel_ref), col(aff_ref)
        s0 = _window_start(starts_ref, c * nsub + s, e, cap, win)
        slot = (lax.broadcasted_iota(jnp.int32, (LANES, win), 1) + s0).astype(F32)
        onehot = jnp.where(jnp.logical_and(slot == pcol, scol > 0.0), 1.0, 0.0).astype(BF16)
        y_ref[rows, :] += gcol * jnp.dot(onehot, ob_ref[0, pl.ds(s0, win), :],
                                         preferred_element_type=F32)


def scatter_combine(starts, sel_t, pos_t, aff_t, outbuf):
    n_tok = sel_t.shape[0]
    ne, cap, d = outbuf.shape
    chunk = min(2048, n_tok)
    nsub = chunk // LANES
    win = min(2 * LANES, cap)
    tspec = pl.BlockSpec((chunk, LANES), lambda c, e, st: (c, 0))
    return pl.pallas_call(
        functools.partial(_scatter_kernel, cap, win, nsub),
        grid_spec=pltpu.PrefetchScalarGridSpec(
            num_scalar_prefetch=1,
            grid=(n_tok // chunk, ne),
            in_specs=[tspec, tspec, tspec, pl.BlockSpec((1, cap, d), lambda c, e, st: (e, 0, 0))],
            out_specs=pl.BlockSpec((chunk, d), lambda c, e, st: (c, 0))),
        out_shape=jax.ShapeDtypeStruct((n_tok, d), F32),
        compiler_params=_cparams(("parallel", "arbitrary")),
        name="scatter_combine",
    )(starts, sel_t, pos_t, aff_t, outbuf)


def _final_kernel(x1_ref, y_ref, gf_ref, fg_ref, o_ref):
    x2 = x1_ref[0] + gf_ref[0] * y_ref[0]
    o_ref[0] = x2 * lax.rsqrt(jnp.mean(x2 * x2, axis=-1, keepdims=True) + NORM_EPS) * fg_ref[...]


def final_norm(x1, y, gate_f, final_g):
    bsz, seq, d = x1.shape
    tl = 256
    per_b = gate_f.shape[0] > 1
    midx = (lambda b, i: (b, 0, 0)) if per_b else (lambda b, i: (0, 0, 0))
    xspec = pl.BlockSpec((1, tl, d), lambda b, i: (b, i, 0))
    return pl.pallas_call(
        _final_kernel,
        grid=(bsz, seq // tl),
        in_specs=[xspec, xspec, pl.BlockSpec((1, 1, d), midx), pl.BlockSpec((1, d), lambda b, i: (0, 0))],
        out_specs=xspec,
        out_shape=jax.ShapeDtypeStruct((bsz, seq, d), F32),
        compiler_params=_cparams(("parallel", "parallel")),
        name="final_norm",
    )(x1, y, gate_f, final_g.reshape(1, d))


def _pad_cols(w, width):
    return jnp.pad(w, ((0, 0), (0, width - w.shape[1])))


def _pad_rows(w, rows):
    return jnp.pad(w, ((0, rows - w.shape[0]), (0, 0)))


def _relayout_columns(w):
    n_ssd = 2 * SSD_WIDTH + 2 * SSD_GROUPS * SSD_STATE + 2 * SSD_HEADS
    ssd, rw = w[:, :n_ssd], w[:, n_ssd:]
    o = 3 * RWKV_WIDTH
    zx = 2 * SSD_WIDTH
    bc = zx + 2 * SSD_GROUPS * SSD_STATE
    parts = [ssd[:, :zx], rw[:, :o], ssd[:, zx:bc], _pad_cols(ssd[:, bc:], LANES),
             _pad_cols(rw[:, o:o + DECAY_LORA], LANES),
             _pad_cols(rw[:, o + DECAY_LORA:o + 2 * DECAY_LORA], LANES),
             _pad_cols(rw[:, o + 2 * DECAY_LORA:o + 2 * DECAY_LORA + ICLR_LORA], LANES),
             _pad_cols(rw[:, o + 2 * DECAY_LORA + ICLR_LORA:], 2 * LANES)]
    return jnp.concatenate(parts, axis=1)


def _layer_params(l, w_in, w_out, ssd_conv_w, ssd_conv_b, ssd_A_log, ssd_dt_bias, ssd_D, ssd_norm_g,
                  rwkv_mu, rwkv_w0, rwkv_w_up, rwkv_a0, rwkv_a_up, rwkv_g_up, rwkv_k_k, rwkv_k_a,
                  rwkv_r_k, rwkv_ln_g, rwkv_ln_b, router_w):
    row = lambda a: a.reshape(1, -1)
    n_ssd = 2 * SSD_WIDTH + 2 * SSD_GROUPS * SSD_STATE + 2 * SSD_HEADS
    mu_full = _relayout_columns(jnp.concatenate([jnp.zeros((1, n_ssd), F32), row(rwkv_mu[l])], axis=1))
    cw, cb = ssd_conv_w[l], row(ssd_conv_b[l])
    xe, be = SSD_WIDTH, SSD_WIDTH + SSD_GROUPS * SSD_STATE
    return dict(
        w_in=_relayout_columns(w_in[l]).astype(BF16),
        w_out=w_out[l].astype(BF16),
        cw_x=cw[:, :xe], cb_x=cb[:, :xe], cw_b=cw[:, xe:be], cb_b=cb[:, xe:be],
        cw_c=cw[:, be:], cb_c=cb[:, be:],
        dt_bias=_pad_cols(row(ssd_dt_bias[l]), LANES), a_log=_pad_cols(row(ssd_A_log[l]), LANES),
        d_exp=row(jnp.repeat(ssd_D[l], SSD_HEAD_DIM)), ssd_ng=row(ssd_norm_g[l]),
        mu_r=mu_full[:, COL_R:COL_K], mu_k=mu_full[:, COL_K:COL_V], mu_v=mu_full[:, COL_V:COL_B],
        mu_lora=mu_full[:, COL_LORA:],
        w_up_f=_pad_rows(rwkv_w_up[l, 0], LANES), w_up_b=_pad_rows(rwkv_w_up[l, 1], LANES),
        a_up=_pad_rows(rwkv_a_up[l], LANES), g_up=_pad_rows(rwkv_g_up[l], 2 * LANES),
        w0_f=row(rwkv_w0[l, 0]), w0_b=row(rwkv_w0[l, 1]), a0=row(rwkv_a0[l]),
        k_k=row(rwkv_k_k[l]), k_a=row(rwkv_k_a[l]), r_k=row(rwkv_r_k[l]),
        ln_g=row(rwkv_ln_g[l]), ln_b=row(rwkv_ln_b[l]),
        router_w=tuple(_pad_cols(t, LANES) for t in _split_bf16(router_w[l])),
    )


def _to_chain(x):
    lead = x.shape[:-3]
    bsz, seq, _ = x.shape[-3:]
    n = len(lead)
    x = x.reshape(lead + (bsz, seq, RWKV_HEADS, RWKV_HEAD_DIM))
    perm = tuple(range(n)) + (n + 1, n + 3, n, n + 2)
    return jnp.transpose(x, perm).reshape(lead + (seq, RWKV_HEAD_DIM, bsz * RWKV_HEADS))


def _from_chain(y, bsz):
    lead = y.shape[:-3]
    seq = y.shape[-3]
    n = len(lead)
    y = y.reshape(lead + (seq, RWKV_HEAD_DIM, bsz, RWKV_HEADS))
    perm = tuple(range(n)) + (n + 2, n, n + 3, n + 1)
    return jnp.transpose(y, perm).reshape(lead + (bsz, seq, RWKV_WIDTH))


def _state_to_chain(s):
    return jnp.transpose(s, (0, 4, 3, 1, 2)).reshape(2, RWKV_HEAD_DIM, RWKV_HEAD_DIM, -1)


def _state_from_chain(s, bsz):
    return jnp.transpose(s.reshape(2, RWKV_HEAD_DIM, RWKV_HEAD_DIM, bsz, RWKV_HEADS), (0, 3, 4, 2, 1))


def _scan_replicas(bsz):
    nch = bsz * RWKV_HEADS
    rep = max(1, LANES // nch)
    assert (nch * rep) % LANES == 0
    return rep


def rwkv_mixer_scan(r, k, v, w2, a, s0, p):
    rep, bsz, seq, _ = r.shape
    nh, hd = RWKV_HEADS, RWKV_HEAD_DIM
    fold = lambda t: t.reshape(t.shape[:-4] + (rep * bsz, seq, RWKV_WIDTH))
    rc, kc, ac, wc = (_to_chain(fold(t)) for t in (r, k, a, w2))
    tabs = [jnp.tile(p[n].reshape(nh, hd).T, (1, LANES // nh)) for n in ("k_k", "k_a")]
    if rep == 1:
        s0c = None if s0 is None else _state_to_chain(s0)
        y, st = rwkv_scan(rc, wc, kc, ac, _to_chain(v), *tabs, s0c)
        return _from_chain(y, bsz), _state_from_chain(st, bsz)

    vq = hd // rep
    vc = jnp.transpose(v.reshape(bsz, seq, nh, rep, vq), (1, 4, 3, 0, 2)).reshape(seq, vq, LANES)
    s0c = None
    if s0 is not None:
        s0c = jnp.transpose(s0.reshape(2, bsz, nh, rep, vq, hd), (0, 5, 4, 3, 1, 2)).reshape(
            2, hd, vq, LANES)
    y, st = rwkv_scan(rc, wc, kc, ac, vc, *tabs, s0c)
    y = jnp.transpose(y.reshape(2, seq, vq, rep, bsz, nh), (0, 4, 1, 5, 3, 2)).reshape(
        2, bsz, seq, RWKV_WIDTH)
    st = jnp.transpose(st.reshape(2, hd, vq, rep, bsz, nh), (0, 4, 5, 3, 2, 1)).reshape(
        2, bsz, nh, hd, hd)
    return y, st


def _grid_pos_embed(n_tokens, dim):
    rows = n_tokens // GRID_W
    row = jnp.repeat(jnp.arange(rows, dtype=F32), GRID_W)
    col = jnp.tile(jnp.arange(GRID_W, dtype=F32), rows)
    quarter = dim // 4
    freqs = jnp.exp(jnp.arange(quarter, dtype=F32) * (-math.log(POS_BASE) / quarter))

    def axis_embed(pos):
        ang = pos[:, None] * freqs[None, :]
        return jnp.concatenate([jnp.sin(ang), jnp.cos(ang)], axis=-1)

    return jnp.concatenate([axis_embed(row), axis_embed(col)], axis=-1)


def layer_to_dispatch(x, pos, mod, p, norm_mix_g, norm_ffn_g, states):
    bsz, seq, d = x.shape
    shift_m, scale_m, gate_m, shift_f, scale_f, gate_f = mod
    n_tok = bsz * seq
    proj, x0 = norm_in_proj(x, pos, norm_mix_g, scale_m, shift_m, p["w_in"], 640)
    proj = proj.reshape(bsz, seq, N_PROJ)
    h0f, h0b, s0f, s0b = states
    y_ssd, hf, hb = ssd_mixer(proj, p, h0f, h0b)
    r, k, v, w2, a, g = rwkv_prep(proj, p, _scan_replicas(bsz))
    s0 = None if s0f is None else jnp.stack([s0f, s0b], axis=0)
    y2, st = rwkv_mixer_scan(r, k, v, w2, a, s0, p)
    y_rwkv = rwkv_post(y2, r[0], k[0], a[0], v, g, p)
    x1, h2, logits = out_proj(y_ssd, y_rwkv, p["w_out"], x0, gate_m, norm_ffn_g, scale_f, shift_f,
                              p["router_w"])
    cap = EC_CAPACITY * n_tok // N_EXPERTS
    sel_r, pos_r, sel_t, pos_t, aff_t = route(logits.reshape(n_tok, LANES), cap)
    starts = jnp.concatenate([pos_r[:, :, 0], jnp.full((1, N_EXPERTS), cap, F32)], axis=0)
    starts = starts.astype(jnp.int32).reshape(-1)
    xs = gather_tokens(starts, sel_r, pos_r, h2.reshape(n_tok, d), cap)
    routed = dict(xs=xs, starts=starts, sel_t=sel_t, pos_t=pos_t, aff_t=aff_t, x1=x1, gate_f=gate_f)
    return routed, (hf, hb, st[0], st[1])


def combine_and_finish(routed, outbuf, final_g):
    x1 = routed["x1"]
    y_ffn = scatter_combine(routed["starts"], routed["sel_t"], routed["pos_t"], routed["aff_t"], outbuf)
    return final_norm(x1, y_ffn.reshape(x1.shape), routed["gate_f"], final_g)


def kernel(x_prompt, x_sample, state_ssd_fwd, state_ssd_bwd, state_rwkv_fwd, state_rwkv_bwd, c, c_ctx,
           w_ada, b_ada, norm_mix_g, norm_ffn_g, w_in, w_out, ssd_conv_w, ssd_conv_b, ssd_A_log,
           ssd_dt_bias, ssd_D, ssd_norm_g, rwkv_mu, rwkv_w0, rwkv_w_up, rwkv_a0, rwkv_a_up, rwkv_g_up,
           rwkv_k_k, rwkv_k_a, rwkv_r_k, rwkv_ln_g, rwkv_ln_b, router_w, exp_w_gate, exp_w_up,
           exp_w_down, final_norm_g):
    depth = w_in.shape[0]
    assert depth == 1, "the final norm runs right after the single layer's FFN residual"
    l = 0
    dec_b = x_sample.shape[0]
    p = _layer_params(l, w_in, w_out, ssd_conv_w, ssd_conv_b, ssd_A_log, ssd_dt_bias, ssd_D, ssd_norm_g,
                      rwkv_mu, rwkv_w0, rwkv_w_up, rwkv_a0, rwkv_a_up, rwkv_g_up, rwkv_k_k, rwkv_k_a,
                      rwkv_r_k, rwkv_ln_g, rwkv_ln_b, router_w)
    cond = jnp.concatenate([c_ctx[None, :], c, jnp.zeros((16 - 1 - dec_b, D_MODEL), F32)], axis=0)
    mod = ada_modulation(cond, w_ada[l], b_ada[l]).reshape(16, N_MOD, 1, D_MODEL)
    mod_ctx = [mod[0:1, i] for i in range(N_MOD)]
    mod_lat = [mod[1:1 + dec_b, i] for i in range(N_MOD)]

    routed_ctx, st = layer_to_dispatch(x_prompt, None, mod_ctx, p, norm_mix_g[l], norm_ffn_g[l],
                                       (None, None, None, None))
    pos = _grid_pos_embed(x_sample.shape[1], D_MODEL)
    cached = (state_ssd_fwd[:, l], state_ssd_bwd[:, l], state_rwkv_fwd[:, l], state_rwkv_bwd[:, l])
    routed_lat, _ = layer_to_dispatch(x_sample, pos, mod_lat, p, norm_mix_g[l], norm_ffn_g[l], cached)
    out_ctx, out_lat = expert_ffn([routed_ctx["xs"], routed_lat["xs"]],
                                  exp_w_gate[l], exp_w_up[l], exp_w_down[l])
    y_prompt = combine_and_finish(routed_ctx, out_ctx, final_norm_g)
    y_sample = combine_and_finish(routed_lat, out_lat, final_norm_g)
    return (y_prompt, y_sample, st[0][:, None], st[1][:, None], st[2][:, None], st[3][:, None])
```

```python
import functools
import math

import jax
import jax.numpy as jnp
from jax import lax
from jax.experimental import pallas as pl
from jax.experimental.pallas import tpu as pltpu

F32 = jnp.float32
BF16 = jnp.bfloat16
HIGHEST = lax.Precision.HIGHEST

D_MODEL = 2048
GRID_W = 64
SSD_WIDTH = 1024
SSD_HEAD_DIM = 64
SSD_HEADS = 16
SSD_GROUPS = 2
SSD_STATE = 128
SSD_CONV_W = 5
SSD_CHUNK = 128
RWKV_WIDTH = 1024
RWKV_HEAD_DIM = 64
RWKV_HEADS = 16
DECAY_LORA = 64
ICLR_LORA = 64
GATE_LORA = 160
N_EXPERTS = 16
EC_CAPACITY = 2
EXPERT_FF = 1024
N_MOD = 6
NORM_EPS = 1e-6
GN_EPS = 64e-5
DECAY_SCALE = 0.606531
POS_BASE = 10000.0
SEARCH_ITERS = 48

LANES = 128
SUBLANES = 8
GATHER_ROWS = 64
GATHER_EXPERTS = 4

COL_Z = 0
COL_X = 1024
COL_R = 2048
COL_K = 3072
COL_V = 4096
COL_B = 5120
COL_C = 5376
COL_DT = 5632
COL_LORA = 5760
PREP_BLOCK_ELEMS = 256 * 1024
LORA_W = 640
N_PROJ = COL_LORA + LORA_W

VMEM_LIMIT = 56 * 1024 * 1024

ADA_COLS = 1024
TOKEN_ROWS = 256
PROJ_ROWS = 2048
PROJ_COLS = 640
OUT_PROJ_ROWS = 512
SCAN_STEPS = 32
GATHER_TOKENS = 1024
FFN_COLS = 256
SCATTER_TOKENS = 2048


def _cparams(sem):
    return pltpu.CompilerParams(dimension_semantics=sem, vmem_limit_bytes=VMEM_LIMIT)


def _silu(x):
    return x * jax.nn.sigmoid(x)


def _ada_kernel(c_ref, w_ref, b_ref, o_ref):
    s = _silu(c_ref[...]).astype(BF16)
    o_ref[...] = jnp.dot(s, w_ref[...].astype(BF16), preferred_element_type=F32) + b_ref[...]


def ada_modulation(cond, w, b):
    m, d = cond.shape
    n = w.shape[1]
    tn = ADA_COLS
    return pl.pallas_call(
        _ada_kernel,
        grid=(n // tn,),
        in_specs=[pl.BlockSpec((m, d), lambda j: (0, 0)),
                  pl.BlockSpec((d, tn), lambda j: (0, j)),
                  pl.BlockSpec((1, tn), lambda j: (0, j))],
        out_specs=pl.BlockSpec((m, tn), lambda j: (0, j)),
        out_shape=jax.ShapeDtypeStruct((m, n), F32),
        compiler_params=_cparams(("parallel",)),
        name="ada_modulation",
    )(cond, w, b.reshape(1, n))


def _norm_mod_kernel(has_pos, *refs):
    if has_pos:
        x_ref, pos_ref, g_ref, sc_ref, sh_ref, h_ref, x0_ref = refs
    else:
        x_ref, g_ref, sc_ref, sh_ref, h_ref = refs
    x = x_ref[0]
    if has_pos:
        x = x + pos_ref[...]
        x0_ref[0] = x
    y = x * lax.rsqrt(jnp.mean(x * x, axis=-1, keepdims=True) + NORM_EPS) * g_ref[...]
    h_ref[0] = (y * (1.0 + sc_ref[0]) + sh_ref[0]).astype(BF16)


def norm_modulate(x, pos, g, scale, shift):
    bsz, seq, d = x.shape
    tl = TOKEN_ROWS
    per_b = scale.shape[0] > 1
    midx = (lambda b, i: (b, 0, 0)) if per_b else (lambda b, i: (0, 0, 0))
    xspec = pl.BlockSpec((1, tl, d), lambda b, i: (b, i, 0))
    in_specs = [xspec]
    args = [x]
    if pos is not None:
        in_specs.append(pl.BlockSpec((tl, d), lambda b, i: (i, 0)))
        args.append(pos)
    in_specs += [pl.BlockSpec((1, d), lambda b, i: (0, 0)),
                 pl.BlockSpec((1, 1, d), midx), pl.BlockSpec((1, 1, d), midx)]
    args += [g.reshape(1, d), scale, shift]
    out_shape = [jax.ShapeDtypeStruct((bsz, seq, d), BF16)]
    out_specs = [xspec]
    if pos is not None:
        out_shape.append(jax.ShapeDtypeStruct((bsz, seq, d), F32))
        out_specs.append(xspec)
    res = pl.pallas_call(
        functools.partial(_norm_mod_kernel, pos is not None),
        grid=(bsz, seq // tl),
        in_specs=in_specs, out_specs=out_specs, out_shape=out_shape,
        compiler_params=_cparams(("parallel", "parallel")),
        name="norm_modulate",
    )(*args)
    return (res[0], res[1]) if pos is not None else (res[0], x)


def _mm_kernel(a_ref, b_ref, o_ref):
    o_ref[...] = jnp.dot(a_ref[...], b_ref[...], preferred_element_type=F32)


def matmul_bf16(a, b, tm, tn):
    m, k = a.shape
    n = b.shape[1]
    return pl.pallas_call(
        _mm_kernel,
        grid=(m // tm, n // tn),
        in_specs=[pl.BlockSpec((tm, k), lambda i, j: (i, 0)),
                  pl.BlockSpec((k, tn), lambda i, j: (0, j))],
        out_specs=pl.BlockSpec((tm, tn), lambda i, j: (i, j)),
        out_shape=jax.ShapeDtypeStruct((m, n), F32),
        compiler_params=_cparams(("parallel", "arbitrary")),
        name="in_proj",
    )(a, b)


def _shifted(u, off):
    n = u.shape[0]
    row = lax.broadcasted_iota(jnp.int32, (n, 1), 0)
    rolled = pltpu.roll(u, (-off) % n, axis=0)
    valid = jnp.logical_and(row + off >= 0, row + off < n)
    return jnp.where(valid, rolled, 0.0)


def _conv_silu(u, w_ref, b_ref, cols):
    pad = SSD_CONV_W // 2
    acc = u * w_ref[pad:pad + 1, cols] + b_ref[:, cols]
    for j in range(SSD_CONV_W):
        if j != pad:
            acc = acc + _shifted(u, j - pad) * w_ref[j:j + 1, cols]
    return _silu(acc)


def _softplus(x):
    return jnp.maximum(x, 0.0) + jnp.log1p(jnp.exp(-jnp.abs(x)))


def _ssd_kernel(nc, zero_init, *refs):
    q = SSD_CHUNK
    (z_ref, x_ref, b_ref, c_ref, dt_ref, cwx, cbx, cwb, cbb, cwc, cbc,
     dtb_ref, alog_ref, d_ref, ng_ref) = refs[:15]
    refs = refs[15:]
    if not zero_init:
        h0f_ref, h0b_ref = refs[:2]
        refs = refs[2:]
    y_ref, hf_ref, hb_ref, xh_s, bm_s, cm_s, dt_s, a_s, at_s, y_s, h_s = refs

    slab = 256
    for cb in range(SSD_WIDTH // slab):
        cols = slice(cb * slab, (cb + 1) * slab)
        xh = _conv_silu(x_ref[0, :, cols], cwx, cbx, cols)
        for c in range(nc):
            rows = slice(c * q, (c + 1) * q)
            xh_s[c, :, cols] = xh[rows]
            y_s[c, :, cols] = xh[rows] * d_ref[:, cols]
    allc = slice(0, SSD_GROUPS * SSD_STATE)
    bm = _conv_silu(b_ref[0], cwb, cbb, allc)
    cm = _conv_silu(c_ref[0], cwc, cbc, allc)
    dt = _softplus(dt_ref[0] + dtb_ref[...])
    a = dt * (-jnp.exp(alog_ref[...]))
    ii = lax.broadcasted_iota(jnp.int32, (q, q), 0)
    jj = lax.broadcasted_iota(jnp.int32, (q, q), 1)
    eye = (ii == jj).astype(F32)
    lower = ii >= jj
    upper = ii <= jj
    first_half = jj < SSD_HEAD_DIM
    for c in range(nc):
        rows = slice(c * q, (c + 1) * q)
        for g in range(SSD_GROUPS):
            gcols = slice(g * SSD_STATE, (g + 1) * SSD_STATE)
            bm_s[c, gcols, :] = bm[rows, gcols].T
        cm_s[c] = cm[rows]
        dt_s[c] = dt[rows]
        a_s[c] = a[rows]
        at_s[c] = lax.dot_general(eye, a[rows], (((1,), (1,)), ((), ())),
                                  precision=HIGHEST, preferred_element_type=F32)

    for d in range(2):
        mask = lower if d == 0 else upper
        tri = mask.astype(F32)
        tri_t = (upper if d == 0 else lower).astype(F32)
        if zero_init:
            h_s[...] = jnp.zeros_like(h_s)
        else:
            h0_ref = h0f_ref if d == 0 else h0b_ref
            for p in range(SSD_HEADS // 2):
                h_s[p] = jnp.concatenate([h0_ref[0, 2 * p], h0_ref[0, 2 * p + 1]], axis=0).T

        def chunk_body(ci, carry, d=d, mask=mask, tri=tri, tri_t=tri_t):
            c = ci if d == 0 else nc - 1 - ci
            a_c = a_s[c]
            cum = jnp.dot(tri, a_c, precision=HIGHEST, preferred_element_type=F32)
            cum_t = jnp.dot(at_s[c], tri_t, precision=HIGHEST, preferred_element_type=F32)
            dt_c = dt_s[c]
            for g in range(SSD_GROUPS):
                gcols = slice(g * SSD_STATE, (g + 1) * SSD_STATE)
                bt = bm_s[c, gcols, :]
                cg = cm_s[c, :, gcols]
                gram = jnp.dot(cg.astype(BF16), bt.astype(BF16), preferred_element_type=F32)
                for pp in range(SSD_HEADS // SSD_GROUPS // 2):
                    p = g * (SSD_HEADS // SSD_GROUPS // 2) + pp
                    pcols = slice(p * LANES, (p + 1) * LANES)
                    xh_pair = xh_s[c, :, pcols]
                    hprev = h_s[p]
                    hprev_b = hprev.astype(BF16)
                    parts = []
                    for e in range(2):
                        col = d * SSD_HEADS + 2 * p + e
                        mine = first_half if e == 0 else jnp.logical_not(first_half)
                        cc = jnp.broadcast_to(cum[:, col:col + 1], (q, q))
                        cr = cum_t[col:col + 1, :]
                        dec = jnp.exp(jnp.where(mask, cc - cr, -jnp.inf))
                        m = (gram * dec).astype(BF16)
                        xdt = jnp.where(mine, xh_pair * dt_c[:, col:col + 1], 0.0).astype(BF16)
                        tot = cum[q - 1:q, col:col + 1] if d == 0 else cum[0:1, col:col + 1]
                        bd_t = (bt * jnp.exp(tot - cr)).astype(BF16)
                        cd = (cg * jnp.exp(cc)).astype(BF16)
                        s_c = jnp.dot(bd_t, xdt, preferred_element_type=F32)
                        y_off = jnp.dot(cd, hprev_b, preferred_element_type=F32)
                        y_diag = jnp.dot(m, xdt, preferred_element_type=F32)
                        parts.append((s_c, y_off, y_diag, jnp.exp(tot)))
                    (s0, yo0, yd0, g0), (s1, yo1, yd1, g1) = parts
                    h_s[p] = hprev * jnp.where(first_half, g0, g1) + (s0 + s1)
                    y_s[c, :, pcols] += (yd0 + yd1) + jnp.where(first_half, yo0, yo1)
            return carry

        lax.fori_loop(0, nc, chunk_body, 0)
        out_ref = hf_ref if d == 0 else hb_ref
        for p in range(SSD_HEADS // 2):
            tr = h_s[p].T
            out_ref[0, 2 * p] = tr[0:SSD_HEAD_DIM]
            out_ref[0, 2 * p + 1] = tr[SSD_HEAD_DIM:]

    gw = SSD_WIDTH // SSD_GROUPS
    for c in range(nc):
        rows = slice(c * q, (c + 1) * q)
        for g in range(SSD_GROUPS):
            cols = slice(g * gw, (g + 1) * gw)
            y = y_s[c, :, cols] * _silu(z_ref[0, rows, cols])
            y = y * lax.rsqrt(jnp.mean(y * y, axis=-1, keepdims=True) + NORM_EPS)
            y_ref[0, rows, cols] = (y * ng_ref[:, cols]).astype(BF16)


def ssd_mixer(proj, p, h0f, h0b):
    bsz, seq, _ = proj.shape
    nc = seq // SSD_CHUNK
    zero_init = h0f is None

    def col_spec(width, start):
        blk = start // width
        return pl.BlockSpec((1, seq, width), lambda b: (b, 0, blk))

    def full2(arr):
        return pl.BlockSpec(arr.shape, lambda b: (0, 0))

    st_spec = pl.BlockSpec((1, SSD_HEADS, SSD_HEAD_DIM, SSD_STATE), lambda b: (b, 0, 0, 0))
    bc = SSD_GROUPS * SSD_STATE
    small = [p["cw_x"], p["cb_x"], p["cw_b"], p["cb_b"], p["cw_c"], p["cb_c"],
             p["dt_bias"], p["a_log"], p["d_exp"], p["ssd_ng"]]
    in_specs = [col_spec(SSD_WIDTH, COL_Z), col_spec(SSD_WIDTH, COL_X), col_spec(bc, COL_B),
                col_spec(bc, COL_C), col_spec(LANES, COL_DT)] + [full2(s) for s in small]
    args = [proj] * 5 + small
    if not zero_init:
        in_specs += [st_spec, st_spec]
        args += [h0f, h0b]
    st_shape = jax.ShapeDtypeStruct((bsz, SSD_HEADS, SSD_HEAD_DIM, SSD_STATE), F32)
    q = SSD_CHUNK
    return pl.pallas_call(
        functools.partial(_ssd_kernel, nc, zero_init),
        grid=(bsz,),
        in_specs=in_specs,
        out_specs=[pl.BlockSpec((1, seq, SSD_WIDTH), lambda b: (b, 0, 0)), st_spec, st_spec],
        out_shape=[jax.ShapeDtypeStruct((bsz, seq, SSD_WIDTH), BF16), st_shape, st_shape],
        scratch_shapes=[pltpu.VMEM((nc, q, SSD_WIDTH), F32), pltpu.VMEM((nc, bc, q), F32),
                        pltpu.VMEM((nc, q, bc), F32), pltpu.VMEM((nc, q, LANES), F32),
                        pltpu.VMEM((nc, q, LANES), F32), pltpu.VMEM((nc, LANES, q), F32),
                        pltpu.VMEM((nc, q, SSD_WIDTH), F32),
                        pltpu.VMEM((SSD_HEADS // 2, SSD_STATE, 2 * SSD_HEAD_DIM), F32)],
        compiler_params=_cparams(("parallel",)),
        name="ssd_mixer",
    )(*args)


def _seg_sum(x):
    lane = lax.broadcasted_iota(jnp.int32, x.shape, 1)
    first = lane < RWKV_HEAD_DIM
    s0 = jnp.sum(jnp.where(first, x, 0.0), axis=-1, keepdims=True)
    s1 = jnp.sum(jnp.where(first, 0.0, x), axis=-1, keepdims=True)
    return jnp.where(first, s0, s1)


def _shift_mix(u, mu):
    return u + mu * (0.5 * (_shifted(u, -1) + _shifted(u, 1)) - u)


def _rwkv_prep_kernel(ncb, rep, *refs):
    (r_ref, k_ref, v_ref, lo_ref, mur, muk, muv, mul, wupf, wupb, aup, gup,
     w0f, w0b, a0, ro, ko, vo, wo, ao, go) = refs[:21]
    staged = dict(zip(("r", "k", "a", "wf", "wb"), refs[21:]))
    q = pl.program_id(2)

    def put(name, cols, val):
        if staged:
            staged[name][:, cols] = val
        elif name in ("wf", "wb"):
            wo[0 if name == "wf" else 1, 0, 0, :, cols] = val
        else:
            {"r": ro, "k": ko, "a": ao}[name][0, 0, :, cols] = val

    def compute():
        lo = _shift_mix(lo_ref[0], mul[...])
        wdf = jnp.tanh(lo[:, 0:128]).astype(BF16)
        wdb = jnp.tanh(lo[:, 128:256]).astype(BF16)
        ad = lo[:, 256:384].astype(BF16)
        gd = jax.nn.sigmoid(lo[:, 384:640]).astype(BF16)
        for cb in range(ncb):
            cols = slice(cb * LANES, (cb + 1) * LANES)

            def mm(x, w_ref, cols=cols):
                return jnp.dot(x, w_ref[:, cols].astype(BF16), preferred_element_type=F32)

            r = _shift_mix(r_ref[0, :, cols], mur[:, cols])
            k = _shift_mix(k_ref[0, :, cols], muk[:, cols])
            v = _shift_mix(v_ref[0, :, cols], muv[:, cols])
            put("wf", cols, jnp.exp(-DECAY_SCALE * jax.nn.sigmoid(w0f[:, cols] + mm(wdf, wupf))))
            put("wb", cols, jnp.exp(-DECAY_SCALE * jax.nn.sigmoid(w0b[:, cols] + mm(wdb, wupb))))
            put("a", cols, jax.nn.sigmoid(a0[:, cols] + mm(ad, aup)))
            go[0, :, cols] = mm(gd, gup)
            put("r", cols, r)
            put("k", cols, k)
            vo[0, :, cols] = v

    if staged:
        pl.when(q == 0)(compute)
        ro[0, 0] = staged["r"][...]
        ko[0, 0] = staged["k"][...]
        ao[0, 0] = staged["a"][...]
        wo[0, 0, 0] = staged["wf"][...]
        wo[1, 0, 0] = staged["wb"][...]
    else:
        compute()


def rwkv_prep(proj, p, rep):
    bsz, seq, _ = proj.shape
    cw = min(RWKV_WIDTH, PREP_BLOCK_ELEMS // seq)
    nblk = RWKV_WIDTH // cw

    def col_spec(start):
        blk = start // cw
        return pl.BlockSpec((1, seq, cw), lambda b, j, q: (b, 0, blk + j))

    vec = pl.BlockSpec((1, cw), lambda b, j, q: (0, j))

    def mat(rows):
        return pl.BlockSpec((rows, cw), lambda b, j, q: (0, j))

    tok_spec = pl.BlockSpec((1, seq, cw), lambda b, j, q: (b, 0, j))
    rep_spec = pl.BlockSpec((1, 1, seq, cw), lambda b, j, q: (q, b, 0, j))
    w_spec = pl.BlockSpec((2, 1, 1, seq, cw), lambda b, j, q: (0, q, b, 0, j))
    tok_sh = jax.ShapeDtypeStruct((bsz, seq, RWKV_WIDTH), F32)
    rep_sh = jax.ShapeDtypeStruct((rep, bsz, seq, RWKV_WIDTH), F32)
    w_sh = jax.ShapeDtypeStruct((2, rep, bsz, seq, RWKV_WIDTH), F32)
    return pl.pallas_call(
        functools.partial(_rwkv_prep_kernel, cw // LANES, rep),
        grid=(bsz, nblk, rep),
        in_specs=[col_spec(COL_R), col_spec(COL_K), col_spec(COL_V),
                  pl.BlockSpec((1, seq, LORA_W), lambda b, j, q: (b, 0, COL_LORA // LORA_W)),
                  vec, vec, vec, pl.BlockSpec((1, LORA_W), lambda b, j, q: (0, 0)),
                  mat(128), mat(128), mat(128), mat(256),
                  vec, vec, vec],
        out_specs=[rep_spec, rep_spec, tok_spec, w_spec, rep_spec, tok_spec],
        out_shape=[rep_sh, rep_sh, tok_sh, w_sh, rep_sh, tok_sh],
        scratch_shapes=[pltpu.VMEM((seq, cw), F32)] * (5 if rep > 1 else 0),
        compiler_params=_cparams(("parallel", "arbitrary", "arbitrary")),
        name="rwkv_prep",
    )(proj, proj, proj, proj, p["mu_r"], p["mu_k"], p["mu_v"], p["mu_lora"],
      p["w_up_f"], p["w_up_b"], p["a_up"], p["g_up"],
      p["w0_f"], p["w0_b"], p["a0"])


def _rwkv_scan_kernel(tb, zero_init, *refs):
    if zero_init:
        r_ref, w_ref, kr_ref, a_ref, v_ref, kk_ref, ka_ref, y_ref, st_ref, s_s, k_ref, al_ref, be_ref = refs
    else:
        (r_ref, w_ref, kr_ref, a_ref, v_ref, kk_ref, ka_ref, s0_ref, y_ref, st_ref,
         s_s, k_ref, al_ref, be_ref) = refs
    kd = RWKV_HEAD_DIM
    d = pl.program_id(0)
    step_i = pl.program_id(2)

    def time_index(i):
        return i + d * (tb - 1 - 2 * i)

    @pl.when(step_i == 0)
    def _():
        if zero_init:
            s_s[...] = jnp.zeros_like(s_s)
        else:
            s_s[...] = s0_ref[...]

    def derive(t, carry):
        kraw = kr_ref[t]
        a = a_ref[t]
        kk = kraw * kk_ref[...]
        kk = kk * (1.0 / jnp.maximum(jnp.sqrt(jnp.sum(kk * kk, axis=0, keepdims=True)), 1e-12))
        al_ref[t] = -kk
        be_ref[t] = kk * a
        k_ref[t] = kraw * (1.0 + (a - 1.0) * ka_ref[...])
        return carry

    lax.fori_loop(0, tb, derive, 0, unroll=4)

    t0 = time_index(0)
    sa0 = jnp.zeros(s_s.shape[1:], F32)
    sa1 = jnp.zeros(s_s.shape[1:], F32)
    for kq in range(0, kd, 2):
        sa0 = sa0 + s_s[kq] * al_ref[t0, kq:kq + 1, :]
        sa1 = sa1 + s_s[kq + 1] * al_ref[t0, kq + 1:kq + 2, :]

    def step(i, sa):
        t = time_index(i)
        tn = time_index(jnp.minimum(i + 1, tb - 1))
        vt = v_ref[t]
        y = jnp.zeros_like(sa)
        sa_next = jnp.zeros_like(sa)
        for kq in range(kd):
            row = slice(kq, kq + 1)
            sk = s_s[kq] * w_ref[t, row, :] + sa * be_ref[t, row, :] + vt * k_ref[t, row, :]
            s_s[kq] = sk
            y = y + sk * r_ref[t, row, :]
            sa_next = sa_next + sk * al_ref[tn, row, :]
        y_ref[t] = y
        return sa_next

    lax.fori_loop(0, tb, step, sa0 + sa1)

    @pl.when(step_i == pl.num_programs(2) - 1)
    def _():
        st_ref[...] = s_s[...]


def rwkv_scan(r, w, k, a, v, kk_tab, ka_tab, s0):
    seq, kd, nch = r.shape
    vv = v.shape[1]
    tb = SCAN_STEPS
    nblk = seq // tb
    zero_init = s0 is None

    def tblk(d, i):
        return i + d * (nblk - 1 - 2 * i)

    kspec = pl.BlockSpec((tb, kd, LANES), lambda d, g, i: (tblk(d, i), 0, g))
    wspec = pl.BlockSpec((None, tb, kd, LANES), lambda d, g, i: (d, tblk(d, i), 0, g))
    vspec = pl.BlockSpec((tb, vv, LANES), lambda d, g, i: (tblk(d, i), 0, g))
    yspec = pl.BlockSpec((None, tb, vv, LANES), lambda d, g, i: (d, tblk(d, i), 0, g))
    sspec = pl.BlockSpec((None, kd, vv, LANES), lambda d, g, i: (d, 0, 0, g))
    tabspec = pl.BlockSpec((kd, LANES), lambda d, g, i: (0, 0))
    in_specs = [kspec, wspec, kspec, kspec, vspec, tabspec, tabspec]
    args = [r, w, k, a, v, kk_tab, ka_tab]
    if not zero_init:
        in_specs.append(sspec)
        args.append(s0)
    return pl.pallas_call(
        functools.partial(_rwkv_scan_kernel, tb, zero_init),
        grid=(2, nch // LANES, nblk),
        in_specs=in_specs,
        out_specs=[yspec, sspec],
        out_shape=[jax.ShapeDtypeStruct((2, seq, vv, nch), F32),
                   jax.ShapeDtypeStruct((2, kd, vv, nch), F32)],
        scratch_shapes=[pltpu.VMEM((kd, vv, LANES), F32)] + [pltpu.VMEM((tb, kd, LANES), F32)] * 3,
        compiler_params=_cparams(("parallel", "parallel", "arbitrary")),
        name="rwkv_scan",
    )(*args)


def _rwkv_post_kernel(y_ref, r_ref, k_ref, a_ref, v_ref, g_ref, lng, lnb, rk, ka, o_ref):
    inv = 1.0 / RWKV_HEAD_DIM
    for cb in range(RWKV_WIDTH // LANES):
        cols = slice(cb * LANES, (cb + 1) * LANES)
        y = y_ref[0, 0, :, cols] + y_ref[1, 0, :, cols]
        mean = _seg_sum(y) * inv
        yc = y - mean
        var = _seg_sum(yc * yc) * inv
        yn = yc * lax.rsqrt(var + GN_EPS) * lng[:, cols] + lnb[:, cols]
        k = k_ref[0, :, cols] * (1.0 + (a_ref[0, :, cols] - 1.0) * ka[:, cols])
        bonus = _seg_sum(r_ref[0, :, cols] * k * rk[:, cols]) * v_ref[0, :, cols]
        o_ref[0, :, cols] = ((yn + bonus) * g_ref[0, :, cols]).astype(BF16)


def rwkv_post(y2, r, k, a, v, g, p):
    _, bsz, seq, wd = y2.shape
    tl = TOKEN_ROWS
    spec = pl.BlockSpec((1, tl, wd), lambda b, i: (b, i, 0))
    vec = pl.BlockSpec((1, wd), lambda b, i: (0, 0))
    return pl.pallas_call(
        _rwkv_post_kernel,
        grid=(bsz, seq // tl),
        in_specs=[pl.BlockSpec((2, 1, tl, wd), lambda b, i: (0, b, i, 0))] + [spec] * 5 + [vec] * 4,
        out_specs=spec,
        out_shape=jax.ShapeDtypeStruct((bsz, seq, wd), BF16),
        compiler_params=_cparams(("parallel", "parallel")),
        name="rwkv_post",
    )(y2, r, k, a, v, g, p["ln_g"], p["ln_b"], p["r_k"], p["k_a"])


def _split_bf16(x):
    hi = x.astype(BF16)
    return hi, (x - hi.astype(F32)).astype(BF16)


def _out_proj_kernel(ys_ref, yr_ref, wo_ref, x_ref, gm_ref, ng_ref, sc_ref, sh_ref, rwh_ref, rwl_ref,
                     x1_ref, h2_ref, lg_ref):
    acc = jnp.dot(ys_ref[0], wo_ref[0:SSD_WIDTH, :], preferred_element_type=F32)
    acc = acc + jnp.dot(yr_ref[0], wo_ref[SSD_WIDTH:, :], preferred_element_type=F32)
    x1 = x_ref[0] + gm_ref[0] * acc
    x1_ref[0] = x1
    hn = x1 * lax.rsqrt(jnp.mean(x1 * x1, axis=-1, keepdims=True) + NORM_EPS) * ng_ref[...]
    h2 = hn * (1.0 + sc_ref[0]) + sh_ref[0]
    hi, lo = _split_bf16(h2)
    h2_ref[0] = hi
    lg_ref[0] = (jnp.dot(hi, rwh_ref[...], preferred_element_type=F32)
                 + jnp.dot(lo, rwh_ref[...], preferred_element_type=F32)
                 + jnp.dot(hi, rwl_ref[...], preferred_element_type=F32))


def out_proj(y_ssd, y_rwkv, w_out, x, gate_m, norm_g, scale_f, shift_f, router_w):
    bsz, seq, d = x.shape
    tm = min(OUT_PROJ_ROWS, seq)
    per_b = gate_m.shape[0] > 1
    midx = (lambda b, i: (b, 0, 0)) if per_b else (lambda b, i: (0, 0, 0))
    half = pl.BlockSpec((1, tm, d // 2), lambda b, i: (b, i, 0))
    full = pl.BlockSpec((1, tm, d), lambda b, i: (b, i, 0))
    mspec = pl.BlockSpec((1, 1, d), midx)
    return pl.pallas_call(
        _out_proj_kernel,
        grid=(bsz, seq // tm),
        in_specs=[half, half, pl.BlockSpec((d, d), lambda b, i: (0, 0)), full, mspec,
                  pl.BlockSpec((1, d), lambda b, i: (0, 0)), mspec, mspec,
                  pl.BlockSpec((d, LANES), lambda b, i: (0, 0)),
                  pl.BlockSpec((d, LANES), lambda b, i: (0, 0))],
        out_specs=[full, full, pl.BlockSpec((1, tm, LANES), lambda b, i: (b, i, 0))],
        out_shape=[jax.ShapeDtypeStruct((bsz, seq, d), F32), jax.ShapeDtypeStruct((bsz, seq, d), BF16),
                   jax.ShapeDtypeStruct((bsz, seq, LANES), F32)],
        compiler_params=_cparams(("parallel", "parallel")),
        name="out_proj",
    )(y_ssd, y_rwkv, w_out, x, gate_m, norm_g.reshape(1, d), scale_f, shift_f, *router_w)


def _route_kernel(nt, cap, lg_ref, selr_ref, posr_ref, selt_ref, post_ref, afft_ref, aff_s, pre_s):
    ne = N_EXPERTS
    lane = lax.broadcasted_iota(jnp.int32, (LANES, LANES), 1)
    sub = lax.broadcasted_iota(jnp.int32, (LANES, LANES), 0)
    upper_incl = (sub <= lane).astype(BF16)
    for i in range(nt):
        lg = jnp.where(lane < ne, lg_ref[i * LANES:(i + 1) * LANES, :], -jnp.inf)
        e = jnp.exp(lg - jnp.max(lg, axis=-1, keepdims=True))
        aff = e / jnp.sum(e, axis=-1, keepdims=True)
        afft_ref[i * LANES:(i + 1) * LANES, :] = aff
        aff_s[i] = aff.T[0:ne, :]
    aff3 = aff_s[...]

    def count(mask):
        s = jnp.sum(mask.astype(F32), axis=0, keepdims=True)
        return jnp.sum(s, axis=2, keepdims=True)

    def search(_, carry):
        lo, hi = carry
        mid = 0.5 * (lo + hi)
        ok = count(aff3 >= mid) >= cap
        return jnp.where(ok, mid, lo), jnp.where(ok, hi, mid)

    lo0 = jnp.zeros((1, ne, 1), F32)
    hi0 = jnp.full((1, ne, 1), 2.0, F32)
    lo, _ = lax.fori_loop(0, SEARCH_ITERS, search, (lo0, hi0))
    cand = jnp.where(aff3 >= lo, aff3, 4.0)
    thr = jnp.min(jnp.min(cand, axis=0, keepdims=True), axis=2, keepdims=True)
    gt = (aff3 > thr).astype(F32)
    eq = (aff3 == thr).astype(F32)
    need = cap - count(aff3 > thr)[0]

    def prefix_excl(m3):
        off = jnp.zeros((ne, 1), F32)
        for i in range(nt):
            inc = jnp.dot(m3[i].astype(BF16), upper_incl, preferred_element_type=F32)
            pre_s[i] = inc - m3[i] + off
            off = off + inc[:, LANES - 1:LANES]
        return pre_s[...]

    sel = jnp.maximum(gt, eq * (prefix_excl(eq) < need[None]).astype(F32))
    pos = prefix_excl(sel)
    selr_ref[...] = sel
    posr_ref[...] = pos
    zpad = jnp.zeros((LANES - ne, LANES), F32)
    for i in range(nt):
        rows = slice(i * LANES, (i + 1) * LANES)
        selt_ref[rows, :] = jnp.concatenate([sel[i], zpad], axis=0).T
        post_ref[rows, :] = jnp.concatenate([pos[i], zpad], axis=0).T


def route(logits, cap):
    n_tok = logits.shape[0]
    nt = n_tok // LANES
    row_sh = jax.ShapeDtypeStruct((nt, N_EXPERTS, LANES), F32)
    tm_sh = jax.ShapeDtypeStruct((n_tok, LANES), F32)
    return pl.pallas_call(
        functools.partial(_route_kernel, nt, cap),
        out_shape=[row_sh, row_sh, tm_sh, tm_sh, tm_sh],
        scratch_shapes=[pltpu.VMEM((nt, N_EXPERTS, LANES), F32), pltpu.VMEM((nt, N_EXPERTS, LANES), F32)],
        compiler_params=pltpu.CompilerParams(vmem_limit_bytes=VMEM_LIMIT),
        name="route",
    )(logits)


def _window_start(starts_ref, tile, e, cap, win, align=SUBLANES):
    ps = starts_ref[tile * N_EXPERTS + e]
    return pl.multiple_of(jnp.minimum((ps // align) * align, cap - win), align)


def _gather_kernel(cap, win, nsub, starts_ref, sel_ref, pos_ref, h_ref, o_ref):
    eg = pl.program_id(0)
    j = pl.program_id(1)

    @pl.when(j == 0)
    def _():
        o_ref[...] = jnp.zeros_like(o_ref)

    for el in range(GATHER_EXPERTS):
        e = eg * GATHER_EXPERTS + el
        for s in range(0, nsub, 2):
            tile = j * nsub + s
            s0 = _window_start(starts_ref, tile, e, cap, win, align=2 * SUBLANES)
            lo = starts_ref[tile * N_EXPERTS + e]
            hi = starts_ref[(tile + 2) * N_EXPERTS + e]
            prows = [pos_ref[s + u, pl.ds(e, 1), :] for u in range(2)]
            srows = [sel_ref[s + u, pl.ds(e, 1), :] for u in range(2)]
            for piece in range(win // GATHER_ROWS):
                p0 = pl.multiple_of(s0 + piece * GATHER_ROWS, 2 * SUBLANES)

                @pl.when(jnp.logical_and(p0 < hi, p0 + GATHER_ROWS > lo))
                def _(p0=p0, s=s, el=el, prows=prows, srows=srows):
                    slot = (lax.broadcasted_iota(jnp.int32, (GATHER_ROWS, LANES), 0) + p0).astype(F32)
                    onehot = jnp.concatenate(
                        [jnp.where(jnp.logical_and(slot == prows[u], srows[u] > 0.0), 1.0, 0.0).astype(BF16)
                         for u in range(2)], axis=1)
                    rows = pl.ds(p0, GATHER_ROWS)
                    got = jnp.dot(onehot, h_ref[s * LANES:(s + 2) * LANES, :], preferred_element_type=F32)
                    o_ref[el, rows, :] = (o_ref[el, rows, :].astype(F32) + got).astype(BF16)


def gather_tokens(starts, sel_r, pos_r, h, cap):
    n_tok, d = h.shape
    tt = min(GATHER_TOKENS, n_tok)
    nsub = tt // LANES
    win = min(-(-(2 * LANES + 2 * SUBLANES) // GATHER_ROWS) * GATHER_ROWS, cap)
    assert win % GATHER_ROWS == 0 and (cap - win) % (2 * SUBLANES) == 0
    rspec = pl.BlockSpec((nsub, N_EXPERTS, LANES), lambda e, j, st: (j, 0, 0))
    return pl.pallas_call(
        functools.partial(_gather_kernel, cap, win, nsub),
        grid_spec=pltpu.PrefetchScalarGridSpec(
            num_scalar_prefetch=1,
            grid=(N_EXPERTS // GATHER_EXPERTS, n_tok // tt),
            in_specs=[rspec, rspec, pl.BlockSpec((tt, d), lambda e, j, st: (j, 0))],
            out_specs=pl.BlockSpec((GATHER_EXPERTS, cap, d), lambda e, j, st: (e, 0, 0))),
        out_shape=jax.ShapeDtypeStruct((N_EXPERTS, cap, d), BF16),
        compiler_params=_cparams(("parallel", "arbitrary")),
        name="gather_tokens",
    )(starts, sel_r, pos_r, h)


def _ffn_kernel(n_sets, *refs):
    x_refs = refs[:n_sets]
    wg_ref, wu_ref, wd_ref = refs[n_sets:n_sets + 3]
    o_refs = refs[n_sets + 3:2 * n_sets + 3]
    accs = refs[2 * n_sets + 3:]
    f = pl.program_id(1)
    wg = wg_ref[0].astype(BF16)
    wu = wu_ref[0].astype(BF16)
    wd = wd_ref[0].astype(BF16)
    for x_ref, o_ref, acc in zip(x_refs, o_refs, accs):
        @pl.when(f == 0)
        def _():
            acc[...] = jnp.zeros_like(acc)

        x = x_ref[0]
        hg = jnp.dot(x, wg, preferred_element_type=F32)
        hu = jnp.dot(x, wu, preferred_element_type=F32)
        hid = (_silu(hg) * hu).astype(BF16)
        acc[...] += jnp.dot(hid, wd, preferred_element_type=F32)

        @pl.when(f == pl.num_programs(1) - 1)
        def _():
            o_ref[0] = acc[...].astype(BF16)


def expert_ffn(xs_sets, wg, wu, wd):
    ne, _, d = xs_sets[0].shape
    ff = wg.shape[2]
    tf = FFN_COLS
    xspecs = [pl.BlockSpec((1, xs.shape[1], d), lambda e, f: (e, 0, 0)) for xs in xs_sets]
    return pl.pallas_call(
        functools.partial(_ffn_kernel, len(xs_sets)),
        grid=(ne, ff // tf),
        in_specs=xspecs + [pl.BlockSpec((1, d, tf), lambda e, f: (e, 0, f)),
                           pl.BlockSpec((1, d, tf), lambda e, f: (e, 0, f)),
                           pl.BlockSpec((1, tf, d), lambda e, f: (e, f, 0))],
        out_specs=xspecs,
        out_shape=[jax.ShapeDtypeStruct(xs.shape, BF16) for xs in xs_sets],
        scratch_shapes=[pltpu.VMEM(xs.shape[1:], F32) for xs in xs_sets],
        compiler_params=_cparams(("parallel", "arbitrary")),
        name="expert_ffn",
    )(*xs_sets, wg, wu, wd)


def _scatter_kernel(cap, win, nsub, starts_ref, sel_ref, pos_ref, aff_ref, ob_ref, y_ref):
    c = pl.program_id(0)
    e = pl.program_id(1)

    @pl.when(e == 0)
    def _():
        y_ref[...] = jnp.zeros_like(y_ref)

    mine = lax.broadcasted_iota(jnp.int32, (LANES, LANES), 1) == e
    for s in range(nsub):
        rows = slice(s * LANES, (s + 1) * LANES)

        def col(ref):
            return jnp.sum(jnp.where(mine, ref[rows, :], 0.0), axis=-1, keepdims=True)

        pcol, scol, gcol = col(pos_ref), col(sel_ref), col(aff_ref)
        s0 = _window_start(starts_ref, c * nsub + s, e, cap, win)
        slot = (lax.broadcasted_iota(jnp.int32, (LANES, win), 1) + s0).astype(F32)
        onehot = jnp.where(jnp.logical_and(slot == pcol, scol > 0.0), 1.0, 0.0).astype(BF16)
        y_ref[rows, :] += gcol * jnp.dot(onehot, ob_ref[0, pl.ds(s0, win), :],
                                         preferred_element_type=F32)


def scatter_combine(starts, sel_t, pos_t, aff_t, outbuf):
    n_tok = sel_t.shape[0]
    ne, cap, d = outbuf.shape
    chunk = min(SCATTER_TOKENS, n_tok)
    nsub = chunk // LANES
    win = min(2 * LANES, cap)
    tspec = pl.BlockSpec((chunk, LANES), lambda c, e, st: (c, 0))
    return pl.pallas_call(
        functools.partial(_scatter_kernel, cap, win, nsub),
        grid_spec=pltpu.PrefetchScalarGridSpec(
            num_scalar_prefetch=1,
            grid=(n_tok // chunk, ne),
            in_specs=[tspec, tspec, tspec, pl.BlockSpec((1, cap, d), lambda c, e, st: (e, 0, 0))],
            out_specs=pl.BlockSpec((chunk, d), lambda c, e, st: (c, 0))),
        out_shape=jax.ShapeDtypeStruct((n_tok, d), F32),
        compiler_params=_cparams(("parallel", "arbitrary")),
        name="scatter_combine",
    )(starts, sel_t, pos_t, aff_t, outbuf)


def _final_kernel(x1_ref, y_ref, gf_ref, fg_ref, o_ref):
    x2 = x1_ref[0] + gf_ref[0] * y_ref[0]
    o_ref[0] = x2 * lax.rsqrt(jnp.mean(x2 * x2, axis=-1, keepdims=True) + NORM_EPS) * fg_ref[...]


def final_norm(x1, y, gate_f, final_g):
    bsz, seq, d = x1.shape
    tl = TOKEN_ROWS
    per_b = gate_f.shape[0] > 1
    midx = (lambda b, i: (b, 0, 0)) if per_b else (lambda b, i: (0, 0, 0))
    xspec = pl.BlockSpec((1, tl, d), lambda b, i: (b, i, 0))
    return pl.pallas_call(
        _final_kernel,
        grid=(bsz, seq // tl),
        in_specs=[xspec, xspec, pl.BlockSpec((1, 1, d), midx), pl.BlockSpec((1, d), lambda b, i: (0, 0))],
        out_specs=xspec,
        out_shape=jax.ShapeDtypeStruct((bsz, seq, d), F32),
        compiler_params=_cparams(("parallel", "parallel")),
        name="final_norm",
    )(x1, y, gate_f, final_g.reshape(1, d))


def _pad_cols(w, width):
    return jnp.pad(w, ((0, 0), (0, width - w.shape[1])))


def _pad_rows(w, rows):
    return jnp.pad(w, ((0, rows - w.shape[0]), (0, 0)))


def _relayout_columns(w):
    n_ssd = 2 * SSD_WIDTH + 2 * SSD_GROUPS * SSD_STATE + 2 * SSD_HEADS
    ssd, rw = w[:, :n_ssd], w[:, n_ssd:]
    o = 3 * RWKV_WIDTH
    zx = 2 * SSD_WIDTH
    bc = zx + 2 * SSD_GROUPS * SSD_STATE
    parts = [ssd[:, :zx], rw[:, :o], ssd[:, zx:bc], _pad_cols(ssd[:, bc:], LANES),
             _pad_cols(rw[:, o:o + DECAY_LORA], LANES),
             _pad_cols(rw[:, o + DECAY_LORA:o + 2 * DECAY_LORA], LANES),
             _pad_cols(rw[:, o + 2 * DECAY_LORA:o + 2 * DECAY_LORA + ICLR_LORA], LANES),
             _pad_cols(rw[:, o + 2 * DECAY_LORA + ICLR_LORA:], 2 * LANES)]
    return jnp.concatenate(parts, axis=1)


def _layer_params(l, w_in, w_out, ssd_conv_w, ssd_conv_b, ssd_A_log, ssd_dt_bias, ssd_D, ssd_norm_g,
                  rwkv_mu, rwkv_w0, rwkv_w_up, rwkv_a0, rwkv_a_up, rwkv_g_up, rwkv_k_k, rwkv_k_a,
                  rwkv_r_k, rwkv_ln_g, rwkv_ln_b, router_w):
    row = lambda a: a.reshape(1, -1)
    n_ssd = 2 * SSD_WIDTH + 2 * SSD_GROUPS * SSD_STATE + 2 * SSD_HEADS
    mu_full = _relayout_columns(jnp.concatenate([jnp.zeros((1, n_ssd), F32), row(rwkv_mu[l])], axis=1))
    cw, cb = ssd_conv_w[l], row(ssd_conv_b[l])
    xe, be = SSD_WIDTH, SSD_WIDTH + SSD_GROUPS * SSD_STATE
    return dict(
        w_in=_relayout_columns(w_in[l]).astype(BF16),
        w_out=w_out[l].astype(BF16),
        cw_x=cw[:, :xe], cb_x=cb[:, :xe], cw_b=cw[:, xe:be], cb_b=cb[:, xe:be],
        cw_c=cw[:, be:], cb_c=cb[:, be:],
        dt_bias=_pad_cols(row(ssd_dt_bias[l]), LANES), a_log=_pad_cols(row(ssd_A_log[l]), LANES),
        d_exp=row(jnp.repeat(ssd_D[l], SSD_HEAD_DIM)), ssd_ng=row(ssd_norm_g[l]),
        mu_r=mu_full[:, COL_R:COL_K], mu_k=mu_full[:, COL_K:COL_V], mu_v=mu_full[:, COL_V:COL_B],
        mu_lora=mu_full[:, COL_LORA:],
        w_up_f=_pad_rows(rwkv_w_up[l, 0], LANES), w_up_b=_pad_rows(rwkv_w_up[l, 1], LANES),
        a_up=_pad_rows(rwkv_a_up[l], LANES), g_up=_pad_rows(rwkv_g_up[l], 2 * LANES),
        w0_f=row(rwkv_w0[l, 0]), w0_b=row(rwkv_w0[l, 1]), a0=row(rwkv_a0[l]),
        k_k=row(rwkv_k_k[l]), k_a=row(rwkv_k_a[l]), r_k=row(rwkv_r_k[l]),
        ln_g=row(rwkv_ln_g[l]), ln_b=row(rwkv_ln_b[l]),
        router_w=tuple(_pad_cols(t, LANES) for t in _split_bf16(router_w[l])),
    )


def _to_chain(x):
    lead = x.shape[:-3]
    bsz, seq, _ = x.shape[-3:]
    n = len(lead)
    x = x.reshape(lead + (bsz, seq, RWKV_HEADS, RWKV_HEAD_DIM))
    perm = tuple(range(n)) + (n + 1, n + 3, n, n + 2)
    return jnp.transpose(x, perm).reshape(lead + (seq, RWKV_HEAD_DIM, bsz * RWKV_HEADS))


def _from_chain(y, bsz):
    lead = y.shape[:-3]
    seq = y.shape[-3]
    n = len(lead)
    y = y.reshape(lead + (seq, RWKV_HEAD_DIM, bsz, RWKV_HEADS))
    perm = tuple(range(n)) + (n + 2, n, n + 3, n + 1)
    return jnp.transpose(y, perm).reshape(lead + (bsz, seq, RWKV_WIDTH))


def _state_to_chain(s):
    return jnp.transpose(s, (0, 4, 3, 1, 2)).reshape(2, RWKV_HEAD_DIM, RWKV_HEAD_DIM, -1)


def _state_from_chain(s, bsz):
    return jnp.transpose(s.reshape(2, RWKV_HEAD_DIM, RWKV_HEAD_DIM, bsz, RWKV_HEADS), (0, 3, 4, 2, 1))


def _scan_replicas(bsz):
    nch = bsz * RWKV_HEADS
    rep = max(1, LANES // nch)
    assert (nch * rep) % LANES == 0
    return rep


def rwkv_mixer_scan(r, k, v, w2, a, s0, p):
    rep, bsz, seq, _ = r.shape
    nh, hd = RWKV_HEADS, RWKV_HEAD_DIM
    fold = lambda t: t.reshape(t.shape[:-4] + (rep * bsz, seq, RWKV_WIDTH))
    rc, kc, ac, wc = (_to_chain(fold(t)) for t in (r, k, a, w2))
    tabs = [jnp.tile(p[n].reshape(nh, hd).T, (1, LANES // nh)) for n in ("k_k", "k_a")]
    if rep == 1:
        s0c = None if s0 is None else _state_to_chain(s0)
        y, st = rwkv_scan(rc, wc, kc, ac, _to_chain(v), *tabs, s0c)
        return _from_chain(y, bsz), _state_from_chain(st, bsz)

    vq = hd // rep
    vc = jnp.transpose(v.reshape(bsz, seq, nh, rep, vq), (1, 4, 3, 0, 2)).reshape(seq, vq, LANES)
    s0c = None
    if s0 is not None:
        s0c = jnp.transpose(s0.reshape(2, bsz, nh, rep, vq, hd), (0, 5, 4, 3, 1, 2)).reshape(
            2, hd, vq, LANES)
    y, st = rwkv_scan(rc, wc, kc, ac, vc, *tabs, s0c)
    y = jnp.transpose(y.reshape(2, seq, vq, rep, bsz, nh), (0, 4, 1, 5, 3, 2)).reshape(
        2, bsz, seq, RWKV_WIDTH)
    st = jnp.transpose(st.reshape(2, hd, vq, rep, bsz, nh), (0, 4, 5, 3, 2, 1)).reshape(
        2, bsz, nh, hd, hd)
    return y, st


def _grid_pos_embed(n_tokens, dim):
    rows = n_tokens // GRID_W
    row = jnp.repeat(jnp.arange(rows, dtype=F32), GRID_W)
    col = jnp.tile(jnp.arange(GRID_W, dtype=F32), rows)
    quarter = dim // 4
    freqs = jnp.exp(jnp.arange(quarter, dtype=F32) * (-math.log(POS_BASE) / quarter))

    def axis_embed(pos):
        ang = pos[:, None] * freqs[None, :]
        return jnp.concatenate([jnp.sin(ang), jnp.cos(ang)], axis=-1)

    return jnp.concatenate([axis_embed(row), axis_embed(col)], axis=-1)


def layer_to_dispatch(x, pos, mod, p, norm_mix_g, norm_ffn_g, states):
    bsz, seq, d = x.shape
    shift_m, scale_m, gate_m, shift_f, scale_f, gate_f = mod
    h, x0 = norm_modulate(x, pos, norm_mix_g, scale_m, shift_m)
    n_tok = bsz * seq
    proj = matmul_bf16(h.reshape(n_tok, d), p["w_in"], min(PROJ_ROWS, n_tok), PROJ_COLS)
    proj = proj.reshape(bsz, seq, N_PROJ)
    h0f, h0b, s0f, s0b = states
    y_ssd, hf, hb = ssd_mixer(proj, p, h0f, h0b)
    r, k, v, w2, a, g = rwkv_prep(proj, p, _scan_replicas(bsz))
    s0 = None if s0f is None else jnp.stack([s0f, s0b], axis=0)
    y2, st = rwkv_mixer_scan(r, k, v, w2, a, s0, p)
    y_rwkv = rwkv_post(y2, r[0], k[0], a[0], v, g, p)
    x1, h2, logits = out_proj(y_ssd, y_rwkv, p["w_out"], x0, gate_m, norm_ffn_g, scale_f, shift_f,
                              p["router_w"])
    cap = EC_CAPACITY * n_tok // N_EXPERTS
    sel_r, pos_r, sel_t, pos_t, aff_t = route(logits.reshape(n_tok, LANES), cap)
    starts = jnp.concatenate([pos_r[:, :, 0], jnp.full((1, N_EXPERTS), cap, F32)], axis=0)
    starts = starts.astype(jnp.int32).reshape(-1)
    xs = gather_tokens(starts, sel_r, pos_r, h2.reshape(n_tok, d), cap)
    routed = dict(xs=xs, starts=starts, sel_t=sel_t, pos_t=pos_t, aff_t=aff_t, x1=x1, gate_f=gate_f)
    return routed, (hf, hb, st[0], st[1])


def combine_and_finish(routed, outbuf, final_g):
    x1 = routed["x1"]
    y_ffn = scatter_combine(routed["starts"], routed["sel_t"], routed["pos_t"], routed["aff_t"], outbuf)
    return final_norm(x1, y_ffn.reshape(x1.shape), routed["gate_f"], final_g)


def kernel(x_prompt, x_sample, state_ssd_fwd, state_ssd_bwd, state_rwkv_fwd, state_rwkv_bwd, c, c_ctx,
           w_ada, b_ada, norm_mix_g, norm_ffn_g, w_in, w_out, ssd_conv_w, ssd_conv_b, ssd_A_log,
           ssd_dt_bias, ssd_D, ssd_norm_g, rwkv_mu, rwkv_w0, rwkv_w_up, rwkv_a0, rwkv_a_up, rwkv_g_up,
           rwkv_k_k, rwkv_k_a, rwkv_r_k, rwkv_ln_g, rwkv_ln_b, router_w, exp_w_gate, exp_w_up,
           exp_w_down, final_norm_g):
    depth = w_in.shape[0]
    assert depth == 1, "the final norm runs right after the single layer's FFN residual"
    l = 0
    dec_b = x_sample.shape[0]
    p = _layer_params(l, w_in, w_out, ssd_conv_w, ssd_conv_b, ssd_A_log, ssd_dt_bias, ssd_D, ssd_norm_g,
                      rwkv_mu, rwkv_w0, rwkv_w_up, rwkv_a0, rwkv_a_up, rwkv_g_up, rwkv_k_k, rwkv_k_a,
                      rwkv_r_k, rwkv_ln_g, rwkv_ln_b, router_w)
    cond = jnp.concatenate([c_ctx[None, :], c, jnp.zeros((16 - 1 - dec_b, D_MODEL), F32)], axis=0)
    mod = ada_modulation(cond, w_ada[l], b_ada[l]).reshape(16, N_MOD, 1, D_MODEL)
    mod_ctx = [mod[0:1, i] for i in range(N_MOD)]
    mod_lat = [mod[1:1 + dec_b, i] for i in range(N_MOD)]

    routed_ctx, st = layer_to_dispatch(x_prompt, None, mod_ctx, p, norm_mix_g[l], norm_ffn_g[l],
                                       (None, None, None, None))
    pos = _grid_pos_embed(x_sample.shape[1], D_MODEL)
    cached = (state_ssd_fwd[:, l], state_ssd_bwd[:, l], state_rwkv_fwd[:, l], state_rwkv_bwd[:, l])
    routed_lat, _ = layer_to_dispatch(x_sample, pos, mod_lat, p, norm_mix_g[l], norm_ffn_g[l], cached)
    out_ctx, out_lat = expert_ffn([routed_ctx["xs"], routed_lat["xs"]],
                                  exp_w_gate[l], exp_w_up[l], exp_w_down[l])
    y_prompt = combine_and_finish(routed_ctx, out_ctx, final_norm_g)
    y_sample = combine_and_finish(routed_lat, out_lat, final_norm_g)
    return (y_prompt, y_sample, st[0][:, None], st[1][:, None], st[2][:, None], st[3][:, None])
```

```python
import functools
import math

import jax
import jax.numpy as jnp
from jax import lax
from jax.experimental import pallas as pl
from jax.experimental.pallas import tpu as pltpu

F32 = jnp.float32
BF16 = jnp.bfloat16
HIGHEST = lax.Precision.HIGHEST

D_MODEL = 2048
GRID_W = 64
SSD_WIDTH = 1024
SSD_HEAD_DIM = 64
SSD_HEADS = 16
SSD_GROUPS = 2
SSD_STATE = 128
SSD_CONV_W = 5
SSD_CHUNK = 128
RWKV_WIDTH = 1024
RWKV_HEAD_DIM = 64
RWKV_HEADS = 16
DECAY_LORA = 64
ICLR_LORA = 64
GATE_LORA = 160
N_EXPERTS = 16
EC_CAPACITY = 2
EXPERT_FF = 1024
N_MOD = 6
NORM_EPS = 1e-6
GN_EPS = 64e-5
DECAY_SCALE = 0.606531
POS_BASE = 10000.0
SEARCH_ITERS = 48

LANES = 128
SUBLANES = 8
GATHER_ROWS = 64
GATHER_EXPERTS = 4

COL_Z = 0
COL_X = 1024
COL_R = 2048
COL_K = 3072
COL_V = 4096
COL_B = 5120
COL_C = 5376
COL_DT = 5632
COL_LORA = 5760
PREP_BLOCK_ELEMS = 256 * 1024
LORA_W = 640
N_PROJ = COL_LORA + LORA_W

VMEM_LIMIT = 56 * 1024 * 1024

ADA_COLS = 1024
TOKEN_ROWS = 256
PROJ_ROWS = 2048
PROJ_COLS = 640
OUT_PROJ_ROWS = 512
SCAN_STEPS = 32
GATHER_TOKENS = 1024
FFN_COLS = 256
SCATTER_TOKENS = 2048


def _cparams(sem):
    return pltpu.CompilerParams(dimension_semantics=sem, vmem_limit_bytes=VMEM_LIMIT)


def _silu(x):
    return x * jax.nn.sigmoid(x)


def _ada_kernel(c_ref, w_ref, b_ref, o_ref):
    s = _silu(c_ref[...]).astype(BF16)
    o_ref[...] = jnp.dot(s, w_ref[...].astype(BF16), preferred_element_type=F32) + b_ref[...]


def ada_modulation(cond, w, b):
    m, d = cond.shape
    n = w.shape[1]
    tn = ADA_COLS
    return pl.pallas_call(
        _ada_kernel,
        grid=(n // tn,),
        in_specs=[pl.BlockSpec((m, d), lambda j: (0, 0)),
                  pl.BlockSpec((d, tn), lambda j: (0, j)),
                  pl.BlockSpec((1, tn), lambda j: (0, j))],
        out_specs=pl.BlockSpec((m, tn), lambda j: (0, j)),
        out_shape=jax.ShapeDtypeStruct((m, n), F32),
        compiler_params=_cparams(("parallel",)),
        name="ada_modulation",
    )(cond, w, b.reshape(1, n))


def _norm_mod_kernel(has_pos, *refs):
    if has_pos:
        x_ref, pos_ref, g_ref, sc_ref, sh_ref, h_ref, x0_ref = refs
    else:
        x_ref, g_ref, sc_ref, sh_ref, h_ref = refs
    x = x_ref[0]
    if has_pos:
        x = x + pos_ref[...]
        x0_ref[0] = x
    y = x * lax.rsqrt(jnp.mean(x * x, axis=-1, keepdims=True) + NORM_EPS) * g_ref[...]
    h_ref[0] = (y * (1.0 + sc_ref[0]) + sh_ref[0]).astype(BF16)


def norm_modulate(x, pos, g, scale, shift):
    bsz, seq, d = x.shape
    tl = TOKEN_ROWS
    per_b = scale.shape[0] > 1
    midx = (lambda b, i: (b, 0, 0)) if per_b else (lambda b, i: (0, 0, 0))
    xspec = pl.BlockSpec((1, tl, d), lambda b, i: (b, i, 0))
    in_specs = [xspec]
    args = [x]
    if pos is not None:
        in_specs.append(pl.BlockSpec((tl, d), lambda b, i: (i, 0)))
        args.append(pos)
    in_specs += [pl.BlockSpec((1, d), lambda b, i: (0, 0)),
                 pl.BlockSpec((1, 1, d), midx), pl.BlockSpec((1, 1, d), midx)]
    args += [g.reshape(1, d), scale, shift]
    out_shape = [jax.ShapeDtypeStruct((bsz, seq, d), BF16)]
    out_specs = [xspec]
    if pos is not None:
        out_shape.append(jax.ShapeDtypeStruct((bsz, seq, d), F32))
        out_specs.append(xspec)
    res = pl.pallas_call(
        functools.partial(_norm_mod_kernel, pos is not None),
        grid=(bsz, seq // tl),
        in_specs=in_specs, out_specs=out_specs, out_shape=out_shape,
        compiler_params=_cparams(("parallel", "parallel")),
        name="norm_modulate",
    )(*args)
    return (res[0], res[1]) if pos is not None else (res[0], x)


def _mm_kernel(a_ref, b_ref, o_ref):
    o_ref[...] = jnp.dot(a_ref[...], b_ref[...], preferred_element_type=F32)


def matmul_bf16(a, b, tm, tn):
    m, k = a.shape
    n = b.shape[1]
    return pl.pallas_call(
        _mm_kernel,
        grid=(m // tm, n // tn),
        in_specs=[pl.BlockSpec((tm, k), lambda i, j: (i, 0)),
                  pl.BlockSpec((k, tn), lambda i, j: (0, j))],
        out_specs=pl.BlockSpec((tm, tn), lambda i, j: (i, j)),
        out_shape=jax.ShapeDtypeStruct((m, n), F32),
        compiler_params=_cparams(("parallel", "arbitrary")),
        name="in_proj",
    )(a, b)


def _shifted(u, off):
    n = u.shape[0]
    row = lax.broadcasted_iota(jnp.int32, (n, 1), 0)
    rolled = pltpu.roll(u, (-off) % n, axis=0)
    valid = jnp.logical_and(row + off >= 0, row + off < n)
    return jnp.where(valid, rolled, 0.0)


def _conv_silu(u, w_ref, b_ref, cols):
    pad = SSD_CONV_W // 2
    acc = u * w_ref[pad:pad + 1, cols] + b_ref[:, cols]
    for j in range(SSD_CONV_W):
        if j != pad:
            acc = acc + _shifted(u, j - pad) * w_ref[j:j + 1, cols]
    return _silu(acc)


def _softplus(x):
    return jnp.maximum(x, 0.0) + jnp.log1p(jnp.exp(-jnp.abs(x)))


def _ssd_kernel(nc, zero_init, *refs):
    q = SSD_CHUNK
    (z_ref, x_ref, b_ref, c_ref, dt_ref, cwx, cbx, cwb, cbb, cwc, cbc,
     dtb_ref, alog_ref, d_ref, ng_ref) = refs[:15]
    refs = refs[15:]
    if not zero_init:
        h0f_ref, h0b_ref = refs[:2]
        refs = refs[2:]
    y_ref, hf_ref, hb_ref, xh_s, bm_s, cm_s, dt_s, a_s, at_s, y_s, h_s = refs

    slab = 256
    for cb in range(SSD_WIDTH // slab):
        cols = slice(cb * slab, (cb + 1) * slab)
        xh = _conv_silu(x_ref[0, :, cols], cwx, cbx, cols)
        for c in range(nc):
            rows = slice(c * q, (c + 1) * q)
            xh_s[c, :, cols] = xh[rows]
            y_s[c, :, cols] = xh[rows] * d_ref[:, cols]
    allc = slice(0, SSD_GROUPS * SSD_STATE)
    bm = _conv_silu(b_ref[0], cwb, cbb, allc)
    cm = _conv_silu(c_ref[0], cwc, cbc, allc)
    dt = _softplus(dt_ref[0] + dtb_ref[...])
    a = dt * (-jnp.exp(alog_ref[...]))
    ii = lax.broadcasted_iota(jnp.int32, (q, q), 0)
    jj = lax.broadcasted_iota(jnp.int32, (q, q), 1)
    eye = (ii == jj).astype(F32)
    lower = ii >= jj
    upper = ii <= jj
    first_half = jj < SSD_HEAD_DIM
    for c in range(nc):
        rows = slice(c * q, (c + 1) * q)
        for g in range(SSD_GROUPS):
            gcols = slice(g * SSD_STATE, (g + 1) * SSD_STATE)
            bm_s[c, gcols, :] = bm[rows, gcols].T
        cm_s[c] = cm[rows]
        dt_s[c] = dt[rows]
        a_s[c] = a[rows]
        at_s[c] = lax.dot_general(eye, a[rows], (((1,), (1,)), ((), ())),
                                  precision=HIGHEST, preferred_element_type=F32)

    for d in range(2):
        mask = lower if d == 0 else upper
        tri = mask.astype(F32)
        tri_t = (upper if d == 0 else lower).astype(F32)
        if zero_init:
            h_s[...] = jnp.zeros_like(h_s)
        else:
            h0_ref = h0f_ref if d == 0 else h0b_ref
            for p in range(SSD_HEADS // 2):
                h_s[p] = jnp.concatenate([h0_ref[0, 2 * p], h0_ref[0, 2 * p + 1]], axis=0).T

        def chunk_body(ci, carry, d=d, mask=mask, tri=tri, tri_t=tri_t):
            c = ci if d == 0 else nc - 1 - ci
            a_c = a_s[c]
            cum = jnp.dot(tri, a_c, precision=HIGHEST, preferred_element_type=F32)
            cum_t = jnp.dot(at_s[c], tri_t, precision=HIGHEST, preferred_element_type=F32)
            dt_c = dt_s[c]
            for g in range(SSD_GROUPS):
                gcols = slice(g * SSD_STATE, (g + 1) * SSD_STATE)
                bt = bm_s[c, gcols, :]
                cg = cm_s[c, :, gcols]
                gram = jnp.dot(cg.astype(BF16), bt.astype(BF16), preferred_element_type=F32)
                for pp in range(SSD_HEADS // SSD_GROUPS // 2):
                    p = g * (SSD_HEADS // SSD_GROUPS // 2) + pp
                    pcols = slice(p * LANES, (p + 1) * LANES)
                    xh_pair = xh_s[c, :, pcols]
                    hprev = h_s[p]
                    hprev_b = hprev.astype(BF16)
                    parts = []
                    for e in range(2):
                        col = d * SSD_HEADS + 2 * p + e
                        mine = first_half if e == 0 else jnp.logical_not(first_half)
                        cc = jnp.broadcast_to(cum[:, col:col + 1], (q, q))
                        cr = cum_t[col:col + 1, :]
                        dec = jnp.exp(jnp.where(mask, cc - cr, -jnp.inf))
                        m = (gram * dec).astype(BF16)
                        xdt = jnp.where(mine, xh_pair * dt_c[:, col:col + 1], 0.0).astype(BF16)
                        tot = cum[q - 1:q, col:col + 1] if d == 0 else cum[0:1, col:col + 1]
                        bd_t = (bt * jnp.exp(tot - cr)).astype(BF16)
                        cd = (cg * jnp.exp(cc)).astype(BF16)
                        s_c = jnp.dot(bd_t, xdt, preferred_element_type=F32)
                        y_off = jnp.dot(cd, hprev_b, preferred_element_type=F32)
                        y_diag = jnp.dot(m, xdt, preferred_element_type=F32)
                        parts.append((s_c, y_off, y_diag, jnp.exp(tot)))
                    (s0, yo0, yd0, g0), (s1, yo1, yd1, g1) = parts
                    h_s[p] = hprev * jnp.where(first_half, g0, g1) + (s0 + s1)
                    y_s[c, :, pcols] += (yd0 + yd1) + jnp.where(first_half, yo0, yo1)
            return carry

        lax.fori_loop(0, nc, chunk_body, 0)
        out_ref = hf_ref if d == 0 else hb_ref
        for p in range(SSD_HEADS // 2):
            tr = h_s[p].T
            out_ref[0, 2 * p] = tr[0:SSD_HEAD_DIM]
            out_ref[0, 2 * p + 1] = tr[SSD_HEAD_DIM:]

    gw = SSD_WIDTH // SSD_GROUPS
    for c in range(nc):
        rows = slice(c * q, (c + 1) * q)
        for g in range(SSD_GROUPS):
            cols = slice(g * gw, (g + 1) * gw)
            y = y_s[c, :, cols] * _silu(z_ref[0, rows, cols])
            y = y * lax.rsqrt(jnp.mean(y * y, axis=-1, keepdims=True) + NORM_EPS)
            y_ref[0, rows, cols] = (y * ng_ref[:, cols]).astype(BF16)


def ssd_mixer(proj, p, h0f, h0b):
    bsz, seq, _ = proj.shape
    nc = seq // SSD_CHUNK
    zero_init = h0f is None

    def col_spec(width, start):
        blk = start // width
        return pl.BlockSpec((1, seq, width), lambda b: (b, 0, blk))

    def full2(arr):
        return pl.BlockSpec(arr.shape, lambda b: (0, 0))

    st_spec = pl.BlockSpec((1, SSD_HEADS, SSD_HEAD_DIM, SSD_STATE), lambda b: (b, 0, 0, 0))
    bc = SSD_GROUPS * SSD_STATE
    small = [p["cw_x"], p["cb_x"], p["cw_b"], p["cb_b"], p["cw_c"], p["cb_c"],
             p["dt_bias"], p["a_log"], p["d_exp"], p["ssd_ng"]]
    in_specs = [col_spec(SSD_WIDTH, COL_Z), col_spec(SSD_WIDTH, COL_X), col_spec(bc, COL_B),
                col_spec(bc, COL_C), col_spec(LANES, COL_DT)] + [full2(s) for s in small]
    args = [proj] * 5 + small
    if not zero_init:
        in_specs += [st_spec, st_spec]
        args += [h0f, h0b]
    st_shape = jax.ShapeDtypeStruct((bsz, SSD_HEADS, SSD_HEAD_DIM, SSD_STATE), F32)
    q = SSD_CHUNK
    return pl.pallas_call(
        functools.partial(_ssd_kernel, nc, zero_init),
        grid=(bsz,),
        in_specs=in_specs,
        out_specs=[pl.BlockSpec((1, seq, SSD_WIDTH), lambda b: (b, 0, 0)), st_spec, st_spec],
        out_shape=[jax.ShapeDtypeStruct((bsz, seq, SSD_WIDTH), BF16), st_shape, st_shape],
        scratch_shapes=[pltpu.VMEM((nc, q, SSD_WIDTH), F32), pltpu.VMEM((nc, bc, q), F32),
                        pltpu.VMEM((nc, q, bc), F32), pltpu.VMEM((nc, q, LANES), F32),
                        pltpu.VMEM((nc, q, LANES), F32), pltpu.VMEM((nc, LANES, q), F32),
                        pltpu.VMEM((nc, q, SSD_WIDTH), F32),
                        pltpu.VMEM((SSD_HEADS // 2, SSD_STATE, 2 * SSD_HEAD_DIM), F32)],
        compiler_params=_cparams(("parallel",)),
        name="ssd_mixer",
    )(*args)


def _seg_sum(x):
    lane = lax.broadcasted_iota(jnp.int32, x.shape, 1)
    first = lane < RWKV_HEAD_DIM
    s0 = jnp.sum(jnp.where(first, x, 0.0), axis=-1, keepdims=True)
    s1 = jnp.sum(jnp.where(first, 0.0, x), axis=-1, keepdims=True)
    return jnp.where(first, s0, s1)


def _shift_mix(u, mu):
    return u + mu * (0.5 * (_shifted(u, -1) + _shifted(u, 1)) - u)


def _rwkv_prep_kernel(ncb, rep, *refs):
    (r_ref, k_ref, v_ref, lo_ref, mur, muk, muv, mul, wupf, wupb, aup, gup,
     w0f, w0b, a0, ro, ko, vo, wo, ao, go) = refs[:21]
    staged = dict(zip(("r", "k", "a", "wf", "wb"), refs[21:]))
    q = pl.program_id(2)

    def put(name, cols, val):
        if staged:
            staged[name][:, cols] = val
        elif name in ("wf", "wb"):
            wo[0 if name == "wf" else 1, 0, 0, :, cols] = val
        else:
            {"r": ro, "k": ko, "a": ao}[name][0, 0, :, cols] = val

    def compute():
        lo = _shift_mix(lo_ref[0], mul[...])
        wdf = jnp.tanh(lo[:, 0:128]).astype(BF16)
        wdb = jnp.tanh(lo[:, 128:256]).astype(BF16)
        ad = lo[:, 256:384].astype(BF16)
        gd = jax.nn.sigmoid(lo[:, 384:640]).astype(BF16)
        for cb in range(ncb):
            cols = slice(cb * LANES, (cb + 1) * LANES)

            def mm(x, w_ref, cols=cols):
                return jnp.dot(x, w_ref[:, cols].astype(BF16), preferred_element_type=F32)

            r = _shift_mix(r_ref[0, :, cols], mur[:, cols])
            k = _shift_mix(k_ref[0, :, cols], muk[:, cols])
            v = _shift_mix(v_ref[0, :, cols], muv[:, cols])
            put("wf", cols, jnp.exp(-DECAY_SCALE * jax.nn.sigmoid(w0f[:, cols] + mm(wdf, wupf))))
            put("wb", cols, jnp.exp(-DECAY_SCALE * jax.nn.sigmoid(w0b[:, cols] + mm(wdb, wupb))))
            put("a", cols, jax.nn.sigmoid(a0[:, cols] + mm(ad, aup)))
            go[0, :, cols] = mm(gd, gup)
            put("r", cols, r)
            put("k", cols, k)
            vo[0, :, cols] = v

    if staged:
        pl.when(q == 0)(compute)
        ro[0, 0] = staged["r"][...]
        ko[0, 0] = staged["k"][...]
        ao[0, 0] = staged["a"][...]
        wo[0, 0, 0] = staged["wf"][...]
        wo[1, 0, 0] = staged["wb"][...]
    else:
        compute()


def rwkv_prep(proj, p, rep):
    bsz, seq, _ = proj.shape
    cw = min(RWKV_WIDTH, PREP_BLOCK_ELEMS // seq)
    nblk = RWKV_WIDTH // cw

    def col_spec(start):
        blk = start // cw
        return pl.BlockSpec((1, seq, cw), lambda b, j, q: (b, 0, blk + j))

    vec = pl.BlockSpec((1, cw), lambda b, j, q: (0, j))

    def mat(rows):
        return pl.BlockSpec((rows, cw), lambda b, j, q: (0, j))

    tok_spec = pl.BlockSpec((1, seq, cw), lambda b, j, q: (b, 0, j))
    rep_spec = pl.BlockSpec((1, 1, seq, cw), lambda b, j, q: (q, b, 0, j))
    w_spec = pl.BlockSpec((2, 1, 1, seq, cw), lambda b, j, q: (0, q, b, 0, j))
    tok_sh = jax.ShapeDtypeStruct((bsz, seq, RWKV_WIDTH), F32)
    rep_sh = jax.ShapeDtypeStruct((rep, bsz, seq, RWKV_WIDTH), F32)
    w_sh = jax.ShapeDtypeStruct((2, rep, bsz, seq, RWKV_WIDTH), F32)
    return pl.pallas_call(
        functools.partial(_rwkv_prep_kernel, cw // LANES, rep),
        grid=(bsz, nblk, rep),
        in_specs=[col_spec(COL_R), col_spec(COL_K), col_spec(COL_V),
                  pl.BlockSpec((1, seq, LORA_W), lambda b, j, q: (b, 0, COL_LORA // LORA_W)),
                  vec, vec, vec, pl.BlockSpec((1, LORA_W), lambda b, j, q: (0, 0)),
                  mat(128), mat(128), mat(128), mat(256),
                  vec, vec, vec],
        out_specs=[rep_spec, rep_spec, tok_spec, w_spec, rep_spec, tok_spec],
        out_shape=[rep_sh, rep_sh, tok_sh, w_sh, rep_sh, tok_sh],
        scratch_shapes=[pltpu.VMEM((seq, cw), F32)] * (5 if rep > 1 else 0),
        compiler_params=_cparams(("parallel", "arbitrary", "arbitrary")),
        name="rwkv_prep",
    )(proj, proj, proj, proj, p["mu_r"], p["mu_k"], p["mu_v"], p["mu_lora"],
      p["w_up_f"], p["w_up_b"], p["a_up"], p["g_up"],
      p["w0_f"], p["w0_b"], p["a0"])


def _rwkv_scan_kernel(tb, zero_init, *refs):
    if zero_init:
        r_ref, w_ref, kr_ref, a_ref, v_ref, kk_ref, ka_ref, y_ref, st_ref, s_s, k_ref, al_ref, be_ref = refs
    else:
        (r_ref, w_ref, kr_ref, a_ref, v_ref, kk_ref, ka_ref, s0_ref, y_ref, st_ref,
         s_s, k_ref, al_ref, be_ref) = refs
    kd = RWKV_HEAD_DIM
    d = pl.program_id(0)
    step_i = pl.program_id(2)

    def time_index(i):
        return i + d * (tb - 1 - 2 * i)

    @pl.when(step_i == 0)
    def _():
        if zero_init:
            s_s[...] = jnp.zeros_like(s_s)
        else:
            s_s[...] = s0_ref[...]

    def derive(t, carry):
        kraw = kr_ref[t]
        a = a_ref[t]
        kk = kraw * kk_ref[...]
        kk = kk * (1.0 / jnp.maximum(jnp.sqrt(jnp.sum(kk * kk, axis=0, keepdims=True)), 1e-12))
        al_ref[t] = -kk
        be_ref[t] = kk * a
        k_ref[t] = kraw * (1.0 + (a - 1.0) * ka_ref[...])
        return carry

    lax.fori_loop(0, tb, derive, 0, unroll=4)

    t0 = time_index(0)
    sa0 = jnp.zeros(s_s.shape[1:], F32)
    sa1 = jnp.zeros(s_s.shape[1:], F32)
    for kq in range(0, kd, 2):
        sa0 = sa0 + s_s[kq] * al_ref[t0, kq:kq + 1, :]
        sa1 = sa1 + s_s[kq + 1] * al_ref[t0, kq + 1:kq + 2, :]

    def step(i, sa):
        t = time_index(i)
        tn = time_index(jnp.minimum(i + 1, tb - 1))
        vt = v_ref[t]
        y = jnp.zeros_like(sa)
        sa_next = jnp.zeros_like(sa)
        for kq in range(kd):
            row = slice(kq, kq + 1)
            sk = s_s[kq] * w_ref[t, row, :] + sa * be_ref[t, row, :] + vt * k_ref[t, row, :]
            s_s[kq] = sk
            y = y + sk * r_ref[t, row, :]
            sa_next = sa_next + sk * al_ref[tn, row, :]
        y_ref[t] = y
        return sa_next

    lax.fori_loop(0, tb, step, sa0 + sa1)

    @pl.when(step_i == pl.num_programs(2) - 1)
    def _():
        st_ref[...] = s_s[...]


def rwkv_scan(r, w, k, a, v, kk_tab, ka_tab, s0):
    seq, kd, nch = r.shape
    vv = v.shape[1]
    tb = SCAN_STEPS
    nblk = seq // tb
    zero_init = s0 is None

    def tblk(d, i):
        return i + d * (nblk - 1 - 2 * i)

    kspec = pl.BlockSpec((tb, kd, LANES), lambda d, g, i: (tblk(d, i), 0, g))
    wspec = pl.BlockSpec((None, tb, kd, LANES), lambda d, g, i: (d, tblk(d, i), 0, g))
    vspec = pl.BlockSpec((tb, vv, LANES), lambda d, g, i: (tblk(d, i), 0, g))
    yspec = pl.BlockSpec((None, tb, vv, LANES), lambda d, g, i: (d, tblk(d, i), 0, g))
    sspec = pl.BlockSpec((None, kd, vv, LANES), lambda d, g, i: (d, 0, 0, g))
    tabspec = pl.BlockSpec((kd, LANES), lambda d, g, i: (0, 0))
    in_specs = [kspec, wspec, kspec, kspec, vspec, tabspec, tabspec]
    args = [r, w, k, a, v, kk_tab, ka_tab]
    if not zero_init:
        in_specs.append(sspec)
        args.append(s0)
    return pl.pallas_call(
        functools.partial(_rwkv_scan_kernel, tb, zero_init),
        grid=(2, nch // LANES, nblk),
        in_specs=in_specs,
        out_specs=[yspec, sspec],
        out_shape=[jax.ShapeDtypeStruct((2, seq, vv, nch), F32),
                   jax.ShapeDtypeStruct((2, kd, vv, nch), F32)],
        scratch_shapes=[pltpu.VMEM((kd, vv, LANES), F32)] + [pltpu.VMEM((tb, kd, LANES), F32)] * 3,
        compiler_params=_cparams(("parallel", "parallel", "arbitrary")),
        name="rwkv_scan",
    )(*args)


def _rwkv_post_kernel(y_ref, r_ref, k_ref, a_ref, v_ref, g_ref, lng, lnb, rk, ka, o_ref):
    inv = 1.0 / RWKV_HEAD_DIM
    for cb in range(RWKV_WIDTH // LANES):
        cols = slice(cb * LANES, (cb + 1) * LANES)
        y = y_ref[0, 0, :, cols] + y_ref[1, 0, :, cols]
        mean = _seg_sum(y) * inv
        yc = y - mean
        var = _seg_sum(yc * yc) * inv
        yn = yc * lax.rsqrt(var + GN_EPS) * lng[:, cols] + lnb[:, cols]
        k = k_ref[0, :, cols] * (1.0 + (a_ref[0, :, cols] - 1.0) * ka[:, cols])
        bonus = _seg_sum(r_ref[0, :, cols] * k * rk[:, cols]) * v_ref[0, :, cols]
        o_ref[0, :, cols] = ((yn + bonus) * g_ref[0, :, cols]).astype(BF16)


def rwkv_post(y2, r, k, a, v, g, p):
    _, bsz, seq, wd = y2.shape
    tl = TOKEN_ROWS
    spec = pl.BlockSpec((1, tl, wd), lambda b, i: (b, i, 0))
    vec = pl.BlockSpec((1, wd), lambda b, i: (0, 0))
    return pl.pallas_call(
        _rwkv_post_kernel,
        grid=(bsz, seq // tl),
        in_specs=[pl.BlockSpec((2, 1, tl, wd), lambda b, i: (0, b, i, 0))] + [spec] * 5 + [vec] * 4,
        out_specs=spec,
        out_shape=jax.ShapeDtypeStruct((bsz, seq, wd), BF16),
        compiler_params=_cparams(("parallel", "parallel")),
        name="rwkv_post",
    )(y2, r, k, a, v, g, p["ln_g"], p["ln_b"], p["r_k"], p["k_a"])


def _split_bf16(x):
    hi = x.astype(BF16)
    return hi, (x - hi.astype(F32)).astype(BF16)


def _out_proj_kernel(ys_ref, yr_ref, wo_ref, x_ref, gm_ref, ng_ref, sc_ref, sh_ref, rwh_ref, rwl_ref,
                     x1_ref, h2_ref, lg_ref):
    acc = jnp.dot(ys_ref[0], wo_ref[0:SSD_WIDTH, :], preferred_element_type=F32)
    acc = acc + jnp.dot(yr_ref[0], wo_ref[SSD_WIDTH:, :], preferred_element_type=F32)
    x1 = x_ref[0] + gm_ref[0] * acc
    x1_ref[0] = x1
    hn = x1 * lax.rsqrt(jnp.mean(x1 * x1, axis=-1, keepdims=True) + NORM_EPS) * ng_ref[...]
    h2 = hn * (1.0 + sc_ref[0]) + sh_ref[0]
    hi, lo = _split_bf16(h2)
    h2_ref[0] = hi
    lg_ref[0] = (jnp.dot(hi, rwh_ref[...], preferred_element_type=F32)
                 + jnp.dot(lo, rwh_ref[...], preferred_element_type=F32)
                 + jnp.dot(hi, rwl_ref[...], preferred_element_type=F32))


def out_proj(y_ssd, y_rwkv, w_out, x, gate_m, norm_g, scale_f, shift_f, router_w):
    bsz, seq, d = x.shape
    tm = min(OUT_PROJ_ROWS, seq)
    per_b = gate_m.shape[0] > 1
    midx = (lambda b, i: (b, 0, 0)) if per_b else (lambda b, i: (0, 0, 0))
    half = pl.BlockSpec((1, tm, d // 2), lambda b, i: (b, i, 0))
    full = pl.BlockSpec((1, tm, d), lambda b, i: (b, i, 0))
    mspec = pl.BlockSpec((1, 1, d), midx)
    return pl.pallas_call(
        _out_proj_kernel,
        grid=(bsz, seq // tm),
        in_specs=[half, half, pl.BlockSpec((d, d), lambda b, i: (0, 0)), full, mspec,
                  pl.BlockSpec((1, d), lambda b, i: (0, 0)), mspec, mspec,
                  pl.BlockSpec((d, LANES), lambda b, i: (0, 0)),
                  pl.BlockSpec((d, LANES), lambda b, i: (0, 0))],
        out_specs=[full, full, pl.BlockSpec((1, tm, LANES), lambda b, i: (b, i, 0))],
        out_shape=[jax.ShapeDtypeStruct((bsz, seq, d), F32), jax.ShapeDtypeStruct((bsz, seq, d), BF16),
                   jax.ShapeDtypeStruct((bsz, seq, LANES), F32)],
        compiler_params=_cparams(("parallel", "parallel")),
        name="out_proj",
    )(y_ssd, y_rwkv, w_out, x, gate_m, norm_g.reshape(1, d), scale_f, shift_f, *router_w)


def _route_kernel(nt, cap, lg_ref, selr_ref, posr_ref, selt_ref, post_ref, afft_ref, aff_s, pre_s):
    ne = N_EXPERTS
    lane = lax.broadcasted_iota(jnp.int32, (LANES, LANES), 1)
    sub = lax.broadcasted_iota(jnp.int32, (LANES, LANES), 0)
    upper_incl = (sub <= lane).astype(BF16)
    for i in range(nt):
        lg = jnp.where(lane < ne, lg_ref[i * LANES:(i + 1) * LANES, :], -jnp.inf)
        e = jnp.exp(lg - jnp.max(lg, axis=-1, keepdims=True))
        aff = e / jnp.sum(e, axis=-1, keepdims=True)
        afft_ref[i * LANES:(i + 1) * LANES, :] = aff
        aff_s[i] = aff.T[0:ne, :]
    aff3 = aff_s[...]

    def count(mask):
        s = jnp.sum(mask.astype(F32), axis=0, keepdims=True)
        return jnp.sum(s, axis=2, keepdims=True)

    def search(_, carry):
        lo, hi = carry
        mid = 0.5 * (lo + hi)
        ok = count(aff3 >= mid) >= cap
        return jnp.where(ok, mid, lo), jnp.where(ok, hi, mid)

    lo0 = jnp.zeros((1, ne, 1), F32)
    hi0 = jnp.full((1, ne, 1), 2.0, F32)
    lo, _ = lax.fori_loop(0, SEARCH_ITERS, search, (lo0, hi0))
    cand = jnp.where(aff3 >= lo, aff3, 4.0)
    thr = jnp.min(jnp.min(cand, axis=0, keepdims=True), axis=2, keepdims=True)
    gt = (aff3 > thr).astype(F32)
    eq = (aff3 == thr).astype(F32)
    need = cap - count(aff3 > thr)[0]

    def prefix_excl(m3):
        off = jnp.zeros((ne, 1), F32)
        for i in range(nt):
            inc = jnp.dot(m3[i].astype(BF16), upper_incl, preferred_element_type=F32)
            pre_s[i] = inc - m3[i] + off
            off = off + inc[:, LANES - 1:LANES]
        return pre_s[...]

    sel = jnp.maximum(gt, eq * (prefix_excl(eq) < need[None]).astype(F32))
    pos = prefix_excl(sel)
    selr_ref[...] = sel
    posr_ref[...] = pos
    zpad = jnp.zeros((LANES - ne, LANES), F32)
    for i in range(nt):
        rows = slice(i * LANES, (i + 1) * LANES)
        selt_ref[rows, :] = jnp.concatenate([sel[i], zpad], axis=0).T
        post_ref[rows, :] = jnp.concatenate([pos[i], zpad], axis=0).T


def route(logits, cap):
    n_tok = logits.shape[0]
    nt = n_tok // LANES
    row_sh = jax.ShapeDtypeStruct((nt, N_EXPERTS, LANES), F32)
    tm_sh = jax.ShapeDtypeStruct((n_tok, LANES), F32)
    return pl.pallas_call(
        functools.partial(_route_kernel, nt, cap),
        out_shape=[row_sh, row_sh, tm_sh, tm_sh, tm_sh],
        scratch_shapes=[pltpu.VMEM((nt, N_EXPERTS, LANES), F32), pltpu.VMEM((nt, N_EXPERTS, LANES), F32)],
        compiler_params=pltpu.CompilerParams(vmem_limit_bytes=VMEM_LIMIT),
        name="route",
    )(logits)


def _window_start(starts_ref, tile, e, cap, win, align=SUBLANES):
    ps = starts_ref[tile * N_EXPERTS + e]
    return pl.multiple_of(jnp.minimum((ps // align) * align, cap - win), align)


def _gather_kernel(cap, win, nsub, starts_ref, sel_ref, pos_ref, h_ref, o_ref):
    eg = pl.program_id(0)
    j = pl.program_id(1)

    @pl.when(j == 0)
    def _():
        o_ref[...] = jnp.zeros_like(o_ref)

    for el in range(GATHER_EXPERTS):
        e = eg * GATHER_EXPERTS + el
        for s in range(0, nsub, 2):
            tile = j * nsub + s
            s0 = _window_start(starts_ref, tile, e, cap, win, align=2 * SUBLANES)
            lo = starts_ref[tile * N_EXPERTS + e]
            hi = starts_ref[(tile + 2) * N_EXPERTS + e]
            prows = [pos_ref[s + u, pl.ds(e, 1), :] for u in range(2)]
            srows = [sel_ref[s + u, pl.ds(e, 1), :] for u in range(2)]
            for piece in range(win // GATHER_ROWS):
                p0 = pl.multiple_of(s0 + piece * GATHER_ROWS, 2 * SUBLANES)

                @pl.when(jnp.logical_and(p0 < hi, p0 + GATHER_ROWS > lo))
                def _(p0=p0, s=s, el=el, prows=prows, srows=srows):
                    slot = (lax.broadcasted_iota(jnp.int32, (GATHER_ROWS, LANES), 0) + p0).astype(F32)
                    onehot = jnp.concatenate(
                        [jnp.where(jnp.logical_and(slot == prows[u], srows[u] > 0.0), 1.0, 0.0).astype(BF16)
                         for u in range(2)], axis=1)
                    rows = pl.ds(p0, GATHER_ROWS)
                    got = jnp.dot(onehot, h_ref[s * LANES:(s + 2) * LANES, :], preferred_element_type=F32)
                    o_ref[el, rows, :] = (o_ref[el, rows, :].astype(F32) + got).astype(BF16)


def gather_tokens(starts, sel_r, pos_r, h, cap):
    n_tok, d = h.shape
    tt = min(GATHER_TOKENS, n_tok)
    nsub = tt // LANES
    win = min(-(-(2 * LANES + 2 * SUBLANES) // GATHER_ROWS) * GATHER_ROWS, cap)
    assert win % GATHER_ROWS == 0 and (cap - win) % (2 * SUBLANES) == 0
    rspec = pl.BlockSpec((nsub, N_EXPERTS, LANES), lambda e, j, st: (j, 0, 0))
    return pl.pallas_call(
        functools.partial(_gather_kernel, cap, win, nsub),
        grid_spec=pltpu.PrefetchScalarGridSpec(
            num_scalar_prefetch=1,
            grid=(N_EXPERTS // GATHER_EXPERTS, n_tok // tt),
            in_specs=[rspec, rspec, pl.BlockSpec((tt, d), lambda e, j, st: (j, 0))],
            out_specs=pl.BlockSpec((GATHER_EXPERTS, cap, d), lambda e, j, st: (e, 0, 0))),
        out_shape=jax.ShapeDtypeStruct((N_EXPERTS, cap, d), BF16),
        compiler_params=_cparams(("parallel", "arbitrary")),
        name="gather_tokens",
    )(starts, sel_r, pos_r, h)


def _ffn_kernel(n_sets, *refs):
    x_refs = refs[:n_sets]
    wg_ref, wu_ref, wd_ref = refs[n_sets:n_sets + 3]
    o_refs = refs[n_sets + 3:2 * n_sets + 3]
    accs = refs[2 * n_sets + 3:]
    f = pl.program_id(1)
    wg = wg_ref[0].astype(BF16)
    wu = wu_ref[0].astype(BF16)
    wd = wd_ref[0].astype(BF16)
    for x_ref, o_ref, acc in zip(x_refs, o_refs, accs):
        @pl.when(f == 0)
        def _():
            acc[...] = jnp.zeros_like(acc)

        x = x_ref[0]
        hg = jnp.dot(x, wg, preferred_element_type=F32)
        hu = jnp.dot(x, wu, preferred_element_type=F32)
        hid = (_silu(hg) * hu).astype(BF16)
        acc[...] += jnp.dot(hid, wd, preferred_element_type=F32)

        @pl.when(f == pl.num_programs(1) - 1)
        def _():
            o_ref[0] = acc[...].astype(BF16)


def expert_ffn(xs_sets, wg, wu, wd):
    ne, _, d = xs_sets[0].shape
    ff = wg.shape[2]
    tf = FFN_COLS
    xspecs = [pl.BlockSpec((1, xs.shape[1], d), lambda e, f: (e, 0, 0)) for xs in xs_sets]
    return pl.pallas_call(
        functools.partial(_ffn_kernel, len(xs_sets)),
        grid=(ne, ff // tf),
        in_specs=xspecs + [pl.BlockSpec((1, d, tf), lambda e, f: (e, 0, f)),
                           pl.BlockSpec((1, d, tf), lambda e, f: (e, 0, f)),
                           pl.BlockSpec((1, tf, d), lambda e, f: (e, f, 0))],
        out_specs=xspecs,
        out_shape=[jax.ShapeDtypeStruct(xs.shape, BF16) for xs in xs_sets],
        scratch_shapes=[pltpu.VMEM(xs.shape[1:], F32) for xs in xs_sets],
        compiler_params=_cparams(("parallel", "arbitrary")),
        name="expert_ffn",
    )(*xs_sets, wg, wu, wd)


def _scatter_kernel(cap, win, nsub, starts_ref, sel_ref, pos_ref, aff_ref, ob_ref, y_ref):
    c = pl.program_id(0)
    e = pl.program_id(1)

    @pl.when(e == 0)
    def _():
        y_ref[...] = jnp.zeros_like(y_ref)

    mine = lax.broadcasted_iota(jnp.int32, (LANES, LANES), 1) == e
    for s in range(nsub):
        rows = slice(s * LANES, (s + 1) * LANES)

        def col(ref):
            return jnp.sum(jnp.where(mine, ref[rows, :], 0.0), axis=-1, keepdims=True)

        pcol, scol, gcol = col(pos_ref), col(sel_ref), col(aff_ref)
        s0 = _window_start(starts_ref, c * nsub + s, e, cap, win)
        slot = (lax.broadcasted_iota(jnp.int32, (LANES, win), 1) + s0).astype(F32)
        onehot = jnp.where(jnp.logical_and(slot == pcol, scol > 0.0), 1.0, 0.0).astype(BF16)
        y_ref[rows, :] += gcol * jnp.dot(onehot, ob_ref[0, pl.ds(s0, win), :],
                                         preferred_element_type=F32)


def scatter_combine(starts, sel_t, pos_t, aff_t, outbuf):
    n_tok = sel_t.shape[0]
    ne, cap, d = outbuf.shape
    chunk = min(SCATTER_TOKENS, n_tok)
    nsub = chunk // LANES
    win = min(LANES + 2 * SUBLANES, cap)
    assert (cap - win) % SUBLANES == 0
    tspec = pl.BlockSpec((chunk, LANES), lambda c, e, st: (c, 0))
    return pl.pallas_call(
        functools.partial(_scatter_kernel, cap, win, nsub),
        grid_spec=pltpu.PrefetchScalarGridSpec(
            num_scalar_prefetch=1,
            grid=(n_tok // chunk, ne),
            in_specs=[tspec, tspec, tspec, pl.BlockSpec((1, cap, d), lambda c, e, st: (e, 0, 0))],
            out_specs=pl.BlockSpec((chunk, d), lambda c, e, st: (c, 0))),
        out_shape=jax.ShapeDtypeStruct((n_tok, d), F32),
        compiler_params=_cparams(("parallel", "arbitrary")),
        name="scatter_combine",
    )(starts, sel_t, pos_t, aff_t, outbuf)


def _final_kernel(x1_ref, y_ref, gf_ref, fg_ref, o_ref):
    x2 = x1_ref[0] + gf_ref[0] * y_ref[0]
    o_ref[0] = x2 * lax.rsqrt(jnp.mean(x2 * x2, axis=-1, keepdims=True) + NORM_EPS) * fg_ref[...]


def final_norm(x1, y, gate_f, final_g):
    bsz, seq, d = x1.shape
    tl = TOKEN_ROWS
    per_b = gate_f.shape[0] > 1
    midx = (lambda b, i: (b, 0, 0)) if per_b else (lambda b, i: (0, 0, 0))
    xspec = pl.BlockSpec((1, tl, d), lambda b, i: (b, i, 0))
    return pl.pallas_call(
        _final_kernel,
        grid=(bsz, seq // tl),
        in_specs=[xspec, xspec, pl.BlockSpec((1, 1, d), midx), pl.BlockSpec((1, d), lambda b, i: (0, 0))],
        out_specs=xspec,
        out_shape=jax.ShapeDtypeStruct((bsz, seq, d), F32),
        compiler_params=_cparams(("parallel", "parallel")),
        name="final_norm",
    )(x1, y, gate_f, final_g.reshape(1, d))


def _pad_cols(w, width):
    return jnp.pad(w, ((0, 0), (0, width - w.shape[1])))


def _pad_rows(w, rows):
    return jnp.pad(w, ((0, rows - w.shape[0]), (0, 0)))


def _relayout_columns(w):
    n_ssd = 2 * SSD_WIDTH + 2 * SSD_GROUPS * SSD_STATE + 2 * SSD_HEADS
    ssd, rw = w[:, :n_ssd], w[:, n_ssd:]
    o = 3 * RWKV_WIDTH
    zx = 2 * SSD_WIDTH
    bc = zx + 2 * SSD_GROUPS * SSD_STATE
    parts = [ssd[:, :zx], rw[:, :o], ssd[:, zx:bc], _pad_cols(ssd[:, bc:], LANES),
             _pad_cols(rw[:, o:o + DECAY_LORA], LANES),
             _pad_cols(rw[:, o + DECAY_LORA:o + 2 * DECAY_LORA], LANES),
             _pad_cols(rw[:, o + 2 * DECAY_LORA:o + 2 * DECAY_LORA + ICLR_LORA], LANES),
             _pad_cols(rw[:, o + 2 * DECAY_LORA + ICLR_LORA:], 2 * LANES)]
    return jnp.concatenate(parts, axis=1)


def _layer_params(l, w_in, w_out, ssd_conv_w, ssd_conv_b, ssd_A_log, ssd_dt_bias, ssd_D, ssd_norm_g,
                  rwkv_mu, rwkv_w0, rwkv_w_up, rwkv_a0, rwkv_a_up, rwkv_g_up, rwkv_k_k, rwkv_k_a,
                  rwkv_r_k, rwkv_ln_g, rwkv_ln_b, router_w):
    row = lambda a: a.reshape(1, -1)
    n_ssd = 2 * SSD_WIDTH + 2 * SSD_GROUPS * SSD_STATE + 2 * SSD_HEADS
    mu_full = _relayout_columns(jnp.concatenate([jnp.zeros((1, n_ssd), F32), row(rwkv_mu[l])], axis=1))
    cw, cb = ssd_conv_w[l], row(ssd_conv_b[l])
    xe, be = SSD_WIDTH, SSD_WIDTH + SSD_GROUPS * SSD_STATE
    return dict(
        w_in=_relayout_columns(w_in[l]).astype(BF16),
        w_out=w_out[l].astype(BF16),
        cw_x=cw[:, :xe], cb_x=cb[:, :xe], cw_b=cw[:, xe:be], cb_b=cb[:, xe:be],
        cw_c=cw[:, be:], cb_c=cb[:, be:],
        dt_bias=_pad_cols(row(ssd_dt_bias[l]), LANES), a_log=_pad_cols(row(ssd_A_log[l]), LANES),
        d_exp=row(jnp.repeat(ssd_D[l], SSD_HEAD_DIM)), ssd_ng=row(ssd_norm_g[l]),
        mu_r=mu_full[:, COL_R:COL_K], mu_k=mu_full[:, COL_K:COL_V], mu_v=mu_full[:, COL_V:COL_B],
        mu_lora=mu_full[:, COL_LORA:],
        w_up_f=_pad_rows(rwkv_w_up[l, 0], LANES), w_up_b=_pad_rows(rwkv_w_up[l, 1], LANES),
        a_up=_pad_rows(rwkv_a_up[l], LANES), g_up=_pad_rows(rwkv_g_up[l], 2 * LANES),
        w0_f=row(rwkv_w0[l, 0]), w0_b=row(rwkv_w0[l, 1]), a0=row(rwkv_a0[l]),
        k_k=row(rwkv_k_k[l]), k_a=row(rwkv_k_a[l]), r_k=row(rwkv_r_k[l]),
        ln_g=row(rwkv_ln_g[l]), ln_b=row(rwkv_ln_b[l]),
        router_w=tuple(_pad_cols(t, LANES) for t in _split_bf16(router_w[l])),
    )


def _to_chain(x):
    lead = x.shape[:-3]
    bsz, seq, _ = x.shape[-3:]
    n = len(lead)
    x = x.reshape(lead + (bsz, seq, RWKV_HEADS, RWKV_HEAD_DIM))
    perm = tuple(range(n)) + (n + 1, n + 3, n, n + 2)
    return jnp.transpose(x, perm).reshape(lead + (seq, RWKV_HEAD_DIM, bsz * RWKV_HEADS))


def _from_chain(y, bsz):
    lead = y.shape[:-3]
    seq = y.shape[-3]
    n = len(lead)
    y = y.reshape(lead + (seq, RWKV_HEAD_DIM, bsz, RWKV_HEADS))
    perm = tuple(range(n)) + (n + 2, n, n + 3, n + 1)
    return jnp.transpose(y, perm).reshape(lead + (bsz, seq, RWKV_WIDTH))


def _state_to_chain(s):
    return jnp.transpose(s, (0, 4, 3, 1, 2)).reshape(2, RWKV_HEAD_DIM, RWKV_HEAD_DIM, -1)


def _state_from_chain(s, bsz):
    return jnp.transpose(s.reshape(2, RWKV_HEAD_DIM, RWKV_HEAD_DIM, bsz, RWKV_HEADS), (0, 3, 4, 2, 1))


def _scan_replicas(bsz):
    nch = bsz * RWKV_HEADS
    rep = max(1, LANES // nch)
    assert (nch * rep) % LANES == 0
    return rep


def rwkv_mixer_scan(r, k, v, w2, a, s0, p):
    rep, bsz, seq, _ = r.shape
    nh, hd = RWKV_HEADS, RWKV_HEAD_DIM
    fold = lambda t: t.reshape(t.shape[:-4] + (rep * bsz, seq, RWKV_WIDTH))
    rc, kc, ac, wc = (_to_chain(fold(t)) for t in (r, k, a, w2))
    tabs = [jnp.tile(p[n].reshape(nh, hd).T, (1, LANES // nh)) for n in ("k_k", "k_a")]
    if rep == 1:
        s0c = None if s0 is None else _state_to_chain(s0)
        y, st = rwkv_scan(rc, wc, kc, ac, _to_chain(v), *tabs, s0c)
        return _from_chain(y, bsz), _state_from_chain(st, bsz)

    vq = hd // rep
    vc = jnp.transpose(v.reshape(bsz, seq, nh, rep, vq), (1, 4, 3, 0, 2)).reshape(seq, vq, LANES)
    s0c = None
    if s0 is not None:
        s0c = jnp.transpose(s0.reshape(2, bsz, nh, rep, vq, hd), (0, 5, 4, 3, 1, 2)).reshape(
            2, hd, vq, LANES)
    y, st = rwkv_scan(rc, wc, kc, ac, vc, *tabs, s0c)
    y = jnp.transpose(y.reshape(2, seq, vq, rep, bsz, nh), (0, 4, 1, 5, 3, 2)).reshape(
        2, bsz, seq, RWKV_WIDTH)
    st = jnp.transpose(st.reshape(2, hd, vq, rep, bsz, nh), (0, 4, 5, 3, 2, 1)).reshape(
        2, bsz, nh, hd, hd)
    return y, st


def _grid_pos_embed(n_tokens, dim):
    rows = n_tokens // GRID_W
    row = jnp.repeat(jnp.arange(rows, dtype=F32), GRID_W)
    col = jnp.tile(jnp.arange(GRID_W, dtype=F32), rows)
    quarter = dim // 4
    freqs = jnp.exp(jnp.arange(quarter, dtype=F32) * (-math.log(POS_BASE) / quarter))

    def axis_embed(pos):
        ang = pos[:, None] * freqs[None, :]
        return jnp.concatenate([jnp.sin(ang), jnp.cos(ang)], axis=-1)

    return jnp.concatenate([axis_embed(row), axis_embed(col)], axis=-1)


def layer_to_dispatch(x, pos, mod, p, norm_mix_g, norm_ffn_g, states):
    bsz, seq, d = x.shape
    shift_m, scale_m, gate_m, shift_f, scale_f, gate_f = mod
    h, x0 = norm_modulate(x, pos, norm_mix_g, scale_m, shift_m)
    n_tok = bsz * seq
    proj = matmul_bf16(h.reshape(n_tok, d), p["w_in"], min(PROJ_ROWS, n_tok), PROJ_COLS)
    proj = proj.reshape(bsz, seq, N_PROJ)
    h0f, h0b, s0f, s0b = states
    y_ssd, hf, hb = ssd_mixer(proj, p, h0f, h0b)
    r, k, v, w2, a, g = rwkv_prep(proj, p, _scan_replicas(bsz))
    s0 = None if s0f is None else jnp.stack([s0f, s0b], axis=0)
    y2, st = rwkv_mixer_scan(r, k, v, w2, a, s0, p)
    y_rwkv = rwkv_post(y2, r[0], k[0], a[0], v, g, p)
    x1, h2, logits = out_proj(y_ssd, y_rwkv, p["w_out"], x0, gate_m, norm_ffn_g, scale_f, shift_f,
                              p["router_w"])
    cap = EC_CAPACITY * n_tok // N_EXPERTS
    sel_r, pos_r, sel_t, pos_t, aff_t = route(logits.reshape(n_tok, LANES), cap)
    starts = jnp.concatenate([pos_r[:, :, 0], jnp.full((1, N_EXPERTS), cap, F32)], axis=0)
    starts = starts.astype(jnp.int32).reshape(-1)
    xs = gather_tokens(starts, sel_r, pos_r, h2.reshape(n_tok, d), cap)
    routed = dict(xs=xs, starts=starts, sel_t=sel_t, pos_t=pos_t, aff_t=aff_t, x1=x1, gate_f=gate_f)
    return routed, (hf, hb, st[0], st[1])


def combine_and_finish(routed, outbuf, final_g):
    x1 = routed["x1"]
    y_ffn = scatter_combine(routed["starts"], routed["sel_t"], routed["pos_t"], routed["aff_t"], outbuf)
    return final_norm(x1, y_ffn.reshape(x1.shape), routed["gate_f"], final_g)


def kernel(x_prompt, x_sample, state_ssd_fwd, state_ssd_bwd, state_rwkv_fwd, state_rwkv_bwd, c, c_ctx,
           w_ada, b_ada, norm_mix_g, norm_ffn_g, w_in, w_out, ssd_conv_w, ssd_conv_b, ssd_A_log,
           ssd_dt_bias, ssd_D, ssd_norm_g, rwkv_mu, rwkv_w0, rwkv_w_up, rwkv_a0, rwkv_a_up, rwkv_g_up,
           rwkv_k_k, rwkv_k_a, rwkv_r_k, rwkv_ln_g, rwkv_ln_b, router_w, exp_w_gate, exp_w_up,
           exp_w_down, final_norm_g):
    depth = w_in.shape[0]
    assert depth == 1, "the final norm runs right after the single layer's FFN residual"
    l = 0
    dec_b = x_sample.shape[0]
    p = _layer_params(l, w_in, w_out, ssd_conv_w, ssd_conv_b, ssd_A_log, ssd_dt_bias, ssd_D, ssd_norm_g,
                      rwkv_mu, rwkv_w0, rwkv_w_up, rwkv_a0, rwkv_a_up, rwkv_g_up, rwkv_k_k, rwkv_k_a,
                      rwkv_r_k, rwkv_ln_g, rwkv_ln_b, router_w)
    cond = jnp.concatenate([c_ctx[None, :], c, jnp.zeros((16 - 1 - dec_b, D_MODEL), F32)], axis=0)
    mod = ada_modulation(cond, w_ada[l], b_ada[l]).reshape(16, N_MOD, 1, D_MODEL)
    mod_ctx = [mod[0:1, i] for i in range(N_MOD)]
    mod_lat = [mod[1:1 + dec_b, i] for i in range(N_MOD)]

    routed_ctx, st = layer_to_dispatch(x_prompt, None, mod_ctx, p, norm_mix_g[l], norm_ffn_g[l],
                                       (None, None, None, None))
    pos = _grid_pos_embed(x_sample.shape[1], D_MODEL)
    cached = (state_ssd_fwd[:, l], state_ssd_bwd[:, l], state_rwkv_fwd[:, l], state_rwkv_bwd[:, l])
    routed_lat, _ = layer_to_dispatch(x_sample, pos, mod_lat, p, norm_mix_g[l], norm_ffn_g[l], cached)
    out_ctx, out_lat = expert_ffn([routed_ctx["xs"], routed_lat["xs"]],
                                  exp_w_gate[l], exp_w_up[l], exp_w_down[l])
    y_prompt = combine_and_finish(routed_ctx, out_ctx, final_norm_g)
    y_sample = combine_and_finish(routed_lat, out_lat, final_norm_g)
    return (y_prompt, y_sample, st[0][:, None], st[1][:, None], st[2][:, None], st[3][:, None])
```

```python
import functools
import math

import jax
import jax.numpy as jnp
from jax import lax
from jax.experimental import pallas as pl
from jax.experimental.pallas import tpu as pltpu

F32 = jnp.float32
BF16 = jnp.bfloat16
HIGHEST = lax.Precision.HIGHEST

D_MODEL = 2048
GRID_W = 64
SSD_WIDTH = 1024
SSD_HEAD_DIM = 64
SSD_HEADS = 16
SSD_GROUPS = 2
SSD_STATE = 128
SSD_CONV_W = 5
SSD_CHUNK = 128
RWKV_WIDTH = 1024
RWKV_HEAD_DIM = 64
RWKV_HEADS = 16
DECAY_LORA = 64
ICLR_LORA = 64
GATE_LORA = 160
N_EXPERTS = 16
EC_CAPACITY = 2
EXPERT_FF = 1024
N_MOD = 6
NORM_EPS = 1e-6
GN_EPS = 64e-5
DECAY_SCALE = 0.606531
POS_BASE = 10000.0
SEARCH_ITERS = 48

LANES = 128
SUBLANES = 8
GATHER_ROWS = 64
GATHER_EXPERTS = 4

COL_Z = 0
COL_X = 1024
COL_R = 2048
COL_K = 3072
COL_V = 4096
COL_B = 5120
COL_C = 5376
COL_DT = 5632
COL_LORA = 5760
PREP_BLOCK_ELEMS = 256 * 1024
LORA_W = 640
N_PROJ = COL_LORA + LORA_W

VMEM_LIMIT = 56 * 1024 * 1024

ADA_COLS = 1024
TOKEN_ROWS = 256
PROJ_ROWS = 2048
PROJ_COLS = 640
OUT_PROJ_ROWS = 512
SCAN_STEPS = 32
GATHER_TOKENS = 1024
FFN_COLS = 256
SCATTER_TOKENS = 2048


def _cparams(sem):
    return pltpu.CompilerParams(dimension_semantics=sem, vmem_limit_bytes=VMEM_LIMIT)


def _silu(x):
    return x * jax.nn.sigmoid(x)


def _ada_kernel(c_ref, w_ref, b_ref, o_ref):
    s = _silu(c_ref[...]).astype(BF16)
    o_ref[...] = jnp.dot(s, w_ref[...].astype(BF16), preferred_element_type=F32) + b_ref[...]


def ada_modulation(cond, w, b):
    m, d = cond.shape
    n = w.shape[1]
    tn = ADA_COLS
    return pl.pallas_call(
        _ada_kernel,
        grid=(n // tn,),
        in_specs=[pl.BlockSpec((m, d), lambda j: (0, 0)),
                  pl.BlockSpec((d, tn), lambda j: (0, j)),
                  pl.BlockSpec((1, tn), lambda j: (0, j))],
        out_specs=pl.BlockSpec((m, tn), lambda j: (0, j)),
        out_shape=jax.ShapeDtypeStruct((m, n), F32),
        compiler_params=_cparams(("parallel",)),
        name="ada_modulation",
    )(cond, w, b.reshape(1, n))


def _norm_mod_kernel(has_pos, *refs):
    if has_pos:
        x_ref, pos_ref, g_ref, sc_ref, sh_ref, h_ref, x0_ref = refs
    else:
        x_ref, g_ref, sc_ref, sh_ref, h_ref = refs
    x = x_ref[0]
    if has_pos:
        x = x + pos_ref[...]
        x0_ref[0] = x
    y = x * lax.rsqrt(jnp.mean(x * x, axis=-1, keepdims=True) + NORM_EPS) * g_ref[...]
    h_ref[0] = (y * (1.0 + sc_ref[0]) + sh_ref[0]).astype(BF16)


def norm_modulate(x, pos, g, scale, shift):
    bsz, seq, d = x.shape
    tl = TOKEN_ROWS
    per_b = scale.shape[0] > 1
    midx = (lambda b, i: (b, 0, 0)) if per_b else (lambda b, i: (0, 0, 0))
    xspec = pl.BlockSpec((1, tl, d), lambda b, i: (b, i, 0))
    in_specs = [xspec]
    args = [x]
    if pos is not None:
        in_specs.append(pl.BlockSpec((tl, d), lambda b, i: (i, 0)))
        args.append(pos)
    in_specs += [pl.BlockSpec((1, d), lambda b, i: (0, 0)),
                 pl.BlockSpec((1, 1, d), midx), pl.BlockSpec((1, 1, d), midx)]
    args += [g.reshape(1, d), scale, shift]
    out_shape = [jax.ShapeDtypeStruct((bsz, seq, d), BF16)]
    out_specs = [xspec]
    if pos is not None:
        out_shape.append(jax.ShapeDtypeStruct((bsz, seq, d), F32))
        out_specs.append(xspec)
    res = pl.pallas_call(
        functools.partial(_norm_mod_kernel, pos is not None),
        grid=(bsz, seq // tl),
        in_specs=in_specs, out_specs=out_specs, out_shape=out_shape,
        compiler_params=_cparams(("parallel", "parallel")),
        name="norm_modulate",
    )(*args)
    return (res[0], res[1]) if pos is not None else (res[0], x)


def _mm_kernel(a_ref, b_ref, o_ref):
    o_ref[...] = jnp.dot(a_ref[...], b_ref[...], preferred_element_type=F32)


def matmul_bf16(a, b, tm, tn):
    m, k = a.shape
    n = b.shape[1]
    return pl.pallas_call(
        _mm_kernel,
        grid=(m // tm, n // tn),
        in_specs=[pl.BlockSpec((tm, k), lambda i, j: (i, 0)),
                  pl.BlockSpec((k, tn), lambda i, j: (0, j))],
        out_specs=pl.BlockSpec((tm, tn), lambda i, j: (i, j)),
        out_shape=jax.ShapeDtypeStruct((m, n), F32),
        compiler_params=_cparams(("parallel", "arbitrary")),
        name="in_proj",
    )(a, b)


def _shifted(u, off):
    n = u.shape[0]
    row = lax.broadcasted_iota(jnp.int32, (n, 1), 0)
    rolled = pltpu.roll(u, (-off) % n, axis=0)
    valid = jnp.logical_and(row + off >= 0, row + off < n)
    return jnp.where(valid, rolled, 0.0)


def _conv_silu(u, w_ref, b_ref, cols):
    pad = SSD_CONV_W // 2
    acc = u * w_ref[pad:pad + 1, cols] + b_ref[:, cols]
    for j in range(SSD_CONV_W):
        if j != pad:
            acc = acc + _shifted(u, j - pad) * w_ref[j:j + 1, cols]
    return _silu(acc)


def _softplus(x):
    return jnp.maximum(x, 0.0) + jnp.log1p(jnp.exp(-jnp.abs(x)))


def _ssd_kernel(nc, zero_init, *refs):
    q = SSD_CHUNK
    (z_ref, x_ref, b_ref, c_ref, dt_ref, cwx, cbx, cwb, cbb, cwc, cbc,
     dtb_ref, alog_ref, d_ref, ng_ref) = refs[:15]
    refs = refs[15:]
    if not zero_init:
        h0f_ref, h0b_ref = refs[:2]
        refs = refs[2:]
    y_ref, hf_ref, hb_ref, xh_s, bm_s, cm_s, dt_s, a_s, at_s, y_s, h_s = refs

    slab = 256
    for cb in range(SSD_WIDTH // slab):
        cols = slice(cb * slab, (cb + 1) * slab)
        xh = _conv_silu(x_ref[0, :, cols], cwx, cbx, cols)
        for c in range(nc):
            rows = slice(c * q, (c + 1) * q)
            xh_s[c, :, cols] = xh[rows]
            y_s[c, :, cols] = xh[rows] * d_ref[:, cols]
    allc = slice(0, SSD_GROUPS * SSD_STATE)
    bm = _conv_silu(b_ref[0], cwb, cbb, allc)
    cm = _conv_silu(c_ref[0], cwc, cbc, allc)
    dt = _softplus(dt_ref[0] + dtb_ref[...])
    a = dt * (-jnp.exp(alog_ref[...]))
    ii = lax.broadcasted_iota(jnp.int32, (q, q), 0)
    jj = lax.broadcasted_iota(jnp.int32, (q, q), 1)
    eye = (ii == jj).astype(F32)
    lower = ii >= jj
    upper = ii <= jj
    first_half = jj < SSD_HEAD_DIM
    for c in range(nc):
        rows = slice(c * q, (c + 1) * q)
        for g in range(SSD_GROUPS):
            gcols = slice(g * SSD_STATE, (g + 1) * SSD_STATE)
            bm_s[c, gcols, :] = bm[rows, gcols].T
        cm_s[c] = cm[rows]
        dt_s[c] = dt[rows]
        a_s[c] = a[rows]
        at_s[c] = lax.dot_general(eye, a[rows], (((1,), (1,)), ((), ())),
                                  precision=HIGHEST, preferred_element_type=F32)

    for d in range(2):
        mask = lower if d == 0 else upper
        tri = mask.astype(F32)
        tri_t = (upper if d == 0 else lower).astype(F32)
        if zero_init:
            h_s[...] = jnp.zeros_like(h_s)
        else:
            h0_ref = h0f_ref if d == 0 else h0b_ref
            for p in range(SSD_HEADS // 2):
                h_s[p] = jnp.concatenate([h0_ref[0, 2 * p], h0_ref[0, 2 * p + 1]], axis=0).T

        def chunk_body(ci, carry, d=d, mask=mask, tri=tri, tri_t=tri_t):
            c = ci if d == 0 else nc - 1 - ci
            a_c = a_s[c]
            cum = jnp.dot(tri, a_c, precision=HIGHEST, preferred_element_type=F32)
            cum_t = jnp.dot(at_s[c], tri_t, precision=HIGHEST, preferred_element_type=F32)
            dt_c = dt_s[c]
            for g in range(SSD_GROUPS):
                gcols = slice(g * SSD_STATE, (g + 1) * SSD_STATE)
                bt = bm_s[c, gcols, :]
                cg = cm_s[c, :, gcols]
                gram = jnp.dot(cg.astype(BF16), bt.astype(BF16), preferred_element_type=F32)
                for pp in range(SSD_HEADS // SSD_GROUPS // 2):
                    p = g * (SSD_HEADS // SSD_GROUPS // 2) + pp
                    pcols = slice(p * LANES, (p + 1) * LANES)
                    xh_pair = xh_s[c, :, pcols]
                    hprev = h_s[p]
                    hprev_b = hprev.astype(BF16)
                    parts = []
                    for e in range(2):
                        col = d * SSD_HEADS + 2 * p + e
                        mine = first_half if e == 0 else jnp.logical_not(first_half)
                        cc = jnp.broadcast_to(cum[:, col:col + 1], (q, q))
                        cr = cum_t[col:col + 1, :]
                        dec = jnp.exp(jnp.where(mask, cc - cr, -jnp.inf))
                        m = (gram * dec).astype(BF16)
                        xdt = jnp.where(mine, xh_pair * dt_c[:, col:col + 1], 0.0).astype(BF16)
                        tot = cum[q - 1:q, col:col + 1] if d == 0 else cum[0:1, col:col + 1]
                        bd_t = (bt * jnp.exp(tot - cr)).astype(BF16)
                        cd = (cg * jnp.exp(cc)).astype(BF16)
                        s_c = jnp.dot(bd_t, xdt, preferred_element_type=F32)
                        y_off = jnp.dot(cd, hprev_b, preferred_element_type=F32)
                        y_diag = jnp.dot(m, xdt, preferred_element_type=F32)
                        parts.append((s_c, y_off, y_diag, jnp.exp(tot)))
                    (s0, yo0, yd0, g0), (s1, yo1, yd1, g1) = parts
                    h_s[p] = hprev * jnp.where(first_half, g0, g1) + (s0 + s1)
                    y_s[c, :, pcols] += (yd0 + yd1) + jnp.where(first_half, yo0, yo1)
            return carry

        lax.fori_loop(0, nc, chunk_body, 0)
        out_ref = hf_ref if d == 0 else hb_ref
        for p in range(SSD_HEADS // 2):
            tr = h_s[p].T
            out_ref[0, 2 * p] = tr[0:SSD_HEAD_DIM]
            out_ref[0, 2 * p + 1] = tr[SSD_HEAD_DIM:]

    gw = SSD_WIDTH // SSD_GROUPS
    for c in range(nc):
        rows = slice(c * q, (c + 1) * q)
        for g in range(SSD_GROUPS):
            cols = slice(g * gw, (g + 1) * gw)
            y = y_s[c, :, cols] * _silu(z_ref[0, rows, cols])
            y = y * lax.rsqrt(jnp.mean(y * y, axis=-1, keepdims=True) + NORM_EPS)
            y_ref[0, rows, cols] = (y * ng_ref[:, cols]).astype(BF16)


def ssd_mixer(proj, p, h0f, h0b):
    bsz, seq, _ = proj.shape
    nc = seq // SSD_CHUNK
    zero_init = h0f is None

    def col_spec(width, start):
        blk = start // width
        return pl.BlockSpec((1, seq, width), lambda b: (b, 0, blk))

    def full2(arr):
        return pl.BlockSpec(arr.shape, lambda b: (0, 0))

    st_spec = pl.BlockSpec((1, SSD_HEADS, SSD_HEAD_DIM, SSD_STATE), lambda b: (b, 0, 0, 0))
    bc = SSD_GROUPS * SSD_STATE
    small = [p["cw_x"], p["cb_x"], p["cw_b"], p["cb_b"], p["cw_c"], p["cb_c"],
             p["dt_bias"], p["a_log"], p["d_exp"], p["ssd_ng"]]
    in_specs = [col_spec(SSD_WIDTH, COL_Z), col_spec(SSD_WIDTH, COL_X), col_spec(bc, COL_B),
                col_spec(bc, COL_C), col_spec(LANES, COL_DT)] + [full2(s) for s in small]
    args = [proj] * 5 + small
    if not zero_init:
        in_specs += [st_spec, st_spec]
        args += [h0f, h0b]
    st_shape = jax.ShapeDtypeStruct((bsz, SSD_HEADS, SSD_HEAD_DIM, SSD_STATE), F32)
    q = SSD_CHUNK
    return pl.pallas_call(
        functools.partial(_ssd_kernel, nc, zero_init),
        grid=(bsz,),
        in_specs=in_specs,
        out_specs=[pl.BlockSpec((1, seq, SSD_WIDTH), lambda b: (b, 0, 0)), st_spec, st_spec],
        out_shape=[jax.ShapeDtypeStruct((bsz, seq, SSD_WIDTH), BF16), st_shape, st_shape],
        scratch_shapes=[pltpu.VMEM((nc, q, SSD_WIDTH), F32), pltpu.VMEM((nc, bc, q), F32),
                        pltpu.VMEM((nc, q, bc), F32), pltpu.VMEM((nc, q, LANES), F32),
                        pltpu.VMEM((nc, q, LANES), F32), pltpu.VMEM((nc, LANES, q), F32),
                        pltpu.VMEM((nc, q, SSD_WIDTH), F32),
                        pltpu.VMEM((SSD_HEADS // 2, SSD_STATE, 2 * SSD_HEAD_DIM), F32)],
        compiler_params=_cparams(("parallel",)),
        name="ssd_mixer",
    )(*args)


def _seg_sum(x):
    lane = lax.broadcasted_iota(jnp.int32, x.shape, 1)
    first = lane < RWKV_HEAD_DIM
    s0 = jnp.sum(jnp.where(first, x, 0.0), axis=-1, keepdims=True)
    s1 = jnp.sum(jnp.where(first, 0.0, x), axis=-1, keepdims=True)
    return jnp.where(first, s0, s1)


def _shift_mix(u, mu):
    return u + mu * (0.5 * (_shifted(u, -1) + _shifted(u, 1)) - u)


def _rwkv_prep_kernel(ncb, rep, *refs):
    (r_ref, k_ref, v_ref, lo_ref, mur, muk, muv, mul, wupf, wupb, aup, gup,
     w0f, w0b, a0, ro, ko, vo, wo, ao, go) = refs[:21]
    staged = dict(zip(("r", "k", "a", "wf", "wb"), refs[21:]))
    q = pl.program_id(2)

    def put(name, cols, val):
        if staged:
            staged[name][:, cols] = val
        elif name in ("wf", "wb"):
            wo[0 if name == "wf" else 1, 0, 0, :, cols] = val
        else:
            {"r": ro, "k": ko, "a": ao}[name][0, 0, :, cols] = val

    def compute():
        lo = _shift_mix(lo_ref[0], mul[...])
        wdf = jnp.tanh(lo[:, 0:128]).astype(BF16)
        wdb = jnp.tanh(lo[:, 128:256]).astype(BF16)
        ad = lo[:, 256:384].astype(BF16)
        gd = jax.nn.sigmoid(lo[:, 384:640]).astype(BF16)
        for cb in range(ncb):
            cols = slice(cb * LANES, (cb + 1) * LANES)

            def mm(x, w_ref, cols=cols):
                return jnp.dot(x, w_ref[:, cols].astype(BF16), preferred_element_type=F32)

            r = _shift_mix(r_ref[0, :, cols], mur[:, cols])
            k = _shift_mix(k_ref[0, :, cols], muk[:, cols])
            v = _shift_mix(v_ref[0, :, cols], muv[:, cols])
            put("wf", cols, jnp.exp(-DECAY_SCALE * jax.nn.sigmoid(w0f[:, cols] + mm(wdf, wupf))))
            put("wb", cols, jnp.exp(-DECAY_SCALE * jax.nn.sigmoid(w0b[:, cols] + mm(wdb, wupb))))
            put("a", cols, jax.nn.sigmoid(a0[:, cols] + mm(ad, aup)))
            go[0, :, cols] = mm(gd, gup)
            put("r", cols, r)
            put("k", cols, k)
            vo[0, :, cols] = v

    if staged:
        pl.when(q == 0)(compute)
        ro[0, 0] = staged["r"][...]
        ko[0, 0] = staged["k"][...]
        ao[0, 0] = staged["a"][...]
        wo[0, 0, 0] = staged["wf"][...]
        wo[1, 0, 0] = staged["wb"][...]
    else:
        compute()


def rwkv_prep(proj, p, rep):
    bsz, seq, _ = proj.shape
    cw = min(RWKV_WIDTH, PREP_BLOCK_ELEMS // seq)
    nblk = RWKV_WIDTH // cw

    def col_spec(start):
        blk = start // cw
        return pl.BlockSpec((1, seq, cw), lambda b, j, q: (b, 0, blk + j))

    vec = pl.BlockSpec((1, cw), lambda b, j, q: (0, j))

    def mat(rows):
        return pl.BlockSpec((rows, cw), lambda b, j, q: (0, j))

    tok_spec = pl.BlockSpec((1, seq, cw), lambda b, j, q: (b, 0, j))
    rep_spec = pl.BlockSpec((1, 1, seq, cw), lambda b, j, q: (q, b, 0, j))
    w_spec = pl.BlockSpec((2, 1, 1, seq, cw), lambda b, j, q: (0, q, b, 0, j))
    tok_sh = jax.ShapeDtypeStruct((bsz, seq, RWKV_WIDTH), F32)
    rep_sh = jax.ShapeDtypeStruct((rep, bsz, seq, RWKV_WIDTH), F32)
    w_sh = jax.ShapeDtypeStruct((2, rep, bsz, seq, RWKV_WIDTH), F32)
    return pl.pallas_call(
        functools.partial(_rwkv_prep_kernel, cw // LANES, rep),
        grid=(bsz, nblk, rep),
        in_specs=[col_spec(COL_R), col_spec(COL_K), col_spec(COL_V),
                  pl.BlockSpec((1, seq, LORA_W), lambda b, j, q: (b, 0, COL_LORA // LORA_W)),
                  vec, vec, vec, pl.BlockSpec((1, LORA_W), lambda b, j, q: (0, 0)),
                  mat(128), mat(128), mat(128), mat(256),
                  vec, vec, vec],
        out_specs=[rep_spec, rep_spec, tok_spec, w_spec, rep_spec, tok_spec],
        out_shape=[rep_sh, rep_sh, tok_sh, w_sh, rep_sh, tok_sh],
        scratch_shapes=[pltpu.VMEM((seq, cw), F32)] * (5 if rep > 1 else 0),
        compiler_params=_cparams(("parallel", "arbitrary", "arbitrary")),
        name="rwkv_prep",
    )(proj, proj, proj, proj, p["mu_r"], p["mu_k"], p["mu_v"], p["mu_lora"],
      p["w_up_f"], p["w_up_b"], p["a_up"], p["g_up"],
      p["w0_f"], p["w0_b"], p["a0"])


def _rwkv_scan_kernel(tb, zero_init, *refs):
    if zero_init:
        r_ref, w_ref, kr_ref, a_ref, v_ref, kk_ref, ka_ref, y_ref, st_ref, s_s, k_ref, al_ref, be_ref = refs
    else:
        (r_ref, w_ref, kr_ref, a_ref, v_ref, kk_ref, ka_ref, s0_ref, y_ref, st_ref,
         s_s, k_ref, al_ref, be_ref) = refs
    kd = RWKV_HEAD_DIM
    d = pl.program_id(0)
    step_i = pl.program_id(2)

    def time_index(i):
        return i + d * (tb - 1 - 2 * i)

    @pl.when(step_i == 0)
    def _():
        if zero_init:
            s_s[...] = jnp.zeros_like(s_s)
        else:
            s_s[...] = s0_ref[...]

    def derive(t, carry):
        kraw = kr_ref[t]
        a = a_ref[t]
        kk = kraw * kk_ref[...]
        kk = kk * (1.0 / jnp.maximum(jnp.sqrt(jnp.sum(kk * kk, axis=0, keepdims=True)), 1e-12))
        al_ref[t] = -kk
        be_ref[t] = kk * a
        k_ref[t] = kraw * (1.0 + (a - 1.0) * ka_ref[...])
        return carry

    lax.fori_loop(0, tb, derive, 0, unroll=4)

    t0 = time_index(0)
    sa0 = jnp.zeros(s_s.shape[1:], F32)
    sa1 = jnp.zeros(s_s.shape[1:], F32)
    for kq in range(0, kd, 2):
        sa0 = sa0 + s_s[kq] * al_ref[t0, kq:kq + 1, :]
        sa1 = sa1 + s_s[kq + 1] * al_ref[t0, kq + 1:kq + 2, :]

    def step(i, sa):
        t = time_index(i)
        tn = time_index(jnp.minimum(i + 1, tb - 1))
        vt = v_ref[t]
        y = jnp.zeros_like(sa)
        sa_next = jnp.zeros_like(sa)
        for kq in range(kd):
            row = slice(kq, kq + 1)
            sk = s_s[kq] * w_ref[t, row, :] + sa * be_ref[t, row, :] + vt * k_ref[t, row, :]
            s_s[kq] = sk
            y = y + sk * r_ref[t, row, :]
            sa_next = sa_next + sk * al_ref[tn, row, :]
        y_ref[t] = y
        return sa_next

    lax.fori_loop(0, tb, step, sa0 + sa1)

    @pl.when(step_i == pl.num_programs(2) - 1)
    def _():
        st_ref[...] = s_s[...]


def rwkv_scan(r, w, k, a, v, kk_tab, ka_tab, s0):
    seq, kd, nch = r.shape
    vv = v.shape[1]
    tb = SCAN_STEPS
    nblk = seq // tb
    zero_init = s0 is None

    def tblk(d, i):
        return i + d * (nblk - 1 - 2 * i)

    kspec = pl.BlockSpec((tb, kd, LANES), lambda d, g, i: (tblk(d, i), 0, g))
    wspec = pl.BlockSpec((None, tb, kd, LANES), lambda d, g, i: (d, tblk(d, i), 0, g))
    vspec = pl.BlockSpec((tb, vv, LANES), lambda d, g, i: (tblk(d, i), 0, g))
    yspec = pl.BlockSpec((None, tb, vv, LANES), lambda d, g, i: (d, tblk(d, i), 0, g))
    sspec = pl.BlockSpec((None, kd, vv, LANES), lambda d, g, i: (d, 0, 0, g))
    tabspec = pl.BlockSpec((kd, LANES), lambda d, g, i: (0, 0))
    in_specs = [kspec, wspec, kspec, kspec, vspec, tabspec, tabspec]
    args = [r, w, k, a, v, kk_tab, ka_tab]
    if not zero_init:
        in_specs.append(sspec)
        args.append(s0)
    return pl.pallas_call(
        functools.partial(_rwkv_scan_kernel, tb, zero_init),
        grid=(2, nch // LANES, nblk),
        in_specs=in_specs,
        out_specs=[yspec, sspec],
        out_shape=[jax.ShapeDtypeStruct((2, seq, vv, nch), F32),
                   jax.ShapeDtypeStruct((2, kd, vv, nch), F32)],
        scratch_shapes=[pltpu.VMEM((kd, vv, LANES), F32)] + [pltpu.VMEM((tb, kd, LANES), F32)] * 3,
        compiler_params=_cparams(("parallel", "parallel", "arbitrary")),
        name="rwkv_scan",
    )(*args)


def _rwkv_post_kernel(y_ref, r_ref, k_ref, a_ref, v_ref, g_ref, lng, lnb, rk, ka, o_ref):
    inv = 1.0 / RWKV_HEAD_DIM
    for cb in range(RWKV_WIDTH // LANES):
        cols = slice(cb * LANES, (cb + 1) * LANES)
        y = y_ref[0, 0, :, cols] + y_ref[1, 0, :, cols]
        mean = _seg_sum(y) * inv
        yc = y - mean
        var = _seg_sum(yc * yc) * inv
        yn = yc * lax.rsqrt(var + GN_EPS) * lng[:, cols] + lnb[:, cols]
        k = k_ref[0, :, cols] * (1.0 + (a_ref[0, :, cols] - 1.0) * ka[:, cols])
        bonus = _seg_sum(r_ref[0, :, cols] * k * rk[:, cols]) * v_ref[0, :, cols]
        o_ref[0, :, cols] = ((yn + bonus) * g_ref[0, :, cols]).astype(BF16)


def rwkv_post(y2, r, k, a, v, g, p):
    _, bsz, seq, wd = y2.shape
    tl = TOKEN_ROWS
    spec = pl.BlockSpec((1, tl, wd), lambda b, i: (b, i, 0))
    vec = pl.BlockSpec((1, wd), lambda b, i: (0, 0))
    return pl.pallas_call(
        _rwkv_post_kernel,
        grid=(bsz, seq // tl),
        in_specs=[pl.BlockSpec((2, 1, tl, wd), lambda b, i: (0, b, i, 0))] + [spec] * 5 + [vec] * 4,
        out_specs=spec,
        out_shape=jax.ShapeDtypeStruct((bsz, seq, wd), BF16),
        compiler_params=_cparams(("parallel", "parallel")),
        name="rwkv_post",
    )(y2, r, k, a, v, g, p["ln_g"], p["ln_b"], p["r_k"], p["k_a"])


def _split_bf16(x):
    hi = x.astype(BF16)
    return hi, (x - hi.astype(F32)).astype(BF16)


def _out_proj_kernel(ys_ref, yr_ref, wo_ref, x_ref, gm_ref, ng_ref, sc_ref, sh_ref, rwh_ref, rwl_ref,
                     x1_ref, h2_ref, lg_ref):
    acc = jnp.dot(ys_ref[0], wo_ref[0:SSD_WIDTH, :], preferred_element_type=F32)
    acc = acc + jnp.dot(yr_ref[0], wo_ref[SSD_WIDTH:, :], preferred_element_type=F32)
    x1 = x_ref[0] + gm_ref[0] * acc
    x1_ref[0] = x1
    hn = x1 * lax.rsqrt(jnp.mean(x1 * x1, axis=-1, keepdims=True) + NORM_EPS) * ng_ref[...]
    h2 = hn * (1.0 + sc_ref[0]) + sh_ref[0]
    hi, lo = _split_bf16(h2)
    h2_ref[0] = hi
    lg_ref[0] = (jnp.dot(hi, rwh_ref[...], preferred_element_type=F32)
                 + jnp.dot(lo, rwh_ref[...], preferred_element_type=F32)
                 + jnp.dot(hi, rwl_ref[...], preferred_element_type=F32))


def out_proj(y_ssd, y_rwkv, w_out, x, gate_m, norm_g, scale_f, shift_f, router_w):
    bsz, seq, d = x.shape
    per_b = gate_m.shape[0] > 1
    if not per_b and bsz > 1:
        flat = lambda t: t.reshape((1, bsz * seq) + t.shape[2:])
        x1, h2, lg = out_proj(flat(y_ssd), flat(y_rwkv), w_out, flat(x), gate_m, norm_g, scale_f, shift_f,
                              router_w)
        return x1.reshape(bsz, seq, d), h2.reshape(bsz, seq, d), lg.reshape(bsz, seq, LANES)
    tm = min(OUT_PROJ_ROWS, seq)
    midx = (lambda b, i: (b, 0, 0)) if per_b else (lambda b, i: (0, 0, 0))
    half = pl.BlockSpec((1, tm, d // 2), lambda b, i: (b, i, 0))
    full = pl.BlockSpec((1, tm, d), lambda b, i: (b, i, 0))
    mspec = pl.BlockSpec((1, 1, d), midx)
    return pl.pallas_call(
        _out_proj_kernel,
        grid=(bsz, seq // tm),
        in_specs=[half, half, pl.BlockSpec((d, d), lambda b, i: (0, 0)), full, mspec,
                  pl.BlockSpec((1, d), lambda b, i: (0, 0)), mspec, mspec,
                  pl.BlockSpec((d, LANES), lambda b, i: (0, 0)),
                  pl.BlockSpec((d, LANES), lambda b, i: (0, 0))],
        out_specs=[full, full, pl.BlockSpec((1, tm, LANES), lambda b, i: (b, i, 0))],
        out_shape=[jax.ShapeDtypeStruct((bsz, seq, d), F32), jax.ShapeDtypeStruct((bsz, seq, d), BF16),
                   jax.ShapeDtypeStruct((bsz, seq, LANES), F32)],
        compiler_params=_cparams(("parallel", "parallel")),
        name="out_proj",
    )(y_ssd, y_rwkv, w_out, x, gate_m, norm_g.reshape(1, d), scale_f, shift_f, *router_w)


def _route_kernel(nt, cap, lg_ref, selr_ref, posr_ref, selt_ref, post_ref, afft_ref, aff_s, pre_s):
    ne = N_EXPERTS
    lane = lax.broadcasted_iota(jnp.int32, (LANES, LANES), 1)
    sub = lax.broadcasted_iota(jnp.int32, (LANES, LANES), 0)
    upper_incl = (sub <= lane).astype(BF16)
    for i in range(nt):
        lg = jnp.where(lane < ne, lg_ref[i * LANES:(i + 1) * LANES, :], -jnp.inf)
        e = jnp.exp(lg - jnp.max(lg, axis=-1, keepdims=True))
        aff = e / jnp.sum(e, axis=-1, keepdims=True)
        afft_ref[i * LANES:(i + 1) * LANES, :] = aff
        aff_s[i] = aff.T[0:ne, :]
    aff3 = aff_s[...]

    def count(mask):
        s = jnp.sum(mask.astype(F32), axis=0, keepdims=True)
        return jnp.sum(s, axis=2, keepdims=True)

    def search(_, carry):
        lo, hi = carry
        mid = 0.5 * (lo + hi)
        ok = count(aff3 >= mid) >= cap
        return jnp.where(ok, mid, lo), jnp.where(ok, hi, mid)

    lo0 = jnp.zeros((1, ne, 1), F32)
    hi0 = jnp.full((1, ne, 1), 2.0, F32)
    lo, _ = lax.fori_loop(0, SEARCH_ITERS, search, (lo0, hi0))
    cand = jnp.where(aff3 >= lo, aff3, 4.0)
    thr = jnp.min(jnp.min(cand, axis=0, keepdims=True), axis=2, keepdims=True)
    gt = (aff3 > thr).astype(F32)
    eq = (aff3 == thr).astype(F32)
    need = cap - count(aff3 > thr)[0]

    def prefix_excl(m3):
        off = jnp.zeros((ne, 1), F32)
        for i in range(nt):
            inc = jnp.dot(m3[i].astype(BF16), upper_incl, preferred_element_type=F32)
            pre_s[i] = inc - m3[i] + off
            off = off + inc[:, LANES - 1:LANES]
        return pre_s[...]

    sel = jnp.maximum(gt, eq * (prefix_excl(eq) < need[None]).astype(F32))
    pos = prefix_excl(sel)
    selr_ref[...] = sel
    posr_ref[...] = pos
    zpad = jnp.zeros((LANES - ne, LANES), F32)
    for i in range(nt):
        rows = slice(i * LANES, (i + 1) * LANES)
        selt_ref[rows, :] = jnp.concatenate([sel[i], zpad], axis=0).T
        post_ref[rows, :] = jnp.concatenate([pos[i], zpad], axis=0).T


def route(logits, cap):
    n_tok = logits.shape[0]
    nt = n_tok // LANES
    row_sh = jax.ShapeDtypeStruct((nt, N_EXPERTS, LANES), F32)
    tm_sh = jax.ShapeDtypeStruct((n_tok, LANES), F32)
    return pl.pallas_call(
        functools.partial(_route_kernel, nt, cap),
        out_shape=[row_sh, row_sh, tm_sh, tm_sh, tm_sh],
        scratch_shapes=[pltpu.VMEM((nt, N_EXPERTS, LANES), F32), pltpu.VMEM((nt, N_EXPERTS, LANES), F32)],
        compiler_params=pltpu.CompilerParams(vmem_limit_bytes=VMEM_LIMIT),
        name="route",
    )(logits)


def _window_start(starts_ref, tile, e, cap, win, align=SUBLANES):
    ps = starts_ref[tile * N_EXPERTS + e]
    return pl.multiple_of(jnp.minimum((ps // align) * align, cap - win), align)


def _gather_kernel(cap, win, nsub, starts_ref, sel_ref, pos_ref, h_ref, o_ref):
    eg = pl.program_id(0)
    j = pl.program_id(1)

    @pl.when(j == 0)
    def _():
        o_ref[...] = jnp.zeros_like(o_ref)

    for el in range(GATHER_EXPERTS):
        e = eg * GATHER_EXPERTS + el
        for s in range(0, nsub, 2):
            tile = j * nsub + s
            s0 = _window_start(starts_ref, tile, e, cap, win, align=2 * SUBLANES)
            lo = starts_ref[tile * N_EXPERTS + e]
            hi = starts_ref[(tile + 2) * N_EXPERTS + e]
            prows = [pos_ref[s + u, pl.ds(e, 1), :] for u in range(2)]
            srows = [sel_ref[s + u, pl.ds(e, 1), :] for u in range(2)]
            for piece in range(win // GATHER_ROWS):
                p0 = pl.multiple_of(s0 + piece * GATHER_ROWS, 2 * SUBLANES)

                @pl.when(jnp.logical_and(p0 < hi, p0 + GATHER_ROWS > lo))
                def _(p0=p0, s=s, el=el, prows=prows, srows=srows):
                    slot = (lax.broadcasted_iota(jnp.int32, (GATHER_ROWS, LANES), 0) + p0).astype(F32)
                    onehot = jnp.concatenate(
                        [jnp.where(jnp.logical_and(slot == prows[u], srows[u] > 0.0), 1.0, 0.0).astype(BF16)
                         for u in range(2)], axis=1)
                    rows = pl.ds(p0, GATHER_ROWS)
                    got = jnp.dot(onehot, h_ref[s * LANES:(s + 2) * LANES, :], preferred_element_type=F32)
                    o_ref[el, rows, :] = (o_ref[el, rows, :].astype(F32) + got).astype(BF16)


def gather_tokens(starts, sel_r, pos_r, h, cap):
    n_tok, d = h.shape
    tt = min(GATHER_TOKENS, n_tok)
    nsub = tt // LANES
    win = min(-(-(2 * LANES + 2 * SUBLANES) // GATHER_ROWS) * GATHER_ROWS, cap)
    assert win % GATHER_ROWS == 0 and (cap - win) % (2 * SUBLANES) == 0
    rspec = pl.BlockSpec((nsub, N_EXPERTS, LANES), lambda e, j, st: (j, 0, 0))
    return pl.pallas_call(
        functools.partial(_gather_kernel, cap, win, nsub),
        grid_spec=pltpu.PrefetchScalarGridSpec(
            num_scalar_prefetch=1,
            grid=(N_EXPERTS // GATHER_EXPERTS, n_tok // tt),
            in_specs=[rspec, rspec, pl.BlockSpec((tt, d), lambda e, j, st: (j, 0))],
            out_specs=pl.BlockSpec((GATHER_EXPERTS, cap, d), lambda e, j, st: (e, 0, 0))),
        out_shape=jax.ShapeDtypeStruct((N_EXPERTS, cap, d), BF16),
        compiler_params=_cparams(("parallel", "arbitrary")),
        name="gather_tokens",
    )(starts, sel_r, pos_r, h)


def _ffn_kernel(n_sets, *refs):
    x_refs = refs[:n_sets]
    wg_ref, wu_ref, wd_ref = refs[n_sets:n_sets + 3]
    o_refs = refs[n_sets + 3:2 * n_sets + 3]
    accs = refs[2 * n_sets + 3:]
    f = pl.program_id(1)
    wg = wg_ref[0].astype(BF16)
    wu = wu_ref[0].astype(BF16)
    wd = wd_ref[0].astype(BF16)
    for x_ref, o_ref, acc in zip(x_refs, o_refs, accs):
        @pl.when(f == 0)
        def _():
            acc[...] = jnp.zeros_like(acc)

        x = x_ref[0]
        hg = jnp.dot(x, wg, preferred_element_type=F32)
        hu = jnp.dot(x, wu, preferred_element_type=F32)
        hid = (_silu(hg) * hu).astype(BF16)
        acc[...] += jnp.dot(hid, wd, preferred_element_type=F32)

        @pl.when(f == pl.num_programs(1) - 1)
        def _():
            o_ref[0] = acc[...].astype(BF16)


def expert_ffn(xs_sets, wg, wu, wd):
    ne, _, d = xs_sets[0].shape
    ff = wg.shape[2]
    tf = FFN_COLS
    xspecs = [pl.BlockSpec((1, xs.shape[1], d), lambda e, f: (e, 0, 0)) for xs in xs_sets]
    return pl.pallas_call(
        functools.partial(_ffn_kernel, len(xs_sets)),
        grid=(ne, ff // tf),
        in_specs=xspecs + [pl.BlockSpec((1, d, tf), lambda e, f: (e, 0, f)),
                           pl.BlockSpec((1, d, tf), lambda e, f: (e, 0, f)),
                           pl.BlockSpec((1, tf, d), lambda e, f: (e, f, 0))],
        out_specs=xspecs,
        out_shape=[jax.ShapeDtypeStruct(xs.shape, BF16) for xs in xs_sets],
        scratch_shapes=[pltpu.VMEM(xs.shape[1:], F32) for xs in xs_sets],
        compiler_params=_cparams(("parallel", "arbitrary")),
        name="expert_ffn",
    )(*xs_sets, wg, wu, wd)


def _scatter_kernel(cap, win, nsub, starts_ref, sel_ref, pos_ref, aff_ref, ob_ref, y_ref):
    c = pl.program_id(0)
    e = pl.program_id(1)

    @pl.when(e == 0)
    def _():
        y_ref[...] = jnp.zeros_like(y_ref)

    mine = lax.broadcasted_iota(jnp.int32, (LANES, LANES), 1) == e
    for s in range(nsub):
        rows = slice(s * LANES, (s + 1) * LANES)

        def col(ref):
            return jnp.sum(jnp.where(mine, ref[rows, :], 0.0), axis=-1, keepdims=True)

        pcol, scol, gcol = col(pos_ref), col(sel_ref), col(aff_ref)
        s0 = _window_start(starts_ref, c * nsub + s, e, cap, win)
        slot = (lax.broadcasted_iota(jnp.int32, (LANES, win), 1) + s0).astype(F32)
        onehot = jnp.where(jnp.logical_and(slot == pcol, scol > 0.0), 1.0, 0.0).astype(BF16)
        y_ref[rows, :] += gcol * jnp.dot(onehot, ob_ref[0, pl.ds(s0, win), :],
                                         preferred_element_type=F32)


def scatter_combine(starts, sel_t, pos_t, aff_t, outbuf):
    n_tok = sel_t.shape[0]
    ne, cap, d = outbuf.shape
    chunk = min(SCATTER_TOKENS, n_tok)
    nsub = chunk // LANES
    win = min(2 * LANES, cap)
    tspec = pl.BlockSpec((chunk, LANES), lambda c, e, st: (c, 0))
    return pl.pallas_call(
        functools.partial(_scatter_kernel, cap, win, nsub),
        grid_spec=pltpu.PrefetchScalarGridSpec(
            num_scalar_prefetch=1,
            grid=(n_tok // chunk, ne),
            in_specs=[tspec, tspec, tspec, pl.BlockSpec((1, cap, d), lambda c, e, st: (e, 0, 0))],
            out_specs=pl.BlockSpec((chunk, d), lambda c, e, st: (c, 0))),
        out_shape=jax.ShapeDtypeStruct((n_tok, d), F32),
        compiler_params=_cparams(("parallel", "arbitrary")),
        name="scatter_combine",
    )(starts, sel_t, pos_t, aff_t, outbuf)


def _final_kernel(x1_ref, y_ref, gf_ref, fg_ref, o_ref):
    x2 = x1_ref[0] + gf_ref[0] * y_ref[0]
    o_ref[0] = x2 * lax.rsqrt(jnp.mean(x2 * x2, axis=-1, keepdims=True) + NORM_EPS) * fg_ref[...]


def final_norm(x1, y, gate_f, final_g):
    bsz, seq, d = x1.shape
    tl = TOKEN_ROWS
    per_b = gate_f.shape[0] > 1
    midx = (lambda b, i: (b, 0, 0)) if per_b else (lambda b, i: (0, 0, 0))
    xspec = pl.BlockSpec((1, tl, d), lambda b, i: (b, i, 0))
    return pl.pallas_call(
        _final_kernel,
        grid=(bsz, seq // tl),
        in_specs=[xspec, xspec, pl.BlockSpec((1, 1, d), midx), pl.BlockSpec((1, d), lambda b, i: (0, 0))],
        out_specs=xspec,
        out_shape=jax.ShapeDtypeStruct((bsz, seq, d), F32),
        compiler_params=_cparams(("parallel", "parallel")),
        name="final_norm",
    )(x1, y, gate_f, final_g.reshape(1, d))


def _pad_cols(w, width):
    return jnp.pad(w, ((0, 0), (0, width - w.shape[1])))


def _pad_rows(w, rows):
    return jnp.pad(w, ((0, rows - w.shape[0]), (0, 0)))


def _relayout_columns(w):
    n_ssd = 2 * SSD_WIDTH + 2 * SSD_GROUPS * SSD_STATE + 2 * SSD_HEADS
    ssd, rw = w[:, :n_ssd], w[:, n_ssd:]
    o = 3 * RWKV_WIDTH
    zx = 2 * SSD_WIDTH
    bc = zx + 2 * SSD_GROUPS * SSD_STATE
    parts = [ssd[:, :zx], rw[:, :o], ssd[:, zx:bc], _pad_cols(ssd[:, bc:], LANES),
             _pad_cols(rw[:, o:o + DECAY_LORA], LANES),
             _pad_cols(rw[:, o + DECAY_LORA:o + 2 * DECAY_LORA], LANES),
             _pad_cols(rw[:, o + 2 * DECAY_LORA:o + 2 * DECAY_LORA + ICLR_LORA], LANES),
             _pad_cols(rw[:, o + 2 * DECAY_LORA + ICLR_LORA:], 2 * LANES)]
    return jnp.concatenate(parts, axis=1)


def _layer_params(l, w_in, w_out, ssd_conv_w, ssd_conv_b, ssd_A_log, ssd_dt_bias, ssd_D, ssd_norm_g,
                  rwkv_mu, rwkv_w0, rwkv_w_up, rwkv_a0, rwkv_a_up, rwkv_g_up, rwkv_k_k, rwkv_k_a,
                  rwkv_r_k, rwkv_ln_g, rwkv_ln_b, router_w):
    row = lambda a: a.reshape(1, -1)
    n_ssd = 2 * SSD_WIDTH + 2 * SSD_GROUPS * SSD_STATE + 2 * SSD_HEADS
    mu_full = _relayout_columns(jnp.concatenate([jnp.zeros((1, n_ssd), F32), row(rwkv_mu[l])], axis=1))
    cw, cb = ssd_conv_w[l], row(ssd_conv_b[l])
    xe, be = SSD_WIDTH, SSD_WIDTH + SSD_GROUPS * SSD_STATE
    return dict(
        w_in=_relayout_columns(w_in[l]).astype(BF16),
        w_out=w_out[l].astype(BF16),
        cw_x=cw[:, :xe], cb_x=cb[:, :xe], cw_b=cw[:, xe:be], cb_b=cb[:, xe:be],
        cw_c=cw[:, be:], cb_c=cb[:, be:],
        dt_bias=_pad_cols(row(ssd_dt_bias[l]), LANES), a_log=_pad_cols(row(ssd_A_log[l]), LANES),
        d_exp=row(jnp.repeat(ssd_D[l], SSD_HEAD_DIM)), ssd_ng=row(ssd_norm_g[l]),
        mu_r=mu_full[:, COL_R:COL_K], mu_k=mu_full[:, COL_K:COL_V], mu_v=mu_full[:, COL_V:COL_B],
        mu_lora=mu_full[:, COL_LORA:],
        w_up_f=_pad_rows(rwkv_w_up[l, 0], LANES), w_up_b=_pad_rows(rwkv_w_up[l, 1], LANES),
        a_up=_pad_rows(rwkv_a_up[l], LANES), g_up=_pad_rows(rwkv_g_up[l], 2 * LANES),
        w0_f=row(rwkv_w0[l, 0]), w0_b=row(rwkv_w0[l, 1]), a0=row(rwkv_a0[l]),
        k_k=row(rwkv_k_k[l]), k_a=row(rwkv_k_a[l]), r_k=row(rwkv_r_k[l]),
        ln_g=row(rwkv_ln_g[l]), ln_b=row(rwkv_ln_b[l]),
        router_w=tuple(_pad_cols(t, LANES) for t in _split_bf16(router_w[l])),
    )


def _to_chain(x):
    lead = x.shape[:-3]
    bsz, seq, _ = x.shape[-3:]
    n = len(lead)
    x = x.reshape(lead + (bsz, seq, RWKV_HEADS, RWKV_HEAD_DIM))
    perm = tuple(range(n)) + (n + 1, n + 3, n, n + 2)
    return jnp.transpose(x, perm).reshape(lead + (seq, RWKV_HEAD_DIM, bsz * RWKV_HEADS))


def _from_chain(y, bsz):
    lead = y.shape[:-3]
    seq = y.shape[-3]
    n = len(lead)
    y = y.reshape(lead + (seq, RWKV_HEAD_DIM, bsz, RWKV_HEADS))
    perm = tuple(range(n)) + (n + 2, n, n + 3, n + 1)
    return jnp.transpose(y, perm).reshape(lead + (bsz, seq, RWKV_WIDTH))


def _state_to_chain(s):
    return jnp.transpose(s, (0, 4, 3, 1, 2)).reshape(2, RWKV_HEAD_DIM, RWKV_HEAD_DIM, -1)


def _state_from_chain(s, bsz):
    return jnp.transpose(s.reshape(2, RWKV_HEAD_DIM, RWKV_HEAD_DIM, bsz, RWKV_HEADS), (0, 3, 4, 2, 1))


def _scan_replicas(bsz):
    nch = bsz * RWKV_HEADS
    rep = max(1, LANES // nch)
    assert (nch * rep) % LANES == 0
    return rep


def rwkv_mixer_scan(r, k, v, w2, a, s0, p):
    rep, bsz, seq, _ = r.shape
    nh, hd = RWKV_HEADS, RWKV_HEAD_DIM
    fold = lambda t: t.reshape(t.shape[:-4] + (rep * bsz, seq, RWKV_WIDTH))
    rc, kc, ac, wc = (_to_chain(fold(t)) for t in (r, k, a, w2))
    tabs = [jnp.tile(p[n].reshape(nh, hd).T, (1, LANES // nh)) for n in ("k_k", "k_a")]
    if rep == 1:
        s0c = None if s0 is None else _state_to_chain(s0)
        y, st = rwkv_scan(rc, wc, kc, ac, _to_chain(v), *tabs, s0c)
        return _from_chain(y, bsz), _state_from_chain(st, bsz)

    vq = hd // rep
    vc = jnp.transpose(v.reshape(bsz, seq, nh, rep, vq), (1, 4, 3, 0, 2)).reshape(seq, vq, LANES)
    s0c = None
    if s0 is not None:
        s0c = jnp.transpose(s0.reshape(2, bsz, nh, rep, vq, hd), (0, 5, 4, 3, 1, 2)).reshape(
            2, hd, vq, LANES)
    y, st = rwkv_scan(rc, wc, kc, ac, vc, *tabs, s0c)
    y = jnp.transpose(y.reshape(2, seq, vq, rep, bsz, nh), (0, 4, 1, 5, 3, 2)).reshape(
        2, bsz, seq, RWKV_WIDTH)
    st = jnp.transpose(st.reshape(2, hd, vq, rep, bsz, nh), (0, 4, 5, 3, 2, 1)).reshape(
        2, bsz, nh, hd, hd)
    return y, st


def _grid_pos_embed(n_tokens, dim):
    rows = n_tokens // GRID_W
    row = jnp.repeat(jnp.arange(rows, dtype=F32), GRID_W)
    col = jnp.tile(jnp.arange(GRID_W, dtype=F32), rows)
    quarter = dim // 4
    freqs = jnp.exp(jnp.arange(quarter, dtype=F32) * (-math.log(POS_BASE) / quarter))

    def axis_embed(pos):
        ang = pos[:, None] * freqs[None, :]
        return jnp.concatenate([jnp.sin(ang), jnp.cos(ang)], axis=-1)

    return jnp.concatenate([axis_embed(row), axis_embed(col)], axis=-1)


def layer_to_dispatch(x, pos, mod, p, norm_mix_g, norm_ffn_g, states):
    bsz, seq, d = x.shape
    shift_m, scale_m, gate_m, shift_f, scale_f, gate_f = mod
    h, x0 = norm_modulate(x, pos, norm_mix_g, scale_m, shift_m)
    n_tok = bsz * seq
    proj = matmul_bf16(h.reshape(n_tok, d), p["w_in"], min(PROJ_ROWS, n_tok), PROJ_COLS)
    proj = proj.reshape(bsz, seq, N_PROJ)
    h0f, h0b, s0f, s0b = states
    y_ssd, hf, hb = ssd_mixer(proj, p, h0f, h0b)
    r, k, v, w2, a, g = rwkv_prep(proj, p, _scan_replicas(bsz))
    s0 = None if s0f is None else jnp.stack([s0f, s0b], axis=0)
    y2, st = rwkv_mixer_scan(r, k, v, w2, a, s0, p)
    y_rwkv = rwkv_post(y2, r[0], k[0], a[0], v, g, p)
    x1, h2, logits = out_proj(y_ssd, y_rwkv, p["w_out"], x0, gate_m, norm_ffn_g, scale_f, shift_f,
                              p["router_w"])
    cap = EC_CAPACITY * n_tok // N_EXPERTS
    sel_r, pos_r, sel_t, pos_t, aff_t = route(logits.reshape(n_tok, LANES), cap)
    starts = jnp.concatenate([pos_r[:, :, 0], jnp.full((1, N_EXPERTS), cap, F32)], axis=0)
    starts = starts.astype(jnp.int32).reshape(-1)
    xs = gather_tokens(starts, sel_r, pos_r, h2.reshape(n_tok, d), cap)
    routed = dict(xs=xs, starts=starts, sel_t=sel_t, pos_t=pos_t, aff_t=aff_t, x1=x1, gate_f=gate_f)
    return routed, (hf, hb, st[0], st[1])


def combine_and_finish(routed, outbuf, final_g):
    x1 = routed["x1"]
    y_ffn = scatter_combine(routed["starts"], routed["sel_t"], routed["pos_t"], routed["aff_t"], outbuf)
    return final_norm(x1, y_ffn.reshape(x1.shape), routed["gate_f"], final_g)


def kernel(x_prompt, x_sample, state_ssd_fwd, state_ssd_bwd, state_rwkv_fwd, state_rwkv_bwd, c, c_ctx,
           w_ada, b_ada, norm_mix_g, norm_ffn_g, w_in, w_out, ssd_conv_w, ssd_conv_b, ssd_A_log,
           ssd_dt_bias, ssd_D, ssd_norm_g, rwkv_mu, rwkv_w0, rwkv_w_up, rwkv_a0, rwkv_a_up, rwkv_g_up,
           rwkv_k_k, rwkv_k_a, rwkv_r_k, rwkv_ln_g, rwkv_ln_b, router_w, exp_w_gate, exp_w_up,
           exp_w_down, final_norm_g):
    depth = w_in.shape[0]
    assert depth == 1, "the final norm runs right after the single layer's FFN residual"
    l = 0
    dec_b = x_sample.shape[0]
    p = _layer_params(l, w_in, w_out, ssd_conv_w, ssd_conv_b, ssd_A_log, ssd_dt_bias, ssd_D, ssd_norm_g,
                      rwkv_mu, rwkv_w0, rwkv_w_up, rwkv_a0, rwkv_a_up, rwkv_g_up, rwkv_k_k, rwkv_k_a,
                      rwkv_r_k, rwkv_ln_g, rwkv_ln_b, router_w)
    cond = jnp.concatenate([c_ctx[None, :], c, jnp.zeros((16 - 1 - dec_b, D_MODEL), F32)], axis=0)
    mod = ada_modulation(cond, w_ada[l], b_ada[l]).reshape(16, N_MOD, 1, D_MODEL)
    mod_ctx = [mod[0:1, i] for i in range(N_MOD)]
    mod_lat = [mod[1:1 + dec_b, i] for i in range(N_MOD)]

    routed_ctx, st = layer_to_dispatch(x_prompt, None, mod_ctx, p, norm_mix_g[l], norm_ffn_g[l],
                                       (None, None, None, None))
    pos = _grid_pos_embed(x_sample.shape[1], D_MODEL)
    cached = (state_ssd_fwd[:, l], state_ssd_bwd[:, l], state_rwkv_fwd[:, l], state_rwkv_bwd[:, l])
    routed_lat, _ = layer_to_dispatch(x_sample, pos, mod_lat, p, norm_mix_g[l], norm_ffn_g[l], cached)
    out_ctx, out_lat = expert_ffn([routed_ctx["xs"], routed_lat["xs"]],
                                  exp_w_gate[l], exp_w_up[l], exp_w_down[l])
    y_prompt = combine_and_finish(routed_ctx, out_ctx, final_norm_g)
    y_sample = combine_and_finish(routed_lat, out_lat, final_norm_g)
    return (y_prompt, y_sample, st[0][:, None], st[1][:, None], st[2][:, None], st[3][:, None])
```
